```python
import math
import jax, jax.numpy as jnp
from jax import lax
import numpy as np

D_MODEL = 1024
BATCH = 8
SEQ = 8192
DEPTH = 2

CHUNK = 64
SB_BLOCK = 128
RET_HEADS = 4
RET_DIM = 128
SB_HEADS = 4
SB_DIM = 128
SSM_HEADS = 8
SSM_HEAD_DIM = 64
SSM_STATE = 128
SSM_GROUPS = 2
SSM_CONV = 4
D_FF = 2816
ROPE_BASE = 10000.0
NORM_EPS = 1e-6
N_SUB = 3

RET_W = RET_HEADS * RET_DIM
SB_W = SB_HEADS * SB_DIM
SSM_W = SSM_HEADS * SSM_HEAD_DIM
MIX_W = RET_W + SB_W + SSM_W
SSM_XBC = SSM_W + 2 * SSM_GROUPS * SSM_STATE
IN_W = 4 * RET_W + 3 * SB_W + SSM_W + SSM_XBC + SSM_HEADS

kernel_name = "hybrid_ret_sb_ssd_macaron_adaln"


def rmsnorm(x, gain):
    xf = x.astype(jnp.float32)
    y = xf * lax.rsqrt(jnp.mean(xf * xf, axis=-1, keepdims=True) + NORM_EPS)
    return (y * gain.astype(jnp.float32)).astype(x.dtype)


def modulate(h, shift, scale):
    return h * (1.0 + scale) + shift


def swiglu(u, wg, wu, wd):
    return (jax.nn.silu(u @ wg) * (u @ wu)) @ wd


def rope(x, pos):
    half = x.shape[-1] // 2
    inv_freq = ROPE_BASE ** (-jnp.arange(half, dtype=jnp.float32) / half)
    ang = pos[:, None] * inv_freq[None, :]
    cos = jnp.cos(ang)[None, :, None, :]
    sin = jnp.sin(ang)[None, :, None, :]
    x1 = x[..., :half].astype(jnp.float32)
    x2 = x[..., half:].astype(jnp.float32)
    return jnp.concatenate([x1 * cos - x2 * sin, x1 * sin + x2 * cos], axis=-1)


def retention(q, k, v, g, gn_gain):
    B, S, _ = q.shape
    nc = S // CHUNK
    H, Dh = RET_HEADS, RET_DIM
    pos = jnp.arange(S, dtype=jnp.float32)
    q = rope(q.reshape(B, S, H, Dh), pos)
    k = rope(k.reshape(B, S, H, Dh), pos) * (Dh ** -0.5)
    v = v.reshape(B, S, H, Dh).astype(jnp.float32)
    log_gamma = jnp.log1p(-(2.0 ** (-5.0 - jnp.arange(H, dtype=jnp.float32))))
    idx = jnp.arange(CHUNK, dtype=jnp.float32)
    dmat = jnp.exp(log_gamma[:, None, None] * jnp.abs(idx[:, None] - idx[None, :]))
    qc = q.reshape(B, nc, CHUNK, H, Dh)
    kc = k.reshape(B, nc, CHUNK, H, Dh)
    vc = v.reshape(B, nc, CHUNK, H, Dh)
    scores = jnp.einsum('bclhd,bcshd->bchls', qc, kc) * dmat[None, None]
    y_intra = jnp.einsum('bchls,bcshe->bclhe', scores, vc)
    k_decay = jnp.exp(log_gamma[:, None] * (CHUNK - 1 - idx)[None, :])
    kv = jnp.einsum('bcshd,hs,bcshe->bchde', kc, k_decay, vc).astype(jnp.float32)
    chunk_decay = jnp.exp(log_gamma * CHUNK)[None, :, None, None]

    def step(state, kv_c):
        return state * chunk_decay + kv_c, state

    _, s_prev = lax.scan(step, jnp.zeros((B, H, Dh, Dh), jnp.float32), jnp.moveaxis(kv, 1, 0))
    s_prev = jnp.moveaxis(s_prev, 0, 1)
    q_decay = jnp.exp(log_gamma[:, None] * (idx + 1.0)[None, :])
    y_cross = jnp.einsum('bclhd,hl,bchde->bclhe', qc, q_decay, s_prev)
    y = (y_intra + y_cross).reshape(B, S, H, Dh)
    y = rmsnorm(y, gn_gain.reshape(H, Dh)).reshape(B, S, RET_W)
    return y * jax.nn.silu(g.astype(jnp.float32))


def stick_breaking(q, k, v):
    B, S, _ = q.shape
    H, Dh = SB_HEADS, SB_DIM
    q = q.reshape(B, S, H, Dh).transpose(0, 2, 1, 3)
    k = k.reshape(B, S, H, Dh).transpose(0, 2, 1, 3)
    v = v.reshape(B, S, H, Dh).transpose(0, 2, 1, 3)
    scale = Dh ** -0.5
    outs = []
    for i in range(S // SB_BLOCK):
        q0 = i * SB_BLOCK
        kend = q0 + SB_BLOCK
        qb = q[:, :, q0:kend]
        kb = k[:, :, :kend]
        vb = v[:, :, :kend]
        z = jnp.einsum('bhtd,bhsd->bhts', qb, kb).astype(jnp.float32) * scale
        t_pos = q0 + jnp.arange(SB_BLOCK)
        s_pos = jnp.arange(kend)
        visible = s_pos[None, :] < t_pos[:, None]
        log_beta = jax.nn.log_sigmoid(z)
        log_keep = jnp.where(visible, jax.nn.log_sigmoid(-z), 0.0)
        tail = lax.cumsum(log_keep, axis=3, reverse=True) - log_keep
        w = jnp.where(visible, jnp.exp(log_beta + tail), 0.0)
        outs.append(jnp.einsum('bhts,bhsd->bhtd', w, vb.astype(jnp.float32)))
    y = jnp.concatenate(outs, axis=2)
    return y.transpose(0, 2, 1, 3).reshape(B, S, SB_W)


def mamba2(z, xbc, dt_raw, conv_w, conv_b, dt_bias, a_log, d_skip, norm_gain):
    B, S, _ = xbc.shape
    G, Hg, P, N = SSM_GROUPS, SSM_HEADS // SSM_GROUPS, SSM_HEAD_DIM, SSM_STATE
    nc = S // CHUNK
    xbc = lax.conv_general_dilated(
        xbc, conv_w[:, None, :], window_strides=(1,), padding=[(SSM_CONV - 1, 0)],
        dimension_numbers=('NWC', 'WIO', 'NWC'), feature_group_count=SSM_XBC) + conv_b
    xbc = jax.nn.silu(xbc).astype(jnp.float32)
    xs = xbc[..., :SSM_W].reshape(B, nc, CHUNK, G, Hg, P)
    bm = xbc[..., SSM_W:SSM_W + G * N].reshape(B, nc, CHUNK, G, N)
    cm = xbc[..., SSM_W + G * N:].reshape(B, nc, CHUNK, G, N)
    dt = jax.nn.softplus(dt_raw.astype(jnp.float32) + dt_bias).reshape(B, nc, CHUNK, G, Hg)
    a = -jnp.exp(a_log.astype(jnp.float32)).reshape(G, Hg)
    acum = jnp.cumsum(dt * a, axis=2)
    xdt = xs * dt[..., None]
    causal = jnp.tril(jnp.ones((CHUNK, CHUNK), dtype=bool))[None, None, :, :, None, None]
    seg = acum[:, :, :, None] - acum[:, :, None, :]
    decay = jnp.exp(jnp.where(causal, seg, -jnp.inf))
    cb = jnp.einsum('bclgn,bcsgn->bcgls', cm, bm)
    y_intra = jnp.einsum('bcgls,bclsgh,bcsghp->bclghp', cb, decay, xdt)
    decay_end = jnp.exp(acum[:, :, -1:] - acum)
    states = jnp.einsum('bcsgn,bcsgh,bcsghp->bcghpn', bm, decay_end, xdt)
    chunk_decay = jnp.exp(acum[:, :, -1])

    def step(h, inp):
        st, dec = inp
        return h * dec[..., None, None] + st, h

    _, h_prev = lax.scan(step, jnp.zeros((B, G, Hg, P, N), jnp.float32),
                         (jnp.moveaxis(states, 1, 0), jnp.moveaxis(chunk_decay, 1, 0)))
    h_prev = jnp.moveaxis(h_prev, 0, 1)
    y_inter = jnp.einsum('bclgn,bcghpn,bclgh->bclghp', cm, h_prev, jnp.exp(acum))
    y = y_intra + y_inter + xs * d_skip.astype(jnp.float32).reshape(G, Hg)[..., None]
    y = y.reshape(B, S, SSM_W)
    return rmsnorm(y * jax.nn.silu(z.astype(jnp.float32)), norm_gain)


def _fwd_setup_inputs(seed: int = 0) -> dict:
    key = jax.random.key(seed)
    ks = jax.random.split(key, 32)
    f32 = jnp.float32
    nrm = lambda k, shape, s: jax.random.normal(k, shape, f32) * s
    gain = lambda k, shape: 1.0 + 0.01 * jax.random.normal(k, shape, f32)
    dt0 = jnp.exp(jax.random.uniform(ks[11], (DEPTH, SSM_HEADS), f32, math.log(1e-3), math.log(1e-1)))
    return {
        "x": nrm(ks[0], (BATCH, SEQ, D_MODEL), 1.0),
        "c": nrm(ks[1], (BATCH, D_MODEL), 1.0),
        "ada_w": nrm(ks[2], (DEPTH, D_MODEL, 3 * N_SUB * D_MODEL), 0.1 * D_MODEL ** -0.5),
        "ada_b": nrm(ks[3], (DEPTH, 3 * N_SUB * D_MODEL), 0.01),
        "norm_ffn1": gain(ks[4], (DEPTH, D_MODEL)),
        "ffn1_wg": nrm(ks[5], (DEPTH, D_MODEL, D_FF), D_MODEL ** -0.5),
        "ffn1_wu": nrm(ks[6], (DEPTH, D_MODEL, D_FF), D_MODEL ** -0.5),
        "ffn1_wd": nrm(ks[7], (DEPTH, D_FF, D_MODEL), D_FF ** -0.5),
        "norm_mix": gain(ks[8], (DEPTH, D_MODEL)),
        "w_in": nrm(ks[9], (DEPTH, D_MODEL, IN_W), D_MODEL ** -0.5),
        "conv_w": nrm(ks[10], (DEPTH, SSM_CONV, SSM_XBC), SSM_CONV ** -0.5),
        "conv_b": nrm(ks[12], (DEPTH, SSM_XBC), 0.01),
        "dt_bias": dt0 + jnp.log(-jnp.expm1(-dt0)),
        "a_log": jnp.log(jax.random.uniform(ks[13], (DEPTH, SSM_HEADS), f32, 1.0, 16.0)),
        "d_skip": gain(ks[14], (DEPTH, SSM_HEADS)),
        "ret_gn": gain(ks[15], (DEPTH, RET_W)),
        "ssm_norm": gain(ks[16], (DEPTH, SSM_W)),
        "w_out": nrm(ks[17], (DEPTH, MIX_W, D_MODEL), MIX_W ** -0.5),
        "norm_ffn2": gain(ks[18], (DEPTH, D_MODEL)),
        "ffn2_wg": nrm(ks[19], (DEPTH, D_MODEL, D_FF), D_MODEL ** -0.5),
        "ffn2_wu": nrm(ks[20], (DEPTH, D_MODEL, D_FF), D_MODEL ** -0.5),
        "ffn2_wd": nrm(ks[21], (DEPTH, D_FF, D_MODEL), D_FF ** -0.5),
        "final_ada_w": nrm(ks[22], (D_MODEL, 2 * D_MODEL), 0.1 * D_MODEL ** -0.5),
        "final_ada_b": nrm(ks[23], (2 * D_MODEL,), 0.01),
        "final_norm": gain(ks[24], (D_MODEL,)),
    }


def _fwd_reference(x, c, ada_w, ada_b, norm_ffn1, ffn1_wg, ffn1_wu, ffn1_wd, norm_mix, w_in,
              conv_w, conv_b, dt_bias, a_log, d_skip, ret_gn, ssm_norm, w_out,
              norm_ffn2, ffn2_wg, ffn2_wu, ffn2_wd, final_ada_w, final_ada_b, final_norm):
    B, S, _ = x.shape
    cond = jax.nn.silu(c)
    splits = [RET_W, 2 * RET_W, 3 * RET_W, 4 * RET_W,
              4 * RET_W + SB_W, 4 * RET_W + 2 * SB_W, 4 * RET_W + 3 * SB_W,
              4 * RET_W + 3 * SB_W + SSM_W, 4 * RET_W + 3 * SB_W + SSM_W + SSM_XBC]
    h = x
    for l in range(DEPTH):
        mod = (cond @ ada_w[l] + ada_b[l]).reshape(B, 3 * N_SUB, D_MODEL)[:, None]
        u = modulate(rmsnorm(h, norm_ffn1[l]), mod[:, :, 0], mod[:, :, 1])
        h = h + 0.5 * (1.0 + mod[:, :, 2]) * swiglu(u, ffn1_wg[l], ffn1_wu[l], ffn1_wd[l])
        u = modulate(rmsnorm(h, norm_mix[l]), mod[:, :, 3], mod[:, :, 4])
        proj = u @ w_in[l]
        rq, rk, rv, rg, sq, sk, sv, mz, mxbc, mdt = jnp.split(proj, splits, axis=-1)
        y_ret = retention(rq, rk, rv, rg, ret_gn[l]).astype(x.dtype)
        y_sb = stick_breaking(sq, sk, sv).astype(x.dtype)
        y_ssm = mamba2(mz, mxbc, mdt, conv_w[l], conv_b[l], dt_bias[l], a_log[l],
                       d_skip[l], ssm_norm[l]).astype(x.dtype)
        mixed = jnp.concatenate([y_ret, y_sb, y_ssm], axis=-1) @ w_out[l]
        h = h + (1.0 + mod[:, :, 5]) * mixed
        u = modulate(rmsnorm(h, norm_ffn2[l]), mod[:, :, 6], mod[:, :, 7])
        h = h + 0.5 * (1.0 + mod[:, :, 8]) * swiglu(u, ffn2_wg[l], ffn2_wu[l], ffn2_wd[l])
    fmod = (cond @ final_ada_w + final_ada_b).reshape(B, 2, D_MODEL)[:, None]
    return modulate(rmsnorm(h, final_norm), fmod[:, :, 0], fmod[:, :, 1])


import jax as _jax
import jax.numpy as _jnp

TWIN_FORMAT = 'train_step'
FWD_PARAMS = ['x', 'c', 'ada_w', 'ada_b', 'norm_ffn1', 'ffn1_wg', 'ffn1_wu', 'ffn1_wd', 'norm_mix', 'w_in', 'conv_w', 'conv_b', 'dt_bias', 'a_log', 'd_skip', 'ret_gn', 'ssm_norm', 'w_out', 'norm_ffn2', 'ffn2_wg', 'ffn2_wu', 'ffn2_wd', 'final_ada_w', 'final_ada_b', 'final_norm']
TWIN_WEIGHTS = ['ada_w', 'ada_b', 'norm_ffn1', 'ffn1_wg', 'ffn1_wu', 'ffn1_wd', 'norm_mix', 'w_in', 'conv_w', 'conv_b', 'dt_bias', 'a_log', 'd_skip', 'ret_gn', 'ssm_norm', 'w_out', 'norm_ffn2', 'ffn2_wg', 'ffn2_wu', 'ffn2_wd', 'final_ada_w', 'final_ada_b', 'final_norm']
TWIN_DIFF_INPUT = 'x'
TWIN_INPUTS = ['x', 'c', 'ada_w', 'ada_b', 'norm_ffn1', 'ffn1_wg', 'ffn1_wu', 'ffn1_wd', 'norm_mix', 'w_in', 'conv_w', 'conv_b', 'dt_bias', 'a_log', 'd_skip', 'ret_gn', 'ssm_norm', 'w_out', 'norm_ffn2', 'ffn2_wg', 'ffn2_wu', 'ffn2_wd', 'final_ada_w', 'final_ada_b', 'final_norm', 'loss_target', 'm_ada_w', 'm_ada_b', 'm_norm_ffn1', 'm_ffn1_wg', 'm_ffn1_wu', 'm_ffn1_wd', 'm_norm_mix', 'm_w_in', 'm_conv_w', 'm_conv_b', 'm_dt_bias', 'm_a_log', 'm_d_skip', 'm_ret_gn', 'm_ssm_norm', 'm_w_out', 'm_norm_ffn2', 'm_ffn2_wg', 'm_ffn2_wu', 'm_ffn2_wd', 'm_final_ada_w', 'm_final_ada_b', 'm_final_norm', 'v_ada_w', 'v_ada_b', 'v_norm_ffn1', 'v_ffn1_wg', 'v_ffn1_wu', 'v_ffn1_wd', 'v_norm_mix', 'v_w_in', 'v_conv_w', 'v_conv_b', 'v_dt_bias', 'v_a_log', 'v_d_skip', 'v_ret_gn', 'v_ssm_norm', 'v_w_out', 'v_norm_ffn2', 'v_ffn2_wg', 'v_ffn2_wu', 'v_ffn2_wd', 'v_final_ada_w', 'v_final_ada_b', 'v_final_norm']
TWIN_OUTPUTS = ['loss', 'grad_x', 'grad_ada_w', 'grad_ada_b', 'grad_norm_ffn1', 'grad_ffn1_wg', 'grad_ffn1_wu', 'grad_ffn1_wd', 'grad_norm_mix', 'grad_w_in', 'grad_conv_w', 'grad_conv_b', 'grad_dt_bias', 'grad_a_log', 'grad_d_skip', 'grad_ret_gn', 'grad_ssm_norm', 'grad_w_out', 'grad_norm_ffn2', 'grad_ffn2_wg', 'grad_ffn2_wu', 'grad_ffn2_wd', 'grad_final_ada_w', 'grad_final_ada_b', 'grad_final_norm', 'delta_ada_w', 'delta_ada_b', 'delta_norm_ffn1', 'delta_ffn1_wg', 'delta_ffn1_wu', 'delta_ffn1_wd', 'delta_norm_mix', 'delta_w_in', 'delta_conv_w', 'delta_conv_b', 'delta_dt_bias', 'delta_a_log', 'delta_d_skip', 'delta_ret_gn', 'delta_ssm_norm', 'delta_w_out', 'delta_norm_ffn2', 'delta_ffn2_wg', 'delta_ffn2_wu', 'delta_ffn2_wd', 'delta_final_ada_w', 'delta_final_ada_b', 'delta_final_norm', 'new_m_ada_w', 'new_m_ada_b', 'new_m_norm_ffn1', 'new_m_ffn1_wg', 'new_m_ffn1_wu', 'new_m_ffn1_wd', 'new_m_norm_mix', 'new_m_w_in', 'new_m_conv_w', 'new_m_conv_b', 'new_m_dt_bias', 'new_m_a_log', 'new_m_d_skip', 'new_m_ret_gn', 'new_m_ssm_norm', 'new_m_w_out', 'new_m_norm_ffn2', 'new_m_ffn2_wg', 'new_m_ffn2_wu', 'new_m_ffn2_wd', 'new_m_final_ada_w', 'new_m_final_ada_b', 'new_m_final_norm', 'new_v_ada_w', 'new_v_ada_b', 'new_v_norm_ffn1', 'new_v_ffn1_wg', 'new_v_ffn1_wu', 'new_v_ffn1_wd', 'new_v_norm_mix', 'new_v_w_in', 'new_v_conv_w', 'new_v_conv_b', 'new_v_dt_bias', 'new_v_a_log', 'new_v_d_skip', 'new_v_ret_gn', 'new_v_ssm_norm', 'new_v_w_out', 'new_v_norm_ffn2', 'new_v_ffn2_wg', 'new_v_ffn2_wu', 'new_v_ffn2_wd', 'new_v_final_ada_w', 'new_v_final_ada_b', 'new_v_final_norm']
TWIN_LEAF_KINDS = {'loss': 'loss', 'grad_x': 'grad_x', 'grad_ada_w': 'grad_w', 'grad_ada_b': 'grad_w', 'grad_norm_ffn1': 'grad_w', 'grad_ffn1_wg': 'grad_w', 'grad_ffn1_wu': 'grad_w', 'grad_ffn1_wd': 'grad_w', 'grad_norm_mix': 'grad_w', 'grad_w_in': 'grad_w', 'grad_conv_w': 'grad_w', 'grad_conv_b': 'grad_w', 'grad_dt_bias': 'grad_w', 'grad_a_log': 'grad_w', 'grad_d_skip': 'grad_w', 'grad_ret_gn': 'grad_w', 'grad_ssm_norm': 'grad_w', 'grad_w_out': 'grad_w', 'grad_norm_ffn2': 'grad_w', 'grad_ffn2_wg': 'grad_w', 'grad_ffn2_wu': 'grad_w', 'grad_ffn2_wd': 'grad_w', 'grad_final_ada_w': 'grad_w', 'grad_final_ada_b': 'grad_w', 'grad_final_norm': 'grad_w', 'delta_ada_w': 'delta_w', 'delta_ada_b': 'delta_w', 'delta_norm_ffn1': 'delta_w', 'delta_ffn1_wg': 'delta_w', 'delta_ffn1_wu': 'delta_w', 'delta_ffn1_wd': 'delta_w', 'delta_norm_mix': 'delta_w', 'delta_w_in': 'delta_w', 'delta_conv_w': 'delta_w', 'delta_conv_b': 'delta_w', 'delta_dt_bias': 'delta_w', 'delta_a_log': 'delta_w', 'delta_d_skip': 'delta_w', 'delta_ret_gn': 'delta_w', 'delta_ssm_norm': 'delta_w', 'delta_w_out': 'delta_w', 'delta_norm_ffn2': 'delta_w', 'delta_ffn2_wg': 'delta_w', 'delta_ffn2_wu': 'delta_w', 'delta_ffn2_wd': 'delta_w', 'delta_final_ada_w': 'delta_w', 'delta_final_ada_b': 'delta_w', 'delta_final_norm': 'delta_w', 'new_m_ada_w': 'new_m', 'new_m_ada_b': 'new_m', 'new_m_norm_ffn1': 'new_m', 'new_m_ffn1_wg': 'new_m', 'new_m_ffn1_wu': 'new_m', 'new_m_ffn1_wd': 'new_m', 'new_m_norm_mix': 'new_m', 'new_m_w_in': 'new_m', 'new_m_conv_w': 'new_m', 'new_m_conv_b': 'new_m', 'new_m_dt_bias': 'new_m', 'new_m_a_log': 'new_m', 'new_m_d_skip': 'new_m', 'new_m_ret_gn': 'new_m', 'new_m_ssm_norm': 'new_m', 'new_m_w_out': 'new_m', 'new_m_norm_ffn2': 'new_m', 'new_m_ffn2_wg': 'new_m', 'new_m_ffn2_wu': 'new_m', 'new_m_ffn2_wd': 'new_m', 'new_m_final_ada_w': 'new_m', 'new_m_final_ada_b': 'new_m', 'new_m_final_norm': 'new_m', 'new_v_ada_w': 'new_v', 'new_v_ada_b': 'new_v', 'new_v_norm_ffn1': 'new_v', 'new_v_ffn1_wg': 'new_v', 'new_v_ffn1_wu': 'new_v', 'new_v_ffn1_wd': 'new_v', 'new_v_norm_mix': 'new_v', 'new_v_w_in': 'new_v', 'new_v_conv_w': 'new_v', 'new_v_conv_b': 'new_v', 'new_v_dt_bias': 'new_v', 'new_v_a_log': 'new_v', 'new_v_d_skip': 'new_v', 'new_v_ret_gn': 'new_v', 'new_v_ssm_norm': 'new_v', 'new_v_w_out': 'new_v', 'new_v_norm_ffn2': 'new_v', 'new_v_ffn2_wg': 'new_v', 'new_v_ffn2_wu': 'new_v', 'new_v_ffn2_wd': 'new_v', 'new_v_final_ada_w': 'new_v', 'new_v_final_ada_b': 'new_v', 'new_v_final_norm': 'new_v'}


def _forward(args):
    return _fwd_reference(*[args[k] for k in FWD_PARAMS])


def _output_shape():
    def fwd():
        inp = _fwd_setup_inputs(0)
        return _fwd_reference(*[inp[k] for k in FWD_PARAMS])
    out = _jax.eval_shape(fwd)
    return out.shape, out.dtype

N_MICROBATCH = 1
ADAM_LR = 0.001
ADAM_B1 = 0.9
ADAM_B2 = 0.999
ADAM_EPS = 1e-08
ADAM_WD = 0.01
ADAM_STEP = 10
PER_EXAMPLE_BATCH_AXIS = {'x': 0, 'c': 0, 'loss_target': 0}
SHARED_INPUTS = []
_WEIGHT_DTYPES = {'ada_w': _jnp.float32, 'ada_b': _jnp.float32, 'norm_ffn1': _jnp.float32, 'ffn1_wg': _jnp.float32, 'ffn1_wu': _jnp.float32, 'ffn1_wd': _jnp.float32, 'norm_mix': _jnp.float32, 'w_in': _jnp.float32, 'conv_w': _jnp.float32, 'conv_b': _jnp.float32, 'dt_bias': _jnp.float32, 'a_log': _jnp.float32, 'd_skip': _jnp.float32, 'ret_gn': _jnp.float32, 'ssm_norm': _jnp.float32, 'w_out': _jnp.float32, 'norm_ffn2': _jnp.float32, 'ffn2_wg': _jnp.float32, 'ffn2_wu': _jnp.float32, 'ffn2_wd': _jnp.float32, 'final_ada_w': _jnp.float32, 'final_ada_b': _jnp.float32, 'final_norm': _jnp.float32}
MOMENT_SCALE = {'ada_w': 1.516960e-01, 'ada_b': 3.157585e-01, 'norm_ffn1': 1.342363e-01, 'ffn1_wg': 5.707014e-02, 'ffn1_wu': 5.553888e-02, 'ffn1_wd': 9.181636e-02, 'norm_mix': 2.531394e-01, 'w_in': 1.144054e-01, 'conv_w': 1.464822e-01, 'conv_b': 2.269694e-01, 'dt_bias': 5.588581e-01, 'a_log': 5.644240e-01, 'd_skip': 9.410040e-01, 'ret_gn': 1.019478e-01, 'ssm_norm': 2.033193e-01, 'w_out': 1.780059e-01, 'norm_ffn2': 8.742808e-02, 'ffn2_wg': 3.734502e-02, 'ffn2_wu': 3.649768e-02, 'ffn2_wd': 6.042039e-02, 'final_ada_w': 1.309096e+01, 'final_ada_b': 4.546760e+01, 'final_norm': 6.427895e+01}


def _to_microbatches(a, axis):
    t = _jnp.moveaxis(a, axis, 0)
    t = t.reshape((N_MICROBATCH, t.shape[0] // N_MICROBATCH) + t.shape[1:])
    return _jnp.moveaxis(t, 1, axis + 1)


def setup_inputs(seed: int = 0) -> dict:
    inp = _fwd_setup_inputs(seed)
    key = _jax.random.fold_in(_jax.random.key(seed), 7919)
    shape, _ = _output_shape()
    out = dict(inp)
    out["loss_target"] = _jax.random.normal(_jax.random.fold_in(key, 0), shape, _jnp.float32)
    for i, name in enumerate(TWIN_WEIGHTS):
        w = inp[name].astype(_jnp.float32)
        if MOMENT_SCALE is None:
            s = _jnp.sqrt(_jnp.mean(_jnp.square(w)) + 1e-30)
        else:
            s = MOMENT_SCALE[name]
        km, kv = _jax.random.split(_jax.random.fold_in(key, i + 1))
        out[name] = w
        out["m_" + name] = s * _jax.random.normal(km, w.shape, _jnp.float32)
        out["v_" + name] = (s * s) * _jax.random.uniform(kv, w.shape, _jnp.float32, 0.5, 1.5)
    if N_MICROBATCH > 1:
        for name, axis in PER_EXAMPLE_BATCH_AXIS.items():
            out[name] = _to_microbatches(out[name], axis)
    return {'x': out['x'], 'c': out['c'], 'ada_w': out['ada_w'], 'ada_b': out['ada_b'], 'norm_ffn1': out['norm_ffn1'], 'ffn1_wg': out['ffn1_wg'], 'ffn1_wu': out['ffn1_wu'], 'ffn1_wd': out['ffn1_wd'], 'norm_mix': out['norm_mix'], 'w_in': out['w_in'], 'conv_w': out['conv_w'], 'conv_b': out['conv_b'], 'dt_bias': out['dt_bias'], 'a_log': out['a_log'], 'd_skip': out['d_skip'], 'ret_gn': out['ret_gn'], 'ssm_norm': out['ssm_norm'], 'w_out': out['w_out'], 'norm_ffn2': out['norm_ffn2'], 'ffn2_wg': out['ffn2_wg'], 'ffn2_wu': out['ffn2_wu'], 'ffn2_wd': out['ffn2_wd'], 'final_ada_w': out['final_ada_w'], 'final_ada_b': out['final_ada_b'], 'final_norm': out['final_norm'], 'loss_target': out['loss_target'], 'm_ada_w': out['m_ada_w'], 'm_ada_b': out['m_ada_b'], 'm_norm_ffn1': out['m_norm_ffn1'], 'm_ffn1_wg': out['m_ffn1_wg'], 'm_ffn1_wu': out['m_ffn1_wu'], 'm_ffn1_wd': out['m_ffn1_wd'], 'm_norm_mix': out['m_norm_mix'], 'm_w_in': out['m_w_in'], 'm_conv_w': out['m_conv_w'], 'm_conv_b': out['m_conv_b'], 'm_dt_bias': out['m_dt_bias'], 'm_a_log': out['m_a_log'], 'm_d_skip': out['m_d_skip'], 'm_ret_gn': out['m_ret_gn'], 'm_ssm_norm': out['m_ssm_norm'], 'm_w_out': out['m_w_out'], 'm_norm_ffn2': out['m_norm_ffn2'], 'm_ffn2_wg': out['m_ffn2_wg'], 'm_ffn2_wu': out['m_ffn2_wu'], 'm_ffn2_wd': out['m_ffn2_wd'], 'm_final_ada_w': out['m_final_ada_w'], 'm_final_ada_b': out['m_final_ada_b'], 'm_final_norm': out['m_final_norm'], 'v_ada_w': out['v_ada_w'], 'v_ada_b': out['v_ada_b'], 'v_norm_ffn1': out['v_norm_ffn1'], 'v_ffn1_wg': out['v_ffn1_wg'], 'v_ffn1_wu': out['v_ffn1_wu'], 'v_ffn1_wd': out['v_ffn1_wd'], 'v_norm_mix': out['v_norm_mix'], 'v_w_in': out['v_w_in'], 'v_conv_w': out['v_conv_w'], 'v_conv_b': out['v_conv_b'], 'v_dt_bias': out['v_dt_bias'], 'v_a_log': out['v_a_log'], 'v_d_skip': out['v_d_skip'], 'v_ret_gn': out['v_ret_gn'], 'v_ssm_norm': out['v_ssm_norm'], 'v_w_out': out['v_w_out'], 'v_norm_ffn2': out['v_norm_ffn2'], 'v_ffn2_wg': out['v_ffn2_wg'], 'v_ffn2_wu': out['v_ffn2_wu'], 'v_ffn2_wd': out['v_ffn2_wd'], 'v_final_ada_w': out['v_final_ada_w'], 'v_final_ada_b': out['v_final_ada_b'], 'v_final_norm': out['v_final_norm']}


def _loss(weights, diff, rest, loss_target):
    with _jax.named_scope("forward"):
        args = {**rest, TWIN_DIFF_INPUT: diff, **{k: w.astype(_WEIGHT_DTYPES[k]) for k, w in weights.items()}}
        y = _forward(args)
    with _jax.named_scope("loss_head"):
        err = _jnp.square(y.astype(_jnp.float32) - loss_target)
        return 0.5 * _jnp.sum(_jnp.mean(err, axis=-1)) if err.ndim else 0.5 * err


def _adamw(w, g, m, v):
    m = ADAM_B1 * m + (1.0 - ADAM_B1) * g
    v = ADAM_B2 * v + (1.0 - ADAM_B2) * _jnp.square(g)
    m_hat = m / (1.0 - ADAM_B1 ** ADAM_STEP)
    v_hat = v / (1.0 - ADAM_B2 ** ADAM_STEP)
    delta = -ADAM_LR * (m_hat / (_jnp.sqrt(v_hat) + ADAM_EPS) + ADAM_WD * w)
    return delta, m, v


def reference(x, c, ada_w, ada_b, norm_ffn1, ffn1_wg, ffn1_wu, ffn1_wd, norm_mix, w_in, conv_w, conv_b, dt_bias, a_log, d_skip, ret_gn, ssm_norm, w_out, norm_ffn2, ffn2_wg, ffn2_wu, ffn2_wd, final_ada_w, final_ada_b, final_norm, loss_target, m_ada_w, m_ada_b, m_norm_ffn1, m_ffn1_wg, m_ffn1_wu, m_ffn1_wd, m_norm_mix, m_w_in, m_conv_w, m_conv_b, m_dt_bias, m_a_log, m_d_skip, m_ret_gn, m_ssm_norm, m_w_out, m_norm_ffn2, m_ffn2_wg, m_ffn2_wu, m_ffn2_wd, m_final_ada_w, m_final_ada_b, m_final_norm, v_ada_w, v_ada_b, v_norm_ffn1, v_ffn1_wg, v_ffn1_wu, v_ffn1_wd, v_norm_mix, v_w_in, v_conv_w, v_conv_b, v_dt_bias, v_a_log, v_d_skip, v_ret_gn, v_ssm_norm, v_w_out, v_norm_ffn2, v_ffn2_wg, v_ffn2_wu, v_ffn2_wd, v_final_ada_w, v_final_ada_b, v_final_norm):
    given = dict(x=x, c=c, ada_w=ada_w, ada_b=ada_b, norm_ffn1=norm_ffn1, ffn1_wg=ffn1_wg, ffn1_wu=ffn1_wu, ffn1_wd=ffn1_wd, norm_mix=norm_mix, w_in=w_in, conv_w=conv_w, conv_b=conv_b, dt_bias=dt_bias, a_log=a_log, d_skip=d_skip, ret_gn=ret_gn, ssm_norm=ssm_norm, w_out=w_out, norm_ffn2=norm_ffn2, ffn2_wg=ffn2_wg, ffn2_wu=ffn2_wu, ffn2_wd=ffn2_wd, final_ada_w=final_ada_w, final_ada_b=final_ada_b, final_norm=final_norm, loss_target=loss_target, m_ada_w=m_ada_w, m_ada_b=m_ada_b, m_norm_ffn1=m_norm_ffn1, m_ffn1_wg=m_ffn1_wg, m_ffn1_wu=m_ffn1_wu, m_ffn1_wd=m_ffn1_wd, m_norm_mix=m_norm_mix, m_w_in=m_w_in, m_conv_w=m_conv_w, m_conv_b=m_conv_b, m_dt_bias=m_dt_bias, m_a_log=m_a_log, m_d_skip=m_d_skip, m_ret_gn=m_ret_gn, m_ssm_norm=m_ssm_norm, m_w_out=m_w_out, m_norm_ffn2=m_norm_ffn2, m_ffn2_wg=m_ffn2_wg, m_ffn2_wu=m_ffn2_wu, m_ffn2_wd=m_ffn2_wd, m_final_ada_w=m_final_ada_w, m_final_ada_b=m_final_ada_b, m_final_norm=m_final_norm, v_ada_w=v_ada_w, v_ada_b=v_ada_b, v_norm_ffn1=v_norm_ffn1, v_ffn1_wg=v_ffn1_wg, v_ffn1_wu=v_ffn1_wu, v_ffn1_wd=v_ffn1_wd, v_norm_mix=v_norm_mix, v_w_in=v_w_in, v_conv_w=v_conv_w, v_conv_b=v_conv_b, v_dt_bias=v_dt_bias, v_a_log=v_a_log, v_d_skip=v_d_skip, v_ret_gn=v_ret_gn, v_ssm_norm=v_ssm_norm, v_w_out=v_w_out, v_norm_ffn2=v_norm_ffn2, v_ffn2_wg=v_ffn2_wg, v_ffn2_wu=v_ffn2_wu, v_ffn2_wd=v_ffn2_wd, v_final_ada_w=v_final_ada_w, v_final_ada_b=v_final_ada_b, v_final_norm=v_final_norm)
    weights = {n: given[n] for n in TWIN_WEIGHTS}
    shared = {n: given[n] for n in SHARED_INPUTS}
    per_example = {n: given[n] for n in ['x', 'c']}
    grad_fn = _jax.value_and_grad(_loss, argnums=(0, 1))

    def one_microbatch(ex, loss_target):
        ex = dict(ex)
        diff = ex.pop(TWIN_DIFF_INPUT)
        return grad_fn(weights, diff, {**shared, **ex}, loss_target)

    if N_MICROBATCH == 1:
        loss, (grad_w, grad_x) = one_microbatch(per_example, given["loss_target"])
    else:
        def body(carry, xs):
            loss_sum, grad_sum = carry
            l_k, (gw_k, gx_k) = one_microbatch(xs[0], xs[1])
            with _jax.named_scope("update"):
                return (loss_sum + l_k, _jax.tree.map(_jnp.add, grad_sum, gw_k)), gx_k

        init = (_jnp.zeros((), _jnp.float32), _jax.tree.map(_jnp.zeros_like, weights))
        (loss, grad_w), grad_x = _jax.lax.scan(body, init, (per_example, given["loss_target"]))
    with _jax.named_scope("update"):
        delta_w, new_m, new_v = {}, {}, {}
        for n in TWIN_WEIGHTS:
            delta_w[n], new_m[n], new_v[n] = _adamw(weights[n], grad_w[n], given["m_" + n], given["v_" + n])
    return (loss, grad_x, *[grad_w[n] for n in TWIN_WEIGHTS], *[delta_w[n] for n in TWIN_WEIGHTS],
            *[new_m[n] for n in TWIN_WEIGHTS], *[new_v[n] for n in TWIN_WEIGHTS])
```

```python
import functools
import math

import jax
import jax.numpy as jnp
from jax import lax
from jax.experimental import pallas as pl
from jax.experimental.pallas import tpu as pltpu

F32 = jnp.float32
BF16 = jnp.bfloat16
MESH = pl.DeviceIdType.MESH

D_MODEL = 1024
DEPTH = 2
D_FF = 2816
RET_HEADS = 4
SB_HEADS = 4
HEAD_DIM = 128
SSM_HEADS = 8
SSM_P = 64
SSM_N = 128
SSM_GROUPS = 2
SSM_CONV = 4
MIX_W = 1536
IN_W = 5128
IN_WP = 5376
DT_COL = 5120
ROPE_BASE = 10000.0
NORM_EPS = 1e-6
ADAM_LR = 0.001
ADAM_B1 = 0.9
ADAM_B2 = 0.999
ADAM_EPS = 1e-08
ADAM_WD = 0.01
ADAM_STEP = 10

LANES = 128
VMEM_LIMIT = 56 * 1024 * 1024
RET_T = 256
SB_T = 128
SSD_T = 128
CONV_T = 512
ROW_T = 512
SB_DEAD = -150.0

NN = (((1,), (0,)), ((), ()))
NT = (((1,), (1,)), ((), ()))
TN = (((0,), (0,)), ((), ()))


def _dot(a, b, dims=NN):
    return lax.dot_general(a.astype(BF16), b.astype(BF16), dims, preferred_element_type=F32)


def _params(sem):
    return pltpu.CompilerParams(dimension_semantics=sem, vmem_limit_bytes=VMEM_LIMIT)


def _sigmoid(x):
    return 1.0 / (1.0 + jnp.exp(-x))


def _split_dot(mat01, x, dims=NN, lhs01=True):
    hi = x.astype(BF16)
    r1 = x - hi.astype(F32)
    mid = r1.astype(BF16)
    lo = (r1 - mid.astype(F32)).astype(BF16)
    m = mat01.astype(BF16)
    if lhs01:
        f = lambda p: lax.dot_general(m, p, dims, preferred_element_type=F32)
    else:
        f = lambda p: lax.dot_general(p, m, dims, preferred_element_type=F32)
    return f(hi) + f(mid) + f(lo)


def _mm(name, terms, mode, out_dtype, tm=512, tn=512, tk=1024):
    a0, b0 = terms[0]
    if mode == "nn":
        (M, K), N = a0.shape, b0.shape[1]
    elif mode == "nt":
        (M, K), N = a0.shape, b0.shape[0]
    else:
        (K, M), N = a0.shape, b0.shape[1]
    tm, tn, tk = min(tm, M), min(tn, N), min(tk, K)
    assert M % tm == 0 and N % tn == 0 and K % tk == 0, (name, M, N, K, tm, tn, tk)
    nk = K // tk
    nterm = len(terms)
    dims = {"nn": NN, "nt": NT, "tn": TN}[mode]

    def body(*refs):
        o_ref, acc = refs[2 * nterm], refs[2 * nterm + 1]
        part = None
        for t in range(nterm):
            p = _dot(refs[2 * t][...], refs[2 * t + 1][...], dims)
            part = p if part is None else part + p
        if nk == 1:
            o_ref[...] = part.astype(o_ref.dtype)
        else:
            k = pl.program_id(2)

            @pl.when(k == 0)
            def _():
                acc[...] = part

            @pl.when(k > 0)
            def _():
                acc[...] += part

            @pl.when(k == nk - 1)
            def _():
                o_ref[...] = acc[...].astype(o_ref.dtype)

    if mode == "nn":
        a_spec = pl.BlockSpec((tm, tk), lambda i, j, k: (i, k))
        b_spec = pl.BlockSpec((tk, tn), lambda i, j, k: (k, j))
    elif mode == "nt":
        a_spec = pl.BlockSpec((tm, tk), lambda i, j, k: (i, k))
        b_spec = pl.BlockSpec((tn, tk), lambda i, j, k: (j, k))
    else:
        a_spec = pl.BlockSpec((tk, tm), lambda i, j, k: (k, i))
        b_spec = pl.BlockSpec((tk, tn), lambda i, j, k: (k, j))
    flat = [r for ab in terms for r in ab]
    return pl.pallas_call(
        body, name=name, grid=(M // tm, N // tn, nk),
        in_specs=[a_spec, b_spec] * nterm,
        out_specs=pl.BlockSpec((tm, tn), lambda i, j, k: (i, j)),
        out_shape=jax.ShapeDtypeStruct((M, N), out_dtype),
        scratch_shapes=[pltpu.VMEM((tm, tn) if nk > 1 else (8, LANES), F32)],
        compiler_params=_params(("parallel", "parallel", "arbitrary")),
    )(*flat)


def _proj_res(name, terms, h, gatefac):
    S, D = h.shape
    nterm = len(terms)
    tm = ROW_T

    def body(*refs):
        h_ref, gf_ref, out_ref, hn_ref = refs[2 * nterm:2 * nterm + 4]
        o = None
        for t in range(nterm):
            p = _dot(refs[2 * t][...], refs[2 * t + 1][...])
            o = p if o is None else o + p
        out_ref[...] = o.astype(out_ref.dtype)
        hn_ref[...] = h_ref[...] + gf_ref[...] * o

    in_specs = []
    for a, w in terms:
        in_specs += [pl.BlockSpec((tm, a.shape[1]), lambda i: (i, 0)), pl.BlockSpec(w.shape, lambda i: (0, 0))]
    in_specs += [pl.BlockSpec((tm, D), lambda i: (i, 0)), pl.BlockSpec((1, D), lambda i: (0, 0))]
    flat = [r for aw in terms for r in aw]
    return pl.pallas_call(
        body, name=name, grid=(S // tm,), in_specs=in_specs,
        out_specs=[pl.BlockSpec((tm, D), lambda i: (i, 0))] * 2,
        out_shape=[jax.ShapeDtypeStruct((S, D), BF16), jax.ShapeDtypeStruct((S, D), F32)],
        compiler_params=_params(("parallel",)),
    )(*flat, h, gatefac)


def _norm_mod(name, h, gain, shift, scale):
    S, D = h.shape
    tm = ROW_T

    def body(h_ref, g_ref, sh_ref, sc_ref, u_ref):
        x = h_ref[...]
        r = lax.rsqrt(jnp.mean(x * x, axis=-1, keepdims=True) + NORM_EPS)
        u_ref[...] = (x * r * g_ref[...] * (1.0 + sc_ref[...]) + sh_ref[...]).astype(u_ref.dtype)

    vec = pl.BlockSpec((1, D), lambda i: (0, 0))
    return pl.pallas_call(
        body, name=name, grid=(S // tm,),
        in_specs=[pl.BlockSpec((tm, D), lambda i: (i, 0)), vec, vec, vec],
        out_specs=pl.BlockSpec((tm, D), lambda i: (i, 0)),
        out_shape=jax.ShapeDtypeStruct((S, D), BF16),
        compiler_params=_params(("parallel",)),
    )(h, gain, shift, scale)


def _norm_mod_bwd(name, du, h, gain, scale, gres):
    S, D = h.shape
    tm = ROW_T

    def body(du_ref, h_ref, g_ref, sc_ref, gr_ref, dh_ref, sums_ref):
        @pl.when(pl.program_id(0) == 0)
        def _():
            sums_ref[...] = jnp.zeros_like(sums_ref)

        x = h_ref[...]
        d = du_ref[...].astype(F32)
        gain = g_ref[...]
        r = lax.rsqrt(jnp.mean(x * x, axis=-1, keepdims=True) + NORM_EPS)
        xh = x * r
        dn = d * (1.0 + sc_ref[...])
        sums_ref[0:1, :] += jnp.sum(d, axis=0, keepdims=True)
        sums_ref[1:2, :] += jnp.sum(d * xh * gain, axis=0, keepdims=True)
        sums_ref[2:3, :] += jnp.sum(dn * xh, axis=0, keepdims=True)
        dxh = dn * gain
        dh_ref[...] = gr_ref[...] + r * (dxh - xh * jnp.mean(dxh * xh, axis=-1, keepdims=True))

    row = pl.BlockSpec((tm, D), lambda i: (i, 0))
    vec = pl.BlockSpec((1, D), lambda i: (0, 0))
    return pl.pallas_call(
        body, name=name, grid=(S // tm,),
        in_specs=[row, row, vec, vec, row],
        out_specs=[row, pl.BlockSpec((8, D), lambda i: (0, 0))],
        out_shape=[jax.ShapeDtypeStruct((S, D), F32), jax.ShapeDtypeStruct((8, D), F32)],
        compiler_params=_params(("arbitrary",)),
    )(du, h, gain, scale, gres)


def _gate_bwd(name, g, out, gatefac):
    S, D = g.shape
    tm = ROW_T

    def body(g_ref, o_ref, gf_ref, d_ref, sums_ref):
        @pl.when(pl.program_id(0) == 0)
        def _():
            sums_ref[...] = jnp.zeros_like(sums_ref)

        gv = g_ref[...]
        d_ref[...] = (gv * gf_ref[...]).astype(d_ref.dtype)
        sums_ref[0:1, :] += jnp.sum(gv * o_ref[...].astype(F32), axis=0, keepdims=True)

    row = pl.BlockSpec((tm, D), lambda i: (i, 0))
    return pl.pallas_call(
        body, name=name, grid=(S // tm,),
        in_specs=[row, row, pl.BlockSpec((1, D), lambda i: (0, 0))],
        out_specs=[row, pl.BlockSpec((8, D), lambda i: (0, 0))],
        out_shape=[jax.ShapeDtypeStruct((S, D), BF16), jax.ShapeDtypeStruct((8, D), F32)],
        compiler_params=_params(("arbitrary",)),
    )(g, out, gatefac)


def _final_loss(name, h, target, gain, shift, scale):
    S, D = h.shape
    tm = ROW_T

    def body(h_ref, t_ref, g_ref, sh_ref, sc_ref, dy_ref, loss_ref):
        @pl.when(pl.program_id(0) == 0)
        def _():
            loss_ref[...] = jnp.zeros_like(loss_ref)

        x = h_ref[...]
        r = lax.rsqrt(jnp.mean(x * x, axis=-1, keepdims=True) + NORM_EPS)
        y = x * r * g_ref[...] * (1.0 + sc_ref[...]) + sh_ref[...]
        e = y - t_ref[...]
        dy_ref[...] = e * (1.0 / D)
        loss_ref[0:1, :] += 0.5 * jnp.sum(e * e, axis=0, keepdims=True) * (1.0 / D)

    row = pl.BlockSpec((tm, D), lambda i: (i, 0))
    vec = pl.BlockSpec((1, D), lambda i: (0, 0))
    return pl.pallas_call(
        body, name=name, grid=(S // tm,),
        in_specs=[row, row, vec, vec, vec],
        out_specs=[row, pl.BlockSpec((8, D), lambda i: (0, 0))],
        out_shape=[jax.ShapeDtypeStruct((S, D), F32), jax.ShapeDtypeStruct((8, D), F32)],
        compiler_params=_params(("arbitrary",)),
    )(h, target, gain, shift, scale)


def _ffn_up(name, u, wg, wu):
    S, D = u.shape
    F = wg.shape[1]
    tm, tn = ROW_T, F // 2

    def body(u_ref, wg_ref, wu_ref, a_ref, b_ref, hm_ref):
        uv = u_ref[...]
        a = _dot(uv, wg_ref[...])
        b = _dot(uv, wu_ref[...])
        a_ref[...] = a.astype(a_ref.dtype)
        b_ref[...] = b.astype(b_ref.dtype)
        hm_ref[...] = (a * _sigmoid(a) * b).astype(hm_ref.dtype)

    w_spec = pl.BlockSpec((D, tn), lambda j, i: (0, j))
    o_spec = pl.BlockSpec((tm, tn), lambda j, i: (i, j))
    return pl.pallas_call(
        body, name=name, grid=(F // tn, S // tm),
        in_specs=[pl.BlockSpec((tm, D), lambda j, i: (i, 0)), w_spec, w_spec],
        out_specs=[o_spec] * 3,
        out_shape=[jax.ShapeDtypeStruct((S, F), BF16)] * 3,
        compiler_params=_params(("parallel", "parallel")),
    )(u, wg, wu)


def _ffn_bwd_mid(name, d_out, wd, a, b):
    S, D = d_out.shape
    F = wd.shape[0]
    tm, tn = ROW_T, F // 2

    def body(d_ref, wd_ref, a_ref, b_ref, da_ref, db_ref):
        dhm = _dot(d_ref[...], wd_ref[...], NT)
        av = a_ref[...].astype(F32)
        bv = b_ref[...].astype(F32)
        sg = _sigmoid(av)
        da_ref[...] = (dhm * bv * sg * (1.0 + av * (1.0 - sg))).astype(da_ref.dtype)
        db_ref[...] = (dhm * av * sg).astype(db_ref.dtype)

    t_spec = pl.BlockSpec((tm, tn), lambda j, i: (i, j))
    return pl.pallas_call(
        body, name=name, grid=(F // tn, S // tm),
        in_specs=[pl.BlockSpec((tm, D), lambda j, i: (i, 0)), pl.BlockSpec((tn, D), lambda j, i: (j, 0)), t_spec, t_spec],
        out_specs=[t_spec] * 2,
        out_shape=[jax.ShapeDtypeStruct((S, F), BF16)] * 2,
        compiler_params=_params(("parallel", "parallel")),
    )(d_out, wd, a, b)


def _ret_tables(S):
    T = RET_T
    half = HEAD_DIM // 2
    inv_freq = ROPE_BASE ** (-jnp.arange(half, dtype=F32) / half)
    ang = jnp.arange(S, dtype=F32)[:, None] * inv_freq[None, :]
    cos, sin = jnp.cos(ang), jnp.sin(ang)
    cosf = jnp.concatenate([cos, cos], axis=-1)
    sinf = jnp.concatenate([-sin, sin], axis=-1)
    log_gamma = jnp.log1p(-(2.0 ** (-5.0 - jnp.arange(RET_HEADS, dtype=F32))))
    idx = jnp.arange(T, dtype=F32)
    chunk = jnp.arange(T) // 64
    vis = (chunk[None, :] <= chunk[:, None]).astype(F32)
    mask = jnp.exp(log_gamma[:, None, None] * jnp.abs(idx[:, None] - idx[None, :])) * vis[None]
    ones = jnp.ones((1, 1, LANES), F32)
    qdec = jnp.exp(log_gamma[:, None] * (idx + 1.0)[None, :])[:, :, None] * ones
    kdec = jnp.exp(log_gamma[:, None] * (T - 1.0 - idx)[None, :])[:, :, None] * ones
    cdec = jnp.exp(log_gamma * T)[:, None, None] * jnp.ones((1, 8, LANES), F32)
    return cosf, sinf, mask, qdec, kdec, cdec


def _rope(x, cosf, sinf):
    return x * cosf + pltpu.roll(x, HEAD_DIM // 2, 1) * sinf


def _rope_bwd(d, cosf, sinf):
    return d * cosf + pltpu.roll(d * sinf, HEAD_DIM // 2, 1)


def _ret_specs(T, rev_nb=None):
    if rev_nb is None:
        blk = lambda b: b
    else:
        blk = lambda b: rev_nb - 1 - b
    proj = lambda off: pl.BlockSpec((T, HEAD_DIM), lambda h, b: (blk(b), off + h))
    rows = pl.BlockSpec((T, HEAD_DIM), lambda h, b: (blk(b), 0))
    per_head = lambda shape: pl.BlockSpec((1,) + shape, lambda h, b: (h, 0, 0))
    return ([proj(0), proj(4), proj(8), proj(12), rows, rows,
             per_head((T, T)), per_head((T, LANES)), per_head((T, LANES)), per_head((8, LANES)),
             pl.BlockSpec((1, HEAD_DIM), lambda h, b: (0, h))], blk)


def _ret_fwd(name, proj, gn, tables):
    S = proj.shape[0]
    T = RET_T
    nb = S // T
    scale = HEAD_DIM ** -0.5
    specs, _ = _ret_specs(T)

    def body(q_ref, k_ref, v_ref, g_ref, cos_ref, sin_ref, m_ref, qd_ref, kd_ref, cd_ref, gn_ref, y_ref, st_ref, st):
        @pl.when(pl.program_id(1) == 0)
        def _():
            st[...] = jnp.zeros_like(st)

        cosf, sinf = cos_ref[...], sin_ref[...]
        qr = _rope(q_ref[...], cosf, sinf)
        kr = _rope(k_ref[...], cosf, sinf) * scale
        v = v_ref[...]
        sp = st[...]
        st_ref[0, 0] = sp
        w = _dot(qr, kr, NT) * m_ref[0]
        y = _dot(w, v) + _dot(qr * qd_ref[0], sp)
        st[...] = cd_ref[0, 0:1, :] * sp + _dot(kr * kd_ref[0], v, TN)
        r = lax.rsqrt(jnp.mean(y * y, axis=-1, keepdims=True) + NORM_EPS)
        g = g_ref[...]
        y_ref[...] = y * r * gn_ref[...] * (g * _sigmoid(g))

    return pl.pallas_call(
        body, name=name, grid=(RET_HEADS, nb), in_specs=specs,
        out_specs=[pl.BlockSpec((T, HEAD_DIM), lambda h, b: (b, h)),
                   pl.BlockSpec((1, 1, HEAD_DIM, HEAD_DIM), lambda h, b: (h, b, 0, 0))],
        out_shape=[jax.ShapeDtypeStruct((S, RET_HEADS * HEAD_DIM), F32),
                   jax.ShapeDtypeStruct((RET_HEADS, nb, HEAD_DIM, HEAD_DIM), F32)],
        scratch_shapes=[pltpu.VMEM((HEAD_DIM, HEAD_DIM), F32)],
        compiler_params=_params(("parallel", "arbitrary")),
    )(proj, proj, proj, proj, tables[0], tables[1], tables[2], tables[3], tables[4], tables[5], gn)


def _ret_bwd(name, proj, gn, tables, states, dmix):
    S = proj.shape[0]
    T = RET_T
    nb = S // T
    scale = HEAD_DIM ** -0.5
    specs, blk = _ret_specs(T, rev_nb=nb)
    specs = specs + [pl.BlockSpec((1, 1, HEAD_DIM, HEAD_DIM), lambda h, b: (h, blk(b), 0, 0)),
                     pl.BlockSpec((T, HEAD_DIM), lambda h, b: (blk(b), h))]

    def body(q_ref, k_ref, v_ref, g_ref, cos_ref, sin_ref, m_ref, qd_ref, kd_ref, cd_ref, gn_ref, st_ref, do_ref,
             d_ref, dgn_ref, dst):
        @pl.when(pl.program_id(1) == 0)
        def _():
            dst[...] = jnp.zeros_like(dst)
            dgn_ref[...] = jnp.zeros_like(dgn_ref)

        cosf, sinf = cos_ref[...], sin_ref[...]
        qr = _rope(q_ref[...], cosf, sinf)
        kr = _rope(k_ref[...], cosf, sinf) * scale
        v = v_ref[...]
        sp = st_ref[0, 0]
        mask, qd, kd = m_ref[0], qd_ref[0], kd_ref[0]
        w = _dot(qr, kr, NT) * mask
        y = _dot(w, v) + _dot(qr * qd, sp)
        r = lax.rsqrt(jnp.mean(y * y, axis=-1, keepdims=True) + NORM_EPS)
        yh = y * r
        gn_v = gn_ref[...]
        g = g_ref[...]
        sg = _sigmoid(g)
        do = do_ref[...]
        dyn = do * g * sg
        dgn_ref[...] += jnp.sum(dyn * yh, axis=0, keepdims=True)
        dyh = dyn * gn_v
        dy = r * (dyh - yh * jnp.mean(dyh * yh, axis=-1, keepdims=True))
        dg = do * yh * gn_v * sg * (1.0 + g * (1.0 - sg))
        ds = dst[...]
        dp = _dot(dy, v, NT) * mask
        dqr = _dot(dp, kr) + _dot(dy, sp, NT) * qd
        dkr = _dot(dp, qr, TN) + _dot(v, ds, NT) * kd
        dv = _dot(w, dy, TN) + _dot(kr * kd, ds)
        dst[...] = cd_ref[0, 0:1, :] * ds + _dot(qr * qd, dy, TN)
        d_ref[0] = _rope_bwd(dqr, cosf, sinf).astype(d_ref.dtype)
        d_ref[1] = _rope_bwd(dkr * scale, cosf, sinf).astype(d_ref.dtype)
        d_ref[2] = dv.astype(d_ref.dtype)
        d_ref[3] = dg.astype(d_ref.dtype)

    return pl.pallas_call(
        body, name=name, grid=(RET_HEADS, nb), in_specs=specs,
        out_specs=[pl.BlockSpec((4, T, HEAD_DIM), lambda h, b: (0, blk(b), h)),
                   pl.BlockSpec((1, HEAD_DIM), lambda h, b: (0, h))],
        out_shape=[jax.ShapeDtypeStruct((4, S, RET_HEADS * HEAD_DIM), BF16),
                   jax.ShapeDtypeStruct((1, RET_HEADS * HEAD_DIM), F32)],
        scratch_shapes=[pltpu.VMEM((HEAD_DIM, HEAD_DIM), F32)],
        compiler_params=_params(("parallel", "arbitrary")),
    )(proj, proj, proj, proj, tables[0], tables[1], tables[2], tables[3], tables[4], tables[5], gn, states, dmix)


def _sb_logits(qb, kb, i, j, scale):
    T = SB_T
    z = lax.dot_general(qb, kb, NT, preferred_element_type=F32) * scale
    row = lax.broadcasted_iota(jnp.int32, (T, T), 0)
    col = lax.broadcasted_iota(jnp.int32, (T, T), 1)
    vis = jnp.logical_or(j < i, col < row)
    lp = jnp.log1p(jnp.exp(-jnp.abs(z)))
    lb = jnp.minimum(z, 0.0) - lp
    lk = jnp.where(vis, -jnp.maximum(z, 0.0) - lp, 0.0)
    return lb, lk, vis


def _sb_weights(lb, lk, vis, tailc):
    T = SB_T
    row = lax.broadcasted_iota(jnp.int32, (T, T), 0)
    col = lax.broadcasted_iota(jnp.int32, (T, T), 1)
    tail = tailc + _split_dot(row > col, lk, lhs01=False)
    return jnp.where(vis, jnp.exp(lb + tail), 0.0)


def _sb_fwd(name, proj):
    S = proj.shape[0]
    T = SB_T
    nq = S // T
    scale = HEAD_DIM ** -0.5

    def body(q_ref, k_ref, v_ref, y_ref):
        i = pl.program_id(1)
        qb = q_ref[...].astype(BF16)

        def cond(c):
            return jnp.logical_and(c[0] >= 0, c[1] == 0)

        def step(c):
            j, _, tailc, acc = c
            rows = pl.ds(pl.multiple_of(j * T, T), T)
            kb = k_ref[rows, :].astype(BF16)
            vb = v_ref[rows, :].astype(BF16)
            lb, lk, vis = _sb_logits(qb, kb, i, j, scale)
            w = _sb_weights(lb, lk, vis, tailc)
            acc = acc + lax.dot_general(w.astype(BF16), vb, NN, preferred_element_type=F32)
            tailc = tailc + jnp.sum(lk, axis=1, keepdims=True)
            dead = (jnp.max(tailc) < SB_DEAD).astype(jnp.int32)
            return j - 1, dead, tailc, acc

        init = (i, jnp.int32(0), jnp.zeros((T, 1), F32), jnp.zeros((T, HEAD_DIM), F32))
        y_ref[...] = lax.while_loop(cond, step, init)[3]

    return pl.pallas_call(
        body, name=name, grid=(SB_HEADS, nq),
        in_specs=[pl.BlockSpec((T, HEAD_DIM), lambda h, i: (i, 16 + h)),
                  pl.BlockSpec((S, HEAD_DIM), lambda h, i: (0, 20 + h)),
                  pl.BlockSpec((S, HEAD_DIM), lambda h, i: (0, 24 + h))],
        out_specs=pl.BlockSpec((T, HEAD_DIM), lambda h, i: (i, h)),
        out_shape=jax.ShapeDtypeStruct((S, SB_HEADS * HEAD_DIM), F32),
        compiler_params=_params(("parallel", "arbitrary")),
    )(proj, proj, proj)


def _sb_bwd(name, proj, dmix):
    S = proj.shape[0]
    T = SB_T
    nq = S // T
    scale = HEAD_DIM ** -0.5

    def body(q_ref, k_ref, v_ref, do_ref, dq_ref, dk_ref, dv_ref, tails):
        i = pl.program_id(1)

        @pl.when(i == 0)
        def _():
            dk_ref[...] = jnp.zeros_like(dk_ref)
            dv_ref[...] = jnp.zeros_like(dv_ref)

        qb = q_ref[...].astype(BF16)
        dob = do_ref[...].astype(BF16)
        row = lax.broadcasted_iota(jnp.int32, (T, T), 0)
        col = lax.broadcasted_iota(jnp.int32, (T, T), 1)

        def cond(c):
            return jnp.logical_and(c[0] >= 0, c[1] == 0)

        def walk_left(c):
            j, _, tailc = c
            tails[j] = tailc
            kb = k_ref[pl.ds(pl.multiple_of(j * T, T), T), :].astype(BF16)
            _, lk, _ = _sb_logits(qb, kb, i, j, scale)
            tailc = tailc + jnp.sum(lk, axis=1, keepdims=True)
            dead = (jnp.max(tailc) < SB_DEAD).astype(jnp.int32)
            return j - 1, dead, tailc

        j_end = lax.while_loop(cond, walk_left, (i, jnp.int32(0), jnp.zeros((T, 1), F32)))[0]

        def walk_back(j, c):
            left, dq = c
            rows = pl.ds(pl.multiple_of(j * T, T), T)
            kb = k_ref[rows, :].astype(BF16)
            vb = v_ref[rows, :].astype(BF16)
            lb, lk, vis = _sb_logits(qb, kb, i, j, scale)
            w = _sb_weights(lb, lk, vis, tails[j])
            de = lax.dot_general(dob, vb, NT, preferred_element_type=F32) * w
            dlk = jnp.where(vis, left + _split_dot(row < col, de, lhs01=False), 0.0)
            sg = jnp.exp(lb)
            dz = ((de * (1.0 - sg) - dlk * sg) * scale).astype(BF16)
            dq = dq + lax.dot_general(dz, kb, NN, preferred_element_type=F32)
            dk_ref[rows, :] += lax.dot_general(dz, qb, TN, preferred_element_type=F32)
            dv_ref[rows, :] += lax.dot_general(w.astype(BF16), dob, TN, preferred_element_type=F32)
            return left + jnp.sum(de, axis=1, keepdims=True), dq

        init = (jnp.zeros((T, 1), F32), jnp.zeros((T, HEAD_DIM), F32))
        dq_ref[...] = lax.fori_loop(j_end + 1, i + 1, walk_back, init)[1]

    blk = pl.BlockSpec((T, HEAD_DIM), lambda h, i: (i, h))
    full = pl.BlockSpec((S, HEAD_DIM), lambda h, i: (0, h))
    shp = jax.ShapeDtypeStruct((S, SB_HEADS * HEAD_DIM), F32)
    return pl.pallas_call(
        body, name=name, grid=(SB_HEADS, nq),
        in_specs=[pl.BlockSpec((T, HEAD_DIM), lambda h, i: (i, 16 + h)),
                  pl.BlockSpec((S, HEAD_DIM), lambda h, i: (0, 20 + h)),
                  pl.BlockSpec((S, HEAD_DIM), lambda h, i: (0, 24 + h)),
                  pl.BlockSpec((T, HEAD_DIM), lambda h, i: (i, 4 + h))],
        out_specs=[blk, full, full],
        out_shape=[shp, shp, shp],
        scratch_shapes=[pltpu.VMEM((nq, T, 1), F32)],
        compiler_params=_params(("parallel", "arbitrary")),
    )(proj, proj, proj, dmix)


def _conv_fwd(name, proj, conv_w, conv_b):
    S = proj.shape[0]
    T = CONV_T
    C = 1024
    K = SSM_CONV

    def body(x_ref, w_ref, b_ref, o_ref, buf):
        @pl.when(pl.program_id(0) == 0)
        def _():
            buf[0:8, :] = jnp.zeros((8, C), F32)

        buf[8:T + 8, :] = x_ref[...]
        acc = b_ref[...] + w_ref[K - 1:K, :] * buf[8:T + 8, :]
        for k in range(K - 1):
            acc = acc + w_ref[k:k + 1, :] * buf[5 + k:5 + k + T, :]
        o_ref[...] = acc * _sigmoid(acc)
        buf[0:8, :] = buf[T:T + 8, :]

    return pl.pallas_call(
        body, name=name, grid=(S // T,),
        in_specs=[pl.BlockSpec((T, C), lambda i: (i, 4)), pl.BlockSpec((K, C), lambda i: (0, 0)),
                  pl.BlockSpec((1, C), lambda i: (0, 0))],
        out_specs=pl.BlockSpec((T, C), lambda i: (i, 0)),
        out_shape=jax.ShapeDtypeStruct((S, C), F32),
        scratch_shapes=[pltpu.VMEM((T + 8, C), F32)],
        compiler_params=_params(("arbitrary",)),
    )(proj, conv_w, conv_b)


def _conv_bwd_act(name, proj, conv_w, conv_b, dxbc):
    S = proj.shape[0]
    T = CONV_T
    C = 1024
    K = SSM_CONV

    def body(x_ref, w_ref, b_ref, d_ref, dp_ref, sums_ref, buf):
        @pl.when(pl.program_id(0) == 0)
        def _():
            buf[0:8, :] = jnp.zeros((8, C), F32)
            sums_ref[...] = jnp.zeros_like(sums_ref)

        buf[8:T + 8, :] = x_ref[...]
        acc = b_ref[...] + w_ref[K - 1:K, :] * buf[8:T + 8, :]
        for k in range(K - 1):
            acc = acc + w_ref[k:k + 1, :] * buf[5 + k:5 + k + T, :]
        sg = _sigmoid(acc)
        dp = d_ref[...] * sg * (1.0 + acc * (1.0 - sg))
        dp_ref[...] = dp
        for k in range(K):
            sums_ref[k:k + 1, :] += jnp.sum(dp * buf[5 + k:5 + k + T, :], axis=0, keepdims=True)
        sums_ref[4:5, :] += jnp.sum(dp, axis=0, keepdims=True)
        buf[0:8, :] = buf[T:T + 8, :]

    row = pl.BlockSpec((T, C), lambda i: (i, 0))
    return pl.pallas_call(
        body, name=name, grid=(S // T,),
        in_specs=[pl.BlockSpec((T, C), lambda i: (i, 4)), pl.BlockSpec((K, C), lambda i: (0, 0)),
                  pl.BlockSpec((1, C), lambda i: (0, 0)), row],
        out_specs=[row, pl.BlockSpec((8, C), lambda i: (0, 0))],
        out_shape=[jax.ShapeDtypeStruct((S, C), F32), jax.ShapeDtypeStruct((8, C), F32)],
        scratch_shapes=[pltpu.VMEM((T + 8, C), F32)],
        compiler_params=_params(("arbitrary",)),
    )(proj, conv_w, conv_b, dxbc)


def _conv_bwd_in(name, dp, conv_w):
    S, C = dp.shape
    T = CONV_T
    K = SSM_CONV
    nb = S // T

    def body(d_ref, w_ref, o_ref, buf):
        @pl.when(pl.program_id(0) == 0)
        def _():
            buf[T:T + 8, :] = jnp.zeros((8, C), F32)

        buf[0:T, :] = d_ref[...]
        acc = w_ref[K - 1:K, :] * buf[0:T, :]
        for k in range(K - 1):
            acc = acc + w_ref[k:k + 1, :] * buf[3 - k:3 - k + T, :]
        o_ref[...] = acc.astype(o_ref.dtype)
        buf[T:T + 8, :] = buf[0:8, :]

    row = pl.BlockSpec((T, C), lambda i: (nb - 1 - i, 0))
    return pl.pallas_call(
        body, name=name, grid=(nb,),
        in_specs=[row, pl.BlockSpec((K, C), lambda i: (0, 0))],
        out_specs=row,
        out_shape=jax.ShapeDtypeStruct((S, C), BF16),
        scratch_shapes=[pltpu.VMEM((T + 8, C), F32)],
        compiler_params=_params(("arbitrary",)),
    )(dp, conv_w)


def _softplus(x):
    return jnp.maximum(x, 0.0) + jnp.log1p(jnp.exp(-jnp.abs(x)))


def _ssd_common(xbc_ref, dt_ref, dtb_ref, alog_ref):
    T = SSD_T
    dtr = dt_ref[...] + dtb_ref[...]
    dt = _softplus(dtr)
    a = -jnp.exp(alog_ref[...])
    dta = dt * a
    row = lax.broadcasted_iota(jnp.int32, (T, T), 0)
    col = lax.broadcasted_iota(jnp.int32, (T, T), 1)
    causal = col <= row
    acum = _split_dot(causal, dta)
    acum_t = _split_dot(row <= col, dta, TN, lhs01=False)
    return dtr, dt, a, acum, acum_t, causal, row


def _ssd_head(xbc_ref, h, dt, acum, acum_t, causal, cb):
    P = SSM_P
    ac = acum[:, h:h + 1]
    decay = jnp.exp(jnp.where(causal, ac - acum_t[h:h + 1, :], -1e30))
    dth = dt[:, h:h + 1]
    xs = xbc_ref[:, h * P:(h + 1) * P]
    xdt = xs * dth
    sc = (cb * decay).astype(BF16)
    aend = acum[SSD_T - 1:SSD_T, h:h + 1]
    return ac, dth, xs, xdt, sc, aend


def _ssd_specs(T, blk):
    vec = lambda n: pl.BlockSpec((1, n), lambda b: (0, 0))
    return [pl.BlockSpec((T, 1024), lambda b: (blk(b), 0)),
            pl.BlockSpec((T, LANES), lambda b: (blk(b), DT_COL // LANES)),
            pl.BlockSpec((T, 512), lambda b: (blk(b), 7)),
            vec(LANES), vec(LANES), vec(LANES), vec(512)]


def _ssd_fwd(name, xbc, proj, dt_bias, a_log, d_skip, gain):
    S = xbc.shape[0]
    T = SSD_T
    nb = S // T
    H, P, N = SSM_HEADS, SSM_P, SSM_N

    def body(xbc_ref, dt_ref, z_ref, dtb_ref, alog_ref, dsk_ref, gain_ref, y_ref, st_ref, st, ybuf):
        @pl.when(pl.program_id(0) == 0)
        def _():
            st[...] = jnp.zeros_like(st)

        _, dt, _, acum, acum_t, causal, _ = _ssd_common(xbc_ref, dt_ref, dtb_ref, alog_ref)
        for g in range(SSM_GROUPS):
            bg = xbc_ref[:, 512 + g * N:512 + (g + 1) * N]
            cg = xbc_ref[:, 768 + g * N:768 + (g + 1) * N]
            cb = _dot(cg, bg, NT)
            for hh in range(H // SSM_GROUPS):
                h = g * (H // SSM_GROUPS) + hh
                ac, _, xs, xdt, sc, aend = _ssd_head(xbc_ref, h, dt, acum, acum_t, causal, cb)
                sp = st[h]
                st_ref[0, h] = sp
                y = _dot(sc, xdt) + jnp.exp(ac) * _dot(cg, sp) + xs * dsk_ref[:, h:h + 1]
                st[h] = jnp.exp(aend) * sp + _dot(bg, xdt * jnp.exp(aend - ac), TN)
                ybuf[:, h * P:(h + 1) * P] = y
        z = z_ref[...]
        yg = ybuf[...] * z * _sigmoid(z)
        r = lax.rsqrt(jnp.mean(yg * yg, axis=-1, keepdims=True) + NORM_EPS)
        y_ref[...] = yg * r * gain_ref[...]

    return pl.pallas_call(
        body, name=name, grid=(nb,), in_specs=_ssd_specs(T, lambda b: b),
        out_specs=[pl.BlockSpec((T, 512), lambda b: (b, 0)), pl.BlockSpec((1, H, N, P), lambda b: (b, 0, 0, 0))],
        out_shape=[jax.ShapeDtypeStruct((S, 512), F32), jax.ShapeDtypeStruct((nb, H, N, P), F32)],
        scratch_shapes=[pltpu.VMEM((H, N, P), F32), pltpu.VMEM((T, 512), F32)],
        compiler_params=_params(("arbitrary",)),
    )(xbc, proj, proj, dt_bias, a_log, d_skip, gain)


def _ssd_bwd(name, xbc, proj, dt_bias, a_log, d_skip, gain, states, dmix):
    S = xbc.shape[0]
    T = SSD_T
    nb = S // T
    H, P, N = SSM_HEADS, SSM_P, SSM_N
    blk = lambda b: nb - 1 - b

    def body(xbc_ref, dt_ref, z_ref, dtb_ref, alog_ref, dsk_ref, gain_ref, st_ref, do_ref,
             dx_ref, dz_ref, ddt_ref, sums_ref, dst, ybuf):
        @pl.when(pl.program_id(0) == 0)
        def _():
            dst[...] = jnp.zeros_like(dst)
            sums_ref[...] = jnp.zeros_like(sums_ref)

        dtr, dt, a, acum, acum_t, causal, row = _ssd_common(xbc_ref, dt_ref, dtb_ref, alog_ref)
        cbs = []
        for g in range(SSM_GROUPS):
            bg = xbc_ref[:, 512 + g * N:512 + (g + 1) * N]
            cg = xbc_ref[:, 768 + g * N:768 + (g + 1) * N]
            cb = _dot(cg, bg, NT)
            cbs.append(cb)
            for hh in range(H // SSM_GROUPS):
                h = g * (H // SSM_GROUPS) + hh
                ac, _, xs, xdt, sc, _ = _ssd_head(xbc_ref, h, dt, acum, acum_t, causal, cb)
                ybuf[:, h * P:(h + 1) * P] = (_dot(sc, xdt) + jnp.exp(ac) * _dot(cg, st_ref[0, h])
                                              + xs * dsk_ref[:, h:h + 1])
        z = z_ref[...]
        sg = _sigmoid(z)
        sz = z * sg
        yfull = ybuf[...]
        yg = yfull * sz
        r = lax.rsqrt(jnp.mean(yg * yg, axis=-1, keepdims=True) + NORM_EPS)
        yh = yg * r
        do = do_ref[...]
        sums_ref[0:1, :] += jnp.sum(do * yh, axis=0, keepdims=True)
        dyh = do * gain_ref[...]
        dyg = r * (dyh - yh * jnp.mean(dyh * yh, axis=-1, keepdims=True))
        dz_ref[...] = (dyg * yfull * sg * (1.0 + z * (1.0 - sg))).astype(dz_ref.dtype)
        dyv = dyg * sz

        lane = lax.broadcasted_iota(jnp.int32, (T, LANES), 1)
        rowl = lax.broadcasted_iota(jnp.int32, (T, 1), 0)
        dacum = jnp.zeros((T, LANES), F32)
        ddt = jnp.zeros((T, LANES), F32)
        dskp = jnp.zeros((T, LANES), F32)
        for g in range(SSM_GROUPS):
            bg = xbc_ref[:, 512 + g * N:512 + (g + 1) * N]
            cg = xbc_ref[:, 768 + g * N:768 + (g + 1) * N]
            cb = cbs[g]
            dcb = jnp.zeros((T, T), F32)
            dbg = jnp.zeros((T, N), F32)
            dcg = jnp.zeros((T, N), F32)
            for hh in range(H // SSM_GROUPS):
                h = g * (H // SSM_GROUPS) + hh
                ac, dth, xs, xdt, sc, aend = _ssd_head(xbc_ref, h, dt, acum, acum_t, causal, cb)
                decay = jnp.exp(jnp.where(causal, ac - acum_t[h:h + 1, :], -1e30))
                dy = dyv[:, h * P:(h + 1) * P]
                sp = st_ref[0, h]
                ea = jnp.exp(ac)
                de = jnp.exp(aend - ac)
                dec = jnp.exp(aend)
                dskp = dskp + jnp.where(lane == h, jnp.sum(dy * xs, axis=1, keepdims=True), 0.0)
                dxdt = _dot(sc, dy, TN)
                dsd = _dot(dy, xdt, NT) * decay
                dcb = dcb + dsd
                e = dsd * cb
                e_col = _split_dot(jnp.ones((T, LANES), F32), e, TN, lhs01=False)[:, 0:1]
                dac = jnp.sum(e, axis=1, keepdims=True) - e_col
                dyea = dy * ea
                dac = dac + jnp.sum(dyea * _dot(cg, sp), axis=1, keepdims=True)
                dcg = dcg + _dot(dyea, sp, NT)
                dsp = _dot(cg, dyea, TN)
                dsn = dst[h]
                xde = xdt * de
                dbg = dbg + _dot(xde, dsn, NT)
                wh = _dot(bg, dsn)
                dxdt = dxdt + wh * de
                r_end = jnp.sum(wh * xde, axis=1, keepdims=True)
                dend = jnp.sum(r_end) + jnp.sum(dsn * sp) * dec
                dst[h] = dsp + dec * dsn
                dac = dac - r_end + jnp.where(rowl == T - 1, dend, 0.0)
                dacum = dacum + jnp.where(lane == h, dac, 0.0)
                ddt = ddt + jnp.where(lane == h, jnp.sum(dxdt * xs, axis=1, keepdims=True), 0.0)
                dx_ref[:, h * P:(h + 1) * P] = dxdt * dth + dy * dsk_ref[:, h:h + 1]
            dx_ref[:, 512 + g * N:512 + (g + 1) * N] = dbg + _dot(dcb, cg, TN)
            dx_ref[:, 768 + g * N:768 + (g + 1) * N] = dcg + _dot(dcb, bg)
        ddta = _split_dot(row <= lax.broadcasted_iota(jnp.int32, (T, T), 1), dacum)
        ddt = ddt + ddta * a
        ddtr = ddt * _sigmoid(dtr)
        ddt_ref[...] = ddtr.astype(ddt_ref.dtype)
        sums_ref[1:2, 0:LANES] += jnp.sum(ddtr, axis=0, keepdims=True)
        sums_ref[1:2, LANES:2 * LANES] += jnp.sum(ddta * dt, axis=0, keepdims=True) * a
        sums_ref[1:2, 2 * LANES:3 * LANES] += jnp.sum(dskp, axis=0, keepdims=True)

    specs = _ssd_specs(T, blk) + [pl.BlockSpec((1, H, N, P), lambda b: (blk(b), 0, 0, 0)),
                                  pl.BlockSpec((T, 512), lambda b: (blk(b), 2))]
    return pl.pallas_call(
        body, name=name, grid=(nb,), in_specs=specs,
        out_specs=[pl.BlockSpec((T, 1024), lambda b: (blk(b), 0)), pl.BlockSpec((T, 512), lambda b: (blk(b), 0)),
                   pl.BlockSpec((T, LANES), lambda b: (blk(b), 0)), pl.BlockSpec((8, 512), lambda b: (0, 0))],
        out_shape=[jax.ShapeDtypeStruct((S, 1024), F32), jax.ShapeDtypeStruct((S, 512), BF16),
                   jax.ShapeDtypeStruct((S, LANES), BF16), jax.ShapeDtypeStruct((8, 512), F32)],
        scratch_shapes=[pltpu.VMEM((H, N, P), F32), pltpu.VMEM((T, 512), F32)],
        compiler_params=_params(("arbitrary",)),
    )(xbc, proj, proj, dt_bias, a_log, d_skip, gain, states, dmix)


def _place():
    return lax.axis_index("x"), lax.axis_index("y"), lax.axis_index("c")


def _flip(v, bit):
    return 1 - v if bit else v


def _gather_small(name, v):
    R, C = v.shape

    def body(v_ref, out_ref, send_sems, recv_sems, local_sem):
        x, y, c = _place()
        me = 4 * x + 2 * y + c
        mine = pltpu.make_async_copy(v_ref, out_ref.at[me], local_sem)
        mine.start()
        peers = [(_flip(x, (k >> 2) & 1), _flip(y, (k >> 1) & 1), _flip(c, k & 1)) for k in range(1, 8)]
        sends = []
        for k, peer in enumerate(peers):
            cp = pltpu.make_async_remote_copy(src_ref=v_ref, dst_ref=out_ref.at[me], send_sem=send_sems.at[k],
                                              recv_sem=recv_sems.at[k], device_id=peer, device_id_type=MESH)
            cp.start()
            sends.append(cp)
        for k, (px, py, pc) in enumerate(peers):
            pltpu.make_async_remote_copy(src_ref=v_ref, dst_ref=out_ref.at[4 * px + 2 * py + pc],
                                         send_sem=send_sems.at[k], recv_sem=recv_sems.at[k],
                                         device_id=(px, py, pc), device_id_type=MESH).wait_recv()
        for cp in sends:
            cp.wait_send()
        mine.wait()

    return pl.pallas_call(
        body, name=name, out_shape=jax.ShapeDtypeStruct((8, R, C), v.dtype),
        in_specs=[pl.BlockSpec(memory_space=pltpu.VMEM)], out_specs=pl.BlockSpec(memory_space=pltpu.VMEM),
        scratch_shapes=[pltpu.SemaphoreType.DMA((7,)), pltpu.SemaphoreType.DMA((7,)), pltpu.SemaphoreType.DMA(())],
    )(v)


def _chip_exchange(name, buf, rows, gather):
    C = buf.shape[-1]

    def body(b_ref, out_ref, send_sems, recv_sems, local_sem):
        x, y, c = _place()
        chip = 2 * x + y

        def src(p):
            return b_ref.at[pl.ds(pl.multiple_of(c * rows, 16), rows), :] if gather else b_ref.at[p]

        mine = pltpu.make_async_copy(src(chip), out_ref.at[chip], local_sem)
        mine.start()
        peers = [(_flip(x, (k >> 1) & 1), _flip(y, k & 1)) for k in range(1, 4)]
        sends = []
        for k, (px, py) in enumerate(peers):
            cp = pltpu.make_async_remote_copy(src_ref=src(2 * px + py), dst_ref=out_ref.at[chip],
                                              send_sem=send_sems.at[k], recv_sem=recv_sems.at[k],
                                              device_id=(px, py, c), device_id_type=MESH)
            cp.start()
            sends.append(cp)
        for k, (px, py) in enumerate(peers):
            pltpu.make_async_remote_copy(src_ref=src(chip), dst_ref=out_ref.at[2 * px + py],
                                         send_sem=send_sems.at[k], recv_sem=recv_sems.at[k],
                                         device_id=(px, py, c), device_id_type=MESH).wait_recv()
        for cp in sends:
            cp.wait_send()
        mine.wait()

    return pl.pallas_call(
        body, name=name, out_shape=jax.ShapeDtypeStruct((4, rows, C), buf.dtype),
        in_specs=[pl.BlockSpec(memory_space=pl.ANY)], out_specs=pl.BlockSpec(memory_space=pl.ANY),
        scratch_shapes=[pltpu.SemaphoreType.DMA((3,)), pltpu.SemaphoreType.DMA((3,)), pltpu.SemaphoreType.DMA(())],
    )(buf)


def _sibling_send_other_half(name, buf, rows):
    C = buf.shape[-1]

    def body(b_ref, out_ref, send_sem, recv_sem):
        x, y, c = _place()
        cp = pltpu.make_async_remote_copy(
            src_ref=b_ref.at[:, pl.ds(pl.multiple_of((1 - c) * rows, 16), rows), :], dst_ref=out_ref,
            send_sem=send_sem, recv_sem=recv_sem, device_id=(x, y, 1 - c), device_id_type=MESH)
        cp.start()
        cp.wait()

    return pl.pallas_call(
        body, name=name, out_shape=jax.ShapeDtypeStruct((4, rows, C), buf.dtype),
        in_specs=[pl.BlockSpec(memory_space=pl.ANY)], out_specs=pl.BlockSpec(memory_space=pl.ANY),
        scratch_shapes=[pltpu.SemaphoreType.DMA(()), pltpu.SemaphoreType.DMA(())],
    )(buf)


def _sibling_complete(name, half):
    n, rows, C = half.shape

    def body(h_ref, out_ref, send_sem, recv_sem, local_sem):
        x, y, c = _place()

        def slot(cc):
            return out_ref.at[:, pl.ds(pl.multiple_of(cc * rows, 16), rows), :]

        mine = pltpu.make_async_copy(h_ref, slot(c), local_sem)
        mine.start()
        cp = pltpu.make_async_remote_copy(src_ref=h_ref, dst_ref=slot(c), send_sem=send_sem, recv_sem=recv_sem,
                                          device_id=(x, y, 1 - c), device_id_type=MESH)
        cp.start()
        pltpu.make_async_remote_copy(src_ref=h_ref, dst_ref=slot(1 - c), send_sem=send_sem, recv_sem=recv_sem,
                                     device_id=(x, y, 1 - c), device_id_type=MESH).wait_recv()
        cp.wait_send()
        mine.wait()

    return pl.pallas_call(
        body, name=name, out_shape=jax.ShapeDtypeStruct((n, 2 * rows, C), half.dtype),
        in_specs=[pl.BlockSpec(memory_space=pl.ANY)], out_specs=pl.BlockSpec(memory_space=pl.ANY),
        scratch_shapes=[pltpu.SemaphoreType.DMA(()), pltpu.SemaphoreType.DMA(()), pltpu.SemaphoreType.DMA(())],
    )(half)


PACK_C = 1024
SUM_T = 512


def _add_own_half(name, packed, other, c):
    _, rows, C = other.shape
    nb = rows // SUM_T

    def body(c_ref, p_ref, o_ref, s_ref):
        s_ref[...] = (p_ref[...].astype(F32) + o_ref[...].astype(F32)).astype(s_ref.dtype)

    grid_spec = pltpu.PrefetchScalarGridSpec(
        num_scalar_prefetch=1, grid=(4, nb),
        in_specs=[pl.BlockSpec((1, SUM_T, C), lambda p, i, c_ref: (p, c_ref[0] * nb + i, 0)),
                  pl.BlockSpec((1, SUM_T, C), lambda p, i, c_ref: (p, i, 0))],
        out_specs=pl.BlockSpec((1, SUM_T, C), lambda p, i, c_ref: (p, i, 0)))
    return pl.pallas_call(
        body, name=name, grid_spec=grid_spec, out_shape=jax.ShapeDtypeStruct(other.shape, BF16),
        compiler_params=_params(("parallel", "parallel")),
    )(jnp.reshape(c, (1,)).astype(jnp.int32), packed, other)


def _sum_chips(name, parts):
    _, rows, C = parts.shape

    def body(p_ref, s_ref):
        s = p_ref[0].astype(F32)
        for q in range(1, 4):
            s = s + p_ref[q].astype(F32)
        s_ref[...] = s

    return pl.pallas_call(
        body, name=name, grid=(rows // SUM_T,),
        in_specs=[pl.BlockSpec((4, SUM_T, C), lambda i: (0, i, 0))],
        out_specs=pl.BlockSpec((SUM_T, C), lambda i: (i, 0)),
        out_shape=jax.ShapeDtypeStruct((rows, C), F32),
        compiler_params=_params(("parallel",)),
    )(parts)


def _sum_devices(name, parts):
    _, R, C = parts.shape

    def body(p_ref, s_ref):
        s = p_ref[0]
        for q in range(1, 8):
            s = s + p_ref[q]
        s_ref[...] = s

    return pl.pallas_call(body, name=name, out_shape=jax.ShapeDtypeStruct((R, C), F32))(parts)


def _adamw(name, w, g, m, v):
    R, C = w.shape
    tr = 128 if R % 128 == 0 else R
    c1 = 1.0 - ADAM_B1 ** ADAM_STEP
    c2 = 1.0 - ADAM_B2 ** ADAM_STEP

    def body(w_ref, g_ref, m_ref, v_ref, d_ref, nm_ref, nv_ref):
        gv = g_ref[...]
        nm = ADAM_B1 * m_ref[...] + (1.0 - ADAM_B1) * gv
        nv = ADAM_B2 * v_ref[...] + (1.0 - ADAM_B2) * (gv * gv)
        nm_ref[...] = nm
        nv_ref[...] = nv
        d_ref[...] = -ADAM_LR * ((nm / c1) / (jnp.sqrt(nv / c2) + ADAM_EPS) + ADAM_WD * w_ref[...])

    spec = pl.BlockSpec((tr, C), lambda i: (i, 0))
    shp = jax.ShapeDtypeStruct((R, C), F32)
    return pl.pallas_call(
        body, name=name, grid=(R // tr,), in_specs=[spec] * 4, out_specs=[spec] * 3, out_shape=[shp] * 3,
        compiler_params=_params(("parallel",)),
    )(w, g, m, v)


BIG = (("ffn1_wg", 1), ("ffn1_wu", 1), ("ffn1_wd", 0), ("w_in", 1), ("w_out", 0),
       ("ffn2_wg", 1), ("ffn2_wu", 1), ("ffn2_wd", 0))
PACK_ROWS = 12288
HALF_ROWS = PACK_ROWS // 2
WEIGHTS = ("ada_w", "ada_b", "norm_ffn1", "ffn1_wg", "ffn1_wu", "ffn1_wd", "norm_mix", "w_in", "conv_w", "conv_b",
           "dt_bias", "a_log", "d_skip", "ret_gn", "ssm_norm", "w_out", "norm_ffn2", "ffn2_wg", "ffn2_wu", "ffn2_wd",
           "final_ada_w", "final_ada_b", "final_norm")
SMALL = ("ada_b", "norm_ffn1", "norm_mix", "conv_w", "conv_b", "dt_bias", "a_log", "d_skip", "ret_gn", "ssm_norm",
         "norm_ffn2", "final_ada_b", "final_norm")


def _pack(arrays, rows, dtype):
    flat = jnp.concatenate([a.reshape(-1).astype(dtype) for a in arrays])
    return jnp.pad(flat, (0, rows * PACK_C - flat.shape[0])).reshape(rows, PACK_C)


def _unpack(slab, shapes):
    flat = slab.reshape(-1)
    out, off = [], 0
    for shp in shapes:
        n = math.prod(shp)
        out.append(flat[off:off + n].reshape(shp))
        off += n
    return out


def _pad_lanes(v):
    return jnp.pad(v, (0, LANES - v.shape[0])).reshape(1, LANES)


def _ffn_fwd(tag, h, gain, mod3, wg, wu, wd):
    u = _norm_mod(tag + "_norm", h, gain, mod3[0:1], mod3[1:2])
    a, b, hm = _ffn_up(tag + "_up", u, wg, wu)
    gatefac = 0.5 * (1.0 + mod3[2:3])
    out, h_new = _proj_res(tag + "_down", [(hm, wd)], h, gatefac)
    return h_new, (h, u, a, b, hm, out, gatefac)


def _ffn_bwd(tag, g, saved, gain, mod3, wg, wu, wd):
    h, u, a, b, hm, out, gatefac = saved
    d_out, s_gate = _gate_bwd(tag + "_gate_bwd", g, out, gatefac)
    da, db = _ffn_bwd_mid(tag + "_mid_bwd", d_out, wd, a, b)
    dwd = _mm(tag + "_dwd", [(hm, d_out)], "tn", F32, tm=1408, tn=1024, tk=512)
    dwg = _mm(tag + "_dwg", [(u, da)], "tn", F32, tm=1024, tn=1408, tk=512)
    dwu = _mm(tag + "_dwu", [(u, db)], "tn", F32, tm=1024, tn=1408, tk=512)
    du = _mm(tag + "_du", [(da, wg), (db, wu)], "nt", F32, tm=512, tn=1024, tk=1408)
    g_in, s_norm = _norm_mod_bwd(tag + "_norm_bwd", du, h, gain, mod3[1:2], g)
    dmod3 = jnp.concatenate([s_norm[0:1], s_norm[1:2], 0.5 * s_gate[0:1]], axis=0)
    return g_in, (dwg, dwu, dwd), dmod3, s_norm[2]


def _mixer_fwd(tag, h, P, mod3, w_in, w_out, tables):
    u = _norm_mod(tag + "_norm", h, P["norm_mix"], mod3[0:1], mod3[1:2])
    proj = _mm(tag + "_in_proj", [(u, w_in)], "nn", F32, tm=512, tn=768, tk=1024)
    y_ret, ret_st = _ret_fwd(tag + "_ret", proj, P["ret_gn"], tables)
    y_sb = _sb_fwd(tag + "_sb", proj)
    xbc = _conv_fwd(tag + "_conv", proj, P["conv_w"], P["conv_b"])
    y_ssm, ssm_st = _ssd_fwd(tag + "_ssd", xbc, proj, P["dt_bias"], P["a_log"], P["d_skip"], P["ssm_norm"])
    gatefac = 1.0 + mod3[2:3]
    ys = (y_ret, y_sb, y_ssm)
    mixed, h_new = _proj_res(tag + "_out_proj", [(y, w_out[512 * i:512 * (i + 1)]) for i, y in enumerate(ys)],
                             h, gatefac)
    return h_new, (h, u, proj, ys, ret_st, xbc, ssm_st, mixed, gatefac)


def _mixer_bwd(tag, g, saved, P, mod3, w_in, w_out, tables):
    h, u, proj, ys, ret_st, xbc, ssm_st, mixed, gatefac = saved
    S = h.shape[0]
    d_mixed, s_gate = _gate_bwd(tag + "_gate_bwd", g, mixed, gatefac)
    dmix = _mm(tag + "_dmix", [(d_mixed, w_out)], "nt", F32, tm=512, tn=512, tk=1024)
    dw_out = jnp.concatenate(
        [_mm(tag + f"_dw_out{i}", [(y, d_mixed)], "tn", F32, tm=512, tn=1024, tk=512) for i, y in enumerate(ys)], axis=0)
    d_ret, d_gn = _ret_bwd(tag + "_ret_bwd", proj, P["ret_gn"], tables, ret_st, dmix)
    dq, dk, dv = _sb_bwd(tag + "_sb_bwd", proj, dmix)
    dxbc, dz, ddt, s_ssd = _ssd_bwd(tag + "_ssd_bwd", xbc, proj, P["dt_bias"], P["a_log"], P["d_skip"], P["ssm_norm"],
                                    ssm_st, dmix)
    dp, s_conv = _conv_bwd_act(tag + "_conv_bwd_act", proj, P["conv_w"], P["conv_b"], dxbc)
    dxr = _conv_bwd_in(tag + "_conv_bwd_in", dp, P["conv_w"])
    dproj = jnp.concatenate(
        [d_ret[0], d_ret[1], d_ret[2], d_ret[3], dq.astype(BF16), dk.astype(BF16), dv.astype(BF16), dz, dxr, ddt,
         jnp.zeros((S, IN_WP - DT_COL - LANES), BF16)], axis=1)
    du = _mm(tag + "_du", [(dproj, w_in)], "nt", F32, tm=512, tn=1024, tk=768)
    dw_in = _mm(tag + "_dw_in", [(u, dproj)], "tn", F32, tm=1024, tn=768, tk=512)[:, :IN_W]
    g_in, s_norm = _norm_mod_bwd(tag + "_norm_bwd", du, h, P["norm_mix"], mod3[1:2], g)
    dmod3 = jnp.concatenate([s_norm[0:1], s_norm[1:2], s_gate[0:1]], axis=0)
    small = dict(norm_mix=s_norm[2], conv_w=s_conv[0:4], conv_b=s_conv[4], dt_bias=s_ssd[1, 0:8],
                 a_log=s_ssd[1, LANES:LANES + 8], d_skip=s_ssd[1, 2 * LANES:2 * LANES + 8],
                 ret_gn=d_gn[0], ssm_norm=s_ssd[0])
    return g_in, dw_in, dw_out, dmod3, small


def kernel(x, c, ada_w, ada_b, norm_ffn1, ffn1_wg, ffn1_wu, ffn1_wd, norm_mix, w_in, conv_w, conv_b, dt_bias, a_log, d_skip, ret_gn, ssm_norm, w_out, norm_ffn2, ffn2_wg, ffn2_wu, ffn2_wd, final_ada_w, final_ada_b, final_norm, loss_target, m_ada_w, m_ada_b, m_norm_ffn1, m_ffn1_wg, m_ffn1_wu, m_ffn1_wd, m_norm_mix, m_w_in, m_conv_w, m_conv_b, m_dt_bias, m_a_log, m_d_skip, m_ret_gn, m_ssm_norm, m_w_out, m_norm_ffn2, m_ffn2_wg, m_ffn2_wu, m_ffn2_wd, m_final_ada_w, m_final_ada_b, m_final_norm, v_ada_w, v_ada_b, v_norm_ffn1, v_ffn1_wg, v_ffn1_wu, v_ffn1_wd, v_norm_mix, v_w_in, v_conv_w, v_conv_b, v_dt_bias, v_a_log, v_d_skip, v_ret_gn, v_ssm_norm, v_w_out, v_norm_ffn2, v_ffn2_wg, v_ffn2_wu, v_ffn2_wd, v_final_ada_w, v_final_ada_b, v_final_norm):
    W = dict(ada_w=ada_w, ada_b=ada_b, norm_ffn1=norm_ffn1, ffn1_wg=ffn1_wg, ffn1_wu=ffn1_wu, ffn1_wd=ffn1_wd,
             norm_mix=norm_mix, w_in=w_in, conv_w=conv_w, conv_b=conv_b, dt_bias=dt_bias, a_log=a_log, d_skip=d_skip,
             ret_gn=ret_gn, ssm_norm=ssm_norm, w_out=w_out, norm_ffn2=norm_ffn2, ffn2_wg=ffn2_wg, ffn2_wu=ffn2_wu,
             ffn2_wd=ffn2_wd, final_ada_w=final_ada_w, final_ada_b=final_ada_b, final_norm=final_norm)
    M = dict(ada_w=m_ada_w, ada_b=m_ada_b, norm_ffn1=m_norm_ffn1, ffn1_wg=m_ffn1_wg, ffn1_wu=m_ffn1_wu,
             ffn1_wd=m_ffn1_wd, norm_mix=m_norm_mix, w_in=m_w_in, conv_w=m_conv_w, conv_b=m_conv_b, dt_bias=m_dt_bias,
             a_log=m_a_log, d_skip=m_d_skip, ret_gn=m_ret_gn, ssm_norm=m_ssm_norm, w_out=m_w_out,
             norm_ffn2=m_norm_ffn2, ffn2_wg=m_ffn2_wg, ffn2_wu=m_ffn2_wu, ffn2_wd=m_ffn2_wd,
             final_ada_w=m_final_ada_w, final_ada_b=m_final_ada_b, final_norm=m_final_norm)
    V = dict(ada_w=v_ada_w, ada_b=v_ada_b, norm_ffn1=v_norm_ffn1, ffn1_wg=v_ffn1_wg, ffn1_wu=v_ffn1_wu,
             ffn1_wd=v_ffn1_wd, norm_mix=v_norm_mix, w_in=v_w_in, conv_w=v_conv_w, conv_b=v_conv_b, dt_bias=v_dt_bias,
             a_log=v_a_log, d_skip=v_d_skip, ret_gn=v_ret_gn, ssm_norm=v_ssm_norm, w_out=v_w_out,
             norm_ffn2=v_norm_ffn2, ffn2_wg=v_ffn2_wg, ffn2_wu=v_ffn2_wu, ffn2_wd=v_ffn2_wd,
             final_ada_w=v_final_ada_w, final_ada_b=v_final_ada_b, final_norm=v_final_norm)
    D = D_MODEL
    S = x.shape[1]
    ax, ay, ac = _place()
    me = 4 * ax + 2 * ay + ac
    chip = 2 * ax + ay
    h0 = x[0]

    c_all = _gather_small("gather_c", jnp.pad(c, ((0, 7), (0, 0))))[:, 0, :]
    cond_all = c_all * jax.nn.sigmoid(c_all)
    nmod = 3 * 3 * D // 4
    mod_part = jnp.concatenate(
        [_mm(f"mod_proj{l}", [(cond_all, ada_w[l])], "nn", F32, tm=8, tn=768, tk=D) for l in range(DEPTH)]
        + [_mm("mod_proj_final", [(cond_all, final_ada_w)], "nn", F32, tm=8, tn=512, tk=D),
           conv_w.reshape(DEPTH * SSM_CONV, -1)], axis=1)
    gathered = _gather_small("gather_mod", mod_part)[0::2]
    mine = lax.dynamic_index_in_dim(gathered, me, axis=1, keepdims=False)
    mods = [(jnp.reshape(mine[:, l * nmod:(l + 1) * nmod], (-1,)) + ada_b[l]).reshape(9, D) for l in range(DEPTH)]
    fmod = (jnp.reshape(mine[:, DEPTH * nmod:DEPTH * nmod + 2 * D // 4], (-1,)) + final_ada_b).reshape(2, D)
    conv_full = jnp.transpose(gathered[:, :, DEPTH * nmod + 2 * D // 4:], (1, 0, 2)).reshape(DEPTH, SSM_CONV, -1)

    shard_shapes = [W[n].shape[1:] for _ in range(DEPTH) for n, _ in BIG]
    packed_w = _pack([W[n][l] for l in range(DEPTH) for n, _ in BIG], PACK_ROWS, BF16)
    halves = _chip_exchange("gather_weights", packed_w, HALF_ROWS, gather=True)
    all_w = _sibling_complete("gather_weights_sibling", halves)
    per_chip = [_unpack(all_w[p], shard_shapes) for p in range(4)]
    full = []
    for l in range(DEPTH):
        fw = {}
        for i, (n, axis) in enumerate(BIG):
            fw[n] = jnp.concatenate([per_chip[p][l * len(BIG) + i] for p in range(4)], axis=axis)
        fw["w_in"] = jnp.pad(fw["w_in"], ((0, 0), (0, IN_WP - IN_W)))
        full.append(fw)

    tables = _ret_tables(S)
    small_p = []
    for l in range(DEPTH):
        small_p.append(dict(
            norm_ffn1=norm_ffn1[l:l + 1], norm_mix=norm_mix[l:l + 1], norm_ffn2=norm_ffn2[l:l + 1],
            ret_gn=ret_gn[l:l + 1], ssm_norm=ssm_norm[l:l + 1], conv_w=conv_full[l], conv_b=conv_b[l:l + 1],
            dt_bias=_pad_lanes(dt_bias[l]), a_log=_pad_lanes(a_log[l]), d_skip=_pad_lanes(d_skip[l])))

    h = h0
    saved = []
    for l in range(DEPTH):
        P, fw, mod = small_p[l], full[l], mods[l]
        h, s1 = _ffn_fwd(f"l{l}_ffn1", h, P["norm_ffn1"], mod[0:3], fw["ffn1_wg"], fw["ffn1_wu"], fw["ffn1_wd"])
        h, sm = _mixer_fwd(f"l{l}_mix", h, P, mod[3:6], fw["w_in"], fw["w_out"], tables)
        h, s2 = _ffn_fwd(f"l{l}_ffn2", h, P["norm_ffn2"], mod[6:9], fw["ffn2_wg"], fw["ffn2_wu"], fw["ffn2_wd"])
        saved.append((s1, sm, s2))
    fgain = final_norm.reshape(1, D)
    dy, loss_rows = _final_loss("final_loss", h, loss_target[0], fgain, fmod[0:1], fmod[1:2])

    g, s_final = _norm_mod_bwd("final_norm_bwd", dy, h, fgain, fmod[1:2], jnp.zeros_like(dy))
    dfmod = s_final[0:2]
    small_g = {n: [None] * DEPTH for n in SMALL}
    big_g = [None] * DEPTH
    dmods = [None] * DEPTH
    for l in reversed(range(DEPTH)):
        P, fw, mod = small_p[l], full[l], mods[l]
        s1, sm, s2 = saved[l]
        g, (dwg2, dwu2, dwd2), dm2, dn2 = _ffn_bwd(f"l{l}_ffn2", g, s2, P["norm_ffn2"], mod[6:9],
                                                   fw["ffn2_wg"], fw["ffn2_wu"], fw["ffn2_wd"])
        g, dw_in, dw_out, dmm, sg = _mixer_bwd(f"l{l}_mix", g, sm, P, mod[3:6], fw["w_in"], fw["w_out"], tables)
        g, (dwg1, dwu1, dwd1), dm1, dn1 = _ffn_bwd(f"l{l}_ffn1", g, s1, P["norm_ffn1"], mod[0:3],
                                                   fw["ffn1_wg"], fw["ffn1_wu"], fw["ffn1_wd"])
        dmods[l] = jnp.concatenate([dm1, dmm, dm2], axis=0)
        big_g[l] = dict(ffn1_wg=dwg1, ffn1_wu=dwu1, ffn1_wd=dwd1, w_in=dw_in, w_out=dw_out,
                        ffn2_wg=dwg2, ffn2_wu=dwu2, ffn2_wd=dwd2)
        sg.update(norm_ffn1=dn1, norm_ffn2=dn2)
        for n, val in sg.items():
            small_g[n][l] = val
    grad_x = g[None]

    n_mod = DEPTH * 9 * D + 2 * D
    vec = [jnp.stack(dmods).reshape(-1), dfmod.reshape(-1)]
    layered = [n for n in SMALL if n not in ("ada_b", "final_ada_b", "final_norm")]
    vec += [jnp.stack(small_g[n]).reshape(-1) for n in layered]
    vec += [s_final[2], jnp.sum(loss_rows[0]).reshape(1)]
    flat = jnp.concatenate(vec)
    vrows = -(-flat.shape[0] // (8 * PACK_C)) * 8
    slab = jnp.pad(flat, (0, vrows * PACK_C - flat.shape[0])).reshape(vrows, PACK_C)
    slabs = _gather_small("gather_small_grads", slab)
    total = _sum_devices("sum_small_grads", slabs).reshape(-1)
    grads = {}
    grads["ada_b"] = total[:DEPTH * 9 * D].reshape(DEPTH, 9 * D)
    grads["final_ada_b"] = total[DEPTH * 9 * D:n_mod]
    off = n_mod
    for n in layered:
        shp = (DEPTH,) + ((SSM_CONV, D) if n == "conv_w" else W[n].shape[1:])
        cnt = math.prod(shp)
        grads[n] = total[off:off + cnt].reshape(shp)
        off += cnt
    grads["final_norm"] = total[off:off + D]
    loss = total[off + D]
    grads["conv_w"] = lax.dynamic_slice_in_dim(grads["conv_w"], chip * (D // 4), D // 4, axis=2)

    dmod_all = slabs[:, :n_mod // PACK_C, :].reshape(8, n_mod)
    grads["ada_w"] = jnp.stack([
        _mm(f"grad_ada_w{l}", [(cond_all, lax.dynamic_slice_in_dim(dmod_all, l * 9 * D + chip * nmod, nmod, axis=1))],
            "tn", F32, tm=D, tn=768, tk=8) for l in range(DEPTH)])
    grads["final_ada_w"] = _mm(
        "grad_final_ada_w",
        [(cond_all, lax.dynamic_slice_in_dim(dmod_all, DEPTH * 9 * D + chip * (2 * D // 4), 2 * D // 4, axis=1))],
        "tn", F32, tm=D, tn=512, tk=8)

    def chip_part(p):
        parts = []
        for l in range(DEPTH):
            for n, axis in BIG:
                gfull = big_g[l][n]
                width = gfull.shape[axis] // 4
                parts.append(lax.slice_in_dim(gfull, p * width, (p + 1) * width, axis=axis))
        return _pack(parts, PACK_ROWS, BF16)

    packed_g = jnp.stack([chip_part(p) for p in range(4)])
    from_sibling = _sibling_send_other_half("reduce_grads_sibling", packed_g, HALF_ROWS)
    chip_sum = _add_own_half("reduce_grads_add", packed_g, from_sibling, ac)
    from_chips = _chip_exchange("reduce_grads_chips", chip_sum, HALF_ROWS, gather=False)
    my_half = _sum_chips("reduce_grads_sum", from_chips)
    shard_g = _unpack(_sibling_complete("reduce_grads_complete", my_half[None])[0], shard_shapes)
    for l in range(DEPTH):
        for i, (n, _) in enumerate(BIG):
            grads.setdefault(n, [None] * DEPTH)[l] = shard_g[l * len(BIG) + i]
    for n, _ in BIG:
        grads[n] = jnp.stack(grads[n])

    delta, new_m, new_v = {}, {}, {}
    for n in WEIGHTS:
        if n in SMALL:
            continue
        shp = W[n].shape
        two_d = lambda a: a.reshape(-1, shp[-1])
        d_, m_, v_ = _adamw("adamw_" + n, two_d(W[n]), two_d(grads[n]), two_d(M[n]), two_d(V[n]))
        delta[n], new_m[n], new_v[n] = d_.reshape(shp), m_.reshape(shp), v_.reshape(shp)
    small_shapes = [W[n].shape for n in SMALL]
    n_small = sum(math.prod(s) for s in small_shapes)
    srows = -(-n_small // (8 * LANES)) * 8
    slab_of = lambda T_: jnp.pad(jnp.concatenate([T_[n].reshape(-1) for n in SMALL]),
                                 (0, srows * LANES - n_small)).reshape(srows, LANES)
    outs = _adamw("adamw_small", slab_of(W), slab_of(grads), slab_of(M), slab_of(V))
    for res, o in zip((delta, new_m, new_v), outs):
        for n, val in zip(SMALL, _unpack(o, small_shapes)):
            res[n] = val

    return (loss, grad_x, *[grads[n] for n in WEIGHTS], *[delta[n] for n in WEIGHTS],
            *[new_m[n] for n in WEIGHTS], *[new_v[n] for n in WEIGHTS])
```

```python
import functools
import math

import jax
import jax.numpy as jnp
from jax import lax
from jax.experimental import pallas as pl
from jax.experimental.pallas import tpu as pltpu

F32 = jnp.float32
BF16 = jnp.bfloat16
MESH = pl.DeviceIdType.MESH

D_MODEL = 1024
DEPTH = 2
D_FF = 2816
RET_HEADS = 4
SB_HEADS = 4
HEAD_DIM = 128
SSM_HEADS = 8
SSM_P = 64
SSM_N = 128
SSM_GROUPS = 2
SSM_CONV = 4
MIX_W = 1536
IN_W = 5128
IN_WP = 5376
DT_COL = 5120
ROPE_BASE = 10000.0
NORM_EPS = 1e-6
ADAM_LR = 0.001
ADAM_B1 = 0.9
ADAM_B2 = 0.999
ADAM_EPS = 1e-08
ADAM_WD = 0.01
ADAM_STEP = 10

LANES = 128
VMEM_LIMIT = 56 * 1024 * 1024
RET_T = 256
SB_T = 256
SB_HPS = 2
SSD_T = 128
CONV_T = 512
ROW_T = 512
SB_DEAD = -150.0

NN = (((1,), (0,)), ((), ()))
NT = (((1,), (1,)), ((), ()))
TN = (((0,), (0,)), ((), ()))


def _dot(a, b, dims=NN):
    return lax.dot_general(a.astype(BF16), b.astype(BF16), dims, preferred_element_type=F32)


def _params(sem):
    return pltpu.CompilerParams(dimension_semantics=sem, vmem_limit_bytes=VMEM_LIMIT)


def _sigmoid(x):
    return 1.0 / (1.0 + jnp.exp(-x))


def _split_dot(mat01, x, dims=NN, lhs01=True, pieces=3):
    m = mat01.astype(BF16)
    total, rest = None, x
    for _ in range(pieces):
        p = rest.astype(BF16)
        rest = rest - p.astype(F32)
        d = lax.dot_general(m, p, dims, preferred_element_type=F32) if lhs01 else lax.dot_general(
            p, m, dims, preferred_element_type=F32)
        total = d if total is None else total + d
    return total


def _mm(name, terms, mode, out_dtype, tm=512, tn=512, tk=1024):
    a0, b0 = terms[0]
    if mode == "nn":
        (M, K), N = a0.shape, b0.shape[1]
    elif mode == "nt":
        (M, K), N = a0.shape, b0.shape[0]
    else:
        (K, M), N = a0.shape, b0.shape[1]
    tm, tn, tk = min(tm, M), min(tn, N), min(tk, K)
    assert M % tm == 0 and N % tn == 0 and K % tk == 0, (name, M, N, K, tm, tn, tk)
    nk = K // tk
    nterm = len(terms)
    dims = {"nn": NN, "nt": NT, "tn": TN}[mode]

    def body(*refs):
        o_ref, acc = refs[2 * nterm], refs[2 * nterm + 1]
        part = None
        for t in range(nterm):
            p = _dot(refs[2 * t][...], refs[2 * t + 1][...], dims)
            part = p if part is None else part + p
        if nk == 1:
            o_ref[...] = part.astype(o_ref.dtype)
        else:
            k = pl.program_id(2)

            @pl.when(k == 0)
            def _():
                acc[...] = part

            @pl.when(k > 0)
            def _():
                acc[...] += part

            @pl.when(k == nk - 1)
            def _():
                o_ref[...] = acc[...].astype(o_ref.dtype)

    if mode == "nn":
        a_spec = pl.BlockSpec((tm, tk), lambda i, j, k: (i, k))
        b_spec = pl.BlockSpec((tk, tn), lambda i, j, k: (k, j))
    elif mode == "nt":
        a_spec = pl.BlockSpec((tm, tk), lambda i, j, k: (i, k))
        b_spec = pl.BlockSpec((tn, tk), lambda i, j, k: (j, k))
    else:
        a_spec = pl.BlockSpec((tk, tm), lambda i, j, k: (k, i))
        b_spec = pl.BlockSpec((tk, tn), lambda i, j, k: (k, j))
    flat = [r for ab in terms for r in ab]
    return pl.pallas_call(
        body, name=name, grid=(M // tm, N // tn, nk),
        in_specs=[a_spec, b_spec] * nterm,
        out_specs=pl.BlockSpec((tm, tn), lambda i, j, k: (i, j)),
        out_shape=jax.ShapeDtypeStruct((M, N), out_dtype),
        scratch_shapes=[pltpu.VMEM((tm, tn) if nk > 1 else (8, LANES), F32)],
        compiler_params=_params(("parallel", "parallel", "arbitrary")),
    )(*flat)


def _rowmm(name, terms, mode, out_dtype=F32, res=None):
    S = terms[0][0].shape[-2]
    N = terms[0][2].shape[-1] if mode == "nn" else terms[0][2].shape[-2]
    nterm = len(terms)
    tm = ROW_T
    dims = NN if mode == "nn" else NT

    def body(*refs):
        o = None
        for t in range(nterm):
            p = _dot(refs[2 * t][...], refs[2 * t + 1][...], dims)
            o = p if o is None else o + p
        if res is None:
            refs[2 * nterm][...] = o.astype(out_dtype)
        else:
            h_ref, gf_ref, out_ref, hn_ref = refs[2 * nterm:2 * nterm + 4]
            out_ref[...] = o.astype(out_ref.dtype)
            hn_ref[...] = h_ref[...] + gf_ref[...] * o

    in_specs, flat = [], []
    for a, ai, w, wi in terms:
        if ai is None:
            in_specs.append(pl.BlockSpec((tm, a.shape[1]), lambda i: (i, 0)))
        else:
            in_specs.append(pl.BlockSpec((None, tm, a.shape[2]), lambda i, g=ai: (g, i, 0)))
        if wi is None:
            in_specs.append(pl.BlockSpec(w.shape, lambda i: (0, 0)))
        else:
            in_specs.append(pl.BlockSpec((None,) + w.shape[1:], lambda i, g=wi: (g, 0, 0)))
        flat += [a, w]
    row = pl.BlockSpec((tm, N), lambda i: (i, 0))
    if res is None:
        return pl.pallas_call(
            body, name=name, grid=(S // tm,), in_specs=in_specs, out_specs=row,
            out_shape=jax.ShapeDtypeStruct((S, N), out_dtype), compiler_params=_params(("parallel",)),
        )(*flat)
    h, gatefac = res
    return pl.pallas_call(
        body, name=name, grid=(S // tm,),
        in_specs=in_specs + [row, pl.BlockSpec((1, N), lambda i: (0, 0))], out_specs=[row, row],
        out_shape=[jax.ShapeDtypeStruct((S, N), BF16), jax.ShapeDtypeStruct((S, N), F32)],
        compiler_params=_params(("parallel",)),
    )(*flat, h, gatefac)


def _mm_tn_groups(name, a, b, out_dtype, tk=512):
    G = a.shape[0] if a.ndim == 3 else b.shape[0]
    S, M, N = a.shape[-2], a.shape[-1], b.shape[-1]
    nk = S // tk

    def body(a_ref, b_ref, o_ref, acc):
        k = pl.program_id(1)
        part = _dot(a_ref[...], b_ref[...], TN)

        @pl.when(k == 0)
        def _():
            acc[...] = part

        @pl.when(k > 0)
        def _():
            acc[...] += part

        @pl.when(k == nk - 1)
        def _():
            o_ref[...] = acc[...].astype(o_ref.dtype)

    def spec(arr, width):
        if arr.ndim == 3:
            return pl.BlockSpec((None, tk, width), lambda g, k: (g, k, 0))
        return pl.BlockSpec((tk, width), lambda g, k: (k, 0))

    return pl.pallas_call(
        body, name=name, grid=(G, nk), in_specs=[spec(a, M), spec(b, N)],
        out_specs=pl.BlockSpec((None, M, N), lambda g, k: (g, 0, 0)),
        out_shape=jax.ShapeDtypeStruct((G, M, N), out_dtype),
        scratch_shapes=[pltpu.VMEM((M, N), F32)],
        compiler_params=_params(("parallel", "arbitrary")),
    )(a, b)


def _norm_mod(name, h, gain, shift, scale):
    S, D = h.shape
    tm = ROW_T

    def body(h_ref, g_ref, sh_ref, sc_ref, u_ref):
        x = h_ref[...]
        r = lax.rsqrt(jnp.mean(x * x, axis=-1, keepdims=True) + NORM_EPS)
        u_ref[...] = (x * r * g_ref[...] * (1.0 + sc_ref[...]) + sh_ref[...]).astype(u_ref.dtype)

    vec = pl.BlockSpec((1, D), lambda i: (0, 0))
    return pl.pallas_call(
        body, name=name, grid=(S // tm,),
        in_specs=[pl.BlockSpec((tm, D), lambda i: (i, 0)), vec, vec, vec],
        out_specs=pl.BlockSpec((tm, D), lambda i: (i, 0)),
        out_shape=jax.ShapeDtypeStruct((S, D), BF16),
        compiler_params=_params(("parallel",)),
    )(h, gain, shift, scale)


def _norm_mod_bwd(name, du, h, gain, scale, gres):
    S, D = h.shape
    tm = ROW_T

    def body(du_ref, h_ref, g_ref, sc_ref, gr_ref, dh_ref, sums_ref):
        @pl.when(pl.program_id(0) == 0)
        def _():
            sums_ref[...] = jnp.zeros_like(sums_ref)

        x = h_ref[...]
        d = du_ref[...].astype(F32)
        gain = g_ref[...]
        r = lax.rsqrt(jnp.mean(x * x, axis=-1, keepdims=True) + NORM_EPS)
        xh = x * r
        dn = d * (1.0 + sc_ref[...])
        sums_ref[0:1, :] += jnp.sum(d, axis=0, keepdims=True)
        sums_ref[1:2, :] += jnp.sum(d * xh * gain, axis=0, keepdims=True)
        sums_ref[2:3, :] += jnp.sum(dn * xh, axis=0, keepdims=True)
        dxh = dn * gain
        dh_ref[...] = gr_ref[...] + r * (dxh - xh * jnp.mean(dxh * xh, axis=-1, keepdims=True))

    row = pl.BlockSpec((tm, D), lambda i: (i, 0))
    vec = pl.BlockSpec((1, D), lambda i: (0, 0))
    return pl.pallas_call(
        body, name=name, grid=(S // tm,),
        in_specs=[row, row, vec, vec, row],
        out_specs=[row, pl.BlockSpec((8, D), lambda i: (0, 0))],
        out_shape=[jax.ShapeDtypeStruct((S, D), F32), jax.ShapeDtypeStruct((8, D), F32)],
        compiler_params=_params(("arbitrary",)),
    )(du, h, gain, scale, gres)


def _gate_bwd(name, g, out, gatefac):
    S, D = g.shape
    tm = ROW_T

    def body(g_ref, o_ref, gf_ref, d_ref, sums_ref):
        @pl.when(pl.program_id(0) == 0)
        def _():
            sums_ref[...] = jnp.zeros_like(sums_ref)

        gv = g_ref[...]
        d_ref[...] = (gv * gf_ref[...]).astype(d_ref.dtype)
        sums_ref[0:1, :] += jnp.sum(gv * o_ref[...].astype(F32), axis=0, keepdims=True)

    row = pl.BlockSpec((tm, D), lambda i: (i, 0))
    return pl.pallas_call(
        body, name=name, grid=(S // tm,),
        in_specs=[row, row, pl.BlockSpec((1, D), lambda i: (0, 0))],
        out_specs=[row, pl.BlockSpec((8, D), lambda i: (0, 0))],
        out_shape=[jax.ShapeDtypeStruct((S, D), BF16), jax.ShapeDtypeStruct((8, D), F32)],
        compiler_params=_params(("arbitrary",)),
    )(g, out, gatefac)


def _final_loss(name, h, target, gain, shift, scale):
    S, D = h.shape
    tm = ROW_T

    def body(h_ref, t_ref, g_ref, sh_ref, sc_ref, dy_ref, loss_ref):
        @pl.when(pl.program_id(0) == 0)
        def _():
            loss_ref[...] = jnp.zeros_like(loss_ref)

        x = h_ref[...]
        r = lax.rsqrt(jnp.mean(x * x, axis=-1, keepdims=True) + NORM_EPS)
        y = x * r * g_ref[...] * (1.0 + sc_ref[...]) + sh_ref[...]
        e = y - t_ref[...]
        dy_ref[...] = e * (1.0 / D)
        loss_ref[0:1, :] += 0.5 * jnp.sum(e * e, axis=0, keepdims=True) * (1.0 / D)

    row = pl.BlockSpec((tm, D), lambda i: (i, 0))
    vec = pl.BlockSpec((1, D), lambda i: (0, 0))
    return pl.pallas_call(
        body, name=name, grid=(S // tm,),
        in_specs=[row, row, vec, vec, vec],
        out_specs=[row, pl.BlockSpec((8, D), lambda i: (0, 0))],
        out_shape=[jax.ShapeDtypeStruct((S, D), F32), jax.ShapeDtypeStruct((8, D), F32)],
        compiler_params=_params(("arbitrary",)),
    )(h, target, gain, shift, scale)


def _ffn_up(name, u, wg, wu):
    S, D = u.shape
    G, _, Fg = wg.shape
    tm = ROW_T

    def body(u_ref, wg_ref, wu_ref, a_ref, b_ref, hm_ref):
        uv = u_ref[...]
        a = _dot(uv, wg_ref[...])
        b = _dot(uv, wu_ref[...])
        a_ref[...] = a.astype(a_ref.dtype)
        b_ref[...] = b.astype(b_ref.dtype)
        hm_ref[...] = (a * _sigmoid(a) * b).astype(hm_ref.dtype)

    w_spec = pl.BlockSpec((None, D, Fg), lambda g, i: (g, 0, 0))
    o_spec = pl.BlockSpec((None, tm, Fg), lambda g, i: (g, i, 0))
    return pl.pallas_call(
        body, name=name, grid=(G, S // tm),
        in_specs=[pl.BlockSpec((tm, D), lambda g, i: (i, 0)), w_spec, w_spec],
        out_specs=[o_spec] * 3,
        out_shape=[jax.ShapeDtypeStruct((G, S, Fg), BF16)] * 3,
        compiler_params=_params(("parallel", "parallel")),
    )(u, wg, wu)


def _ffn_bwd_mid(name, d_out, wd, a, b):
    S, D = d_out.shape
    G, Fg, _ = wd.shape
    tm = ROW_T

    def body(d_ref, wd_ref, a_ref, b_ref, da_ref, db_ref):
        dhm = _dot(d_ref[...], wd_ref[...], NT)
        av = a_ref[...].astype(F32)
        bv = b_ref[...].astype(F32)
        sg = _sigmoid(av)
        da_ref[...] = (dhm * bv * sg * (1.0 + av * (1.0 - sg))).astype(da_ref.dtype)
        db_ref[...] = (dhm * av * sg).astype(db_ref.dtype)

    t_spec = pl.BlockSpec((None, tm, Fg), lambda g, i: (g, i, 0))
    return pl.pallas_call(
        body, name=name, grid=(G, S // tm),
        in_specs=[pl.BlockSpec((tm, D), lambda g, i: (i, 0)), pl.BlockSpec((None, Fg, D), lambda g, i: (g, 0, 0)),
                  t_spec, t_spec],
        out_specs=[t_spec] * 2,
        out_shape=[jax.ShapeDtypeStruct((G, S, Fg), BF16)] * 2,
        compiler_params=_params(("parallel", "parallel")),
    )(d_out, wd, a, b)


def _ret_tables(S):
    T = RET_T
    half = HEAD_DIM // 2
    inv_freq = ROPE_BASE ** (-jnp.arange(half, dtype=F32) / half)
    ang = jnp.arange(S, dtype=F32)[:, None] * inv_freq[None, :]
    cos, sin = jnp.cos(ang), jnp.sin(ang)
    cosf = jnp.concatenate([cos, cos], axis=-1)
    sinf = jnp.concatenate([-sin, sin], axis=-1)
    log_gamma = jnp.log1p(-(2.0 ** (-5.0 - jnp.arange(RET_HEADS, dtype=F32))))
    idx = jnp.arange(T, dtype=F32)
    chunk = jnp.arange(T) // 64
    vis = (chunk[None, :] <= chunk[:, None]).astype(F32)
    mask = jnp.exp(log_gamma[:, None, None] * jnp.abs(idx[:, None] - idx[None, :])) * vis[None]
    ones = jnp.ones((1, 1, LANES), F32)
    qdec = jnp.exp(log_gamma[:, None] * (idx + 1.0)[None, :])[:, :, None] * ones
    kdec = jnp.exp(log_gamma[:, None] * (T - 1.0 - idx)[None, :])[:, :, None] * ones
    cdec = jnp.exp(log_gamma * T)[:, None, None] * jnp.ones((1, 8, LANES), F32)
    return cosf, sinf, mask, qdec, kdec, cdec


def _rope(x, cosf, sinf):
    return x * cosf + pltpu.roll(x, HEAD_DIM // 2, 1) * sinf


def _rope_bwd(d, cosf, sinf):
    return d * cosf + pltpu.roll(d * sinf, HEAD_DIM // 2, 1)


def _ret_specs(T, rev_nb=None):
    if rev_nb is None:
        blk = lambda b: b
    else:
        blk = lambda b: rev_nb - 1 - b
    proj = lambda off: pl.BlockSpec((T, HEAD_DIM), lambda h, b: (blk(b), off + h))
    rows = pl.BlockSpec((T, HEAD_DIM), lambda h, b: (blk(b), 0))
    per_head = lambda shape: pl.BlockSpec((1,) + shape, lambda h, b: (h, 0, 0))
    return ([proj(0), proj(4), proj(8), proj(12), rows, rows,
             per_head((T, T)), per_head((T, LANES)), per_head((T, LANES)), per_head((8, LANES)),
             pl.BlockSpec((1, HEAD_DIM), lambda h, b: (0, h))], blk)


def _ret_fwd(name, proj, gn, tables):
    S = proj.shape[0]
    T = RET_T
    nb = S // T
    scale = HEAD_DIM ** -0.5
    specs, _ = _ret_specs(T)

    def body(q_ref, k_ref, v_ref, g_ref, cos_ref, sin_ref, m_ref, qd_ref, kd_ref, cd_ref, gn_ref, y_ref, st_ref, st):
        @pl.when(pl.program_id(1) == 0)
        def _():
            st[...] = jnp.zeros_like(st)

        cosf, sinf = cos_ref[...], sin_ref[...]
        qr = _rope(q_ref[...], cosf, sinf)
        kr = _rope(k_ref[...], cosf, sinf) * scale
        v = v_ref[...]
        sp = st[...]
        st_ref[0, 0] = sp
        w = _dot(qr, kr, NT) * m_ref[0]
        y = _dot(w, v) + _dot(qr * qd_ref[0], sp)
        st[...] = cd_ref[0, 0:1, :] * sp + _dot(kr * kd_ref[0], v, TN)
        r = lax.rsqrt(jnp.mean(y * y, axis=-1, keepdims=True) + NORM_EPS)
        g = g_ref[...]
        y_ref[...] = y * r * gn_ref[...] * (g * _sigmoid(g))

    return pl.pallas_call(
        body, name=name, grid=(RET_HEADS, nb), in_specs=specs,
        out_specs=[pl.BlockSpec((T, HEAD_DIM), lambda h, b: (b, h)),
                   pl.BlockSpec((1, 1, HEAD_DIM, HEAD_DIM), lambda h, b: (h, b, 0, 0))],
        out_shape=[jax.ShapeDtypeStruct((S, RET_HEADS * HEAD_DIM), F32),
                   jax.ShapeDtypeStruct((RET_HEADS, nb, HEAD_DIM, HEAD_DIM), F32)],
        scratch_shapes=[pltpu.VMEM((HEAD_DIM, HEAD_DIM), F32)],
        compiler_params=_params(("parallel", "arbitrary")),
    )(proj, proj, proj, proj, tables[0], tables[1], tables[2], tables[3], tables[4], tables[5], gn)


def _ret_bwd(name, proj, gn, tables, states, dmix):
    S = proj.shape[0]
    T = RET_T
    nb = S // T
    scale = HEAD_DIM ** -0.5
    specs, blk = _ret_specs(T, rev_nb=nb)
    specs = specs + [pl.BlockSpec((1, 1, HEAD_DIM, HEAD_DIM), lambda h, b: (h, blk(b), 0, 0)),
                     pl.BlockSpec((T, HEAD_DIM), lambda h, b: (blk(b), h))]

    def body(q_ref, k_ref, v_ref, g_ref, cos_ref, sin_ref, m_ref, qd_ref, kd_ref, cd_ref, gn_ref, st_ref, do_ref,
             d_ref, dgn_ref, dst):
        @pl.when(pl.program_id(1) == 0)
        def _():
            dst[...] = jnp.zeros_like(dst)
            dgn_ref[...] = jnp.zeros_like(dgn_ref)

        cosf, sinf = cos_ref[...], sin_ref[...]
        qr = _rope(q_ref[...], cosf, sinf)
        kr = _rope(k_ref[...], cosf, sinf) * scale
        v = v_ref[...]
        sp = st_ref[0, 0]
        mask, qd, kd = m_ref[0], qd_ref[0], kd_ref[0]
        w = _dot(qr, kr, NT) * mask
        y = _dot(w, v) + _dot(qr * qd, sp)
        r = lax.rsqrt(jnp.mean(y * y, axis=-1, keepdims=True) + NORM_EPS)
        yh = y * r
        gn_v = gn_ref[...]
        g = g_ref[...]
        sg = _sigmoid(g)
        do = do_ref[...]
        dyn = do * g * sg
        dgn_ref[...] += jnp.sum(dyn * yh, axis=0, keepdims=True)
        dyh = dyn * gn_v
        dy = r * (dyh - yh * jnp.mean(dyh * yh, axis=-1, keepdims=True))
        dg = do * yh * gn_v * sg * (1.0 + g * (1.0 - sg))
        ds = dst[...]
        dp = _dot(dy, v, NT) * mask
        dqr = _dot(dp, kr) + _dot(dy, sp, NT) * qd
        dkr = _dot(dp, qr, TN) + _dot(v, ds, NT) * kd
        dv = _dot(w, dy, TN) + _dot(kr * kd, ds)
        dst[...] = cd_ref[0, 0:1, :] * ds + _dot(qr * qd, dy, TN)
        d_ref[0] = _rope_bwd(dqr, cosf, sinf).astype(d_ref.dtype)
        d_ref[1] = _rope_bwd(dkr * scale, cosf, sinf).astype(d_ref.dtype)
        d_ref[2] = dv.astype(d_ref.dtype)
        d_ref[3] = dg.astype(d_ref.dtype)

    return pl.pallas_call(
        body, name=name, grid=(RET_HEADS, nb), in_specs=specs,
        out_specs=[pl.BlockSpec((4, T, HEAD_DIM), lambda h, b: (0, blk(b), h)),
                   pl.BlockSpec((1, HEAD_DIM), lambda h, b: (0, h))],
        out_shape=[jax.ShapeDtypeStruct((4, S, RET_HEADS * HEAD_DIM), BF16),
                   jax.ShapeDtypeStruct((1, RET_HEADS * HEAD_DIM), F32)],
        scratch_shapes=[pltpu.VMEM((HEAD_DIM, HEAD_DIM), F32)],
        compiler_params=_params(("parallel", "arbitrary")),
    )(proj, proj, proj, proj, tables[0], tables[1], tables[2], tables[3], tables[4], tables[5], gn, states, dmix)


def _sb_logits(qb, kb, i, j, scale):
    T = SB_T
    z = lax.dot_general(qb, kb, NT, preferred_element_type=F32) * scale
    row = lax.broadcasted_iota(jnp.int32, (T, T), 0)
    col = lax.broadcasted_iota(jnp.int32, (T, T), 1)
    vis = jnp.logical_or(j < i, col < row)
    lp = jnp.log1p(jnp.exp(-jnp.abs(z)))
    lb = jnp.minimum(z, 0.0) - lp
    lk = jnp.where(vis, -jnp.maximum(z, 0.0) - lp, 0.0)
    return lb, lk, vis


def _sb_weights(lb, lk, vis, tailc):
    T = SB_T
    row = lax.broadcasted_iota(jnp.int32, (T, T), 0)
    col = lax.broadcasted_iota(jnp.int32, (T, T), 1)
    tail = tailc + _split_dot(row > col, lk, lhs01=False, pieces=2)
    return jnp.where(vis, jnp.exp(lb + tail), 0.0)


def _sb_specs(S):
    T, W = SB_T, SB_HPS * HEAD_DIM
    return [pl.BlockSpec((T, W), lambda hp, i: (i, 2048 // W + hp)),
            pl.BlockSpec((S, W), lambda hp, i: (0, hp)),
            pl.BlockSpec((S, W), lambda hp, i: (0, 512 // W + hp))]


def _lanes(e):
    return slice(e * HEAD_DIM, (e + 1) * HEAD_DIM)


def _sb_fwd(name, proj, kv):
    S = proj.shape[0]
    T, E = SB_T, SB_HPS
    nq = S // T
    scale = HEAD_DIM ** -0.5

    def body(q_ref, k_ref, v_ref, y_ref):
        i = pl.program_id(1)
        qs = [q_ref[:, _lanes(e)].astype(BF16) for e in range(E)]

        def cond(c):
            return jnp.logical_and(c[0] >= 0, c[1] == 0)

        def step(c):
            j, _, tails, accs = c
            rows = pl.ds(pl.multiple_of(j * T, T), T)
            new_tails, new_accs, worst = [], [], None
            for e in range(E):
                lb, lk, vis = _sb_logits(qs[e], k_ref[rows, _lanes(e)], i, j, scale)
                w = _sb_weights(lb, lk, vis, tails[e])
                new_accs.append(accs[e] + lax.dot_general(w.astype(BF16), v_ref[rows, _lanes(e)], NN,
                                                          preferred_element_type=F32))
                t = tails[e] + jnp.sum(lk, axis=1, keepdims=True)
                new_tails.append(t)
                worst = jnp.max(t) if worst is None else jnp.maximum(worst, jnp.max(t))
            return j - 1, (worst < SB_DEAD).astype(jnp.int32), tuple(new_tails), tuple(new_accs)

        init = (i, jnp.int32(0), (jnp.zeros((T, 1), F32),) * E, (jnp.zeros((T, HEAD_DIM), F32),) * E)
        accs = lax.while_loop(cond, step, init)[3]
        for e in range(E):
            y_ref[:, _lanes(e)] = accs[e]

    return pl.pallas_call(
        body, name=name, grid=(SB_HEADS // E, nq), in_specs=_sb_specs(S),
        out_specs=pl.BlockSpec((T, E * HEAD_DIM), lambda hp, i: (i, hp)),
        out_shape=jax.ShapeDtypeStruct((S, SB_HEADS * HEAD_DIM), F32),
        compiler_params=_params(("parallel", "arbitrary")),
    )(proj, kv, kv)


def _sb_bwd(name, proj, kv, dmix):
    S = proj.shape[0]
    T, E = SB_T, SB_HPS
    nq = S // T
    scale = HEAD_DIM ** -0.5

    def body(q_ref, k_ref, v_ref, do_ref, dq_ref, dk_ref, dv_ref):
        i = pl.program_id(1)

        @pl.when(i == 0)
        def _():
            dk_ref[...] = jnp.zeros_like(dk_ref)
            dv_ref[...] = jnp.zeros_like(dv_ref)

        qs = [q_ref[:, _lanes(e)].astype(BF16) for e in range(E)]
        dos = [do_ref[:, _lanes(e)].astype(BF16) for e in range(E)]
        row = lax.broadcasted_iota(jnp.int32, (T, T), 0)
        col = lax.broadcasted_iota(jnp.int32, (T, T), 1)
        zcol = (jnp.zeros((T, 1), F32),) * E

        def cond(c):
            return jnp.logical_and(c[0] >= 0, c[1] == 0)

        def walk_left(c):
            j, _, tails = c
            rows = pl.ds(pl.multiple_of(j * T, T), T)
            new_tails, worst = [], None
            for e in range(E):
                _, lk, _ = _sb_logits(qs[e], k_ref[rows, _lanes(e)], i, j, scale)
                t = tails[e] + jnp.sum(lk, axis=1, keepdims=True)
                new_tails.append(t)
                worst = jnp.max(t) if worst is None else jnp.maximum(worst, jnp.max(t))
            return j - 1, (worst < SB_DEAD).astype(jnp.int32), tuple(new_tails)

        j_end, _, totals = lax.while_loop(cond, walk_left, (i, jnp.int32(0), zcol))

        def walk_back(j, c):
            lefts, used, dqs = c
            rows = pl.ds(pl.multiple_of(j * T, T), T)
            new_lefts, new_used, new_dqs = [], [], []
            for e in range(E):
                kb = k_ref[rows, _lanes(e)]
                lb, lk, vis = _sb_logits(qs[e], kb, i, j, scale)
                u = used[e] + jnp.sum(lk, axis=1, keepdims=True)
                w = _sb_weights(lb, lk, vis, totals[e] - u)
                de = lax.dot_general(dos[e], v_ref[rows, _lanes(e)], NT, preferred_element_type=F32) * w
                dlk = jnp.where(vis, lefts[e] + _split_dot(row < col, de, lhs01=False, pieces=2), 0.0)
                sg = jnp.exp(lb)
                dz = ((de * (1.0 - sg) - dlk * sg) * scale).astype(BF16)
                new_dqs.append(dqs[e] + lax.dot_general(dz, kb, NN, preferred_element_type=F32))
                dk_ref[rows, _lanes(e)] += lax.dot_general(dz, qs[e], TN, preferred_element_type=F32)
                dv_ref[rows, _lanes(e)] += lax.dot_general(w.astype(BF16), dos[e], TN, preferred_element_type=F32)
                new_lefts.append(lefts[e] + jnp.sum(de, axis=1, keepdims=True))
                new_used.append(u)
            return tuple(new_lefts), tuple(new_used), tuple(new_dqs)

        init = (zcol, zcol, (jnp.zeros((T, HEAD_DIM), F32),) * E)
        dqs = lax.fori_loop(j_end + 1, i + 1, walk_back, init)[2]
        for e in range(E):
            dq_ref[:, _lanes(e)] = dqs[e].astype(dq_ref.dtype)

    W = E * HEAD_DIM
    blk = pl.BlockSpec((T, W), lambda hp, i: (i, hp))
    full = pl.BlockSpec((S, W), lambda hp, i: (0, hp))
    shp = jax.ShapeDtypeStruct((S, SB_HEADS * HEAD_DIM), F32)
    return pl.pallas_call(
        body, name=name, grid=(SB_HEADS // E, nq),
        in_specs=_sb_specs(S) + [pl.BlockSpec((T, W), lambda hp, i: (i, 512 // W + hp))],
        out_specs=[blk, full, full],
        out_shape=[jax.ShapeDtypeStruct((S, SB_HEADS * HEAD_DIM), BF16), shp, shp],
        compiler_params=_params(("parallel", "arbitrary")),
    )(proj, kv, kv, dmix)


def _conv_fwd(name, proj, conv_w, conv_b):
    S = proj.shape[0]
    T = CONV_T
    C = 1024
    K = SSM_CONV

    def body(x_ref, w_ref, b_ref, o_ref, buf):
        @pl.when(pl.program_id(0) == 0)
        def _():
            buf[0:8, :] = jnp.zeros((8, C), F32)

        buf[8:T + 8, :] = x_ref[...]
        acc = b_ref[...] + w_ref[K - 1:K, :] * buf[8:T + 8, :]
        for k in range(K - 1):
            acc = acc + w_ref[k:k + 1, :] * buf[5 + k:5 + k + T, :]
        o_ref[...] = acc * _sigmoid(acc)
        buf[0:8, :] = buf[T:T + 8, :]

    return pl.pallas_call(
        body, name=name, grid=(S // T,),
        in_specs=[pl.BlockSpec((T, C), lambda i: (i, 4)), pl.BlockSpec((K, C), lambda i: (0, 0)),
                  pl.BlockSpec((1, C), lambda i: (0, 0))],
        out_specs=pl.BlockSpec((T, C), lambda i: (i, 0)),
        out_shape=jax.ShapeDtypeStruct((S, C), F32),
        scratch_shapes=[pltpu.VMEM((T + 8, C), F32)],
        compiler_params=_params(("arbitrary",)),
    )(proj, conv_w, conv_b)


def _conv_bwd_act(name, proj, conv_w, conv_b, dxbc):
    S = proj.shape[0]
    T = CONV_T
    C = 1024
    K = SSM_CONV

    def body(x_ref, w_ref, b_ref, d_ref, dp_ref, sums_ref, buf):
        @pl.when(pl.program_id(0) == 0)
        def _():
            buf[0:8, :] = jnp.zeros((8, C), F32)
            sums_ref[...] = jnp.zeros_like(sums_ref)

        buf[8:T + 8, :] = x_ref[...]
        acc = b_ref[...] + w_ref[K - 1:K, :] * buf[8:T + 8, :]
        for k in range(K - 1):
            acc = acc + w_ref[k:k + 1, :] * buf[5 + k:5 + k + T, :]
        sg = _sigmoid(acc)
        dp = d_ref[...] * sg * (1.0 + acc * (1.0 - sg))
        dp_ref[...] = dp
        for k in range(K):
            sums_ref[k:k + 1, :] += jnp.sum(dp * buf[5 + k:5 + k + T, :], axis=0, keepdims=True)
        sums_ref[4:5, :] += jnp.sum(dp, axis=0, keepdims=True)
        buf[0:8, :] = buf[T:T + 8, :]

    row = pl.BlockSpec((T, C), lambda i: (i, 0))
    return pl.pallas_call(
        body, name=name, grid=(S // T,),
        in_specs=[pl.BlockSpec((T, C), lambda i: (i, 4)), pl.BlockSpec((K, C), lambda i: (0, 0)),
                  pl.BlockSpec((1, C), lambda i: (0, 0)), row],
        out_specs=[row, pl.BlockSpec((8, C), lambda i: (0, 0))],
        out_shape=[jax.ShapeDtypeStruct((S, C), F32), jax.ShapeDtypeStruct((8, C), F32)],
        scratch_shapes=[pltpu.VMEM((T + 8, C), F32)],
        compiler_params=_params(("arbitrary",)),
    )(proj, conv_w, conv_b, dxbc)


def _conv_bwd_in(name, dp, conv_w):
    S, C = dp.shape
    T = CONV_T
    K = SSM_CONV
    nb = S // T

    def body(d_ref, w_ref, o_ref, buf):
        @pl.when(pl.program_id(0) == 0)
        def _():
            buf[T:T + 8, :] = jnp.zeros((8, C), F32)

        buf[0:T, :] = d_ref[...]
        acc = w_ref[K - 1:K, :] * buf[0:T, :]
        for k in range(K - 1):
            acc = acc + w_ref[k:k + 1, :] * buf[3 - k:3 - k + T, :]
        o_ref[...] = acc.astype(o_ref.dtype)
        buf[T:T + 8, :] = buf[0:8, :]

    row = pl.BlockSpec((T, C), lambda i: (nb - 1 - i, 0))
    return pl.pallas_call(
        body, name=name, grid=(nb,),
        in_specs=[row, pl.BlockSpec((K, C), lambda i: (0, 0))],
        out_specs=row,
        out_shape=jax.ShapeDtypeStruct((S, C), BF16),
        scratch_shapes=[pltpu.VMEM((T + 8, C), F32)],
        compiler_params=_params(("arbitrary",)),
    )(dp, conv_w)


def _softplus(x):
    return jnp.maximum(x, 0.0) + jnp.log1p(jnp.exp(-jnp.abs(x)))


def _ssd_common(xbc_ref, dt_ref, dtb_ref, alog_ref):
    T = SSD_T
    dtr = dt_ref[...] + dtb_ref[...]
    dt = _softplus(dtr)
    a = -jnp.exp(alog_ref[...])
    dta = dt * a
    row = lax.broadcasted_iota(jnp.int32, (T, T), 0)
    col = lax.broadcasted_iota(jnp.int32, (T, T), 1)
    causal = col <= row
    acum = _split_dot(causal, dta)
    acum_t = _split_dot(row <= col, dta, TN, lhs01=False)
    return dtr, dt, a, acum, acum_t, causal, row


def _ssd_head(xbc_ref, h, dt, acum, acum_t, causal, cb):
    P = SSM_P
    ac = acum[:, h:h + 1]
    decay = jnp.exp(jnp.where(causal, ac - acum_t[h:h + 1, :], -1e30))
    dth = dt[:, h:h + 1]
    xs = xbc_ref[:, h * P:(h + 1) * P]
    xdt = xs * dth
    sc = (cb * decay).astype(BF16)
    aend = acum[SSD_T - 1:SSD_T, h:h + 1]
    return ac, dth, xs, xdt, sc, aend


def _ssd_specs(T, blk):
    vec = lambda n: pl.BlockSpec((1, n), lambda b: (0, 0))
    return [pl.BlockSpec((T, 1024), lambda b: (blk(b), 0)),
            pl.BlockSpec((T, LANES), lambda b: (blk(b), DT_COL // LANES)),
            pl.BlockSpec((T, 512), lambda b: (blk(b), 7)),
            vec(LANES), vec(LANES), vec(LANES), vec(512)]


def _ssd_fwd(name, xbc, proj, dt_bias, a_log, d_skip, gain):
    S = xbc.shape[0]
    T = SSD_T
    nb = S // T
    H, P, N = SSM_HEADS, SSM_P, SSM_N

    def body(xbc_ref, dt_ref, z_ref, dtb_ref, alog_ref, dsk_ref, gain_ref, y_ref, st_ref, st, ybuf):
        @pl.when(pl.program_id(0) == 0)
        def _():
            st[...] = jnp.zeros_like(st)

        _, dt, _, acum, acum_t, causal, _ = _ssd_common(xbc_ref, dt_ref, dtb_ref, alog_ref)
        for g in range(SSM_GROUPS):
            bg = xbc_ref[:, 512 + g * N:512 + (g + 1) * N]
            cg = xbc_ref[:, 768 + g * N:768 + (g + 1) * N]
            cb = _dot(cg, bg, NT)
            for hh in range(H // SSM_GROUPS):
                h = g * (H // SSM_GROUPS) + hh
                ac, _, xs, xdt, sc, aend = _ssd_head(xbc_ref, h, dt, acum, acum_t, causal, cb)
                sp = st[h]
                st_ref[0, h] = sp
                y = _dot(sc, xdt) + jnp.exp(ac) * _dot(cg, sp) + xs * dsk_ref[:, h:h + 1]
                st[h] = jnp.exp(aend) * sp + _dot(bg, xdt * jnp.exp(aend - ac), TN)
                ybuf[:, h * P:(h + 1) * P] = y
        z = z_ref[...]
        yg = ybuf[...] * z * _sigmoid(z)
        r = lax.rsqrt(jnp.mean(yg * yg, axis=-1, keepdims=True) + NORM_EPS)
        y_ref[...] = yg * r * gain_ref[...]

    return pl.pallas_call(
        body, name=name, grid=(nb,), in_specs=_ssd_specs(T, lambda b: b),
        out_specs=[pl.BlockSpec((T, 512), lambda b: (b, 0)), pl.BlockSpec((1, H, N, P), lambda b: (b, 0, 0, 0))],
        out_shape=[jax.ShapeDtypeStruct((S, 512), F32), jax.ShapeDtypeStruct((nb, H, N, P), F32)],
        scratch_shapes=[pltpu.VMEM((H, N, P), F32), pltpu.VMEM((T, 512), F32)],
        compiler_params=_params(("arbitrary",)),
    )(xbc, proj, proj, dt_bias, a_log, d_skip, gain)


def _ssd_bwd(name, xbc, proj, dt_bias, a_log, d_skip, gain, states, dmix):
    S = xbc.shape[0]
    T = SSD_T
    nb = S // T
    H, P, N = SSM_HEADS, SSM_P, SSM_N
    blk = lambda b: nb - 1 - b

    def body(xbc_ref, dt_ref, z_ref, dtb_ref, alog_ref, dsk_ref, gain_ref, st_ref, do_ref,
             dx_ref, dz_ref, ddt_ref, sums_ref, dst, ybuf):
        @pl.when(pl.program_id(0) == 0)
        def _():
            dst[...] = jnp.zeros_like(dst)
            sums_ref[...] = jnp.zeros_like(sums_ref)

        dtr, dt, a, acum, acum_t, causal, row = _ssd_common(xbc_ref, dt_ref, dtb_ref, alog_ref)
        cbs = []
        for g in range(SSM_GROUPS):
            bg = xbc_ref[:, 512 + g * N:512 + (g + 1) * N]
            cg = xbc_ref[:, 768 + g * N:768 + (g + 1) * N]
            cb = _dot(cg, bg, NT)
            cbs.append(cb)
            for hh in range(H // SSM_GROUPS):
                h = g * (H // SSM_GROUPS) + hh
                ac, _, xs, xdt, sc, _ = _ssd_head(xbc_ref, h, dt, acum, acum_t, causal, cb)
                ybuf[:, h * P:(h + 1) * P] = (_dot(sc, xdt) + jnp.exp(ac) * _dot(cg, st_ref[0, h])
                                              + xs * dsk_ref[:, h:h + 1])
        z = z_ref[...]
        sg = _sigmoid(z)
        sz = z * sg
        yfull = ybuf[...]
        yg = yfull * sz
        r = lax.rsqrt(jnp.mean(yg * yg, axis=-1, keepdims=True) + NORM_EPS)
        yh = yg * r
        do = do_ref[...]
        sums_ref[0:1, :] += jnp.sum(do * yh, axis=0, keepdims=True)
        dyh = do * gain_ref[...]
        dyg = r * (dyh - yh * jnp.mean(dyh * yh, axis=-1, keepdims=True))
        dz_ref[...] = (dyg * yfull * sg * (1.0 + z * (1.0 - sg))).astype(dz_ref.dtype)
        dyv = dyg * sz

        lane = lax.broadcasted_iota(jnp.int32, (T, LANES), 1)
        rowl = lax.broadcasted_iota(jnp.int32, (T, 1), 0)
        dacum = jnp.zeros((T, LANES), F32)
        ddt = jnp.zeros((T, LANES), F32)
        dskp = jnp.zeros((T, LANES), F32)
        for g in range(SSM_GROUPS):
            bg = xbc_ref[:, 512 + g * N:512 + (g + 1) * N]
            cg = xbc_ref[:, 768 + g * N:768 + (g + 1) * N]
            cb = cbs[g]
            dcb = jnp.zeros((T, T), F32)
            dbg = jnp.zeros((T, N), F32)
            dcg = jnp.zeros((T, N), F32)
            for hh in range(H // SSM_GROUPS):
                h = g * (H // SSM_GROUPS) + hh
                ac, dth, xs, xdt, sc, aend = _ssd_head(xbc_ref, h, dt, acum, acum_t, causal, cb)
                decay = jnp.exp(jnp.where(causal, ac - acum_t[h:h + 1, :], -1e30))
                dy = dyv[:, h * P:(h + 1) * P]
                sp = st_ref[0, h]
                ea = jnp.exp(ac)
                de = jnp.exp(aend - ac)
                dec = jnp.exp(aend)
                dskp = dskp + jnp.where(lane == h, jnp.sum(dy * xs, axis=1, keepdims=True), 0.0)
                dxdt = _dot(sc, dy, TN)
                dsd = _dot(dy, xdt, NT) * decay
                dcb = dcb + dsd
                e = dsd * cb
                e_col = _split_dot(jnp.ones((T, LANES), F32), e, TN, lhs01=False)[:, 0:1]
                dac = jnp.sum(e, axis=1, keepdims=True) - e_col
                dyea = dy * ea
                dac = dac + jnp.sum(dyea * _dot(cg, sp), axis=1, keepdims=True)
                dcg = dcg + _dot(dyea, sp, NT)
                dsp = _dot(cg, dyea, TN)
                dsn = dst[h]
                xde = xdt * de
                dbg = dbg + _dot(xde, dsn, NT)
                wh = _dot(bg, dsn)
                dxdt = dxdt + wh * de
                r_end = jnp.sum(wh * xde, axis=1, keepdims=True)
                dend = jnp.sum(r_end) + jnp.sum(dsn * sp) * dec
                dst[h] = dsp + dec * dsn
                dac = dac - r_end + jnp.where(rowl == T - 1, dend, 0.0)
                dacum = dacum + jnp.where(lane == h, dac, 0.0)
                ddt = ddt + jnp.where(lane == h, jnp.sum(dxdt * xs, axis=1, keepdims=True), 0.0)
                dx_ref[:, h * P:(h + 1) * P] = dxdt * dth + dy * dsk_ref[:, h:h + 1]
            dx_ref[:, 512 + g * N:512 + (g + 1) * N] = dbg + _dot(dcb, cg, TN)
            dx_ref[:, 768 + g * N:768 + (g + 1) * N] = dcg + _dot(dcb, bg)
        ddta = _split_dot(row <= lax.broadcasted_iota(jnp.int32, (T, T), 1), dacum)
        ddt = ddt + ddta * a
        ddtr = ddt * _sigmoid(dtr)
        ddt_ref[...] = ddtr.astype(ddt_ref.dtype)
        sums_ref[1:2, 0:LANES] += jnp.sum(ddtr, axis=0, keepdims=True)
        sums_ref[1:2, LANES:2 * LANES] += jnp.sum(ddta * dt, axis=0, keepdims=True) * a
        sums_ref[1:2, 2 * LANES:3 * LANES] += jnp.sum(dskp, axis=0, keepdims=True)

    specs = _ssd_specs(T, blk) + [pl.BlockSpec((1, H, N, P), lambda b: (blk(b), 0, 0, 0)),
                                  pl.BlockSpec((T, 512), lambda b: (blk(b), 2))]
    return pl.pallas_call(
        body, name=name, grid=(nb,), in_specs=specs,
        out_specs=[pl.BlockSpec((T, 1024), lambda b: (blk(b), 0)), pl.BlockSpec((T, 512), lambda b: (blk(b), 0)),
                   pl.BlockSpec((T, LANES), lambda b: (blk(b), 0)), pl.BlockSpec((8, 512), lambda b: (0, 0))],
        out_shape=[jax.ShapeDtypeStruct((S, 1024), F32), jax.ShapeDtypeStruct((S, 512), BF16),
                   jax.ShapeDtypeStruct((S, LANES), BF16), jax.ShapeDtypeStruct((8, 512), F32)],
        scratch_shapes=[pltpu.VMEM((H, N, P), F32), pltpu.VMEM((T, 512), F32)],
        compiler_params=_params(("arbitrary",)),
    )(xbc, proj, proj, dt_bias, a_log, d_skip, gain, states, dmix)


def _place():
    return lax.axis_index("x"), lax.axis_index("y"), lax.axis_index("c")


def _flip(v, bit):
    return 1 - v if bit else v


def _gather_small(name, v):
    R, C = v.shape

    def body(v_ref, out_ref, send_sems, recv_sems, local_sem):
        x, y, c = _place()
        me = 4 * x + 2 * y + c
        mine = pltpu.make_async_copy(v_ref, out_ref.at[me], local_sem)
        mine.start()
        peers = [(_flip(x, (k >> 2) & 1), _flip(y, (k >> 1) & 1), _flip(c, k & 1)) for k in range(1, 8)]
        sends = []
        for k, peer in enumerate(peers):
            cp = pltpu.make_async_remote_copy(src_ref=v_ref, dst_ref=out_ref.at[me], send_sem=send_sems.at[k],
                                              recv_sem=recv_sems.at[k], device_id=peer, device_id_type=MESH)
            cp.start()
            sends.append(cp)
        for k, (px, py, pc) in enumerate(peers):
            pltpu.make_async_remote_copy(src_ref=v_ref, dst_ref=out_ref.at[4 * px + 2 * py + pc],
                                         send_sem=send_sems.at[k], recv_sem=recv_sems.at[k],
                                         device_id=(px, py, pc), device_id_type=MESH).wait_recv()
        for cp in sends:
            cp.wait_send()
        mine.wait()

    return pl.pallas_call(
        body, name=name, out_shape=jax.ShapeDtypeStruct((8, R, C), v.dtype),
        in_specs=[pl.BlockSpec(memory_space=pltpu.VMEM)], out_specs=pl.BlockSpec(memory_space=pltpu.VMEM),
        scratch_shapes=[pltpu.SemaphoreType.DMA((7,)), pltpu.SemaphoreType.DMA((7,)), pltpu.SemaphoreType.DMA(())],
    )(v)


def _hbm_call(body, name, arrays, out_shapes, n_sems):
    spec = pl.BlockSpec(memory_space=pl.ANY)
    return pl.pallas_call(
        body, name=name, out_shape=out_shapes, in_specs=[spec] * len(arrays), out_specs=[spec] * len(out_shapes),
        scratch_shapes=[pltpu.SemaphoreType.DMA((n_sems,)), pltpu.SemaphoreType.DMA((n_sems,))],
    )(*arrays)


def _chip_exchange(name, arrays, gather):
    n = len(arrays)

    def body(*refs):
        ins, outs, send_sems, recv_sems = refs[:n], refs[n:2 * n], refs[2 * n], refs[2 * n + 1]
        x, y, c = _place()
        chip = 2 * x + y
        peers = [(_flip(x, (k >> 1) & 1), _flip(y, k & 1)) for k in range(1, 4)]

        def copy(a, k, slot):
            px, py = peers[k]
            return pltpu.make_async_remote_copy(
                src_ref=ins[a].at[c] if gather else ins[a].at[2 * px + py], dst_ref=outs[a].at[slot],
                send_sem=send_sems.at[3 * a + k], recv_sem=recv_sems.at[3 * a + k],
                device_id=(px, py, c), device_id_type=MESH)

        sends = [copy(a, k, chip) for a in range(n) for k in range(3)]
        for cp in sends:
            cp.start()
        for a in range(n):
            for k, (px, py) in enumerate(peers):
                copy(a, k, 2 * px + py).wait_recv()
        for cp in sends:
            cp.wait_send()

    shapes = [jax.ShapeDtypeStruct((4,) + a.shape[1:], a.dtype) for a in arrays]
    return _hbm_call(body, name, arrays, shapes, 3 * n)


def _sibling_swap(name, arrays, alt=None):
    n = len(arrays)

    def body(*refs):
        k = 1 if alt is None else 2
        outs, send_sems, recv_sems = refs[k * n:(k + 1) * n], refs[(k + 1) * n], refs[(k + 1) * n + 1]
        x, y, c = _place()

        def exchange(srcs):
            cps = [pltpu.make_async_remote_copy(src_ref=srcs[a], dst_ref=outs[a], send_sem=send_sems.at[a],
                                                recv_sem=recv_sems.at[a], device_id=(x, y, 1 - c),
                                                device_id_type=MESH) for a in range(n)]
            for cp in cps:
                cp.start()
            for cp in cps:
                cp.wait()

        if alt is None:
            exchange(refs[:n])
        else:
            @pl.when(c == 1)
            def _():
                exchange(refs[:n])

            @pl.when(c == 0)
            def _():
                exchange(refs[n:2 * n])

    shapes = [jax.ShapeDtypeStruct(a.shape, a.dtype) for a in arrays]
    return _hbm_call(body, name, list(arrays) + ([] if alt is None else list(alt)), shapes, n)


PACK_C = 1024
SUM_STEPS = 4


def _add_layers(name, own0, own1, other, c):
    n = len(other)

    def body(c_ref, *refs):
        for a in range(n):
            mine = jnp.where(c_ref[0] == 0, refs[a][...].astype(F32), refs[n + a][...].astype(F32))
            refs[3 * n + a][...] = (mine + refs[2 * n + a][...].astype(F32)).astype(BF16)

    specs = [pl.BlockSpec((1, o.shape[1] // SUM_STEPS, o.shape[2]), lambda p, i, c_ref: (p, i, 0)) for o in other]
    grid_spec = pltpu.PrefetchScalarGridSpec(num_scalar_prefetch=1, grid=(4, SUM_STEPS), in_specs=specs * 3,
                                             out_specs=specs)
    return pl.pallas_call(
        body, name=name, grid_spec=grid_spec, out_shape=[jax.ShapeDtypeStruct(o.shape, BF16) for o in other],
        compiler_params=_params(("parallel", "parallel")),
    )(jnp.reshape(c, (1,)).astype(jnp.int32), *own0, *own1, *other)


def _sum_chips(name, received, own, chip):
    n = len(received)

    def body(chip_ref, *refs):
        for a in range(n):
            s = None
            for q in range(4):
                term = jnp.where(chip_ref[0] == q, refs[n + a][0], refs[a][q]).astype(F32)
                s = term if s is None else s + term
            refs[2 * n + a][...] = s

    rec = [pl.BlockSpec((4, r.shape[1] // SUM_STEPS, r.shape[2]), lambda i, chip_ref: (0, i, 0)) for r in received]
    mine = [pl.BlockSpec((1, r.shape[1] // SUM_STEPS, r.shape[2]), lambda i, chip_ref: (chip_ref[0], i, 0))
            for r in received]
    outs = [pl.BlockSpec((r.shape[1] // SUM_STEPS, r.shape[2]), lambda i, chip_ref: (i, 0)) for r in received]
    grid_spec = pltpu.PrefetchScalarGridSpec(num_scalar_prefetch=1, grid=(SUM_STEPS,), in_specs=rec + mine,
                                             out_specs=outs)
    return pl.pallas_call(
        body, name=name, grid_spec=grid_spec,
        out_shape=[jax.ShapeDtypeStruct(r.shape[1:], F32) for r in received],
        compiler_params=_params(("parallel",)),
    )(jnp.reshape(chip, (1,)).astype(jnp.int32), *received, *own)


def _sum_devices(name, parts):
    _, R, C = parts.shape

    def body(p_ref, s_ref):
        s = p_ref[0]
        for q in range(1, 8):
            s = s + p_ref[q]
        s_ref[...] = s

    return pl.pallas_call(body, name=name, out_shape=jax.ShapeDtypeStruct((R, C), F32))(parts)


def _adamw(name, w, g, m, v):
    R, C = w.shape
    tr = 128 if R % 128 == 0 else R
    c1 = 1.0 - ADAM_B1 ** ADAM_STEP
    c2 = 1.0 - ADAM_B2 ** ADAM_STEP

    def body(w_ref, g_ref, m_ref, v_ref, d_ref, nm_ref, nv_ref):
        gv = g_ref[...]
        nm = ADAM_B1 * m_ref[...] + (1.0 - ADAM_B1) * gv
        nv = ADAM_B2 * v_ref[...] + (1.0 - ADAM_B2) * (gv * gv)
        nm_ref[...] = nm
        nv_ref[...] = nv
        d_ref[...] = -ADAM_LR * ((nm / c1) / (jnp.sqrt(nv / c2) + ADAM_EPS) + ADAM_WD * w_ref[...])

    spec = pl.BlockSpec((tr, C), lambda i: (i, 0))
    shp = jax.ShapeDtypeStruct((R, C), F32)
    return pl.pallas_call(
        body, name=name, grid=(R // tr,), in_specs=[spec] * 4, out_specs=[spec] * 3, out_shape=[shp] * 3,
        compiler_params=_params(("parallel",)),
    )(w, g, m, v)


def _adamw_layers(name, w, g_mine, g_other, m, v, c):
    L, R, C = w.shape
    tr = 128 if R % 128 == 0 else R
    c1 = 1.0 - ADAM_B1 ** ADAM_STEP
    c2 = 1.0 - ADAM_B2 ** ADAM_STEP

    def body(c_ref, w_ref, gm_ref, go_ref, m_ref, v_ref, g_ref, d_ref, nm_ref, nv_ref):
        gv = jnp.where(pl.program_id(0) == c_ref[0], gm_ref[...], go_ref[...])
        nm = ADAM_B1 * m_ref[...] + (1.0 - ADAM_B1) * gv
        nv = ADAM_B2 * v_ref[...] + (1.0 - ADAM_B2) * (gv * gv)
        g_ref[...] = gv
        nm_ref[...] = nm
        nv_ref[...] = nv
        d_ref[...] = -ADAM_LR * ((nm / c1) / (jnp.sqrt(nv / c2) + ADAM_EPS) + ADAM_WD * w_ref[...])

    full = pl.BlockSpec((None, tr, C), lambda l, i, c_ref: (l, i, 0))
    part = pl.BlockSpec((tr, C), lambda l, i, c_ref: (i, 0))
    grid_spec = pltpu.PrefetchScalarGridSpec(num_scalar_prefetch=1, grid=(L, R // tr),
                                             in_specs=[full, part, part, full, full], out_specs=[full] * 4)
    return pl.pallas_call(
        body, name=name, grid_spec=grid_spec, out_shape=[jax.ShapeDtypeStruct((L, R, C), F32)] * 4,
        compiler_params=_params(("parallel", "parallel")),
    )(jnp.reshape(c, (1,)).astype(jnp.int32), w, g_mine, g_other, m, v)


BIG = ("ffn1_wg", "ffn1_wu", "ffn1_wd", "w_in", "w_out", "ffn2_wg", "ffn2_wu", "ffn2_wd")
WEIGHTS =("ada_w", "ada_b", "norm_ffn1", "ffn1_wg", "ffn1_wu", "ffn1_wd", "norm_mix", "w_in", "conv_w", "conv_b",
           "dt_bias", "a_log", "d_skip", "ret_gn", "ssm_norm", "w_out", "norm_ffn2", "ffn2_wg", "ffn2_wu", "ffn2_wd",
           "final_ada_w", "final_ada_b", "final_norm")
SMALL = ("ada_b", "norm_ffn1", "norm_mix", "conv_w", "conv_b", "dt_bias", "a_log", "d_skip", "ret_gn", "ssm_norm",
         "norm_ffn2", "final_ada_b", "final_norm")


def _unpack(slab, shapes):
    flat = slab.reshape(-1)
    out, off = [], 0
    for shp in shapes:
        n = math.prod(shp)
        out.append(flat[off:off + n].reshape(shp))
        off += n
    return out


def _pad_lanes(v):
    return jnp.pad(v, (0, LANES - v.shape[0])).reshape(1, LANES)


def _ffn_fwd(tag, h, gain, mod3, wg, wu, wd):
    u = _norm_mod(tag + "_norm", h, gain, mod3[0:1], mod3[1:2])
    a, b, hm = _ffn_up(tag + "_up", u, wg, wu)
    gatefac = 0.5 * (1.0 + mod3[2:3])
    out, h_new = _rowmm(tag + "_down", [(hm, p, wd, p) for p in range(4)], "nn", res=(h, gatefac))
    return h_new, (h, u, a, b, hm, out, gatefac)


def _ffn_bwd(tag, g, saved, gain, mod3, wg, wu, wd):
    h, u, a, b, hm, out, gatefac = saved
    d_out, s_gate = _gate_bwd(tag + "_gate_bwd", g, out, gatefac)
    da, db = _ffn_bwd_mid(tag + "_mid_bwd", d_out, wd, a, b)
    dwd = _mm_tn_groups(tag + "_dwd", hm, d_out, BF16)
    dwg = _mm_tn_groups(tag + "_dwg", u, da, BF16)
    dwu = _mm_tn_groups(tag + "_dwu", u, db, BF16)
    du = _rowmm(tag + "_du", [(da, p, wg, p) for p in range(4)] + [(db, p, wu, p) for p in range(4)], "nt")
    g_in, s_norm = _norm_mod_bwd(tag + "_norm_bwd", du, h, gain, mod3[1:2], g)
    dmod3 = jnp.concatenate([s_norm[0:1], s_norm[1:2], 0.5 * s_gate[0:1]], axis=0)
    return g_in, (dwg, dwu, dwd), dmod3, s_norm[2]


def _mixer_fwd(tag, h, P, mod3, w_in, w_out, tables):
    u = _norm_mod(tag + "_norm", h, P["norm_mix"], mod3[0:1], mod3[1:2])
    proj = _mm(tag + "_in_proj", [(u, w_in)], "nn", F32, tm=512, tn=768, tk=1024)
    y_ret, ret_st = _ret_fwd(tag + "_ret", proj, P["ret_gn"], tables)
    kv = proj[:, 2560:3584].astype(BF16)
    y_sb = _sb_fwd(tag + "_sb", proj, kv)
    xbc = _conv_fwd(tag + "_conv", proj, P["conv_w"], P["conv_b"])
    y_ssm, ssm_st = _ssd_fwd(tag + "_ssd", xbc, proj, P["dt_bias"], P["a_log"], P["d_skip"], P["ssm_norm"])
    gatefac = 1.0 + mod3[2:3]
    ys = (y_ret, y_sb, y_ssm)
    w_out3 = w_out.reshape(3, 512, D_MODEL)
    mixed, h_new = _rowmm(tag + "_out_proj", [(y, None, w_out3, i) for i, y in enumerate(ys)], "nn", res=(h, gatefac))
    return h_new, (h, u, proj, kv, ys, ret_st, xbc, ssm_st, mixed, gatefac)


def _mixer_bwd(tag, g, saved, P, mod3, w_in, w_out, tables):
    h, u, proj, kv, ys, ret_st, xbc, ssm_st, mixed, gatefac = saved
    S = h.shape[0]
    d_mixed, s_gate = _gate_bwd(tag + "_gate_bwd", g, mixed, gatefac)
    dmix = _mm(tag + "_dmix", [(d_mixed, w_out)], "nt", F32, tm=512, tn=512, tk=1024)
    dw_out = jnp.concatenate(
        [_mm(tag + f"_dw_out{i}", [(y, d_mixed)], "tn", BF16, tm=512, tn=1024, tk=512) for i, y in enumerate(ys)], axis=0)
    d_ret, d_gn = _ret_bwd(tag + "_ret_bwd", proj, P["ret_gn"], tables, ret_st, dmix)
    dq, dk, dv = _sb_bwd(tag + "_sb_bwd", proj, kv, dmix)
    dxbc, dz, ddt, s_ssd = _ssd_bwd(tag + "_ssd_bwd", xbc, proj, P["dt_bias"], P["a_log"], P["d_skip"], P["ssm_norm"],
                                    ssm_st, dmix)
    dp, s_conv = _conv_bwd_act(tag + "_conv_bwd_act", proj, P["conv_w"], P["conv_b"], dxbc)
    dxr = _conv_bwd_in(tag + "_conv_bwd_in", dp, P["conv_w"])
    dproj = jnp.concatenate(
        [d_ret[0], d_ret[1], d_ret[2], d_ret[3], dq, dk.astype(BF16), dv.astype(BF16), dz, dxr, ddt,
         jnp.zeros((S, IN_WP - DT_COL - LANES), BF16)], axis=1)
    du = _mm(tag + "_du", [(dproj, w_in)], "nt", F32, tm=512, tn=1024, tk=768)
    dw_in = _mm(tag + "_dw_in", [(u, dproj)], "tn", BF16, tm=1024, tn=768, tk=512)
    g_in, s_norm = _norm_mod_bwd(tag + "_norm_bwd", du, h, P["norm_mix"], mod3[1:2], g)
    dmod3 = jnp.concatenate([s_norm[0:1], s_norm[1:2], s_gate[0:1]], axis=0)
    small = dict(norm_mix=s_norm[2], conv_w=s_conv[0:4], conv_b=s_conv[4], dt_bias=s_ssd[1, 0:8],
                 a_log=s_ssd[1, LANES:LANES + 8], d_skip=s_ssd[1, 2 * LANES:2 * LANES + 8],
                 ret_gn=d_gn[0], ssm_norm=s_ssd[0])
    return g_in, dw_in, dw_out, dmod3, small


def kernel(x, c, ada_w, ada_b, norm_ffn1, ffn1_wg, ffn1_wu, ffn1_wd, norm_mix, w_in, conv_w, conv_b, dt_bias, a_log, d_skip, ret_gn, ssm_norm, w_out, norm_ffn2, ffn2_wg, ffn2_wu, ffn2_wd, final_ada_w, final_ada_b, final_norm, loss_target, m_ada_w, m_ada_b, m_norm_ffn1, m_ffn1_wg, m_ffn1_wu, m_ffn1_wd, m_norm_mix, m_w_in, m_conv_w, m_conv_b, m_dt_bias, m_a_log, m_d_skip, m_ret_gn, m_ssm_norm, m_w_out, m_norm_ffn2, m_ffn2_wg, m_ffn2_wu, m_ffn2_wd, m_final_ada_w, m_final_ada_b, m_final_norm, v_ada_w, v_ada_b, v_norm_ffn1, v_ffn1_wg, v_ffn1_wu, v_ffn1_wd, v_norm_mix, v_w_in, v_conv_w, v_conv_b, v_dt_bias, v_a_log, v_d_skip, v_ret_gn, v_ssm_norm, v_w_out, v_norm_ffn2, v_ffn2_wg, v_ffn2_wu, v_ffn2_wd, v_final_ada_w, v_final_ada_b, v_final_norm):
    W = dict(ada_w=ada_w, ada_b=ada_b, norm_ffn1=norm_ffn1, ffn1_wg=ffn1_wg, ffn1_wu=ffn1_wu, ffn1_wd=ffn1_wd,
             norm_mix=norm_mix, w_in=w_in, conv_w=conv_w, conv_b=conv_b, dt_bias=dt_bias, a_log=a_log, d_skip=d_skip,
             ret_gn=ret_gn, ssm_norm=ssm_norm, w_out=w_out, norm_ffn2=norm_ffn2, ffn2_wg=ffn2_wg, ffn2_wu=ffn2_wu,
             ffn2_wd=ffn2_wd, final_ada_w=final_ada_w, final_ada_b=final_ada_b, final_norm=final_norm)
    M = dict(ada_w=m_ada_w, ada_b=m_ada_b, norm_ffn1=m_norm_ffn1, ffn1_wg=m_ffn1_wg, ffn1_wu=m_ffn1_wu,
             ffn1_wd=m_ffn1_wd, norm_mix=m_norm_mix, w_in=m_w_in, conv_w=m_conv_w, conv_b=m_conv_b, dt_bias=m_dt_bias,
             a_log=m_a_log, d_skip=m_d_skip, ret_gn=m_ret_gn, ssm_norm=m_ssm_norm, w_out=m_w_out,
             norm_ffn2=m_norm_ffn2, ffn2_wg=m_ffn2_wg, ffn2_wu=m_ffn2_wu, ffn2_wd=m_ffn2_wd,
             final_ada_w=m_final_ada_w, final_ada_b=m_final_ada_b, final_norm=m_final_norm)
    V = dict(ada_w=v_ada_w, ada_b=v_ada_b, norm_ffn1=v_norm_ffn1, ffn1_wg=v_ffn1_wg, ffn1_wu=v_ffn1_wu,
             ffn1_wd=v_ffn1_wd, norm_mix=v_norm_mix, w_in=v_w_in, conv_w=v_conv_w, conv_b=v_conv_b, dt_bias=v_dt_bias,
             a_log=v_a_log, d_skip=v_d_skip, ret_gn=v_ret_gn, ssm_norm=v_ssm_norm, w_out=v_w_out,
             norm_ffn2=v_norm_ffn2, ffn2_wg=v_ffn2_wg, ffn2_wu=v_ffn2_wu, ffn2_wd=v_ffn2_wd,
             final_ada_w=v_final_ada_w, final_ada_b=v_final_ada_b, final_norm=v_final_norm)
    D = D_MODEL
    S = x.shape[1]
    ax, ay, ac = _place()
    me = 4 * ax + 2 * ay + ac
    chip = 2 * ax + ay
    h0 = x[0]

    c_all = _gather_small("gather_c", jnp.pad(c, ((0, 7), (0, 0))))[:, 0, :]
    cond_all = c_all * jax.nn.sigmoid(c_all)
    nmod = 3 * 3 * D // 4
    mod_part = jnp.concatenate(
        [_mm(f"mod_proj{l}", [(cond_all, ada_w[l])], "nn", F32, tm=8, tn=768, tk=D) for l in range(DEPTH)]
        + [_mm("mod_proj_final", [(cond_all, final_ada_w)], "nn", F32, tm=8, tn=512, tk=D),
           conv_w.reshape(DEPTH * SSM_CONV, -1)], axis=1)
    gathered = _gather_small("gather_mod", mod_part)[0::2]
    mine = lax.dynamic_index_in_dim(gathered, me, axis=1, keepdims=False)
    mods = [(jnp.reshape(mine[:, l * nmod:(l + 1) * nmod], (-1,)) + ada_b[l]).reshape(9, D) for l in range(DEPTH)]
    fmod = (jnp.reshape(mine[:, DEPTH * nmod:DEPTH * nmod + 2 * D // 4], (-1,)) + final_ada_b).reshape(2, D)
    conv_full = jnp.transpose(gathered[:, :, DEPTH * nmod + 2 * D // 4:], (1, 0, 2)).reshape(DEPTH, SSM_CONV, -1)

    shards = [W[n].astype(BF16) for n in BIG]
    fetched = _chip_exchange("gather_weights", shards, gather=True)
    swapped = _sibling_swap("gather_weights_sibling", fetched)
    full = []
    for l in range(DEPTH):
        fw = {}
        for i, n in enumerate(BIG):
            arr = jnp.where(ac == l, fetched[i], swapped[i])
            fw[n] = lax.dynamic_update_index_in_dim(arr, shards[i][l], chip, 0)
        fw["w_in"] = jnp.pad(jnp.transpose(fw["w_in"], (1, 0, 2)).reshape(D, IN_W), ((0, 0), (0, IN_WP - IN_W)))
        fw["w_out"] = fw["w_out"].reshape(MIX_W, D)
        full.append(fw)

    tables = _ret_tables(S)
    small_p = []
    for l in range(DEPTH):
        small_p.append(dict(
            norm_ffn1=norm_ffn1[l:l + 1], norm_mix=norm_mix[l:l + 1], norm_ffn2=norm_ffn2[l:l + 1],
            ret_gn=ret_gn[l:l + 1], ssm_norm=ssm_norm[l:l + 1], conv_w=conv_full[l], conv_b=conv_b[l:l + 1],
            dt_bias=_pad_lanes(dt_bias[l]), a_log=_pad_lanes(a_log[l]), d_skip=_pad_lanes(d_skip[l])))

    h = h0
    saved = []
    for l in range(DEPTH):
        P, fw, mod = small_p[l], full[l], mods[l]
        h, s1 = _ffn_fwd(f"l{l}_ffn1", h, P["norm_ffn1"], mod[0:3], fw["ffn1_wg"], fw["ffn1_wu"], fw["ffn1_wd"])
        h, sm = _mixer_fwd(f"l{l}_mix", h, P, mod[3:6], fw["w_in"], fw["w_out"], tables)
        h, s2 = _ffn_fwd(f"l{l}_ffn2", h, P["norm_ffn2"], mod[6:9], fw["ffn2_wg"], fw["ffn2_wu"], fw["ffn2_wd"])
        saved.append((s1, sm, s2))
    fgain = final_norm.reshape(1, D)
    dy, loss_rows = _final_loss("final_loss", h, loss_target[0], fgain, fmod[0:1], fmod[1:2])

    g, s_final = _norm_mod_bwd("final_norm_bwd", dy, h, fgain, fmod[1:2], jnp.zeros_like(dy))
    dfmod = s_final[0:2]
    small_g = {n: [None] * DEPTH for n in SMALL}
    big_g = [None] * DEPTH
    dmods = [None] * DEPTH
    for l in reversed(range(DEPTH)):
        P, fw, mod = small_p[l], full[l], mods[l]
        s1, sm, s2 = saved[l]
        g, (dwg2, dwu2, dwd2), dm2, dn2 = _ffn_bwd(f"l{l}_ffn2", g, s2, P["norm_ffn2"], mod[6:9],
                                                   fw["ffn2_wg"], fw["ffn2_wu"], fw["ffn2_wd"])
        g, dw_in, dw_out, dmm, sg = _mixer_bwd(f"l{l}_mix", g, sm, P, mod[3:6], fw["w_in"], fw["w_out"], tables)
        g, (dwg1, dwu1, dwd1), dm1, dn1 = _ffn_bwd(f"l{l}_ffn1", g, s1, P["norm_ffn1"], mod[0:3],
                                                   fw["ffn1_wg"], fw["ffn1_wu"], fw["ffn1_wd"])
        dmods[l] = jnp.concatenate([dm1, dmm, dm2], axis=0)
        big_g[l] = dict(ffn1_wg=dwg1, ffn1_wu=dwu1, ffn1_wd=dwd1,
                        w_in=jnp.transpose(dw_in[:, :IN_W].reshape(D, 4, IN_W // 4), (1, 0, 2)),
                        w_out=dw_out.reshape(4, MIX_W // 4, D), ffn2_wg=dwg2, ffn2_wu=dwu2, ffn2_wd=dwd2)
        sg.update(norm_ffn1=dn1, norm_ffn2=dn2)
        for n, val in sg.items():
            small_g[n][l] = val
    grad_x = g[None]

    n_mod = DEPTH * 9 * D + 2 * D
    vec = [jnp.stack(dmods).reshape(-1), dfmod.reshape(-1)]
    layered = [n for n in SMALL if n not in ("ada_b", "final_ada_b", "final_norm")]
    vec += [jnp.stack(small_g[n]).reshape(-1) for n in layered]
    vec += [s_final[2], jnp.sum(loss_rows[0]).reshape(1)]
    flat = jnp.concatenate(vec)
    vrows = -(-flat.shape[0] // (8 * PACK_C)) * 8
    slab = jnp.pad(flat, (0, vrows * PACK_C - flat.shape[0])).reshape(vrows, PACK_C)
    slabs = _gather_small("gather_small_grads", slab)
    total = _sum_devices("sum_small_grads", slabs).reshape(-1)
    grads = {}
    grads["ada_b"] = total[:DEPTH * 9 * D].reshape(DEPTH, 9 * D)
    grads["final_ada_b"] = total[DEPTH * 9 * D:n_mod]
    off = n_mod
    for n in layered:
        shp = (DEPTH,) + ((SSM_CONV, D) if n == "conv_w" else W[n].shape[1:])
        cnt = math.prod(shp)
        grads[n] = total[off:off + cnt].reshape(shp)
        off += cnt
    grads["final_norm"] = total[off:off + D]
    loss = total[off + D]
    grads["conv_w"] = lax.dynamic_slice_in_dim(grads["conv_w"], chip * (D // 4), D // 4, axis=2)

    dmod_all = slabs[:, :n_mod // PACK_C, :].reshape(8, n_mod)
    grads["ada_w"] = jnp.stack([
        _mm(f"grad_ada_w{l}", [(cond_all, lax.dynamic_slice_in_dim(dmod_all, l * 9 * D + chip * nmod, nmod, axis=1))],
            "tn", F32, tm=D, tn=768, tk=8) for l in range(DEPTH)])
    grads["final_ada_w"] = _mm(
        "grad_final_ada_w",
        [(cond_all, lax.dynamic_slice_in_dim(dmod_all, DEPTH * 9 * D + chip * (2 * D // 4), 2 * D // 4, axis=1))],
        "tn", F32, tm=D, tn=512, tk=8)

    g0, g1 = [big_g[0][n] for n in BIG], [big_g[1][n] for n in BIG]
    from_sibling = _sibling_swap("reduce_grads_sibling", g0, alt=g1)
    chip_sum = _add_layers("reduce_grads_add", g0, g1, from_sibling, ac)
    from_chips = _chip_exchange("reduce_grads_chips", chip_sum, gather=False)
    reduced = _sum_chips("reduce_grads_sum", from_chips, chip_sum, chip)
    reduced_other = _sibling_swap("reduce_grads_complete", reduced)

    delta, new_m, new_v = {}, {}, {}
    for i, n in enumerate(BIG):
        grads[n], delta[n], new_m[n], new_v[n] = _adamw_layers("adamw_" + n, W[n], reduced[i], reduced_other[i],
                                                               M[n], V[n], ac)
    for n in ("ada_w", "final_ada_w"):
        shp = W[n].shape
        two_d = lambda a: a.reshape(-1, shp[-1])
        d_, m_, v_ = _adamw("adamw_" + n, two_d(W[n]), two_d(grads[n]), two_d(M[n]), two_d(V[n]))
        delta[n], new_m[n], new_v[n] = d_.reshape(shp), m_.reshape(shp), v_.reshape(shp)
    small_shapes = [W[n].shape for n in SMALL]
    n_small = sum(math.prod(s) for s in small_shapes)
    srows = -(-n_small // (8 * LANES)) * 8
    slab_of = lambda T_: jnp.pad(jnp.concatenate([T_[n].reshape(-1) for n in SMALL]),
                                 (0, srows * LANES - n_small)).reshape(srows, LANES)
    outs = _adamw("adamw_small", slab_of(W), slab_of(grads), slab_of(M), slab_of(V))
    for res, o in zip((delta, new_m, new_v), outs):
        for n, val in zip(SMALL, _unpack(o, small_shapes)):
            res[n] = val

    return (loss, grad_x, *[grads[n] for n in WEIGHTS], *[delta[n] for n in WEIGHTS],
            *[new_m[n] for n in WEIGHTS], *[new_v[n] for n in WEIGHTS])
```

```python
import functools
import math

import jax
import jax.numpy as jnp
from jax import lax
from jax.experimental import pallas as pl
from jax.experimental.pallas import tpu as pltpu

F32 = jnp.float32
BF16 = jnp.bfloat16
MESH = pl.DeviceIdType.MESH

D_MODEL = 1024
DEPTH = 2
D_FF = 2816
RET_HEADS = 4
SB_HEADS = 4
HEAD_DIM = 128
SSM_HEADS = 8
SSM_P = 64
SSM_N = 128
SSM_GROUPS = 2
SSM_CONV = 4
MIX_W = 1536
IN_W = 5128
IN_WP = 5376
DT_COL = 5120
ROPE_BASE = 10000.0
NORM_EPS = 1e-6
ADAM_LR = 0.001
ADAM_B1 = 0.9
ADAM_B2 = 0.999
ADAM_EPS = 1e-08
ADAM_WD = 0.01
ADAM_STEP = 10

LANES = 128
VMEM_LIMIT = 56 * 1024 * 1024
RET_T = 256
SB_T = 256
SB_HPS = 2
SSD_T = 128
CONV_T = 512
ROW_T = 512
SB_DEAD = -150.0

NN = (((1,), (0,)), ((), ()))
NT = (((1,), (1,)), ((), ()))
TN = (((0,), (0,)), ((), ()))


def _dot(a, b, dims=NN):
    return lax.dot_general(a.astype(BF16), b.astype(BF16), dims, preferred_element_type=F32)


def _params(sem):
    return pltpu.CompilerParams(dimension_semantics=sem, vmem_limit_bytes=VMEM_LIMIT)


def _sigmoid(x):
    return 1.0 / (1.0 + jnp.exp(-x))


def _split_dot(mat01, x, dims=NN, lhs01=True, pieces=3):
    m = mat01.astype(BF16)
    total, rest = None, x
    for _ in range(pieces):
        p = rest.astype(BF16)
        rest = rest - p.astype(F32)
        d = lax.dot_general(m, p, dims, preferred_element_type=F32) if lhs01 else lax.dot_general(
            p, m, dims, preferred_element_type=F32)
        total = d if total is None else total + d
    return total


def _mm(name, terms, mode, out_dtype, tm=512, tn=512, tk=1024):
    a0, b0 = terms[0]
    if mode == "nn":
        (M, K), N = a0.shape, b0.shape[1]
    elif mode == "nt":
        (M, K), N = a0.shape, b0.shape[0]
    else:
        (K, M), N = a0.shape, b0.shape[1]
    tm, tn, tk = min(tm, M), min(tn, N), min(tk, K)
    assert M % tm == 0 and N % tn == 0 and K % tk == 0, (name, M, N, K, tm, tn, tk)
    nk = K // tk
    nterm = len(terms)
    dims = {"nn": NN, "nt": NT, "tn": TN}[mode]

    def body(*refs):
        o_ref, acc = refs[2 * nterm], refs[2 * nterm + 1]
        part = None
        for t in range(nterm):
            p = _dot(refs[2 * t][...], refs[2 * t + 1][...], dims)
            part = p if part is None else part + p
        if nk == 1:
            o_ref[...] = part.astype(o_ref.dtype)
        else:
            k = pl.program_id(2)

            @pl.when(k == 0)
            def _():
                acc[...] = part

            @pl.when(k > 0)
            def _():
                acc[...] += part

            @pl.when(k == nk - 1)
            def _():
                o_ref[...] = acc[...].astype(o_ref.dtype)

    if mode == "nn":
        a_spec = pl.BlockSpec((tm, tk), lambda i, j, k: (i, k))
        b_spec = pl.BlockSpec((tk, tn), lambda i, j, k: (k, j))
    elif mode == "nt":
        a_spec = pl.BlockSpec((tm, tk), lambda i, j, k: (i, k))
        b_spec = pl.BlockSpec((tn, tk), lambda i, j, k: (j, k))
    else:
        a_spec = pl.BlockSpec((tk, tm), lambda i, j, k: (k, i))
        b_spec = pl.BlockSpec((tk, tn), lambda i, j, k: (k, j))
    flat = [r for ab in terms for r in ab]
    return pl.pallas_call(
        body, name=name, grid=(M // tm, N // tn, nk),
        in_specs=[a_spec, b_spec] * nterm,
        out_specs=pl.BlockSpec((tm, tn), lambda i, j, k: (i, j)),
        out_shape=jax.ShapeDtypeStruct((M, N), out_dtype),
        scratch_shapes=[pltpu.VMEM((tm, tn) if nk > 1 else (8, LANES), F32)],
        compiler_params=_params(("parallel", "parallel", "arbitrary")),
    )(*flat)


def _norm_bwd_rows(d, x, gain, scale, sums_ref):
    r = lax.rsqrt(jnp.mean(x * x, axis=-1, keepdims=True) + NORM_EPS)
    xh = x * r
    dn = d * (1.0 + scale)
    sums_ref[0:1, :] += jnp.sum(d, axis=0, keepdims=True)
    sums_ref[1:2, :] += jnp.sum(d * xh * gain, axis=0, keepdims=True)
    sums_ref[2:3, :] += jnp.sum(dn * xh, axis=0, keepdims=True)
    dxh = dn * gain
    return r * (dxh - xh * jnp.mean(dxh * xh, axis=-1, keepdims=True))


def _rowmm(name, terms, mode, out_dtype=F32, res=None, norm_bwd=None):
    S = terms[0][0].shape[-2]
    N = terms[0][2].shape[-1] if mode == "nn" else terms[0][2].shape[-2]
    nterm = len(terms)
    tm = ROW_T
    dims = NN if mode == "nn" else NT

    def body(*refs):
        o = None
        for t in range(nterm):
            p = _dot(refs[2 * t][...], refs[2 * t + 1][...], dims)
            o = p if o is None else o + p
        if norm_bwd is not None:
            h_ref, g_ref, sc_ref, gr_ref, dh_ref, sums_ref = refs[2 * nterm:2 * nterm + 6]

            @pl.when(pl.program_id(0) == 0)
            def _():
                sums_ref[...] = jnp.zeros_like(sums_ref)

            dh_ref[...] = gr_ref[...] + _norm_bwd_rows(o, h_ref[...], g_ref[...], sc_ref[...], sums_ref)
        elif res is None:
            refs[2 * nterm][...] = o.astype(out_dtype)
        else:
            h_ref, gf_ref, out_ref, hn_ref = refs[2 * nterm:2 * nterm + 4]
            out_ref[...] = o.astype(out_ref.dtype)
            hn_ref[...] = h_ref[...] + gf_ref[...] * o

    in_specs, flat = [], []
    for a, ai, w, wi in terms:
        if ai is None:
            in_specs.append(pl.BlockSpec((tm, a.shape[1]), lambda i: (i, 0)))
        else:
            in_specs.append(pl.BlockSpec((None, tm, a.shape[2]), lambda i, g=ai: (g, i, 0)))
        if wi is None:
            in_specs.append(pl.BlockSpec(w.shape, lambda i: (0, 0)))
        else:
            in_specs.append(pl.BlockSpec((None,) + w.shape[1:], lambda i, g=wi: (g, 0, 0)))
        flat += [a, w]
    row = pl.BlockSpec((tm, N), lambda i: (i, 0))
    vec = pl.BlockSpec((1, N), lambda i: (0, 0))
    if norm_bwd is not None:
        return pl.pallas_call(
            body, name=name, grid=(S // tm,), in_specs=in_specs + [row, vec, vec, row],
            out_specs=[row, pl.BlockSpec((8, N), lambda i: (0, 0))],
            out_shape=[jax.ShapeDtypeStruct((S, N), F32), jax.ShapeDtypeStruct((8, N), F32)],
            compiler_params=_params(("arbitrary",)),
        )(*flat, *norm_bwd)
    if res is None:
        return pl.pallas_call(
            body, name=name, grid=(S // tm,), in_specs=in_specs, out_specs=row,
            out_shape=jax.ShapeDtypeStruct((S, N), out_dtype), compiler_params=_params(("parallel",)),
        )(*flat)
    h, gatefac = res
    return pl.pallas_call(
        body, name=name, grid=(S // tm,),
        in_specs=in_specs + [row, pl.BlockSpec((1, N), lambda i: (0, 0))], out_specs=[row, row],
        out_shape=[jax.ShapeDtypeStruct((S, N), BF16), jax.ShapeDtypeStruct((S, N), F32)],
        compiler_params=_params(("parallel",)),
    )(*flat, h, gatefac)


def _mm_tn_groups(name, a, b, out_dtype, tk=2048):
    G = a.shape[0] if a.ndim == 3 else b.shape[0]
    S, M, N = a.shape[-2], a.shape[-1], b.shape[-1]
    tk = min(tk, S)
    nk = S // tk

    def body(a_ref, b_ref, o_ref, acc):
        k = pl.program_id(1)
        part = _dot(a_ref[...], b_ref[...], TN)

        @pl.when(k == 0)
        def _():
            acc[...] = part

        @pl.when(k > 0)
        def _():
            acc[...] += part

        @pl.when(k == nk - 1)
        def _():
            o_ref[...] = acc[...].astype(o_ref.dtype)

    def spec(arr, width):
        if arr.ndim == 3:
            return pl.BlockSpec((None, tk, width), lambda g, k: (g, k, 0))
        return pl.BlockSpec((tk, width), lambda g, k: (k, 0))

    return pl.pallas_call(
        body, name=name, grid=(G, nk), in_specs=[spec(a, M), spec(b, N)],
        out_specs=pl.BlockSpec((None, M, N), lambda g, k: (g, 0, 0)),
        out_shape=jax.ShapeDtypeStruct((G, M, N), out_dtype),
        scratch_shapes=[pltpu.VMEM((M, N), F32)],
        compiler_params=_params(("parallel", "arbitrary")),
    )(a, b)


def _norm_mod(name, h, gain, shift, scale):
    S, D = h.shape
    tm = ROW_T

    def body(h_ref, g_ref, sh_ref, sc_ref, u_ref):
        x = h_ref[...]
        r = lax.rsqrt(jnp.mean(x * x, axis=-1, keepdims=True) + NORM_EPS)
        u_ref[...] = (x * r * g_ref[...] * (1.0 + sc_ref[...]) + sh_ref[...]).astype(u_ref.dtype)

    vec = pl.BlockSpec((1, D), lambda i: (0, 0))
    return pl.pallas_call(
        body, name=name, grid=(S // tm,),
        in_specs=[pl.BlockSpec((tm, D), lambda i: (i, 0)), vec, vec, vec],
        out_specs=pl.BlockSpec((tm, D), lambda i: (i, 0)),
        out_shape=jax.ShapeDtypeStruct((S, D), BF16),
        compiler_params=_params(("parallel",)),
    )(h, gain, shift, scale)


def _norm_mod_bwd(name, du, h, gain, scale, gres):
    S, D = h.shape
    tm = ROW_T

    def body(du_ref, h_ref, g_ref, sc_ref, gr_ref, dh_ref, sums_ref):
        @pl.when(pl.program_id(0) == 0)
        def _():
            sums_ref[...] = jnp.zeros_like(sums_ref)

        dh_ref[...] = gr_ref[...] + _norm_bwd_rows(du_ref[...].astype(F32), h_ref[...], g_ref[...], sc_ref[...],
                                                   sums_ref)

    row = pl.BlockSpec((tm, D), lambda i: (i, 0))
    vec = pl.BlockSpec((1, D), lambda i: (0, 0))
    return pl.pallas_call(
        body, name=name, grid=(S // tm,),
        in_specs=[row, row, vec, vec, row],
        out_specs=[row, pl.BlockSpec((8, D), lambda i: (0, 0))],
        out_shape=[jax.ShapeDtypeStruct((S, D), F32), jax.ShapeDtypeStruct((8, D), F32)],
        compiler_params=_params(("arbitrary",)),
    )(du, h, gain, scale, gres)


def _gate_bwd(name, g, out, gatefac):
    S, D = g.shape
    tm = ROW_T

    def body(g_ref, o_ref, gf_ref, d_ref, sums_ref):
        @pl.when(pl.program_id(0) == 0)
        def _():
            sums_ref[...] = jnp.zeros_like(sums_ref)

        gv = g_ref[...]
        d_ref[...] = (gv * gf_ref[...]).astype(d_ref.dtype)
        sums_ref[0:1, :] += jnp.sum(gv * o_ref[...].astype(F32), axis=0, keepdims=True)

    row = pl.BlockSpec((tm, D), lambda i: (i, 0))
    return pl.pallas_call(
        body, name=name, grid=(S // tm,),
        in_specs=[row, row, pl.BlockSpec((1, D), lambda i: (0, 0))],
        out_specs=[row, pl.BlockSpec((8, D), lambda i: (0, 0))],
        out_shape=[jax.ShapeDtypeStruct((S, D), BF16), jax.ShapeDtypeStruct((8, D), F32)],
        compiler_params=_params(("arbitrary",)),
    )(g, out, gatefac)


def _final_loss(name, h, target, gain, shift, scale):
    S, D = h.shape
    tm = ROW_T

    def body(h_ref, t_ref, g_ref, sh_ref, sc_ref, dy_ref, loss_ref):
        @pl.when(pl.program_id(0) == 0)
        def _():
            loss_ref[...] = jnp.zeros_like(loss_ref)

        x = h_ref[...]
        r = lax.rsqrt(jnp.mean(x * x, axis=-1, keepdims=True) + NORM_EPS)
        y = x * r * g_ref[...] * (1.0 + sc_ref[...]) + sh_ref[...]
        e = y - t_ref[...]
        dy_ref[...] = e * (1.0 / D)
        loss_ref[0:1, :] += 0.5 * jnp.sum(e * e, axis=0, keepdims=True) * (1.0 / D)

    row = pl.BlockSpec((tm, D), lambda i: (i, 0))
    vec = pl.BlockSpec((1, D), lambda i: (0, 0))
    return pl.pallas_call(
        body, name=name, grid=(S // tm,),
        in_specs=[row, row, vec, vec, vec],
        out_specs=[row, pl.BlockSpec((8, D), lambda i: (0, 0))],
        out_shape=[jax.ShapeDtypeStruct((S, D), F32), jax.ShapeDtypeStruct((8, D), F32)],
        compiler_params=_params(("arbitrary",)),
    )(h, target, gain, shift, scale)


def _ffn_up(name, u, wg, wu):
    S, D = u.shape
    G, _, Fg = wg.shape
    tm = ROW_T

    def body(u_ref, wg_ref, wu_ref, a_ref, b_ref, hm_ref):
        uv = u_ref[...]
        a = _dot(uv, wg_ref[...])
        b = _dot(uv, wu_ref[...])
        sg = _sigmoid(a)
        act = a * sg
        a_ref[...] = (b * sg * (1.0 + a * (1.0 - sg))).astype(a_ref.dtype)
        b_ref[...] = act.astype(b_ref.dtype)
        hm_ref[...] = (act * b).astype(hm_ref.dtype)

    w_spec = pl.BlockSpec((None, D, Fg), lambda g, i: (g, 0, 0))
    o_spec = pl.BlockSpec((None, tm, Fg), lambda g, i: (g, i, 0))
    return pl.pallas_call(
        body, name=name, grid=(G, S // tm),
        in_specs=[pl.BlockSpec((tm, D), lambda g, i: (i, 0)), w_spec, w_spec],
        out_specs=[o_spec] * 3,
        out_shape=[jax.ShapeDtypeStruct((G, S, Fg), BF16)] * 3,
        compiler_params=_params(("parallel", "parallel")),
    )(u, wg, wu)


def _ffn_bwd_mid(name, d_out, wd, a, b):
    S, D = d_out.shape
    G, Fg, _ = wd.shape
    tm = ROW_T

    def body(d_ref, wd_ref, a_ref, b_ref, da_ref, db_ref):
        dhm = _dot(d_ref[...], wd_ref[...], NT)
        da_ref[...] = (dhm * a_ref[...].astype(F32)).astype(da_ref.dtype)
        db_ref[...] = (dhm * b_ref[...].astype(F32)).astype(db_ref.dtype)

    t_spec = pl.BlockSpec((None, tm, Fg), lambda g, i: (g, i, 0))
    return pl.pallas_call(
        body, name=name, grid=(G, S // tm),
        in_specs=[pl.BlockSpec((tm, D), lambda g, i: (i, 0)), pl.BlockSpec((None, Fg, D), lambda g, i: (g, 0, 0)),
                  t_spec, t_spec],
        out_specs=[t_spec] * 2,
        out_shape=[jax.ShapeDtypeStruct((G, S, Fg), BF16)] * 2,
        compiler_params=_params(("parallel", "parallel")),
    )(d_out, wd, a, b)


def _ret_tables(S):
    T = RET_T
    half = HEAD_DIM // 2
    inv_freq = ROPE_BASE ** (-jnp.arange(half, dtype=F32) / half)
    ang = jnp.arange(S, dtype=F32)[:, None] * inv_freq[None, :]
    cos, sin = jnp.cos(ang), jnp.sin(ang)
    cosf = jnp.concatenate([cos, cos], axis=-1)
    sinf = jnp.concatenate([-sin, sin], axis=-1)
    log_gamma = jnp.log1p(-(2.0 ** (-5.0 - jnp.arange(RET_HEADS, dtype=F32))))
    idx = jnp.arange(T, dtype=F32)
    chunk = jnp.arange(T) // 64
    vis = (chunk[None, :] <= chunk[:, None]).astype(F32)
    mask = jnp.exp(log_gamma[:, None, None] * jnp.abs(idx[:, None] - idx[None, :])) * vis[None]
    ones = jnp.ones((1, 1, LANES), F32)
    qdec = jnp.exp(log_gamma[:, None] * (idx + 1.0)[None, :])[:, :, None] * ones
    kdec = jnp.exp(log_gamma[:, None] * (T - 1.0 - idx)[None, :])[:, :, None] * ones
    cdec = jnp.exp(log_gamma * T)[:, None, None] * jnp.ones((1, 8, LANES), F32)
    return cosf, sinf, mask, qdec, kdec, cdec


def _rope(x, cosf, sinf):
    return x * cosf + pltpu.roll(x, HEAD_DIM // 2, 1) * sinf


def _rope_bwd(d, cosf, sinf):
    return d * cosf + pltpu.roll(d * sinf, HEAD_DIM // 2, 1)


def _ret_specs(T, rev_nb=None):
    if rev_nb is None:
        blk = lambda b: b
    else:
        blk = lambda b: rev_nb - 1 - b
    proj = lambda off: pl.BlockSpec((T, HEAD_DIM), lambda h, b: (blk(b), off + h))
    rows = pl.BlockSpec((T, HEAD_DIM), lambda h, b: (blk(b), 0))
    per_head = lambda shape: pl.BlockSpec((1,) + shape, lambda h, b: (h, 0, 0))
    return ([proj(0), proj(4), proj(8), proj(12), rows, rows,
             per_head((T, T)), per_head((T, LANES)), per_head((T, LANES)), per_head((8, LANES)),
             pl.BlockSpec((1, HEAD_DIM), lambda h, b: (0, h))], blk)


def _ret_fwd(name, proj, gn, tables):
    S = proj.shape[0]
    T = RET_T
    nb = S // T
    scale = HEAD_DIM ** -0.5
    specs, _ = _ret_specs(T)

    def body(q_ref, k_ref, v_ref, g_ref, cos_ref, sin_ref, m_ref, qd_ref, kd_ref, cd_ref, gn_ref, y_ref, st_ref, st):
        @pl.when(pl.program_id(1) == 0)
        def _():
            st[...] = jnp.zeros_like(st)

        cosf, sinf = cos_ref[...], sin_ref[...]
        qr = _rope(q_ref[...], cosf, sinf)
        kr = _rope(k_ref[...], cosf, sinf) * scale
        v = v_ref[...]
        sp = st[...]
        st_ref[0, 0] = sp
        w = _dot(qr, kr, NT) * m_ref[0]
        y = _dot(w, v) + _dot(qr * qd_ref[0], sp)
        st[...] = cd_ref[0, 0:1, :] * sp + _dot(kr * kd_ref[0], v, TN)
        r = lax.rsqrt(jnp.mean(y * y, axis=-1, keepdims=True) + NORM_EPS)
        g = g_ref[...]
        y_ref[...] = y * r * gn_ref[...] * (g * _sigmoid(g))

    return pl.pallas_call(
        body, name=name, grid=(RET_HEADS, nb), in_specs=specs,
        out_specs=[pl.BlockSpec((T, HEAD_DIM), lambda h, b: (b, h)),
                   pl.BlockSpec((1, 1, HEAD_DIM, HEAD_DIM), lambda h, b: (h, b, 0, 0))],
        out_shape=[jax.ShapeDtypeStruct((S, RET_HEADS * HEAD_DIM), F32),
                   jax.ShapeDtypeStruct((RET_HEADS, nb, HEAD_DIM, HEAD_DIM), F32)],
        scratch_shapes=[pltpu.VMEM((HEAD_DIM, HEAD_DIM), F32)],
        compiler_params=_params(("parallel", "arbitrary")),
    )(proj, proj, proj, proj, tables[0], tables[1], tables[2], tables[3], tables[4], tables[5], gn)


def _ret_bwd(name, proj, gn, tables, states, dmix):
    S = proj.shape[0]
    T = RET_T
    nb = S // T
    scale = HEAD_DIM ** -0.5
    specs, blk = _ret_specs(T, rev_nb=nb)
    specs = specs + [pl.BlockSpec((1, 1, HEAD_DIM, HEAD_DIM), lambda h, b: (h, blk(b), 0, 0)),
                     pl.BlockSpec((T, HEAD_DIM), lambda h, b: (blk(b), h))]

    def body(q_ref, k_ref, v_ref, g_ref, cos_ref, sin_ref, m_ref, qd_ref, kd_ref, cd_ref, gn_ref, st_ref, do_ref,
             d_ref, dgn_ref, dst):
        @pl.when(pl.program_id(1) == 0)
        def _():
            dst[...] = jnp.zeros_like(dst)
            dgn_ref[...] = jnp.zeros_like(dgn_ref)

        cosf, sinf = cos_ref[...], sin_ref[...]
        qr = _rope(q_ref[...], cosf, sinf)
        kr = _rope(k_ref[...], cosf, sinf) * scale
        v = v_ref[...]
        sp = st_ref[0, 0]
        mask, qd, kd = m_ref[0], qd_ref[0], kd_ref[0]
        w = _dot(qr, kr, NT) * mask
        y = _dot(w, v) + _dot(qr * qd, sp)
        r = lax.rsqrt(jnp.mean(y * y, axis=-1, keepdims=True) + NORM_EPS)
        yh = y * r
        gn_v = gn_ref[...]
        g = g_ref[...]
        sg = _sigmoid(g)
        do = do_ref[...]
        dyn = do * g * sg
        dgn_ref[...] += jnp.sum(dyn * yh, axis=0, keepdims=True)
        dyh = dyn * gn_v
        dy = r * (dyh - yh * jnp.mean(dyh * yh, axis=-1, keepdims=True))
        dg = do * yh * gn_v * sg * (1.0 + g * (1.0 - sg))
        ds = dst[...]
        dp = _dot(dy, v, NT) * mask
        dqr = _dot(dp, kr) + _dot(dy, sp, NT) * qd
        dkr = _dot(dp, qr, TN) + _dot(v, ds, NT) * kd
        dv = _dot(w, dy, TN) + _dot(kr * kd, ds)
        dst[...] = cd_ref[0, 0:1, :] * ds + _dot(qr * qd, dy, TN)
        d_ref[0] = _rope_bwd(dqr, cosf, sinf).astype(d_ref.dtype)
        d_ref[1] = _rope_bwd(dkr * scale, cosf, sinf).astype(d_ref.dtype)
        d_ref[2] = dv.astype(d_ref.dtype)
        d_ref[3] = dg.astype(d_ref.dtype)

    return pl.pallas_call(
        body, name=name, grid=(RET_HEADS, nb), in_specs=specs,
        out_specs=[pl.BlockSpec((4, T, HEAD_DIM), lambda h, b: (0, blk(b), h)),
                   pl.BlockSpec((1, HEAD_DIM), lambda h, b: (0, h))],
        out_shape=[jax.ShapeDtypeStruct((4, S, RET_HEADS * HEAD_DIM), BF16),
                   jax.ShapeDtypeStruct((1, RET_HEADS * HEAD_DIM), F32)],
        scratch_shapes=[pltpu.VMEM((HEAD_DIM, HEAD_DIM), F32)],
        compiler_params=_params(("parallel", "arbitrary")),
    )(proj, proj, proj, proj, tables[0], tables[1], tables[2], tables[3], tables[4], tables[5], gn, states, dmix)


def _sb_logits(qb, kb, i, j, scale):
    T = SB_T
    z = lax.dot_general(qb, kb, NT, preferred_element_type=F32) * scale
    row = lax.broadcasted_iota(jnp.int32, (T, T), 0)
    col = lax.broadcasted_iota(jnp.int32, (T, T), 1)
    vis = jnp.logical_or(j < i, col < row)
    lp = jnp.log1p(jnp.exp(-jnp.abs(z)))
    lb = jnp.minimum(z, 0.0) - lp
    lk = jnp.where(vis, -jnp.maximum(z, 0.0) - lp, 0.0)
    return lb, lk, vis


def _sb_weights(lb, lk, vis, tailc):
    T = SB_T
    row = lax.broadcasted_iota(jnp.int32, (T, T), 0)
    col = lax.broadcasted_iota(jnp.int32, (T, T), 1)
    tail = tailc + _split_dot(row > col, lk, lhs01=False, pieces=2)
    return jnp.where(vis, jnp.exp(lb + tail), 0.0)


def _sb_specs(S):
    T, W = SB_T, SB_HPS * HEAD_DIM
    return [pl.BlockSpec((T, W), lambda hp, i: (i, 2048 // W + hp)),
            pl.BlockSpec((S, W), lambda hp, i: (0, hp)),
            pl.BlockSpec((S, W), lambda hp, i: (0, 512 // W + hp))]


def _lanes(e):
    return slice(e * HEAD_DIM, (e + 1) * HEAD_DIM)


def _sb_fwd(name, proj, kv):
    S = proj.shape[0]
    T, E = SB_T, SB_HPS
    nq = S // T
    scale = HEAD_DIM ** -0.5

    def body(q_ref, k_ref, v_ref, y_ref):
        i = pl.program_id(1)
        qs = [q_ref[:, _lanes(e)].astype(BF16) for e in range(E)]

        def cond(c):
            return jnp.logical_and(c[0] >= 0, c[1] == 0)

        def step(c):
            j, _, tails, accs = c
            rows = pl.ds(pl.multiple_of(j * T, T), T)
            new_tails, new_accs, worst = [], [], None
            for e in range(E):
                lb, lk, vis = _sb_logits(qs[e], k_ref[rows, _lanes(e)], i, j, scale)
                w = _sb_weights(lb, lk, vis, tails[e])
                new_accs.append(accs[e] + lax.dot_general(w.astype(BF16), v_ref[rows, _lanes(e)], NN,
                                                          preferred_element_type=F32))
                t = tails[e] + jnp.sum(lk, axis=1, keepdims=True)
                new_tails.append(t)
                worst = jnp.max(t) if worst is None else jnp.maximum(worst, jnp.max(t))
            return j - 1, (worst < SB_DEAD).astype(jnp.int32), tuple(new_tails), tuple(new_accs)

        init = (i, jnp.int32(0), (jnp.zeros((T, 1), F32),) * E, (jnp.zeros((T, HEAD_DIM), F32),) * E)
        accs = lax.while_loop(cond, step, init)[3]
        for e in range(E):
            y_ref[:, _lanes(e)] = accs[e]

    return pl.pallas_call(
        body, name=name, grid=(SB_HEADS // E, nq), in_specs=_sb_specs(S),
        out_specs=pl.BlockSpec((T, E * HEAD_DIM), lambda hp, i: (i, hp)),
        out_shape=jax.ShapeDtypeStruct((S, SB_HEADS * HEAD_DIM), F32),
        compiler_params=_params(("parallel", "arbitrary")),
    )(proj, kv, kv)


def _sb_bwd(name, proj, kv, dmix):
    S = proj.shape[0]
    T, E = SB_T, SB_HPS
    nq = S // T
    scale = HEAD_DIM ** -0.5

    def body(q_ref, k_ref, v_ref, do_ref, dq_ref, dk_ref, dv_ref):
        i = pl.program_id(1)

        @pl.when(i == 0)
        def _():
            dk_ref[...] = jnp.zeros_like(dk_ref)
            dv_ref[...] = jnp.zeros_like(dv_ref)

        qs = [q_ref[:, _lanes(e)].astype(BF16) for e in range(E)]
        dos = [do_ref[:, _lanes(e)].astype(BF16) for e in range(E)]
        row = lax.broadcasted_iota(jnp.int32, (T, T), 0)
        col = lax.broadcasted_iota(jnp.int32, (T, T), 1)
        zcol = (jnp.zeros((T, 1), F32),) * E

        def cond(c):
            return jnp.logical_and(c[0] >= 0, c[1] == 0)

        def walk_left(c):
            j, _, tails = c
            rows = pl.ds(pl.multiple_of(j * T, T), T)
            new_tails, worst = [], None
            for e in range(E):
                _, lk, _ = _sb_logits(qs[e], k_ref[rows, _lanes(e)], i, j, scale)
                t = tails[e] + jnp.sum(lk, axis=1, keepdims=True)
                new_tails.append(t)
                worst = jnp.max(t) if worst is None else jnp.maximum(worst, jnp.max(t))
            return j - 1, (worst < SB_DEAD).astype(jnp.int32), tuple(new_tails)

        j_end, _, totals = lax.while_loop(cond, walk_left, (i, jnp.int32(0), zcol))

        def walk_back(j, c):
            lefts, used, dqs = c
            rows = pl.ds(pl.multiple_of(j * T, T), T)
            new_lefts, new_used, new_dqs = [], [], []
            for e in range(E):
                kb = k_ref[rows, _lanes(e)]
                lb, lk, vis = _sb_logits(qs[e], kb, i, j, scale)
                u = used[e] + jnp.sum(lk, axis=1, keepdims=True)
                w = _sb_weights(lb, lk, vis, totals[e] - u)
                de = lax.dot_general(dos[e], v_ref[rows, _lanes(e)], NT, preferred_element_type=F32) * w
                dlk = jnp.where(vis, lefts[e] + _split_dot(row < col, de, lhs01=False, pieces=2), 0.0)
                sg = jnp.exp(lb)
                dz = ((de * (1.0 - sg) - dlk * sg) * scale).astype(BF16)
                new_dqs.append(dqs[e] + lax.dot_general(dz, kb, NN, preferred_element_type=F32))
                dk_ref[rows, _lanes(e)] += lax.dot_general(dz, qs[e], TN, preferred_element_type=F32)
                dv_ref[rows, _lanes(e)] += lax.dot_general(w.astype(BF16), dos[e], TN, preferred_element_type=F32)
                new_lefts.append(lefts[e] + jnp.sum(de, axis=1, keepdims=True))
                new_used.append(u)
            return tuple(new_lefts), tuple(new_used), tuple(new_dqs)

        init = (zcol, zcol, (jnp.zeros((T, HEAD_DIM), F32),) * E)
        dqs = lax.fori_loop(j_end + 1, i + 1, walk_back, init)[2]
        for e in range(E):
            dq_ref[:, _lanes(e)] = dqs[e].astype(dq_ref.dtype)

    W = E * HEAD_DIM
    blk = pl.BlockSpec((T, W), lambda hp, i: (i, hp))
    full = pl.BlockSpec((S, W), lambda hp, i: (0, hp))
    shp = jax.ShapeDtypeStruct((S, SB_HEADS * HEAD_DIM), F32)
    return pl.pallas_call(
        body, name=name, grid=(SB_HEADS // E, nq),
        in_specs=_sb_specs(S) + [pl.BlockSpec((T, W), lambda hp, i: (i, 512 // W + hp))],
        out_specs=[blk, full, full],
        out_shape=[jax.ShapeDtypeStruct((S, SB_HEADS * HEAD_DIM), BF16), shp, shp],
        compiler_params=_params(("parallel", "arbitrary")),
    )(proj, kv, kv, dmix)


def _conv_fwd(name, proj, conv_w, conv_b):
    S = proj.shape[0]
    T = CONV_T
    C = 1024
    K = SSM_CONV

    def body(x_ref, w_ref, b_ref, o_ref, buf):
        @pl.when(pl.program_id(0) == 0)
        def _():
            buf[0:8, :] = jnp.zeros((8, C), F32)

        buf[8:T + 8, :] = x_ref[...]
        acc = b_ref[...] + w_ref[K - 1:K, :] * buf[8:T + 8, :]
        for k in range(K - 1):
            acc = acc + w_ref[k:k + 1, :] * buf[5 + k:5 + k + T, :]
        o_ref[...] = acc * _sigmoid(acc)
        buf[0:8, :] = buf[T:T + 8, :]

    return pl.pallas_call(
        body, name=name, grid=(S // T,),
        in_specs=[pl.BlockSpec((T, C), lambda i: (i, 4)), pl.BlockSpec((K, C), lambda i: (0, 0)),
                  pl.BlockSpec((1, C), lambda i: (0, 0))],
        out_specs=pl.BlockSpec((T, C), lambda i: (i, 0)),
        out_shape=jax.ShapeDtypeStruct((S, C), F32),
        scratch_shapes=[pltpu.VMEM((T + 8, C), F32)],
        compiler_params=_params(("arbitrary",)),
    )(proj, conv_w, conv_b)


def _conv_bwd_act(name, proj, conv_w, conv_b, dxbc):
    S = proj.shape[0]
    T = CONV_T
    C = 1024
    K = SSM_CONV

    def body(x_ref, w_ref, b_ref, d_ref, dp_ref, sums_ref, buf):
        @pl.when(pl.program_id(0) == 0)
        def _():
            buf[0:8, :] = jnp.zeros((8, C), F32)
            sums_ref[...] = jnp.zeros_like(sums_ref)

        buf[8:T + 8, :] = x_ref[...]
        acc = b_ref[...] + w_ref[K - 1:K, :] * buf[8:T + 8, :]
        for k in range(K - 1):
            acc = acc + w_ref[k:k + 1, :] * buf[5 + k:5 + k + T, :]
        sg = _sigmoid(acc)
        dp = d_ref[...] * sg * (1.0 + acc * (1.0 - sg))
        dp_ref[...] = dp
        for k in range(K):
            sums_ref[k:k + 1, :] += jnp.sum(dp * buf[5 + k:5 + k + T, :], axis=0, keepdims=True)
        sums_ref[4:5, :] += jnp.sum(dp, axis=0, keepdims=True)
        buf[0:8, :] = buf[T:T + 8, :]

    row = pl.BlockSpec((T, C), lambda i: (i, 0))
    return pl.pallas_call(
        body, name=name, grid=(S // T,),
        in_specs=[pl.BlockSpec((T, C), lambda i: (i, 4)), pl.BlockSpec((K, C), lambda i: (0, 0)),
                  pl.BlockSpec((1, C), lambda i: (0, 0)), row],
        out_specs=[row, pl.BlockSpec((8, C), lambda i: (0, 0))],
        out_shape=[jax.ShapeDtypeStruct((S, C), F32), jax.ShapeDtypeStruct((8, C), F32)],
        scratch_shapes=[pltpu.VMEM((T + 8, C), F32)],
        compiler_params=_params(("arbitrary",)),
    )(proj, conv_w, conv_b, dxbc)


def _conv_bwd_in(name, dp, conv_w):
    S, C = dp.shape
    T = CONV_T
    K = SSM_CONV
    nb = S // T

    def body(d_ref, w_ref, o_ref, buf):
        @pl.when(pl.program_id(0) == 0)
        def _():
            buf[T:T + 8, :] = jnp.zeros((8, C), F32)

        buf[0:T, :] = d_ref[...]
        acc = w_ref[K - 1:K, :] * buf[0:T, :]
        for k in range(K - 1):
            acc = acc + w_ref[k:k + 1, :] * buf[3 - k:3 - k + T, :]
        o_ref[...] = acc.astype(o_ref.dtype)
        buf[T:T + 8, :] = buf[0:8, :]

    row = pl.BlockSpec((T, C), lambda i: (nb - 1 - i, 0))
    return pl.pallas_call(
        body, name=name, grid=(nb,),
        in_specs=[row, pl.BlockSpec((K, C), lambda i: (0, 0))],
        out_specs=row,
        out_shape=jax.ShapeDtypeStruct((S, C), BF16),
        scratch_shapes=[pltpu.VMEM((T + 8, C), F32)],
        compiler_params=_params(("arbitrary",)),
    )(dp, conv_w)


def _softplus(x):
    return jnp.maximum(x, 0.0) + jnp.log1p(jnp.exp(-jnp.abs(x)))


def _ssd_common(xbc_ref, dt_ref, dtb_ref, alog_ref):
    T = SSD_T
    dtr = dt_ref[...] + dtb_ref[...]
    dt = _softplus(dtr)
    a = -jnp.exp(alog_ref[...])
    dta = dt * a
    row = lax.broadcasted_iota(jnp.int32, (T, T), 0)
    col = lax.broadcasted_iota(jnp.int32, (T, T), 1)
    causal = col <= row
    acum = _split_dot(causal, dta)
    acum_t = _split_dot(row <= col, dta, TN, lhs01=False)
    return dtr, dt, a, acum, acum_t, causal, row


def _ssd_head(xbc_ref, h, dt, acum, acum_t, causal, cb):
    P = SSM_P
    ac = acum[:, h:h + 1]
    decay = jnp.exp(jnp.where(causal, ac - acum_t[h:h + 1, :], -1e30))
    dth = dt[:, h:h + 1]
    xs = xbc_ref[:, h * P:(h + 1) * P]
    xdt = xs * dth
    sc = (cb * decay).astype(BF16)
    aend = acum[SSD_T - 1:SSD_T, h:h + 1]
    return ac, dth, xs, xdt, sc, aend


def _ssd_specs(T, blk):
    vec = lambda n: pl.BlockSpec((1, n), lambda b: (0, 0))
    return [pl.BlockSpec((T, 1024), lambda b: (blk(b), 0)),
            pl.BlockSpec((T, LANES), lambda b: (blk(b), DT_COL // LANES)),
            pl.BlockSpec((T, 512), lambda b: (blk(b), 7)),
            vec(LANES), vec(LANES), vec(LANES), vec(512)]


def _ssd_fwd(name, xbc, proj, dt_bias, a_log, d_skip, gain):
    S = xbc.shape[0]
    T = SSD_T
    nb = S // T
    H, P, N = SSM_HEADS, SSM_P, SSM_N

    def body(xbc_ref, dt_ref, z_ref, dtb_ref, alog_ref, dsk_ref, gain_ref, y_ref, st_ref, st, ybuf):
        @pl.when(pl.program_id(0) == 0)
        def _():
            st[...] = jnp.zeros_like(st)

        _, dt, _, acum, acum_t, causal, _ = _ssd_common(xbc_ref, dt_ref, dtb_ref, alog_ref)
        for g in range(SSM_GROUPS):
            bg = xbc_ref[:, 512 + g * N:512 + (g + 1) * N]
            cg = xbc_ref[:, 768 + g * N:768 + (g + 1) * N]
            cb = _dot(cg, bg, NT)
            for hh in range(H // SSM_GROUPS):
                h = g * (H // SSM_GROUPS) + hh
                ac, _, xs, xdt, sc, aend = _ssd_head(xbc_ref, h, dt, acum, acum_t, causal, cb)
                sp = st[h]
                st_ref[0, h] = sp
                y = _dot(sc, xdt) + jnp.exp(ac) * _dot(cg, sp) + xs * dsk_ref[:, h:h + 1]
                st[h] = jnp.exp(aend) * sp + _dot(bg, xdt * jnp.exp(aend - ac), TN)
                ybuf[:, h * P:(h + 1) * P] = y
        z = z_ref[...]
        yg = ybuf[...] * z * _sigmoid(z)
        r = lax.rsqrt(jnp.mean(yg * yg, axis=-1, keepdims=True) + NORM_EPS)
        y_ref[...] = yg * r * gain_ref[...]

    return pl.pallas_call(
        body, name=name, grid=(nb,), in_specs=_ssd_specs(T, lambda b: b),
        out_specs=[pl.BlockSpec((T, 512), lambda b: (b, 0)), pl.BlockSpec((1, H, N, P), lambda b: (b, 0, 0, 0))],
        out_shape=[jax.ShapeDtypeStruct((S, 512), F32), jax.ShapeDtypeStruct((nb, H, N, P), F32)],
        scratch_shapes=[pltpu.VMEM((H, N, P), F32), pltpu.VMEM((T, 512), F32)],
        compiler_params=_params(("arbitrary",)),
    )(xbc, proj, proj, dt_bias, a_log, d_skip, gain)


def _ssd_bwd(name, xbc, proj, dt_bias, a_log, d_skip, gain, states, dmix):
    S = xbc.shape[0]
    T = SSD_T
    nb = S // T
    H, P, N = SSM_HEADS, SSM_P, SSM_N
    blk = lambda b: nb - 1 - b

    def body(xbc_ref, dt_ref, z_ref, dtb_ref, alog_ref, dsk_ref, gain_ref, st_ref, do_ref,
             dx_ref, dz_ref, ddt_ref, sums_ref, dst, ybuf):
        @pl.when(pl.program_id(0) == 0)
        def _():
            dst[...] = jnp.zeros_like(dst)
            sums_ref[...] = jnp.zeros_like(sums_ref)

        dtr, dt, a, acum, acum_t, causal, row = _ssd_common(xbc_ref, dt_ref, dtb_ref, alog_ref)
        cbs = []
        for g in range(SSM_GROUPS):
            bg = xbc_ref[:, 512 + g * N:512 + (g + 1) * N]
            cg = xbc_ref[:, 768 + g * N:768 + (g + 1) * N]
            cb = _dot(cg, bg, NT)
            cbs.append(cb)
            for hh in range(H // SSM_GROUPS):
                h = g * (H // SSM_GROUPS) + hh
                ac, _, xs, xdt, sc, _ = _ssd_head(xbc_ref, h, dt, acum, acum_t, causal, cb)
                ybuf[:, h * P:(h + 1) * P] = (_dot(sc, xdt) + jnp.exp(ac) * _dot(cg, st_ref[0, h])
                                              + xs * dsk_ref[:, h:h + 1])
        z = z_ref[...]
        sg = _sigmoid(z)
        sz = z * sg
        yfull = ybuf[...]
        yg = yfull * sz
        r = lax.rsqrt(jnp.mean(yg * yg, axis=-1, keepdims=True) + NORM_EPS)
        yh = yg * r
        do = do_ref[...]
        sums_ref[0:1, :] += jnp.sum(do * yh, axis=0, keepdims=True)
        dyh = do * gain_ref[...]
        dyg = r * (dyh - yh * jnp.mean(dyh * yh, axis=-1, keepdims=True))
        dz_ref[...] = (dyg * yfull * sg * (1.0 + z * (1.0 - sg))).astype(dz_ref.dtype)
        dyv = dyg * sz

        lane = lax.broadcasted_iota(jnp.int32, (T, LANES), 1)
        rowl = lax.broadcasted_iota(jnp.int32, (T, 1), 0)
        dacum = jnp.zeros((T, LANES), F32)
        ddt = jnp.zeros((T, LANES), F32)
        dskp = jnp.zeros((T, LANES), F32)
        for g in range(SSM_GROUPS):
            bg = xbc_ref[:, 512 + g * N:512 + (g + 1) * N]
            cg = xbc_ref[:, 768 + g * N:768 + (g + 1) * N]
            cb = cbs[g]
            dcb = jnp.zeros((T, T), F32)
            dbg = jnp.zeros((T, N), F32)
            dcg = jnp.zeros((T, N), F32)
            for hh in range(H // SSM_GROUPS):
                h = g * (H // SSM_GROUPS) + hh
                ac, dth, xs, xdt, sc, aend = _ssd_head(xbc_ref, h, dt, acum, acum_t, causal, cb)
                decay = jnp.exp(jnp.where(causal, ac - acum_t[h:h + 1, :], -1e30))
                dy = dyv[:, h * P:(h + 1) * P]
                sp = st_ref[0, h]
                ea = jnp.exp(ac)
                de = jnp.exp(aend - ac)
                dec = jnp.exp(aend)
                dskp = dskp + jnp.where(lane == h, jnp.sum(dy * xs, axis=1, keepdims=True), 0.0)
                dxdt = _dot(sc, dy, TN)
                dsd = _dot(dy, xdt, NT) * decay
                dcb = dcb + dsd
                e = dsd * cb
                e_col = _split_dot(jnp.ones((T, LANES), F32), e, TN, lhs01=False)[:, 0:1]
                dac = jnp.sum(e, axis=1, keepdims=True) - e_col
                dyea = dy * ea
                dac = dac + jnp.sum(dyea * _dot(cg, sp), axis=1, keepdims=True)
                dcg = dcg + _dot(dyea, sp, NT)
                dsp = _dot(cg, dyea, TN)
                dsn = dst[h]
                xde = xdt * de
                dbg = dbg + _dot(xde, dsn, NT)
                wh = _dot(bg, dsn)
                dxdt = dxdt + wh * de
                r_end = jnp.sum(wh * xde, axis=1, keepdims=True)
                dend = jnp.sum(r_end) + jnp.sum(dsn * sp) * dec
                dst[h] = dsp + dec * dsn
                dac = dac - r_end + jnp.where(rowl == T - 1, dend, 0.0)
                dacum = dacum + jnp.where(lane == h, dac, 0.0)
                ddt = ddt + jnp.where(lane == h, jnp.sum(dxdt * xs, axis=1, keepdims=True), 0.0)
                dx_ref[:, h * P:(h + 1) * P] = dxdt * dth + dy * dsk_ref[:, h:h + 1]
            dx_ref[:, 512 + g * N:512 + (g + 1) * N] = dbg + _dot(dcb, cg, TN)
            dx_ref[:, 768 + g * N:768 + (g + 1) * N] = dcg + _dot(dcb, bg)
        ddta = _split_dot(row <= lax.broadcasted_iota(jnp.int32, (T, T), 1), dacum)
        ddt = ddt + ddta * a
        ddtr = ddt * _sigmoid(dtr)
        ddt_ref[...] = ddtr.astype(ddt_ref.dtype)
        sums_ref[1:2, 0:LANES] += jnp.sum(ddtr, axis=0, keepdims=True)
        sums_ref[1:2, LANES:2 * LANES] += jnp.sum(ddta * dt, axis=0, keepdims=True) * a
        sums_ref[1:2, 2 * LANES:3 * LANES] += jnp.sum(dskp, axis=0, keepdims=True)

    specs = _ssd_specs(T, blk) + [pl.BlockSpec((1, H, N, P), lambda b: (blk(b), 0, 0, 0)),
                                  pl.BlockSpec((T, 512), lambda b: (blk(b), 2))]
    return pl.pallas_call(
        body, name=name, grid=(nb,), in_specs=specs,
        out_specs=[pl.BlockSpec((T, 1024), lambda b: (blk(b), 0)), pl.BlockSpec((T, 512), lambda b: (blk(b), 0)),
                   pl.BlockSpec((T, LANES), lambda b: (blk(b), 0)), pl.BlockSpec((8, 512), lambda b: (0, 0))],
        out_shape=[jax.ShapeDtypeStruct((S, 1024), F32), jax.ShapeDtypeStruct((S, 512), BF16),
                   jax.ShapeDtypeStruct((S, LANES), BF16), jax.ShapeDtypeStruct((8, 512), F32)],
        scratch_shapes=[pltpu.VMEM((H, N, P), F32), pltpu.VMEM((T, 512), F32)],
        compiler_params=_params(("arbitrary",)),
    )(xbc, proj, proj, dt_bias, a_log, d_skip, gain, states, dmix)


def _place():
    return lax.axis_index("x"), lax.axis_index("y"), lax.axis_index("c")


def _flip(v, bit):
    return 1 - v if bit else v


def _gather_small(name, v):
    R, C = v.shape

    def body(v_ref, out_ref, send_sems, recv_sems, local_sem):
        x, y, c = _place()
        me = 4 * x + 2 * y + c
        mine = pltpu.make_async_copy(v_ref, out_ref.at[me], local_sem)
        mine.start()
        peers = [(_flip(x, (k >> 2) & 1), _flip(y, (k >> 1) & 1), _flip(c, k & 1)) for k in range(1, 8)]
        sends = []
        for k, peer in enumerate(peers):
            cp = pltpu.make_async_remote_copy(src_ref=v_ref, dst_ref=out_ref.at[me], send_sem=send_sems.at[k],
                                              recv_sem=recv_sems.at[k], device_id=peer, device_id_type=MESH)
            cp.start()
            sends.append(cp)
        for k, (px, py, pc) in enumerate(peers):
            pltpu.make_async_remote_copy(src_ref=v_ref, dst_ref=out_ref.at[4 * px + 2 * py + pc],
                                         send_sem=send_sems.at[k], recv_sem=recv_sems.at[k],
                                         device_id=(px, py, pc), device_id_type=MESH).wait_recv()
        for cp in sends:
            cp.wait_send()
        mine.wait()

    return pl.pallas_call(
        body, name=name, out_shape=jax.ShapeDtypeStruct((8, R, C), v.dtype),
        in_specs=[pl.BlockSpec(memory_space=pltpu.VMEM)], out_specs=pl.BlockSpec(memory_space=pltpu.VMEM),
        scratch_shapes=[pltpu.SemaphoreType.DMA((7,)), pltpu.SemaphoreType.DMA((7,)), pltpu.SemaphoreType.DMA(())],
    )(v)


def _hbm_call(body, name, arrays, out_shapes, n_sems):
    spec = pl.BlockSpec(memory_space=pl.ANY)
    return pl.pallas_call(
        body, name=name, out_shape=out_shapes, in_specs=[spec] * len(arrays), out_specs=[spec] * len(out_shapes),
        scratch_shapes=[pltpu.SemaphoreType.DMA((n_sems,)), pltpu.SemaphoreType.DMA((n_sems,))],
    )(*arrays)


def _chip_exchange(name, arrays, gather):
    n = len(arrays)

    def body(*refs):
        ins, outs, send_sems, recv_sems = refs[:n], refs[n:2 * n], refs[2 * n], refs[2 * n + 1]
        x, y, c = _place()
        chip = 2 * x + y
        peers = [(_flip(x, (k >> 1) & 1), _flip(y, k & 1)) for k in range(1, 4)]

        def copy(a, k, slot):
            px, py = peers[k]
            return pltpu.make_async_remote_copy(
                src_ref=ins[a].at[c] if gather else ins[a].at[2 * px + py], dst_ref=outs[a].at[slot],
                send_sem=send_sems.at[3 * a + k], recv_sem=recv_sems.at[3 * a + k],
                device_id=(px, py, c), device_id_type=MESH)

        sends = [copy(a, k, chip) for a in range(n) for k in range(3)]
        for cp in sends:
            cp.start()
        for a in range(n):
            for k, (px, py) in enumerate(peers):
                copy(a, k, 2 * px + py).wait_recv()
        for cp in sends:
            cp.wait_send()

    shapes = [jax.ShapeDtypeStruct((4,) + a.shape[1:], a.dtype) for a in arrays]
    return _hbm_call(body, name, arrays, shapes, 3 * n)


def _sibling_swap(name, arrays, alt=None):
    n = len(arrays)

    def body(*refs):
        k = 1 if alt is None else 2
        outs, send_sems, recv_sems = refs[k * n:(k + 1) * n], refs[(k + 1) * n], refs[(k + 1) * n + 1]
        x, y, c = _place()

        def exchange(srcs):
            cps = [pltpu.make_async_remote_copy(src_ref=srcs[a], dst_ref=outs[a], send_sem=send_sems.at[a],
                                                recv_sem=recv_sems.at[a], device_id=(x, y, 1 - c),
                                                device_id_type=MESH) for a in range(n)]
            for cp in cps:
                cp.start()
            for cp in cps:
                cp.wait()

        if alt is None:
            exchange(refs[:n])
        else:
            @pl.when(c == 1)
            def _():
                exchange(refs[:n])

            @pl.when(c == 0)
            def _():
                exchange(refs[n:2 * n])

    shapes = [jax.ShapeDtypeStruct(a.shape, a.dtype) for a in arrays]
    return _hbm_call(body, name, list(arrays) + ([] if alt is None else list(alt)), shapes, n)


PACK_C = 1024
SUM_STEPS = 4


def _add_layers(name, own0, own1, other, c):
    n = len(other)

    def body(c_ref, *refs):
        for a in range(n):
            mine = jnp.where(c_ref[0] == 0, refs[a][...].astype(F32), refs[n + a][...].astype(F32))
            refs[3 * n + a][...] = (mine + refs[2 * n + a][...].astype(F32)).astype(BF16)

    specs = [pl.BlockSpec((1, o.shape[1] // SUM_STEPS, o.shape[2]), lambda p, i, c_ref: (p, i, 0)) for o in other]
    grid_spec = pltpu.PrefetchScalarGridSpec(num_scalar_prefetch=1, grid=(4, SUM_STEPS), in_specs=specs * 3,
                                             out_specs=specs)
    return pl.pallas_call(
        body, name=name, grid_spec=grid_spec, out_shape=[jax.ShapeDtypeStruct(o.shape, BF16) for o in other],
        compiler_params=_params(("parallel", "parallel")),
    )(jnp.reshape(c, (1,)).astype(jnp.int32), *own0, *own1, *other)


def _sum_chips(name, received, own, chip):
    n = len(received)

    def body(chip_ref, *refs):
        for a in range(n):
            s = None
            for q in range(4):
                term = jnp.where(chip_ref[0] == q, refs[n + a][0], refs[a][q]).astype(F32)
                s = term if s is None else s + term
            refs[2 * n + a][...] = s

    rec = [pl.BlockSpec((4, r.shape[1] // SUM_STEPS, r.shape[2]), lambda i, chip_ref: (0, i, 0)) for r in received]
    mine = [pl.BlockSpec((1, r.shape[1] // SUM_STEPS, r.shape[2]), lambda i, chip_ref: (chip_ref[0], i, 0))
            for r in received]
    outs = [pl.BlockSpec((r.shape[1] // SUM_STEPS, r.shape[2]), lambda i, chip_ref: (i, 0)) for r in received]
    grid_spec = pltpu.PrefetchScalarGridSpec(num_scalar_prefetch=1, grid=(SUM_STEPS,), in_specs=rec + mine,
                                             out_specs=outs)
    return pl.pallas_call(
        body, name=name, grid_spec=grid_spec,
        out_shape=[jax.ShapeDtypeStruct(r.shape[1:], F32) for r in received],
        compiler_params=_params(("parallel",)),
    )(jnp.reshape(chip, (1,)).astype(jnp.int32), *received, *own)


def _sum_devices(name, parts):
    _, R, C = parts.shape

    def body(p_ref, s_ref):
        s = p_ref[0]
        for q in range(1, 8):
            s = s + p_ref[q]
        s_ref[...] = s

    return pl.pallas_call(body, name=name, out_shape=jax.ShapeDtypeStruct((R, C), F32))(parts)


def _adamw_rule(w, g, m, v):
    nm = ADAM_B1 * m + (1.0 - ADAM_B1) * g
    nv = ADAM_B2 * v + (1.0 - ADAM_B2) * (g * g)
    m_hat = nm / (1.0 - ADAM_B1 ** ADAM_STEP)
    v_hat = nv / (1.0 - ADAM_B2 ** ADAM_STEP)
    return -ADAM_LR * (m_hat / (jnp.sqrt(v_hat) + ADAM_EPS) + ADAM_WD * w), nm, nv


def _adamw(name, w, g, m, v):
    L, R, C = w.shape
    tr = 128 if R % 128 == 0 else R

    def body(w_ref, g_ref, m_ref, v_ref, d_ref, nm_ref, nv_ref):
        d_ref[...], nm_ref[...], nv_ref[...] = _adamw_rule(w_ref[...], g_ref[...], m_ref[...], v_ref[...])

    spec = pl.BlockSpec((None, tr, C), lambda l, i: (l, i, 0))
    return pl.pallas_call(
        body, name=name, grid=(L, R // tr), in_specs=[spec] * 4, out_specs=[spec] * 3,
        out_shape=[jax.ShapeDtypeStruct((L, R, C), F32)] * 3, compiler_params=_params(("parallel", "parallel")),
    )(w, g, m, v)


def _adamw_layers(name, w, g_mine, g_other, m, v, c):
    L, R, C = w.shape
    tr = 128 if R % 128 == 0 else R

    def body(c_ref, w_ref, gm_ref, go_ref, m_ref, v_ref, g_ref, d_ref, nm_ref, nv_ref):
        gv = jnp.where(pl.program_id(0) == c_ref[0], gm_ref[...], go_ref[...])
        g_ref[...] = gv
        d_ref[...], nm_ref[...], nv_ref[...] = _adamw_rule(w_ref[...], gv, m_ref[...], v_ref[...])

    full = pl.BlockSpec((None, tr, C), lambda l, i, c_ref: (l, i, 0))
    part = pl.BlockSpec((tr, C), lambda l, i, c_ref: (i, 0))
    grid_spec = pltpu.PrefetchScalarGridSpec(num_scalar_prefetch=1, grid=(L, R // tr),
                                             in_specs=[full, part, part, full, full], out_specs=[full] * 4)
    return pl.pallas_call(
        body, name=name, grid_spec=grid_spec, out_shape=[jax.ShapeDtypeStruct((L, R, C), F32)] * 4,
        compiler_params=_params(("parallel", "parallel")),
    )(jnp.reshape(c, (1,)).astype(jnp.int32), w, g_mine, g_other, m, v)


BIG = ("ffn1_wg", "ffn1_wu", "ffn1_wd", "w_in", "w_out", "ffn2_wg", "ffn2_wu", "ffn2_wd")
WEIGHTS =("ada_w", "ada_b", "norm_ffn1", "ffn1_wg", "ffn1_wu", "ffn1_wd", "norm_mix", "w_in", "conv_w", "conv_b",
           "dt_bias", "a_log", "d_skip", "ret_gn", "ssm_norm", "w_out", "norm_ffn2", "ffn2_wg", "ffn2_wu", "ffn2_wd",
           "final_ada_w", "final_ada_b", "final_norm")
SMALL = ("ada_b", "norm_ffn1", "norm_mix", "conv_w", "conv_b", "dt_bias", "a_log", "d_skip", "ret_gn", "ssm_norm",
         "norm_ffn2", "final_ada_b", "final_norm")


def _unpack(slab, shapes):
    flat = slab.reshape(-1)
    out, off = [], 0
    for shp in shapes:
        n = math.prod(shp)
        out.append(flat[off:off + n].reshape(shp))
        off += n
    return out


def _pad_lanes(v):
    return jnp.pad(v, (0, LANES - v.shape[0])).reshape(1, LANES)


def _ffn_fwd(tag, h, gain, mod3, wg, wu, wd):
    u = _norm_mod(tag + "_norm", h, gain, mod3[0:1], mod3[1:2])
    a, b, hm = _ffn_up(tag + "_up", u, wg, wu)
    gatefac = 0.5 * (1.0 + mod3[2:3])
    out, h_new = _rowmm(tag + "_down", [(hm, p, wd, p) for p in range(4)], "nn", res=(h, gatefac))
    return h_new, (h, u, a, b, hm, out, gatefac)


def _ffn_bwd(tag, g, saved, gain, mod3, wg, wu, wd):
    h, u, a, b, hm, out, gatefac = saved
    d_out, s_gate = _gate_bwd(tag + "_gate_bwd", g, out, gatefac)
    da, db = _ffn_bwd_mid(tag + "_mid_bwd", d_out, wd, a, b)
    dwd = _mm_tn_groups(tag + "_dwd", hm, d_out, BF16)
    dwg = _mm_tn_groups(tag + "_dwg", u, da, BF16)
    dwu = _mm_tn_groups(tag + "_dwu", u, db, BF16)
    g_in, s_norm = _rowmm(tag + "_du", [(da, p, wg, p) for p in range(4)] + [(db, p, wu, p) for p in range(4)], "nt",
                          norm_bwd=(h, gain, mod3[1:2], g))
    dmod3 = jnp.concatenate([s_norm[0:1], s_norm[1:2], 0.5 * s_gate[0:1]], axis=0)
    return g_in, (dwg, dwu, dwd), dmod3, s_norm[2]


def _mixer_fwd(tag, h, P, mod3, w_in, w_out, tables):
    u = _norm_mod(tag + "_norm", h, P["norm_mix"], mod3[0:1], mod3[1:2])
    proj = _mm(tag + "_in_proj", [(u, w_in)], "nn", F32, tm=512, tn=768, tk=1024)
    y_ret, ret_st = _ret_fwd(tag + "_ret", proj, P["ret_gn"], tables)
    kv = proj[:, 2560:3584].astype(BF16)
    y_sb = _sb_fwd(tag + "_sb", proj, kv)
    xbc = _conv_fwd(tag + "_conv", proj, P["conv_w"], P["conv_b"])
    y_ssm, ssm_st = _ssd_fwd(tag + "_ssd", xbc, proj, P["dt_bias"], P["a_log"], P["d_skip"], P["ssm_norm"])
    gatefac = 1.0 + mod3[2:3]
    ys = (y_ret, y_sb, y_ssm)
    w_out3 = w_out.reshape(3, 512, D_MODEL)
    mixed, h_new = _rowmm(tag + "_out_proj", [(y, None, w_out3, i) for i, y in enumerate(ys)], "nn", res=(h, gatefac))
    return h_new, (h, u, proj, kv, ys, ret_st, xbc, ssm_st, mixed, gatefac)


def _mixer_bwd(tag, g, saved, P, mod3, w_in, w_out, tables):
    h, u, proj, kv, ys, ret_st, xbc, ssm_st, mixed, gatefac = saved
    S = h.shape[0]
    d_mixed, s_gate = _gate_bwd(tag + "_gate_bwd", g, mixed, gatefac)
    dmix = _mm(tag + "_dmix", [(d_mixed, w_out)], "nt", F32, tm=512, tn=512, tk=1024)
    dw_out = jnp.concatenate(
        [_mm(tag + f"_dw_out{i}", [(y, d_mixed)], "tn", BF16, tm=512, tn=1024, tk=2048) for i, y in enumerate(ys)], axis=0)
    d_ret, d_gn = _ret_bwd(tag + "_ret_bwd", proj, P["ret_gn"], tables, ret_st, dmix)
    dq, dk, dv = _sb_bwd(tag + "_sb_bwd", proj, kv, dmix)
    dxbc, dz, ddt, s_ssd = _ssd_bwd(tag + "_ssd_bwd", xbc, proj, P["dt_bias"], P["a_log"], P["d_skip"], P["ssm_norm"],
                                    ssm_st, dmix)
    dp, s_conv = _conv_bwd_act(tag + "_conv_bwd_act", proj, P["conv_w"], P["conv_b"], dxbc)
    dxr = _conv_bwd_in(tag + "_conv_bwd_in", dp, P["conv_w"])
    dproj = jnp.concatenate(
        [d_ret[0], d_ret[1], d_ret[2], d_ret[3], dq, dk.astype(BF16), dv.astype(BF16), dz, dxr, ddt,
         jnp.zeros((S, IN_WP - DT_COL - LANES), BF16)], axis=1)
    dw_in = _mm(tag + "_dw_in", [(u, dproj)], "tn", BF16, tm=1024, tn=768, tk=2048)
    g_in, s_norm = _rowmm(tag + "_du", [(dproj, None, w_in, None)], "nt", norm_bwd=(h, P["norm_mix"], mod3[1:2], g))
    dmod3 = jnp.concatenate([s_norm[0:1], s_norm[1:2], s_gate[0:1]], axis=0)
    small = dict(norm_mix=s_norm[2], conv_w=s_conv[0:4], conv_b=s_conv[4], dt_bias=s_ssd[1, 0:8],
                 a_log=s_ssd[1, LANES:LANES + 8], d_skip=s_ssd[1, 2 * LANES:2 * LANES + 8],
                 ret_gn=d_gn[0], ssm_norm=s_ssd[0])
    return g_in, dw_in, dw_out, dmod3, small


def kernel(x, c, ada_w, ada_b, norm_ffn1, ffn1_wg, ffn1_wu, ffn1_wd, norm_mix, w_in, conv_w, conv_b, dt_bias, a_log, d_skip, ret_gn, ssm_norm, w_out, norm_ffn2, ffn2_wg, ffn2_wu, ffn2_wd, final_ada_w, final_ada_b, final_norm, loss_target, m_ada_w, m_ada_b, m_norm_ffn1, m_ffn1_wg, m_ffn1_wu, m_ffn1_wd, m_norm_mix, m_w_in, m_conv_w, m_conv_b, m_dt_bias, m_a_log, m_d_skip, m_ret_gn, m_ssm_norm, m_w_out, m_norm_ffn2, m_ffn2_wg, m_ffn2_wu, m_ffn2_wd, m_final_ada_w, m_final_ada_b, m_final_norm, v_ada_w, v_ada_b, v_norm_ffn1, v_ffn1_wg, v_ffn1_wu, v_ffn1_wd, v_norm_mix, v_w_in, v_conv_w, v_conv_b, v_dt_bias, v_a_log, v_d_skip, v_ret_gn, v_ssm_norm, v_w_out, v_norm_ffn2, v_ffn2_wg, v_ffn2_wu, v_ffn2_wd, v_final_ada_w, v_final_ada_b, v_final_norm):
    W = dict(ada_w=ada_w, ada_b=ada_b, norm_ffn1=norm_ffn1, ffn1_wg=ffn1_wg, ffn1_wu=ffn1_wu, ffn1_wd=ffn1_wd,
             norm_mix=norm_mix, w_in=w_in, conv_w=conv_w, conv_b=conv_b, dt_bias=dt_bias, a_log=a_log, d_skip=d_skip,
             ret_gn=ret_gn, ssm_norm=ssm_norm, w_out=w_out, norm_ffn2=norm_ffn2, ffn2_wg=ffn2_wg, ffn2_wu=ffn2_wu,
             ffn2_wd=ffn2_wd, final_ada_w=final_ada_w, final_ada_b=final_ada_b, final_norm=final_norm)
    M = dict(ada_w=m_ada_w, ada_b=m_ada_b, norm_ffn1=m_norm_ffn1, ffn1_wg=m_ffn1_wg, ffn1_wu=m_ffn1_wu,
             ffn1_wd=m_ffn1_wd, norm_mix=m_norm_mix, w_in=m_w_in, conv_w=m_conv_w, conv_b=m_conv_b, dt_bias=m_dt_bias,
             a_log=m_a_log, d_skip=m_d_skip, ret_gn=m_ret_gn, ssm_norm=m_ssm_norm, w_out=m_w_out,
             norm_ffn2=m_norm_ffn2, ffn2_wg=m_ffn2_wg, ffn2_wu=m_ffn2_wu, ffn2_wd=m_ffn2_wd,
             final_ada_w=m_final_ada_w, final_ada_b=m_final_ada_b, final_norm=m_final_norm)
    V = dict(ada_w=v_ada_w, ada_b=v_ada_b, norm_ffn1=v_norm_ffn1, ffn1_wg=v_ffn1_wg, ffn1_wu=v_ffn1_wu,
             ffn1_wd=v_ffn1_wd, norm_mix=v_norm_mix, w_in=v_w_in, conv_w=v_conv_w, conv_b=v_conv_b, dt_bias=v_dt_bias,
             a_log=v_a_log, d_skip=v_d_skip, ret_gn=v_ret_gn, ssm_norm=v_ssm_norm, w_out=v_w_out,
             norm_ffn2=v_norm_ffn2, ffn2_wg=v_ffn2_wg, ffn2_wu=v_ffn2_wu, ffn2_wd=v_ffn2_wd,
             final_ada_w=v_final_ada_w, final_ada_b=v_final_ada_b, final_norm=v_final_norm)
    D = D_MODEL
    S = x.shape[1]
    ax, ay, ac = _place()
    me = 4 * ax + 2 * ay + ac
    chip = 2 * ax + ay
    h0 = x[0]

    c_all = _gather_small("gather_c", jnp.pad(c, ((0, 7), (0, 0))))[:, 0, :]
    cond_all = c_all * jax.nn.sigmoid(c_all)
    nmod = 3 * 3 * D // 4
    mod_part = jnp.concatenate(
        [_mm(f"mod_proj{l}", [(cond_all, ada_w[l])], "nn", F32, tm=8, tn=768, tk=D) for l in range(DEPTH)]
        + [_mm("mod_proj_final", [(cond_all, final_ada_w)], "nn", F32, tm=8, tn=512, tk=D),
           conv_w.reshape(DEPTH * SSM_CONV, -1)], axis=1)
    gathered = _gather_small("gather_mod", mod_part)[0::2]
    mine = lax.dynamic_index_in_dim(gathered, me, axis=1, keepdims=False)
    mods = [(jnp.reshape(mine[:, l * nmod:(l + 1) * nmod], (-1,)) + ada_b[l]).reshape(9, D) for l in range(DEPTH)]
    fmod = (jnp.reshape(mine[:, DEPTH * nmod:DEPTH * nmod + 2 * D // 4], (-1,)) + final_ada_b).reshape(2, D)
    conv_full = jnp.transpose(gathered[:, :, DEPTH * nmod + 2 * D // 4:], (1, 0, 2)).reshape(DEPTH, SSM_CONV, -1)

    shards = [W[n].astype(BF16) for n in BIG]
    fetched = _chip_exchange("gather_weights", shards, gather=True)
    swapped = _sibling_swap("gather_weights_sibling", fetched)
    full = []
    for l in range(DEPTH):
        fw = {}
        for i, n in enumerate(BIG):
            arr = jnp.where(ac == l, fetched[i], swapped[i])
            fw[n] = lax.dynamic_update_index_in_dim(arr, shards[i][l], chip, 0)
        fw["w_in"] = jnp.pad(jnp.transpose(fw["w_in"], (1, 0, 2)).reshape(D, IN_W), ((0, 0), (0, IN_WP - IN_W)))
        fw["w_out"] = fw["w_out"].reshape(MIX_W, D)
        full.append(fw)

    tables = _ret_tables(S)
    small_p = []
    for l in range(DEPTH):
        small_p.append(dict(
            norm_ffn1=norm_ffn1[l:l + 1], norm_mix=norm_mix[l:l + 1], norm_ffn2=norm_ffn2[l:l + 1],
            ret_gn=ret_gn[l:l + 1], ssm_norm=ssm_norm[l:l + 1], conv_w=conv_full[l], conv_b=conv_b[l:l + 1],
            dt_bias=_pad_lanes(dt_bias[l]), a_log=_pad_lanes(a_log[l]), d_skip=_pad_lanes(d_skip[l])))

    h = h0
    saved = []
    for l in range(DEPTH):
        P, fw, mod = small_p[l], full[l], mods[l]
        h, s1 = _ffn_fwd(f"l{l}_ffn1", h, P["norm_ffn1"], mod[0:3], fw["ffn1_wg"], fw["ffn1_wu"], fw["ffn1_wd"])
        h, sm = _mixer_fwd(f"l{l}_mix", h, P, mod[3:6], fw["w_in"], fw["w_out"], tables)
        h, s2 = _ffn_fwd(f"l{l}_ffn2", h, P["norm_ffn2"], mod[6:9], fw["ffn2_wg"], fw["ffn2_wu"], fw["ffn2_wd"])
        saved.append((s1, sm, s2))
    fgain = final_norm.reshape(1, D)
    dy, loss_rows = _final_loss("final_loss", h, loss_target[0], fgain, fmod[0:1], fmod[1:2])

    g, s_final = _norm_mod_bwd("final_norm_bwd", dy, h, fgain, fmod[1:2], jnp.zeros_like(dy))
    dfmod = s_final[0:2]
    small_g = {n: [None] * DEPTH for n in SMALL}
    big_g = [None] * DEPTH
    dmods = [None] * DEPTH
    for l in reversed(range(DEPTH)):
        P, fw, mod = small_p[l], full[l], mods[l]
        s1, sm, s2 = saved[l]
        g, (dwg2, dwu2, dwd2), dm2, dn2 = _ffn_bwd(f"l{l}_ffn2", g, s2, P["norm_ffn2"], mod[6:9],
                                                   fw["ffn2_wg"], fw["ffn2_wu"], fw["ffn2_wd"])
        g, dw_in, dw_out, dmm, sg = _mixer_bwd(f"l{l}_mix", g, sm, P, mod[3:6], fw["w_in"], fw["w_out"], tables)
        g, (dwg1, dwu1, dwd1), dm1, dn1 = _ffn_bwd(f"l{l}_ffn1", g, s1, P["norm_ffn1"], mod[0:3],
                                                   fw["ffn1_wg"], fw["ffn1_wu"], fw["ffn1_wd"])
        dmods[l] = jnp.concatenate([dm1, dmm, dm2], axis=0)
        big_g[l] = dict(ffn1_wg=dwg1, ffn1_wu=dwu1, ffn1_wd=dwd1,
                        w_in=jnp.transpose(dw_in[:, :IN_W].reshape(D, 4, IN_W // 4), (1, 0, 2)),
                        w_out=dw_out.reshape(4, MIX_W // 4, D), ffn2_wg=dwg2, ffn2_wu=dwu2, ffn2_wd=dwd2)
        sg.update(norm_ffn1=dn1, norm_ffn2=dn2)
        for n, val in sg.items():
            small_g[n][l] = val
    grad_x = g[None]

    n_mod = DEPTH * 9 * D + 2 * D
    vec = [jnp.stack(dmods).reshape(-1), dfmod.reshape(-1)]
    layered = [n for n in SMALL if n not in ("ada_b", "final_ada_b", "final_norm")]
    vec += [jnp.stack(small_g[n]).reshape(-1) for n in layered]
    vec += [s_final[2], jnp.sum(loss_rows[0]).reshape(1)]
    flat = jnp.concatenate(vec)
    vrows = -(-flat.shape[0] // (8 * PACK_C)) * 8
    slab = jnp.pad(flat, (0, vrows * PACK_C - flat.shape[0])).reshape(vrows, PACK_C)
    slabs = _gather_small("gather_small_grads", slab)
    total = _sum_devices("sum_small_grads", slabs).reshape(-1)
    grads = {}
    grads["ada_b"] = total[:DEPTH * 9 * D].reshape(DEPTH, 9 * D)
    grads["final_ada_b"] = total[DEPTH * 9 * D:n_mod]
    off = n_mod
    for n in layered:
        shp = (DEPTH,) + ((SSM_CONV, D) if n == "conv_w" else W[n].shape[1:])
        cnt = math.prod(shp)
        grads[n] = total[off:off + cnt].reshape(shp)
        off += cnt
    grads["final_norm"] = total[off:off + D]
    loss = total[off + D]
    grads["conv_w"] = lax.dynamic_slice_in_dim(grads["conv_w"], chip * (D // 4), D // 4, axis=2)

    dmod_all = slabs[:, :n_mod // PACK_C, :].reshape(8, n_mod)
    grads["ada_w"] = jnp.stack([
        _mm(f"grad_ada_w{l}", [(cond_all, lax.dynamic_slice_in_dim(dmod_all, l * 9 * D + chip * nmod, nmod, axis=1))],
            "tn", F32, tm=D, tn=768, tk=8) for l in range(DEPTH)])
    grads["final_ada_w"] = _mm(
        "grad_final_ada_w",
        [(cond_all, lax.dynamic_slice_in_dim(dmod_all, DEPTH * 9 * D + chip * (2 * D // 4), 2 * D // 4, axis=1))],
        "tn", F32, tm=D, tn=512, tk=8)

    g0, g1 = [big_g[0][n] for n in BIG], [big_g[1][n] for n in BIG]
    from_sibling = _sibling_swap("reduce_grads_sibling", g0, alt=g1)
    chip_sum = _add_layers("reduce_grads_add", g0, g1, from_sibling, ac)
    from_chips = _chip_exchange("reduce_grads_chips", chip_sum, gather=False)
    reduced = _sum_chips("reduce_grads_sum", from_chips, chip_sum, chip)
    reduced_other = _sibling_swap("reduce_grads_complete", reduced)

    delta, new_m, new_v = {}, {}, {}
    for i, n in enumerate(BIG):
        grads[n], delta[n], new_m[n], new_v[n] = _adamw_layers("adamw_" + n, W[n], reduced[i], reduced_other[i],
                                                               M[n], V[n], ac)
    delta["ada_w"], new_m["ada_w"], new_v["ada_w"] = _adamw("adamw_ada_w", ada_w, grads["ada_w"], m_ada_w, v_ada_w)
    outs = _adamw("adamw_final_ada_w", final_ada_w[None], grads["final_ada_w"][None], m_final_ada_w[None],
                  v_final_ada_w[None])
    delta["final_ada_w"], new_m["final_ada_w"], new_v["final_ada_w"] = [o[0] for o in outs]
    small_shapes = [W[n].shape for n in SMALL]
    n_small = sum(math.prod(s) for s in small_shapes)
    srows = -(-n_small // (8 * LANES)) * 8
    slab_of = lambda T_: jnp.pad(jnp.concatenate([T_[n].reshape(-1) for n in SMALL]),
                                 (0, srows * LANES - n_small)).reshape(1, srows, LANES)
    outs = _adamw("adamw_small", slab_of(W), slab_of(grads), slab_of(M), slab_of(V))
    for res, o in zip((delta, new_m, new_v), outs):
        for n, val in zip(SMALL, _unpack(o[0], small_shapes)):
            res[n] = val

    return (loss, grad_x, *[grads[n] for n in WEIGHTS], *[delta[n] for n in WEIGHTS],
            *[new_m[n] for n in WEIGHTS], *[new_v[n] for n in WEIGHTS])
```

```python
import functools
import math

import jax
import jax.numpy as jnp
from jax import lax
from jax.experimental import pallas as pl
from jax.experimental.pallas import tpu as pltpu

F32 = jnp.float32
BF16 = jnp.bfloat16
MESH = pl.DeviceIdType.MESH

D_MODEL = 1024
DEPTH = 2
D_FF = 2816
RET_HEADS = 4
SB_HEADS = 4
HEAD_DIM = 128
SSM_HEADS = 8
SSM_P = 64
SSM_N = 128
SSM_GROUPS = 2
SSM_CONV = 4
MIX_W = 1536
IN_W = 5128
IN_WP = 5376
DT_COL = 5120
ROPE_BASE = 10000.0
NORM_EPS = 1e-6
ADAM_LR = 0.001
ADAM_B1 = 0.9
ADAM_B2 = 0.999
ADAM_EPS = 1e-08
ADAM_WD = 0.01
ADAM_STEP = 10

LANES = 128
VMEM_LIMIT = 56 * 1024 * 1024
RET_T = 256
SB_T = 256
SB_HPS = 2
SSD_T = 256
CONV_T = 512
ROW_T = 512
SB_DEAD = -150.0

NN = (((1,), (0,)), ((), ()))
NT = (((1,), (1,)), ((), ()))
TN = (((0,), (0,)), ((), ()))


def _dot(a, b, dims=NN):
    return lax.dot_general(a.astype(BF16), b.astype(BF16), dims, preferred_element_type=F32)


def _params(sem):
    return pltpu.CompilerParams(dimension_semantics=sem, vmem_limit_bytes=VMEM_LIMIT)


def _sigmoid(x):
    return 1.0 / (1.0 + jnp.exp(-x))


def _split_dot(mat01, x, dims=NN, lhs01=True, pieces=3):
    m = mat01.astype(BF16)
    total, rest = None, x
    for _ in range(pieces):
        p = rest.astype(BF16)
        rest = rest - p.astype(F32)
        d = lax.dot_general(m, p, dims, preferred_element_type=F32) if lhs01 else lax.dot_general(
            p, m, dims, preferred_element_type=F32)
        total = d if total is None else total + d
    return total


def _mm(name, terms, mode, out_dtype, tm=512, tn=512, tk=1024, n_outer=False):
    a0, b0 = terms[0]
    if mode == "nn":
        (M, K), N = a0.shape, b0.shape[1]
    elif mode == "nt":
        (M, K), N = a0.shape, b0.shape[0]
    else:
        (K, M), N = a0.shape, b0.shape[1]
    tm, tn, tk = min(tm, M), min(tn, N), min(tk, K)
    assert M % tm == 0 and N % tn == 0 and K % tk == 0, (name, M, N, K, tm, tn, tk)
    nk = K // tk
    nterm = len(terms)
    dims = {"nn": NN, "nt": NT, "tn": TN}[mode]

    def body(*refs):
        o_ref, acc = refs[2 * nterm], refs[2 * nterm + 1]
        part = None
        for t in range(nterm):
            p = _dot(refs[2 * t][...], refs[2 * t + 1][...], dims)
            part = p if part is None else part + p
        if nk == 1:
            o_ref[...] = part.astype(o_ref.dtype)
        else:
            k = pl.program_id(2)

            @pl.when(k == 0)
            def _():
                acc[...] = part

            @pl.when(k > 0)
            def _():
                acc[...] += part

            @pl.when(k == nk - 1)
            def _():
                o_ref[...] = acc[...].astype(o_ref.dtype)

    ij = (lambda g0, g1: (g1, g0)) if n_outer else (lambda g0, g1: (g0, g1))
    if mode == "nn":
        a_spec = pl.BlockSpec((tm, tk), lambda g0, g1, k: (ij(g0, g1)[0], k))
        b_spec = pl.BlockSpec((tk, tn), lambda g0, g1, k: (k, ij(g0, g1)[1]))
    elif mode == "nt":
        a_spec = pl.BlockSpec((tm, tk), lambda g0, g1, k: (ij(g0, g1)[0], k))
        b_spec = pl.BlockSpec((tn, tk), lambda g0, g1, k: (ij(g0, g1)[1], k))
    else:
        a_spec = pl.BlockSpec((tk, tm), lambda g0, g1, k: (k, ij(g0, g1)[0]))
        b_spec = pl.BlockSpec((tk, tn), lambda g0, g1, k: (k, ij(g0, g1)[1]))
    flat = [r for ab in terms for r in ab]
    return pl.pallas_call(
        body, name=name, grid=(N // tn, M // tm, nk) if n_outer else (M // tm, N // tn, nk),
        in_specs=[a_spec, b_spec] * nterm,
        out_specs=pl.BlockSpec((tm, tn), lambda g0, g1, k: ij(g0, g1)),
        out_shape=jax.ShapeDtypeStruct((M, N), out_dtype),
        scratch_shapes=[pltpu.VMEM((tm, tn) if nk > 1 else (8, LANES), F32)],
        compiler_params=_params(("parallel", "parallel", "arbitrary")),
    )(*flat)


def _norm_bwd_rows(d, x, gain, scale, sums_ref):
    r = lax.rsqrt(jnp.mean(x * x, axis=-1, keepdims=True) + NORM_EPS)
    xh = x * r
    dn = d * (1.0 + scale)
    sums_ref[0:1, :] += jnp.sum(d, axis=0, keepdims=True)
    sums_ref[1:2, :] += jnp.sum(d * xh * gain, axis=0, keepdims=True)
    sums_ref[2:3, :] += jnp.sum(dn * xh, axis=0, keepdims=True)
    dxh = dn * gain
    return r * (dxh - xh * jnp.mean(dxh * xh, axis=-1, keepdims=True))


def _rowmm(name, terms, mode, out_dtype=F32, res=None, norm_bwd=None):
    S = terms[0][0].shape[-2]
    N = terms[0][2].shape[-1] if mode == "nn" else terms[0][2].shape[-2]
    nterm = len(terms)
    tm = ROW_T
    dims = NN if mode == "nn" else NT

    def body(*refs):
        o = None
        for t in range(nterm):
            p = _dot(refs[2 * t][...], refs[2 * t + 1][...], dims)
            o = p if o is None else o + p
        if norm_bwd is not None:
            h_ref, g_ref, sc_ref, gr_ref, dh_ref, sums_ref = refs[2 * nterm:2 * nterm + 6]

            @pl.when(pl.program_id(0) == 0)
            def _():
                sums_ref[...] = jnp.zeros_like(sums_ref)

            dh_ref[...] = gr_ref[...] + _norm_bwd_rows(o, h_ref[...], g_ref[...], sc_ref[...], sums_ref)
        elif res is None:
            refs[2 * nterm][...] = o.astype(out_dtype)
        else:
            h_ref, gf_ref, out_ref, hn_ref = refs[2 * nterm:2 * nterm + 4]
            out_ref[...] = o.astype(out_ref.dtype)
            hn_ref[...] = h_ref[...] + gf_ref[...] * o

    in_specs, flat = [], []
    for a, ai, w, wi in terms:
        if ai is None:
            in_specs.append(pl.BlockSpec((tm, a.shape[1]), lambda i: (i, 0)))
        else:
            in_specs.append(pl.BlockSpec((None, tm, a.shape[2]), lambda i, g=ai: (g, i, 0)))
        if wi is None:
            in_specs.append(pl.BlockSpec(w.shape, lambda i: (0, 0)))
        else:
            in_specs.append(pl.BlockSpec((None,) + w.shape[1:], lambda i, g=wi: (g, 0, 0)))
        flat += [a, w]
    row = pl.BlockSpec((tm, N), lambda i: (i, 0))
    vec = pl.BlockSpec((1, N), lambda i: (0, 0))
    if norm_bwd is not None:
        return pl.pallas_call(
            body, name=name, grid=(S // tm,), in_specs=in_specs + [row, vec, vec, row],
            out_specs=[row, pl.BlockSpec((8, N), lambda i: (0, 0))],
            out_shape=[jax.ShapeDtypeStruct((S, N), F32), jax.ShapeDtypeStruct((8, N), F32)],
            compiler_params=_params(("arbitrary",)),
        )(*flat, *norm_bwd)
    if res is None:
        return pl.pallas_call(
            body, name=name, grid=(S // tm,), in_specs=in_specs, out_specs=row,
            out_shape=jax.ShapeDtypeStruct((S, N), out_dtype), compiler_params=_params(("parallel",)),
        )(*flat)
    h, gatefac = res
    return pl.pallas_call(
        body, name=name, grid=(S // tm,),
        in_specs=in_specs + [row, pl.BlockSpec((1, N), lambda i: (0, 0))], out_specs=[row, row],
        out_shape=[jax.ShapeDtypeStruct((S, N), BF16), jax.ShapeDtypeStruct((S, N), F32)],
        compiler_params=_params(("parallel",)),
    )(*flat, h, gatefac)


def _mm_tn_groups(name, a, b, out_dtype, tk=2048):
    G = a.shape[0] if a.ndim == 3 else b.shape[0]
    S, M, N = a.shape[-2], a.shape[-1], b.shape[-1]
    tk = min(tk, S)
    nk = S // tk

    def body(a_ref, b_ref, o_ref, acc):
        k = pl.program_id(1)
        part = _dot(a_ref[...], b_ref[...], TN)

        @pl.when(k == 0)
        def _():
            acc[...] = part

        @pl.when(k > 0)
        def _():
            acc[...] += part

        @pl.when(k == nk - 1)
        def _():
            o_ref[...] = acc[...].astype(o_ref.dtype)

    def spec(arr, width):
        if arr.ndim == 3:
            return pl.BlockSpec((None, tk, width), lambda g, k: (g, k, 0))
        return pl.BlockSpec((tk, width), lambda g, k: (k, 0))

    return pl.pallas_call(
        body, name=name, grid=(G, nk), in_specs=[spec(a, M), spec(b, N)],
        out_specs=pl.BlockSpec((None, M, N), lambda g, k: (g, 0, 0)),
        out_shape=jax.ShapeDtypeStruct((G, M, N), out_dtype),
        scratch_shapes=[pltpu.VMEM((M, N), F32)],
        compiler_params=_params(("parallel", "arbitrary")),
    )(a, b)


def _norm_mod(name, h, gain, shift, scale):
    S, D = h.shape
    tm = ROW_T

    def body(h_ref, g_ref, sh_ref, sc_ref, u_ref):
        x = h_ref[...]
        r = lax.rsqrt(jnp.mean(x * x, axis=-1, keepdims=True) + NORM_EPS)
        u_ref[...] = (x * r * g_ref[...] * (1.0 + sc_ref[...]) + sh_ref[...]).astype(u_ref.dtype)

    vec = pl.BlockSpec((1, D), lambda i: (0, 0))
    return pl.pallas_call(
        body, name=name, grid=(S // tm,),
        in_specs=[pl.BlockSpec((tm, D), lambda i: (i, 0)), vec, vec, vec],
        out_specs=pl.BlockSpec((tm, D), lambda i: (i, 0)),
        out_shape=jax.ShapeDtypeStruct((S, D), BF16),
        compiler_params=_params(("parallel",)),
    )(h, gain, shift, scale)


def _norm_mod_bwd(name, du, h, gain, scale, gres):
    S, D = h.shape
    tm = ROW_T

    def body(du_ref, h_ref, g_ref, sc_ref, gr_ref, dh_ref, sums_ref):
        @pl.when(pl.program_id(0) == 0)
        def _():
            sums_ref[...] = jnp.zeros_like(sums_ref)

        dh_ref[...] = gr_ref[...] + _norm_bwd_rows(du_ref[...].astype(F32), h_ref[...], g_ref[...], sc_ref[...],
                                                   sums_ref)

    row = pl.BlockSpec((tm, D), lambda i: (i, 0))
    vec = pl.BlockSpec((1, D), lambda i: (0, 0))
    return pl.pallas_call(
        body, name=name, grid=(S // tm,),
        in_specs=[row, row, vec, vec, row],
        out_specs=[row, pl.BlockSpec((8, D), lambda i: (0, 0))],
        out_shape=[jax.ShapeDtypeStruct((S, D), F32), jax.ShapeDtypeStruct((8, D), F32)],
        compiler_params=_params(("arbitrary",)),
    )(du, h, gain, scale, gres)


def _gate_bwd(name, g, out, gatefac):
    S, D = g.shape
    tm = ROW_T

    def body(g_ref, o_ref, gf_ref, d_ref, sums_ref):
        @pl.when(pl.program_id(0) == 0)
        def _():
            sums_ref[...] = jnp.zeros_like(sums_ref)

        gv = g_ref[...]
        d_ref[...] = (gv * gf_ref[...]).astype(d_ref.dtype)
        sums_ref[0:1, :] += jnp.sum(gv * o_ref[...].astype(F32), axis=0, keepdims=True)

    row = pl.BlockSpec((tm, D), lambda i: (i, 0))
    return pl.pallas_call(
        body, name=name, grid=(S // tm,),
        in_specs=[row, row, pl.BlockSpec((1, D), lambda i: (0, 0))],
        out_specs=[row, pl.BlockSpec((8, D), lambda i: (0, 0))],
        out_shape=[jax.ShapeDtypeStruct((S, D), BF16), jax.ShapeDtypeStruct((8, D), F32)],
        compiler_params=_params(("arbitrary",)),
    )(g, out, gatefac)


def _final_loss(name, h, target, gain, shift, scale):
    S, D = h.shape
    tm = ROW_T

    def body(h_ref, t_ref, g_ref, sh_ref, sc_ref, dy_ref, loss_ref):
        @pl.when(pl.program_id(0) == 0)
        def _():
            loss_ref[...] = jnp.zeros_like(loss_ref)

        x = h_ref[...]
        r = lax.rsqrt(jnp.mean(x * x, axis=-1, keepdims=True) + NORM_EPS)
        y = x * r * g_ref[...] * (1.0 + sc_ref[...]) + sh_ref[...]
        e = y - t_ref[...]
        dy_ref[...] = e * (1.0 / D)
        loss_ref[0:1, :] += 0.5 * jnp.sum(e * e, axis=0, keepdims=True) * (1.0 / D)

    row = pl.BlockSpec((tm, D), lambda i: (i, 0))
    vec = pl.BlockSpec((1, D), lambda i: (0, 0))
    return pl.pallas_call(
        body, name=name, grid=(S // tm,),
        in_specs=[row, row, vec, vec, vec],
        out_specs=[row, pl.BlockSpec((8, D), lambda i: (0, 0))],
        out_shape=[jax.ShapeDtypeStruct((S, D), F32), jax.ShapeDtypeStruct((8, D), F32)],
        compiler_params=_params(("arbitrary",)),
    )(h, target, gain, shift, scale)


def _ffn_up(name, u, wg, wu):
    S, D = u.shape
    G, Fg, _ = wg.shape
    tm = ROW_T

    def body(u_ref, wg_ref, wu_ref, a_ref, b_ref, hm_ref):
        uv = u_ref[...]
        a = _dot(uv, wg_ref[...], NT)
        b = _dot(uv, wu_ref[...], NT)
        sg = _sigmoid(a)
        act = a * sg
        a_ref[...] = (b * sg * (1.0 + a * (1.0 - sg))).astype(a_ref.dtype)
        b_ref[...] = act.astype(b_ref.dtype)
        hm_ref[...] = (act * b).astype(hm_ref.dtype)

    w_spec = pl.BlockSpec((None, Fg, D), lambda g, i: (g, 0, 0))
    o_spec = pl.BlockSpec((None, tm, Fg), lambda g, i: (g, i, 0))
    return pl.pallas_call(
        body, name=name, grid=(G, S // tm),
        in_specs=[pl.BlockSpec((tm, D), lambda g, i: (i, 0)), w_spec, w_spec],
        out_specs=[o_spec] * 3,
        out_shape=[jax.ShapeDtypeStruct((G, S, Fg), BF16)] * 3,
        compiler_params=_params(("parallel", "parallel")),
    )(u, wg, wu)


def _ffn_bwd_mid(name, d_out, wd, a, b):
    S, D = d_out.shape
    G, Fg, _ = wd.shape
    tm = ROW_T

    def body(d_ref, wd_ref, a_ref, b_ref, da_ref, db_ref):
        dhm = _dot(d_ref[...], wd_ref[...], NT)
        da_ref[...] = (dhm * a_ref[...].astype(F32)).astype(da_ref.dtype)
        db_ref[...] = (dhm * b_ref[...].astype(F32)).astype(db_ref.dtype)

    t_spec = pl.BlockSpec((None, tm, Fg), lambda g, i: (g, i, 0))
    return pl.pallas_call(
        body, name=name, grid=(G, S // tm),
        in_specs=[pl.BlockSpec((tm, D), lambda g, i: (i, 0)), pl.BlockSpec((None, Fg, D), lambda g, i: (g, 0, 0)),
                  t_spec, t_spec],
        out_specs=[t_spec] * 2,
        out_shape=[jax.ShapeDtypeStruct((G, S, Fg), BF16)] * 2,
        compiler_params=_params(("parallel", "parallel")),
    )(d_out, wd, a, b)


def _ret_tables(S):
    T = RET_T
    half = HEAD_DIM // 2
    inv_freq = ROPE_BASE ** (-jnp.arange(half, dtype=F32) / half)
    ang = jnp.arange(S, dtype=F32)[:, None] * inv_freq[None, :]
    cos, sin = jnp.cos(ang), jnp.sin(ang)
    cosf = jnp.concatenate([cos, cos], axis=-1)
    sinf = jnp.concatenate([-sin, sin], axis=-1)
    log_gamma = jnp.log1p(-(2.0 ** (-5.0 - jnp.arange(RET_HEADS, dtype=F32))))
    idx = jnp.arange(T, dtype=F32)
    chunk = jnp.arange(T) // 64
    vis = (chunk[None, :] <= chunk[:, None]).astype(F32)
    mask = jnp.exp(log_gamma[:, None, None] * jnp.abs(idx[:, None] - idx[None, :])) * vis[None]
    ones = jnp.ones((1, 1, LANES), F32)
    qdec = jnp.exp(log_gamma[:, None] * (idx + 1.0)[None, :])[:, :, None] * ones
    kdec = jnp.exp(log_gamma[:, None] * (T - 1.0 - idx)[None, :])[:, :, None] * ones
    cdec = jnp.exp(log_gamma * T)[:, None, None] * jnp.ones((1, 8, LANES), F32)
    return cosf, sinf, mask, qdec, kdec, cdec


def _rope(x, cosf, sinf):
    return x * cosf + pltpu.roll(x, HEAD_DIM // 2, 1) * sinf


def _rope_bwd(d, cosf, sinf):
    return d * cosf + pltpu.roll(d * sinf, HEAD_DIM // 2, 1)


def _ret_specs(T, rev_nb=None):
    if rev_nb is None:
        blk = lambda b: b
    else:
        blk = lambda b: rev_nb - 1 - b
    proj = lambda off: pl.BlockSpec((T, HEAD_DIM), lambda h, b: (blk(b), off + h))
    rows = pl.BlockSpec((T, HEAD_DIM), lambda h, b: (blk(b), 0))
    per_head = lambda shape: pl.BlockSpec((1,) + shape, lambda h, b: (h, 0, 0))
    return ([proj(0), proj(4), proj(8), proj(12), rows, rows,
             per_head((T, T)), per_head((T, LANES)), per_head((T, LANES)), per_head((8, LANES)),
             pl.BlockSpec((1, HEAD_DIM), lambda h, b: (0, h))], blk)


def _ret_fwd(name, proj, gn, tables):
    S = proj.shape[0]
    T = RET_T
    nb = S // T
    scale = HEAD_DIM ** -0.5
    specs, _ = _ret_specs(T)

    def body(q_ref, k_ref, v_ref, g_ref, cos_ref, sin_ref, m_ref, qd_ref, kd_ref, cd_ref, gn_ref, y_ref, st_ref, st):
        @pl.when(pl.program_id(1) == 0)
        def _():
            st[...] = jnp.zeros_like(st)

        cosf, sinf = cos_ref[...], sin_ref[...]
        qr = _rope(q_ref[...], cosf, sinf)
        kr = _rope(k_ref[...], cosf, sinf) * scale
        v = v_ref[...]
        sp = st[...]
        st_ref[0, 0] = sp
        w = _dot(qr, kr, NT) * m_ref[0]
        y = _dot(w, v) + _dot(qr * qd_ref[0], sp)
        st[...] = cd_ref[0, 0:1, :] * sp + _dot(kr * kd_ref[0], v, TN)
        r = lax.rsqrt(jnp.mean(y * y, axis=-1, keepdims=True) + NORM_EPS)
        g = g_ref[...]
        y_ref[...] = y * r * gn_ref[...] * (g * _sigmoid(g))

    return pl.pallas_call(
        body, name=name, grid=(RET_HEADS, nb), in_specs=specs,
        out_specs=[pl.BlockSpec((T, HEAD_DIM), lambda h, b: (b, h)),
                   pl.BlockSpec((1, 1, HEAD_DIM, HEAD_DIM), lambda h, b: (h, b, 0, 0))],
        out_shape=[jax.ShapeDtypeStruct((S, RET_HEADS * HEAD_DIM), F32),
                   jax.ShapeDtypeStruct((RET_HEADS, nb, HEAD_DIM, HEAD_DIM), F32)],
        scratch_shapes=[pltpu.VMEM((HEAD_DIM, HEAD_DIM), F32)],
        compiler_params=_params(("parallel", "arbitrary")),
    )(proj, proj, proj, proj, tables[0], tables[1], tables[2], tables[3], tables[4], tables[5], gn)


def _ret_bwd(name, proj, gn, tables, states, dmix):
    S = proj.shape[0]
    T = RET_T
    nb = S // T
    scale = HEAD_DIM ** -0.5
    specs, blk = _ret_specs(T, rev_nb=nb)
    specs = specs + [pl.BlockSpec((1, 1, HEAD_DIM, HEAD_DIM), lambda h, b: (h, blk(b), 0, 0)),
                     pl.BlockSpec((T, HEAD_DIM), lambda h, b: (blk(b), h))]

    def body(q_ref, k_ref, v_ref, g_ref, cos_ref, sin_ref, m_ref, qd_ref, kd_ref, cd_ref, gn_ref, st_ref, do_ref,
             d_ref, dgn_ref, dst):
        @pl.when(pl.program_id(1) == 0)
        def _():
            dst[...] = jnp.zeros_like(dst)
            dgn_ref[...] = jnp.zeros_like(dgn_ref)

        cosf, sinf = cos_ref[...], sin_ref[...]
        qr = _rope(q_ref[...], cosf, sinf)
        kr = _rope(k_ref[...], cosf, sinf) * scale
        v = v_ref[...]
        sp = st_ref[0, 0]
        mask, qd, kd = m_ref[0], qd_ref[0], kd_ref[0]
        w = _dot(qr, kr, NT) * mask
        y = _dot(w, v) + _dot(qr * qd, sp)
        r = lax.rsqrt(jnp.mean(y * y, axis=-1, keepdims=True) + NORM_EPS)
        yh = y * r
        gn_v = gn_ref[...]
        g = g_ref[...]
        sg = _sigmoid(g)
        do = do_ref[...]
        dyn = do * g * sg
        dgn_ref[...] += jnp.sum(dyn * yh, axis=0, keepdims=True)
        dyh = dyn * gn_v
        dy = r * (dyh - yh * jnp.mean(dyh * yh, axis=-1, keepdims=True))
        dg = do * yh * gn_v * sg * (1.0 + g * (1.0 - sg))
        ds = dst[...]
        dp = _dot(dy, v, NT) * mask
        dqr = _dot(dp, kr) + _dot(dy, sp, NT) * qd
        dkr = _dot(dp, qr, TN) + _dot(v, ds, NT) * kd
        dv = _dot(w, dy, TN) + _dot(kr * kd, ds)
        dst[...] = cd_ref[0, 0:1, :] * ds + _dot(qr * qd, dy, TN)
        d_ref[0] = _rope_bwd(dqr, cosf, sinf).astype(d_ref.dtype)
        d_ref[1] = _rope_bwd(dkr * scale, cosf, sinf).astype(d_ref.dtype)
        d_ref[2] = dv.astype(d_ref.dtype)
        d_ref[3] = dg.astype(d_ref.dtype)

    return pl.pallas_call(
        body, name=name, grid=(RET_HEADS, nb), in_specs=specs,
        out_specs=[pl.BlockSpec((4, T, HEAD_DIM), lambda h, b: (0, blk(b), h)),
                   pl.BlockSpec((1, HEAD_DIM), lambda h, b: (0, h))],
        out_shape=[jax.ShapeDtypeStruct((4, S, RET_HEADS * HEAD_DIM), BF16),
                   jax.ShapeDtypeStruct((1, RET_HEADS * HEAD_DIM), F32)],
        scratch_shapes=[pltpu.VMEM((HEAD_DIM, HEAD_DIM), F32)],
        compiler_params=_params(("parallel", "arbitrary")),
    )(proj, proj, proj, proj, tables[0], tables[1], tables[2], tables[3], tables[4], tables[5], gn, states, dmix)


def _sb_logits(qb, kb, i, j, scale):
    T = SB_T
    z = lax.dot_general(qb, kb, NT, preferred_element_type=F32) * scale
    row = lax.broadcasted_iota(jnp.int32, (T, T), 0)
    col = lax.broadcasted_iota(jnp.int32, (T, T), 1)
    vis = jnp.logical_or(j < i, col < row)
    lp = jnp.log1p(jnp.exp(-jnp.abs(z)))
    lb = jnp.minimum(z, 0.0) - lp
    lk = jnp.where(vis, -jnp.maximum(z, 0.0) - lp, 0.0)
    return lb, lk, vis


def _sb_weights(lb, lk, vis, tailc):
    T = SB_T
    row = lax.broadcasted_iota(jnp.int32, (T, T), 0)
    col = lax.broadcasted_iota(jnp.int32, (T, T), 1)
    tail = tailc + _split_dot(row > col, lk, lhs01=False, pieces=2)
    return jnp.where(vis, jnp.exp(lb + tail), 0.0)


def _sb_specs(S):
    T, W = SB_T, SB_HPS * HEAD_DIM
    return [pl.BlockSpec((T, W), lambda hp, i: (i, 2048 // W + hp)),
            pl.BlockSpec((S, W), lambda hp, i: (0, hp)),
            pl.BlockSpec((S, W), lambda hp, i: (0, 512 // W + hp))]


def _lanes(e):
    return slice(e * HEAD_DIM, (e + 1) * HEAD_DIM)


def _sb_fwd(name, proj, kv):
    S = proj.shape[0]
    T, E = SB_T, SB_HPS
    nq = S // T
    scale = HEAD_DIM ** -0.5

    def body(q_ref, k_ref, v_ref, y_ref):
        i = pl.program_id(1)
        qs = [q_ref[:, _lanes(e)].astype(BF16) for e in range(E)]

        def cond(c):
            return jnp.logical_and(c[0] >= 0, c[1] == 0)

        def step(c):
            j, _, tails, accs = c
            rows = pl.ds(pl.multiple_of(j * T, T), T)
            new_tails, new_accs, worst = [], [], None
            for e in range(E):
                lb, lk, vis = _sb_logits(qs[e], k_ref[rows, _lanes(e)], i, j, scale)
                w = _sb_weights(lb, lk, vis, tails[e])
                new_accs.append(accs[e] + lax.dot_general(w.astype(BF16), v_ref[rows, _lanes(e)], NN,
                                                          preferred_element_type=F32))
                t = tails[e] + jnp.sum(lk, axis=1, keepdims=True)
                new_tails.append(t)
                worst = jnp.max(t) if worst is None else jnp.maximum(worst, jnp.max(t))
            return j - 1, (worst < SB_DEAD).astype(jnp.int32), tuple(new_tails), tuple(new_accs)

        init = (i, jnp.int32(0), (jnp.zeros((T, 1), F32),) * E, (jnp.zeros((T, HEAD_DIM), F32),) * E)
        accs = lax.while_loop(cond, step, init)[3]
        for e in range(E):
            y_ref[:, _lanes(e)] = accs[e]

    return pl.pallas_call(
        body, name=name, grid=(SB_HEADS // E, nq), in_specs=_sb_specs(S),
        out_specs=pl.BlockSpec((T, E * HEAD_DIM), lambda hp, i: (i, hp)),
        out_shape=jax.ShapeDtypeStruct((S, SB_HEADS * HEAD_DIM), F32),
        compiler_params=_params(("parallel", "arbitrary")),
    )(proj, kv, kv)


def _sb_bwd(name, proj, kv, dmix):
    S = proj.shape[0]
    T, E = SB_T, SB_HPS
    nq = S // T
    scale = HEAD_DIM ** -0.5

    def body(q_ref, k_ref, v_ref, do_ref, dq_ref, dk_ref, dv_ref):
        i = pl.program_id(1)

        @pl.when(i == 0)
        def _():
            dk_ref[...] = jnp.zeros_like(dk_ref)
            dv_ref[...] = jnp.zeros_like(dv_ref)

        qs = [q_ref[:, _lanes(e)].astype(BF16) for e in range(E)]
        dos = [do_ref[:, _lanes(e)].astype(BF16) for e in range(E)]
        row = lax.broadcasted_iota(jnp.int32, (T, T), 0)
        col = lax.broadcasted_iota(jnp.int32, (T, T), 1)
        zcol = (jnp.zeros((T, 1), F32),) * E

        def cond(c):
            return jnp.logical_and(c[0] >= 0, c[1] == 0)

        def walk_left(c):
            j, _, tails = c
            rows = pl.ds(pl.multiple_of(j * T, T), T)
            new_tails, worst = [], None
            for e in range(E):
                _, lk, _ = _sb_logits(qs[e], k_ref[rows, _lanes(e)], i, j, scale)
                t = tails[e] + jnp.sum(lk, axis=1, keepdims=True)
                new_tails.append(t)
                worst = jnp.max(t) if worst is None else jnp.maximum(worst, jnp.max(t))
            return j - 1, (worst < SB_DEAD).astype(jnp.int32), tuple(new_tails)

        j_end, _, totals = lax.while_loop(cond, walk_left, (i, jnp.int32(0), zcol))

        def walk_back(j, c):
            lefts, used, dqs = c
            rows = pl.ds(pl.multiple_of(j * T, T), T)
            new_lefts, new_used, new_dqs = [], [], []
            for e in range(E):
                kb = k_ref[rows, _lanes(e)]
                lb, lk, vis = _sb_logits(qs[e], kb, i, j, scale)
                u = used[e] + jnp.sum(lk, axis=1, keepdims=True)
                w = _sb_weights(lb, lk, vis, totals[e] - u)
                de = lax.dot_general(dos[e], v_ref[rows, _lanes(e)], NT, preferred_element_type=F32) * w
                dlk = jnp.where(vis, lefts[e] + _split_dot(row < col, de, lhs01=False, pieces=2), 0.0)
                sg = jnp.exp(lb)
                dz = ((de * (1.0 - sg) - dlk * sg) * scale).astype(BF16)
                new_dqs.append(dqs[e] + lax.dot_general(dz, kb, NN, preferred_element_type=F32))
                dk_ref[rows, _lanes(e)] += lax.dot_general(dz, qs[e], TN, preferred_element_type=F32)
                dv_ref[rows, _lanes(e)] += lax.dot_general(w.astype(BF16), dos[e], TN, preferred_element_type=F32)
                new_lefts.append(lefts[e] + jnp.sum(de, axis=1, keepdims=True))
                new_used.append(u)
            return tuple(new_lefts), tuple(new_used), tuple(new_dqs)

        init = (zcol, zcol, (jnp.zeros((T, HEAD_DIM), F32),) * E)
        dqs = lax.fori_loop(j_end + 1, i + 1, walk_back, init)[2]
        for e in range(E):
            dq_ref[:, _lanes(e)] = dqs[e].astype(dq_ref.dtype)

    W = E * HEAD_DIM
    blk = pl.BlockSpec((T, W), lambda hp, i: (i, hp))
    full = pl.BlockSpec((S, W), lambda hp, i: (0, hp))
    shp = jax.ShapeDtypeStruct((S, SB_HEADS * HEAD_DIM), F32)
    return pl.pallas_call(
        body, name=name, grid=(SB_HEADS // E, nq),
        in_specs=_sb_specs(S) + [pl.BlockSpec((T, W), lambda hp, i: (i, 512 // W + hp))],
        out_specs=[blk, full, full],
        out_shape=[jax.ShapeDtypeStruct((S, SB_HEADS * HEAD_DIM), BF16), shp, shp],
        compiler_params=_params(("parallel", "arbitrary")),
    )(proj, kv, kv, dmix)


def _conv_fwd(name, proj, conv_w, conv_b):
    S = proj.shape[0]
    T = CONV_T
    C = 1024
    K = SSM_CONV

    def body(x_ref, w_ref, b_ref, o_ref, buf):
        @pl.when(pl.program_id(0) == 0)
        def _():
            buf[0:8, :] = jnp.zeros((8, C), F32)

        buf[8:T + 8, :] = x_ref[...]
        acc = b_ref[...] + w_ref[K - 1:K, :] * buf[8:T + 8, :]
        for k in range(K - 1):
            acc = acc + w_ref[k:k + 1, :] * buf[5 + k:5 + k + T, :]
        o_ref[...] = acc * _sigmoid(acc)
        buf[0:8, :] = buf[T:T + 8, :]

    return pl.pallas_call(
        body, name=name, grid=(S // T,),
        in_specs=[pl.BlockSpec((T, C), lambda i: (i, 4)), pl.BlockSpec((K, C), lambda i: (0, 0)),
                  pl.BlockSpec((1, C), lambda i: (0, 0))],
        out_specs=pl.BlockSpec((T, C), lambda i: (i, 0)),
        out_shape=jax.ShapeDtypeStruct((S, C), F32),
        scratch_shapes=[pltpu.VMEM((T + 8, C), F32)],
        compiler_params=_params(("arbitrary",)),
    )(proj, conv_w, conv_b)


def _conv_bwd_act(name, proj, conv_w, conv_b, dxbc):
    S = proj.shape[0]
    T = CONV_T
    C = 1024
    K = SSM_CONV

    def body(x_ref, w_ref, b_ref, d_ref, dp_ref, sums_ref, buf):
        @pl.when(pl.program_id(0) == 0)
        def _():
            buf[0:8, :] = jnp.zeros((8, C), F32)
            sums_ref[...] = jnp.zeros_like(sums_ref)

        buf[8:T + 8, :] = x_ref[...]
        acc = b_ref[...] + w_ref[K - 1:K, :] * buf[8:T + 8, :]
        for k in range(K - 1):
            acc = acc + w_ref[k:k + 1, :] * buf[5 + k:5 + k + T, :]
        sg = _sigmoid(acc)
        dp = d_ref[...] * sg * (1.0 + acc * (1.0 - sg))
        dp_ref[...] = dp
        for k in range(K):
            sums_ref[k:k + 1, :] += jnp.sum(dp * buf[5 + k:5 + k + T, :], axis=0, keepdims=True)
        sums_ref[4:5, :] += jnp.sum(dp, axis=0, keepdims=True)
        buf[0:8, :] = buf[T:T + 8, :]

    row = pl.BlockSpec((T, C), lambda i: (i, 0))
    return pl.pallas_call(
        body, name=name, grid=(S // T,),
        in_specs=[pl.BlockSpec((T, C), lambda i: (i, 4)), pl.BlockSpec((K, C), lambda i: (0, 0)),
                  pl.BlockSpec((1, C), lambda i: (0, 0)), row],
        out_specs=[row, pl.BlockSpec((8, C), lambda i: (0, 0))],
        out_shape=[jax.ShapeDtypeStruct((S, C), F32), jax.ShapeDtypeStruct((8, C), F32)],
        scratch_shapes=[pltpu.VMEM((T + 8, C), F32)],
        compiler_params=_params(("arbitrary",)),
    )(proj, conv_w, conv_b, dxbc)


def _conv_bwd_in(name, dp, conv_w):
    S, C = dp.shape
    T = CONV_T
    K = SSM_CONV
    nb = S // T

    def body(d_ref, w_ref, o_ref, buf):
        @pl.when(pl.program_id(0) == 0)
        def _():
            buf[T:T + 8, :] = jnp.zeros((8, C), F32)

        buf[0:T, :] = d_ref[...]
        acc = w_ref[K - 1:K, :] * buf[0:T, :]
        for k in range(K - 1):
            acc = acc + w_ref[k:k + 1, :] * buf[3 - k:3 - k + T, :]
        o_ref[...] = acc.astype(o_ref.dtype)
        buf[T:T + 8, :] = buf[0:8, :]

    row = pl.BlockSpec((T, C), lambda i: (nb - 1 - i, 0))
    return pl.pallas_call(
        body, name=name, grid=(nb,),
        in_specs=[row, pl.BlockSpec((K, C), lambda i: (0, 0))],
        out_specs=row,
        out_shape=jax.ShapeDtypeStruct((S, C), BF16),
        scratch_shapes=[pltpu.VMEM((T + 8, C), F32)],
        compiler_params=_params(("arbitrary",)),
    )(dp, conv_w)


def _softplus(x):
    return jnp.maximum(x, 0.0) + jnp.log1p(jnp.exp(-jnp.abs(x)))


def _ssd_common(xbc_ref, dt_ref, dtb_ref, alog_ref):
    T = SSD_T
    dtr = dt_ref[...] + dtb_ref[...]
    dt = _softplus(dtr)
    a = -jnp.exp(alog_ref[...])
    dta = dt * a
    row = lax.broadcasted_iota(jnp.int32, (T, T), 0)
    col = lax.broadcasted_iota(jnp.int32, (T, T), 1)
    causal = col <= row
    acum = _split_dot(causal, dta)
    acum_t = acum.T
    return dtr, dt, a, acum, acum_t, causal, row


def _ssd_head(xbc_ref, h, dt, acum, acum_t, causal, cb):
    P = SSM_P
    ac = acum[:, h:h + 1]
    decay = jnp.exp(jnp.where(causal, ac - acum_t[h:h + 1, :], -1e30))
    dth = dt[:, h:h + 1]
    xs = xbc_ref[:, h * P:(h + 1) * P]
    xdt = xs * dth
    sc = (cb * decay).astype(BF16)
    aend = acum[SSD_T - 1:SSD_T, h:h + 1]
    return ac, dth, xs, xdt, sc, aend


def _ssd_specs(T, blk):
    vec = lambda n: pl.BlockSpec((1, n), lambda b: (0, 0))
    return [pl.BlockSpec((T, 1024), lambda b: (blk(b), 0)),
            pl.BlockSpec((T, LANES), lambda b: (blk(b), DT_COL // LANES)),
            pl.BlockSpec((T, 512), lambda b: (blk(b), 7)),
            vec(LANES), vec(LANES), vec(LANES), vec(512)]


def _ssd_fwd(name, xbc, proj, dt_bias, a_log, d_skip, gain):
    S = xbc.shape[0]
    T = SSD_T
    nb = S // T
    H, P, N = SSM_HEADS, SSM_P, SSM_N

    def body(xbc_ref, dt_ref, z_ref, dtb_ref, alog_ref, dsk_ref, gain_ref, y_ref, st_ref, st, ybuf):
        @pl.when(pl.program_id(0) == 0)
        def _():
            st[...] = jnp.zeros_like(st)

        _, dt, _, acum, acum_t, causal, _ = _ssd_common(xbc_ref, dt_ref, dtb_ref, alog_ref)
        for g in range(SSM_GROUPS):
            bg = xbc_ref[:, 512 + g * N:512 + (g + 1) * N]
            cg = xbc_ref[:, 768 + g * N:768 + (g + 1) * N]
            cb = _dot(cg, bg, NT)
            for hh in range(H // SSM_GROUPS):
                h = g * (H // SSM_GROUPS) + hh
                ac, _, xs, xdt, sc, aend = _ssd_head(xbc_ref, h, dt, acum, acum_t, causal, cb)
                sp = st[h]
                st_ref[0, h] = sp
                y = _dot(sc, xdt) + jnp.exp(ac) * _dot(cg, sp) + xs * dsk_ref[:, h:h + 1]
                st[h] = jnp.exp(aend) * sp + _dot(bg, xdt * jnp.exp(aend - ac), TN)
                ybuf[:, h * P:(h + 1) * P] = y
        z = z_ref[...]
        yg = ybuf[...] * z * _sigmoid(z)
        r = lax.rsqrt(jnp.mean(yg * yg, axis=-1, keepdims=True) + NORM_EPS)
        y_ref[...] = yg * r * gain_ref[...]

    return pl.pallas_call(
        body, name=name, grid=(nb,), in_specs=_ssd_specs(T, lambda b: b),
        out_specs=[pl.BlockSpec((T, 512), lambda b: (b, 0)), pl.BlockSpec((1, H, N, P), lambda b: (b, 0, 0, 0))],
        out_shape=[jax.ShapeDtypeStruct((S, 512), F32), jax.ShapeDtypeStruct((nb, H, N, P), F32)],
        scratch_shapes=[pltpu.VMEM((H, N, P), F32), pltpu.VMEM((T, 512), F32)],
        compiler_params=_params(("arbitrary",)),
    )(xbc, proj, proj, dt_bias, a_log, d_skip, gain)


def _ssd_bwd(name, xbc, proj, dt_bias, a_log, d_skip, gain, states, dmix):
    S = xbc.shape[0]
    T = SSD_T
    nb = S // T
    H, P, N = SSM_HEADS, SSM_P, SSM_N
    blk = lambda b: nb - 1 - b

    def body(xbc_ref, dt_ref, z_ref, dtb_ref, alog_ref, dsk_ref, gain_ref, st_ref, do_ref,
             dx_ref, dz_ref, ddt_ref, sums_ref, dst, ybuf):
        @pl.when(pl.program_id(0) == 0)
        def _():
            dst[...] = jnp.zeros_like(dst)
            sums_ref[...] = jnp.zeros_like(sums_ref)

        dtr, dt, a, acum, acum_t, causal, row = _ssd_common(xbc_ref, dt_ref, dtb_ref, alog_ref)
        cbs = []
        for g in range(SSM_GROUPS):
            bg = xbc_ref[:, 512 + g * N:512 + (g + 1) * N]
            cg = xbc_ref[:, 768 + g * N:768 + (g + 1) * N]
            cb = _dot(cg, bg, NT)
            cbs.append(cb)
            for hh in range(H // SSM_GROUPS):
                h = g * (H // SSM_GROUPS) + hh
                ac, _, xs, xdt, sc, _ = _ssd_head(xbc_ref, h, dt, acum, acum_t, causal, cb)
                ybuf[:, h * P:(h + 1) * P] = (_dot(sc, xdt) + jnp.exp(ac) * _dot(cg, st_ref[0, h])
                                              + xs * dsk_ref[:, h:h + 1])
        z = z_ref[...]
        sg = _sigmoid(z)
        sz = z * sg
        yfull = ybuf[...]
        yg = yfull * sz
        r = lax.rsqrt(jnp.mean(yg * yg, axis=-1, keepdims=True) + NORM_EPS)
        yh = yg * r
        do = do_ref[...]
        sums_ref[0:1, :] += jnp.sum(do * yh, axis=0, keepdims=True)
        dyh = do * gain_ref[...]
        dyg = r * (dyh - yh * jnp.mean(dyh * yh, axis=-1, keepdims=True))
        dz_ref[...] = (dyg * yfull * sg * (1.0 + z * (1.0 - sg))).astype(dz_ref.dtype)
        dyv = dyg * sz

        lane = lax.broadcasted_iota(jnp.int32, (T, LANES), 1)
        rowl = lax.broadcasted_iota(jnp.int32, (T, 1), 0)
        dacum = jnp.zeros((T, LANES), F32)
        dacum_t = jnp.zeros((LANES, T), F32)
        sub = lax.broadcasted_iota(jnp.int32, (LANES, T), 0)
        ddt = jnp.zeros((T, LANES), F32)
        dskp = jnp.zeros((T, LANES), F32)
        for g in range(SSM_GROUPS):
            bg = xbc_ref[:, 512 + g * N:512 + (g + 1) * N]
            cg = xbc_ref[:, 768 + g * N:768 + (g + 1) * N]
            cb = cbs[g]
            dcb = jnp.zeros((T, T), F32)
            dbg = jnp.zeros((T, N), F32)
            dcg = jnp.zeros((T, N), F32)
            for hh in range(H // SSM_GROUPS):
                h = g * (H // SSM_GROUPS) + hh
                ac, dth, xs, xdt, sc, aend = _ssd_head(xbc_ref, h, dt, acum, acum_t, causal, cb)
                decay = jnp.exp(jnp.where(causal, ac - acum_t[h:h + 1, :], -1e30))
                dy = dyv[:, h * P:(h + 1) * P]
                sp = st_ref[0, h]
                ea = jnp.exp(ac)
                de = jnp.exp(aend - ac)
                dec = jnp.exp(aend)
                dskp = dskp + jnp.where(lane == h, jnp.sum(dy * xs, axis=1, keepdims=True), 0.0)
                dxdt = _dot(sc, dy, TN)
                dsd = _dot(dy, xdt, NT) * decay
                dcb = dcb + dsd
                e = dsd * cb
                dacum_t = dacum_t + jnp.where(sub == h, jnp.sum(e, axis=0, keepdims=True), 0.0)
                dac = jnp.sum(e, axis=1, keepdims=True)
                dyea = dy * ea
                dac = dac + jnp.sum(dyea * _dot(cg, sp), axis=1, keepdims=True)
                dcg = dcg + _dot(dyea, sp, NT)
                dsp = _dot(cg, dyea, TN)
                dsn = dst[h]
                xde = xdt * de
                dbg = dbg + _dot(xde, dsn, NT)
                wh = _dot(bg, dsn)
                dxdt = dxdt + wh * de
                r_end = jnp.sum(wh * xde, axis=1, keepdims=True)
                dend = jnp.sum(r_end) + jnp.sum(dsn * sp) * dec
                dst[h] = dsp + dec * dsn
                dac = dac - r_end + jnp.where(rowl == T - 1, dend, 0.0)
                dacum = dacum + jnp.where(lane == h, dac, 0.0)
                ddt = ddt + jnp.where(lane == h, jnp.sum(dxdt * xs, axis=1, keepdims=True), 0.0)
                dx_ref[:, h * P:(h + 1) * P] = dxdt * dth + dy * dsk_ref[:, h:h + 1]
            dx_ref[:, 512 + g * N:512 + (g + 1) * N] = dbg + _dot(dcb, cg, TN)
            dx_ref[:, 768 + g * N:768 + (g + 1) * N] = dcg + _dot(dcb, bg)
        ddta = _split_dot(row <= lax.broadcasted_iota(jnp.int32, (T, T), 1), dacum - dacum_t.T)
        ddt = ddt + ddta * a
        ddtr = ddt * _sigmoid(dtr)
        ddt_ref[...] = ddtr.astype(ddt_ref.dtype)
        sums_ref[1:2, 0:LANES] += jnp.sum(ddtr, axis=0, keepdims=True)
        sums_ref[1:2, LANES:2 * LANES] += jnp.sum(ddta * dt, axis=0, keepdims=True) * a
        sums_ref[1:2, 2 * LANES:3 * LANES] += jnp.sum(dskp, axis=0, keepdims=True)

    specs = _ssd_specs(T, blk) + [pl.BlockSpec((1, H, N, P), lambda b: (blk(b), 0, 0, 0)),
                                  pl.BlockSpec((T, 512), lambda b: (blk(b), 2))]
    return pl.pallas_call(
        body, name=name, grid=(nb,), in_specs=specs,
        out_specs=[pl.BlockSpec((T, 1024), lambda b: (blk(b), 0)), pl.BlockSpec((T, 512), lambda b: (blk(b), 0)),
                   pl.BlockSpec((T, LANES), lambda b: (blk(b), 0)), pl.BlockSpec((8, 512), lambda b: (0, 0))],
        out_shape=[jax.ShapeDtypeStruct((S, 1024), F32), jax.ShapeDtypeStruct((S, 512), BF16),
                   jax.ShapeDtypeStruct((S, LANES), BF16), jax.ShapeDtypeStruct((8, 512), F32)],
        scratch_shapes=[pltpu.VMEM((H, N, P), F32), pltpu.VMEM((T, 512), F32)],
        compiler_params=_params(("arbitrary",)),
    )(xbc, proj, proj, dt_bias, a_log, d_skip, gain, states, dmix)


def _place():
    return lax.axis_index("x"), lax.axis_index("y"), lax.axis_index("c")


def _flip(v, bit):
    return 1 - v if bit else v


def _gather_small(name, v):
    R, C = v.shape

    def body(v_ref, out_ref, send_sems, recv_sems, local_sem):
        x, y, c = _place()
        me = 4 * x + 2 * y + c
        mine = pltpu.make_async_copy(v_ref, out_ref.at[me], local_sem)
        mine.start()
        peers = [(_flip(x, (k >> 2) & 1), _flip(y, (k >> 1) & 1), _flip(c, k & 1)) for k in range(1, 8)]
        sends = []
        for k, peer in enumerate(peers):
            cp = pltpu.make_async_remote_copy(src_ref=v_ref, dst_ref=out_ref.at[me], send_sem=send_sems.at[k],
                                              recv_sem=recv_sems.at[k], device_id=peer, device_id_type=MESH)
            cp.start()
            sends.append(cp)
        for k, (px, py, pc) in enumerate(peers):
            pltpu.make_async_remote_copy(src_ref=v_ref, dst_ref=out_ref.at[4 * px + 2 * py + pc],
                                         send_sem=send_sems.at[k], recv_sem=recv_sems.at[k],
                                         device_id=(px, py, pc), device_id_type=MESH).wait_recv()
        for cp in sends:
            cp.wait_send()
        mine.wait()

    return pl.pallas_call(
        body, name=name, out_shape=jax.ShapeDtypeStruct((8, R, C), v.dtype),
        in_specs=[pl.BlockSpec(memory_space=pltpu.VMEM)], out_specs=pl.BlockSpec(memory_space=pltpu.VMEM),
        scratch_shapes=[pltpu.SemaphoreType.DMA((7,)), pltpu.SemaphoreType.DMA((7,)), pltpu.SemaphoreType.DMA(())],
    )(v)


def _hbm_call(body, name, arrays, out_shapes, n_sems):
    spec = pl.BlockSpec(memory_space=pl.ANY)
    return pl.pallas_call(
        body, name=name, out_shape=out_shapes, in_specs=[spec] * len(arrays), out_specs=[spec] * len(out_shapes),
        scratch_shapes=[pltpu.SemaphoreType.DMA((n_sems,)), pltpu.SemaphoreType.DMA((n_sems,))],
    )(*arrays)


def _chip_exchange(name, arrays, gather):
    n = len(arrays)

    def body(*refs):
        ins, outs, send_sems, recv_sems = refs[:n], refs[n:2 * n], refs[2 * n], refs[2 * n + 1]
        x, y, c = _place()
        chip = 2 * x + y
        peers = [(_flip(x, (k >> 1) & 1), _flip(y, k & 1)) for k in range(1, 4)]

        def copy(a, k, slot):
            px, py = peers[k]
            return pltpu.make_async_remote_copy(
                src_ref=ins[a].at[c] if gather else ins[a].at[2 * px + py], dst_ref=outs[a].at[slot],
                send_sem=send_sems.at[3 * a + k], recv_sem=recv_sems.at[3 * a + k],
                device_id=(px, py, c), device_id_type=MESH)

        sends = [copy(a, k, chip) for a in range(n) for k in range(3)]
        for cp in sends:
            cp.start()
        for a in range(n):
            for k, (px, py) in enumerate(peers):
                copy(a, k, 2 * px + py).wait_recv()
        for cp in sends:
            cp.wait_send()

    shapes = [jax.ShapeDtypeStruct((4,) + a.shape[1:], a.dtype) for a in arrays]
    return _hbm_call(body, name, arrays, shapes, 3 * n)


def _sibling_swap(name, arrays, alt=None):
    n = len(arrays)

    def body(*refs):
        k = 1 if alt is None else 2
        outs, send_sems, recv_sems = refs[k * n:(k + 1) * n], refs[(k + 1) * n], refs[(k + 1) * n + 1]
        x, y, c = _place()

        def exchange(srcs):
            cps = [pltpu.make_async_remote_copy(src_ref=srcs[a], dst_ref=outs[a], send_sem=send_sems.at[a],
                                                recv_sem=recv_sems.at[a], device_id=(x, y, 1 - c),
                                                device_id_type=MESH) for a in range(n)]
            for cp in cps:
                cp.start()
            for cp in cps:
                cp.wait()

        if alt is None:
            exchange(refs[:n])
        else:
            @pl.when(c == 1)
            def _():
                exchange(refs[:n])

            @pl.when(c == 0)
            def _():
                exchange(refs[n:2 * n])

    shapes = [jax.ShapeDtypeStruct(a.shape, a.dtype) for a in arrays]
    return _hbm_call(body, name, list(arrays) + ([] if alt is None else list(alt)), shapes, n)


PACK_C = 1024
SUM_STEPS = 4


def _add_layers(name, own0, own1, other, c):
    n = len(other)

    def body(c_ref, *refs):
        for a in range(n):
            mine = jnp.where(c_ref[0] == 0, refs[a][...].astype(F32), refs[n + a][...].astype(F32))
            refs[3 * n + a][...] = (mine + refs[2 * n + a][...].astype(F32)).astype(BF16)

    specs = [pl.BlockSpec((1, o.shape[1] // SUM_STEPS, o.shape[2]), lambda p, i, c_ref: (p, i, 0)) for o in other]
    grid_spec = pltpu.PrefetchScalarGridSpec(num_scalar_prefetch=1, grid=(4, SUM_STEPS), in_specs=specs * 3,
                                             out_specs=specs)
    return pl.pallas_call(
        body, name=name, grid_spec=grid_spec, out_shape=[jax.ShapeDtypeStruct(o.shape, BF16) for o in other],
        compiler_params=_params(("parallel", "parallel")),
    )(jnp.reshape(c, (1,)).astype(jnp.int32), *own0, *own1, *other)


def _sum_chips(name, received, own, chip):
    n = len(received)

    def body(chip_ref, *refs):
        for a in range(n):
            s = None
            for q in range(4):
                term = jnp.where(chip_ref[0] == q, refs[n + a][0], refs[a][q]).astype(F32)
                s = term if s is None else s + term
            refs[2 * n + a][...] = s

    rec = [pl.BlockSpec((4, r.shape[1] // SUM_STEPS, r.shape[2]), lambda i, chip_ref: (0, i, 0)) for r in received]
    mine = [pl.BlockSpec((1, r.shape[1] // SUM_STEPS, r.shape[2]), lambda i, chip_ref: (chip_ref[0], i, 0))
            for r in received]
    outs = [pl.BlockSpec((r.shape[1] // SUM_STEPS, r.shape[2]), lambda i, chip_ref: (i, 0)) for r in received]
    grid_spec = pltpu.PrefetchScalarGridSpec(num_scalar_prefetch=1, grid=(SUM_STEPS,), in_specs=rec + mine,
                                             out_specs=outs)
    return pl.pallas_call(
        body, name=name, grid_spec=grid_spec,
        out_shape=[jax.ShapeDtypeStruct(r.shape[1:], F32) for r in received],
        compiler_params=_params(("parallel",)),
    )(jnp.reshape(chip, (1,)).astype(jnp.int32), *received, *own)


def _sum_devices(name, parts):
    _, R, C = parts.shape

    def body(p_ref, s_ref):
        s = p_ref[0]
        for q in range(1, 8):
            s = s + p_ref[q]
        s_ref[...] = s

    return pl.pallas_call(body, name=name, out_shape=jax.ShapeDtypeStruct((R, C), F32))(parts)


def _adamw_rule(w, g, m, v):
    nm = ADAM_B1 * m + (1.0 - ADAM_B1) * g
    nv = ADAM_B2 * v + (1.0 - ADAM_B2) * (g * g)
    m_hat = nm / (1.0 - ADAM_B1 ** ADAM_STEP)
    v_hat = nv / (1.0 - ADAM_B2 ** ADAM_STEP)
    return -ADAM_LR * (m_hat / (jnp.sqrt(v_hat) + ADAM_EPS) + ADAM_WD * w), nm, nv


def _adamw(name, w, g, m, v):
    L, R, C = w.shape
    tr = 128 if R % 128 == 0 else R

    def body(w_ref, g_ref, m_ref, v_ref, d_ref, nm_ref, nv_ref):
        d_ref[...], nm_ref[...], nv_ref[...] = _adamw_rule(w_ref[...], g_ref[...], m_ref[...], v_ref[...])

    spec = pl.BlockSpec((None, tr, C), lambda l, i: (l, i, 0))
    return pl.pallas_call(
        body, name=name, grid=(L, R // tr), in_specs=[spec] * 4, out_specs=[spec] * 3,
        out_shape=[jax.ShapeDtypeStruct((L, R, C), F32)] * 3, compiler_params=_params(("parallel", "parallel")),
    )(w, g, m, v)


def _adamw_layers(name, w, g_mine, g_other, m, v, c):
    L, R, C = w.shape
    tr = 128 if R % 128 == 0 else R

    def body(c_ref, w_ref, gm_ref, go_ref, m_ref, v_ref, g_ref, d_ref, nm_ref, nv_ref):
        gv = jnp.where(pl.program_id(0) == c_ref[0], gm_ref[...], go_ref[...])
        g_ref[...] = gv
        d_ref[...], nm_ref[...], nv_ref[...] = _adamw_rule(w_ref[...], gv, m_ref[...], v_ref[...])

    full = pl.BlockSpec((None, tr, C), lambda l, i, c_ref: (l, i, 0))
    part = pl.BlockSpec((tr, C), lambda l, i, c_ref: (i, 0))
    grid_spec = pltpu.PrefetchScalarGridSpec(num_scalar_prefetch=1, grid=(L, R // tr),
                                             in_specs=[full, part, part, full, full], out_specs=[full] * 4)
    return pl.pallas_call(
        body, name=name, grid_spec=grid_spec, out_shape=[jax.ShapeDtypeStruct((L, R, C), F32)] * 4,
        compiler_params=_params(("parallel", "parallel")),
    )(jnp.reshape(c, (1,)).astype(jnp.int32), w, g_mine, g_other, m, v)


BIG = ("ffn1_wg", "ffn1_wu", "ffn1_wd", "w_in", "w_out", "ffn2_wg", "ffn2_wu", "ffn2_wd")
HELD_TRANSPOSED = ("ffn1_wg", "ffn1_wu", "ffn2_wg", "ffn2_wu")
WEIGHTS =("ada_w", "ada_b", "norm_ffn1", "ffn1_wg", "ffn1_wu", "ffn1_wd", "norm_mix", "w_in", "conv_w", "conv_b",
           "dt_bias", "a_log", "d_skip", "ret_gn", "ssm_norm", "w_out", "norm_ffn2", "ffn2_wg", "ffn2_wu", "ffn2_wd",
           "final_ada_w", "final_ada_b", "final_norm")
SMALL = ("ada_b", "norm_ffn1", "norm_mix", "conv_w", "conv_b", "dt_bias", "a_log", "d_skip", "ret_gn", "ssm_norm",
         "norm_ffn2", "final_ada_b", "final_norm")


def _unpack(slab, shapes):
    flat = slab.reshape(-1)
    out, off = [], 0
    for shp in shapes:
        n = math.prod(shp)
        out.append(flat[off:off + n].reshape(shp))
        off += n
    return out


def _pad_lanes(v):
    return jnp.pad(v, (0, LANES - v.shape[0])).reshape(1, LANES)


def _ffn_fwd(tag, h, gain, mod3, wg, wu, wd):
    u = _norm_mod(tag + "_norm", h, gain, mod3[0:1], mod3[1:2])
    a, b, hm = _ffn_up(tag + "_up", u, wg, wu)
    gatefac = 0.5 * (1.0 + mod3[2:3])
    out, h_new = _rowmm(tag + "_down", [(hm, p, wd, p) for p in range(4)], "nn", res=(h, gatefac))
    return h_new, (h, u, a, b, hm, out, gatefac)


def _ffn_bwd(tag, g, saved, gain, mod3, wg, wu, wd):
    h, u, a, b, hm, out, gatefac = saved
    d_out, s_gate = _gate_bwd(tag + "_gate_bwd", g, out, gatefac)
    da, db = _ffn_bwd_mid(tag + "_mid_bwd", d_out, wd, a, b)
    dwd = _mm_tn_groups(tag + "_dwd", hm, d_out, BF16)
    dwg = _mm_tn_groups(tag + "_dwg", da, u, BF16)
    dwu = _mm_tn_groups(tag + "_dwu", db, u, BF16)
    g_in, s_norm = _rowmm(tag + "_du", [(da, p, wg, p) for p in range(4)] + [(db, p, wu, p) for p in range(4)], "nn",
                          norm_bwd=(h, gain, mod3[1:2], g))
    dmod3 = jnp.concatenate([s_norm[0:1], s_norm[1:2], 0.5 * s_gate[0:1]], axis=0)
    return g_in, (dwg, dwu, dwd), dmod3, s_norm[2]


def _mixer_fwd(tag, h, P, mod3, w_in, w_out, tables):
    u = _norm_mod(tag + "_norm", h, P["norm_mix"], mod3[0:1], mod3[1:2])
    proj = _mm(tag + "_in_proj", [(u, w_in)], "nn", F32, tm=512, tn=1792, tk=1024, n_outer=True)
    y_ret, ret_st = _ret_fwd(tag + "_ret", proj, P["ret_gn"], tables)
    kv = proj[:, 2560:3584].astype(BF16)
    y_sb = _sb_fwd(tag + "_sb", proj, kv)
    xbc = _conv_fwd(tag + "_conv", proj, P["conv_w"], P["conv_b"])
    y_ssm, ssm_st = _ssd_fwd(tag + "_ssd", xbc, proj, P["dt_bias"], P["a_log"], P["d_skip"], P["ssm_norm"])
    gatefac = 1.0 + mod3[2:3]
    ys = (y_ret, y_sb, y_ssm)
    w_out3 = w_out.reshape(3, 512, D_MODEL)
    mixed, h_new = _rowmm(tag + "_out_proj", [(y, None, w_out3, i) for i, y in enumerate(ys)], "nn", res=(h, gatefac))
    return h_new, (h, u, proj, kv, ys, ret_st, xbc, ssm_st, mixed, gatefac)


def _mixer_bwd(tag, g, saved, P, mod3, w_in, w_out, tables):
    h, u, proj, kv, ys, ret_st, xbc, ssm_st, mixed, gatefac = saved
    S = h.shape[0]
    d_mixed, s_gate = _gate_bwd(tag + "_gate_bwd", g, mixed, gatefac)
    dmix = _mm(tag + "_dmix", [(d_mixed, w_out)], "nt", F32, tm=512, tn=512, tk=1024)
    dw_out = jnp.concatenate(
        [_mm(tag + f"_dw_out{i}", [(y, d_mixed)], "tn", BF16, tm=512, tn=1024, tk=2048) for i, y in enumerate(ys)], axis=0)
    d_ret, d_gn = _ret_bwd(tag + "_ret_bwd", proj, P["ret_gn"], tables, ret_st, dmix)
    dq, dk, dv = _sb_bwd(tag + "_sb_bwd", proj, kv, dmix)
    dxbc, dz, ddt, s_ssd = _ssd_bwd(tag + "_ssd_bwd", xbc, proj, P["dt_bias"], P["a_log"], P["d_skip"], P["ssm_norm"],
                                    ssm_st, dmix)
    dp, s_conv = _conv_bwd_act(tag + "_conv_bwd_act", proj, P["conv_w"], P["conv_b"], dxbc)
    dxr = _conv_bwd_in(tag + "_conv_bwd_in", dp, P["conv_w"])
    dproj = jnp.concatenate(
        [d_ret[0], d_ret[1], d_ret[2], d_ret[3], dq, dk.astype(BF16), dv.astype(BF16), dz, dxr, ddt,
         jnp.zeros((S, IN_WP - DT_COL - LANES), BF16)], axis=1)
    dw_in = _mm(tag + "_dw_in", [(u, dproj)], "tn", BF16, tm=1024, tn=768, tk=2048)
    g_in, s_norm = _rowmm(tag + "_du", [(dproj, None, w_in, None)], "nt", norm_bwd=(h, P["norm_mix"], mod3[1:2], g))
    dmod3 = jnp.concatenate([s_norm[0:1], s_norm[1:2], s_gate[0:1]], axis=0)
    small = dict(norm_mix=s_norm[2], conv_w=s_conv[0:4], conv_b=s_conv[4], dt_bias=s_ssd[1, 0:8],
                 a_log=s_ssd[1, LANES:LANES + 8], d_skip=s_ssd[1, 2 * LANES:2 * LANES + 8],
                 ret_gn=d_gn[0], ssm_norm=s_ssd[0])
    return g_in, dw_in, dw_out, dmod3, small


def kernel(x, c, ada_w, ada_b, norm_ffn1, ffn1_wg, ffn1_wu, ffn1_wd, norm_mix, w_in, conv_w, conv_b, dt_bias, a_log, d_skip, ret_gn, ssm_norm, w_out, norm_ffn2, ffn2_wg, ffn2_wu, ffn2_wd, final_ada_w, final_ada_b, final_norm, loss_target, m_ada_w, m_ada_b, m_norm_ffn1, m_ffn1_wg, m_ffn1_wu, m_ffn1_wd, m_norm_mix, m_w_in, m_conv_w, m_conv_b, m_dt_bias, m_a_log, m_d_skip, m_ret_gn, m_ssm_norm, m_w_out, m_norm_ffn2, m_ffn2_wg, m_ffn2_wu, m_ffn2_wd, m_final_ada_w, m_final_ada_b, m_final_norm, v_ada_w, v_ada_b, v_norm_ffn1, v_ffn1_wg, v_ffn1_wu, v_ffn1_wd, v_norm_mix, v_w_in, v_conv_w, v_conv_b, v_dt_bias, v_a_log, v_d_skip, v_ret_gn, v_ssm_norm, v_w_out, v_norm_ffn2, v_ffn2_wg, v_ffn2_wu, v_ffn2_wd, v_final_ada_w, v_final_ada_b, v_final_norm):
    W = dict(ada_w=ada_w, ada_b=ada_b, norm_ffn1=norm_ffn1, ffn1_wg=ffn1_wg, ffn1_wu=ffn1_wu, ffn1_wd=ffn1_wd,
             norm_mix=norm_mix, w_in=w_in, conv_w=conv_w, conv_b=conv_b, dt_bias=dt_bias, a_log=a_log, d_skip=d_skip,
             ret_gn=ret_gn, ssm_norm=ssm_norm, w_out=w_out, norm_ffn2=norm_ffn2, ffn2_wg=ffn2_wg, ffn2_wu=ffn2_wu,
             ffn2_wd=ffn2_wd, final_ada_w=final_ada_w, final_ada_b=final_ada_b, final_norm=final_norm)
    M = dict(ada_w=m_ada_w, ada_b=m_ada_b, norm_ffn1=m_norm_ffn1, ffn1_wg=m_ffn1_wg, ffn1_wu=m_ffn1_wu,
             ffn1_wd=m_ffn1_wd, norm_mix=m_norm_mix, w_in=m_w_in, conv_w=m_conv_w, conv_b=m_conv_b, dt_bias=m_dt_bias,
             a_log=m_a_log, d_skip=m_d_skip, ret_gn=m_ret_gn, ssm_norm=m_ssm_norm, w_out=m_w_out,
             norm_ffn2=m_norm_ffn2, ffn2_wg=m_ffn2_wg, ffn2_wu=m_ffn2_wu, ffn2_wd=m_ffn2_wd,
             final_ada_w=m_final_ada_w, final_ada_b=m_final_ada_b, final_norm=m_final_norm)
    V = dict(ada_w=v_ada_w, ada_b=v_ada_b, norm_ffn1=v_norm_ffn1, ffn1_wg=v_ffn1_wg, ffn1_wu=v_ffn1_wu,
             ffn1_wd=v_ffn1_wd, norm_mix=v_norm_mix, w_in=v_w_in, conv_w=v_conv_w, conv_b=v_conv_b, dt_bias=v_dt_bias,
             a_log=v_a_log, d_skip=v_d_skip, ret_gn=v_ret_gn, ssm_norm=v_ssm_norm, w_out=v_w_out,
             norm_ffn2=v_norm_ffn2, ffn2_wg=v_ffn2_wg, ffn2_wu=v_ffn2_wu, ffn2_wd=v_ffn2_wd,
             final_ada_w=v_final_ada_w, final_ada_b=v_final_ada_b, final_norm=v_final_norm)
    for n in HELD_TRANSPOSED:
        W[n], M[n], V[n] = (jnp.transpose(t[n], (0, 2, 1)) for t in (W, M, V))
    D = D_MODEL
    S = x.shape[1]
    ax, ay, ac = _place()
    me = 4 * ax + 2 * ay + ac
    chip = 2 * ax + ay
    h0 = x[0]

    c_all = _gather_small("gather_c", jnp.pad(c, ((0, 7), (0, 0))))[:, 0, :]
    cond_all = c_all * jax.nn.sigmoid(c_all)
    nmod = 3 * 3 * D // 4
    mod_part = jnp.concatenate(
        [_mm(f"mod_proj{l}", [(cond_all, ada_w[l])], "nn", F32, tm=8, tn=768, tk=D) for l in range(DEPTH)]
        + [_mm("mod_proj_final", [(cond_all, final_ada_w)], "nn", F32, tm=8, tn=512, tk=D),
           conv_w.reshape(DEPTH * SSM_CONV, -1)], axis=1)
    gathered = _gather_small("gather_mod", mod_part)[0::2]
    mine = lax.dynamic_index_in_dim(gathered, me, axis=1, keepdims=False)
    mods = [(jnp.reshape(mine[:, l * nmod:(l + 1) * nmod], (-1,)) + ada_b[l]).reshape(9, D) for l in range(DEPTH)]
    fmod = (jnp.reshape(mine[:, DEPTH * nmod:DEPTH * nmod + 2 * D // 4], (-1,)) + final_ada_b).reshape(2, D)
    conv_full = jnp.transpose(gathered[:, :, DEPTH * nmod + 2 * D // 4:], (1, 0, 2)).reshape(DEPTH, SSM_CONV, -1)

    shards = [W[n].astype(BF16) for n in BIG]
    fetched = _chip_exchange("gather_weights", shards, gather=True)
    swapped = _sibling_swap("gather_weights_sibling", fetched)
    full = []
    for l in range(DEPTH):
        fw = {}
        for i, n in enumerate(BIG):
            arr = jnp.where(ac == l, fetched[i], swapped[i])
            fw[n] = lax.dynamic_update_index_in_dim(arr, shards[i][l], chip, 0)
        fw["w_in"] = jnp.pad(jnp.transpose(fw["w_in"], (1, 0, 2)).reshape(D, IN_W), ((0, 0), (0, IN_WP - IN_W)))
        fw["w_out"] = fw["w_out"].reshape(MIX_W, D)
        full.append(fw)

    tables = _ret_tables(S)
    small_p = []
    for l in range(DEPTH):
        small_p.append(dict(
            norm_ffn1=norm_ffn1[l:l + 1], norm_mix=norm_mix[l:l + 1], norm_ffn2=norm_ffn2[l:l + 1],
            ret_gn=ret_gn[l:l + 1], ssm_norm=ssm_norm[l:l + 1], conv_w=conv_full[l], conv_b=conv_b[l:l + 1],
            dt_bias=_pad_lanes(dt_bias[l]), a_log=_pad_lanes(a_log[l]), d_skip=_pad_lanes(d_skip[l])))

    h = h0
    saved = []
    for l in range(DEPTH):
        P, fw, mod = small_p[l], full[l], mods[l]
        h, s1 = _ffn_fwd(f"l{l}_ffn1", h, P["norm_ffn1"], mod[0:3], fw["ffn1_wg"], fw["ffn1_wu"], fw["ffn1_wd"])
        h, sm = _mixer_fwd(f"l{l}_mix", h, P, mod[3:6], fw["w_in"], fw["w_out"], tables)
        h, s2 = _ffn_fwd(f"l{l}_ffn2", h, P["norm_ffn2"], mod[6:9], fw["ffn2_wg"], fw["ffn2_wu"], fw["ffn2_wd"])
        saved.append((s1, sm, s2))
    fgain = final_norm.reshape(1, D)
    dy, loss_rows = _final_loss("final_loss", h, loss_target[0], fgain, fmod[0:1], fmod[1:2])

    g, s_final = _norm_mod_bwd("final_norm_bwd", dy, h, fgain, fmod[1:2], jnp.zeros_like(dy))
    dfmod = s_final[0:2]
    small_g = {n: [None] * DEPTH for n in SMALL}
    big_g = [None] * DEPTH
    dmods = [None] * DEPTH
    for l in reversed(range(DEPTH)):
        P, fw, mod = small_p[l], full[l], mods[l]
        s1, sm, s2 = saved[l]
        g, (dwg2, dwu2, dwd2), dm2, dn2 = _ffn_bwd(f"l{l}_ffn2", g, s2, P["norm_ffn2"], mod[6:9],
                                                   fw["ffn2_wg"], fw["ffn2_wu"], fw["ffn2_wd"])
        g, dw_in, dw_out, dmm, sg = _mixer_bwd(f"l{l}_mix", g, sm, P, mod[3:6], fw["w_in"], fw["w_out"], tables)
        g, (dwg1, dwu1, dwd1), dm1, dn1 = _ffn_bwd(f"l{l}_ffn1", g, s1, P["norm_ffn1"], mod[0:3],
                                                   fw["ffn1_wg"], fw["ffn1_wu"], fw["ffn1_wd"])
        dmods[l] = jnp.concatenate([dm1, dmm, dm2], axis=0)
        big_g[l] = dict(ffn1_wg=dwg1, ffn1_wu=dwu1, ffn1_wd=dwd1,
                        w_in=jnp.transpose(dw_in[:, :IN_W].reshape(D, 4, IN_W // 4), (1, 0, 2)),
                        w_out=dw_out.reshape(4, MIX_W // 4, D), ffn2_wg=dwg2, ffn2_wu=dwu2, ffn2_wd=dwd2)
        sg.update(norm_ffn1=dn1, norm_ffn2=dn2)
        for n, val in sg.items():
            small_g[n][l] = val
    grad_x = g[None]

    n_mod = DEPTH * 9 * D + 2 * D
    vec = [jnp.stack(dmods).reshape(-1), dfmod.reshape(-1)]
    layered = [n for n in SMALL if n not in ("ada_b", "final_ada_b", "final_norm")]
    vec += [jnp.stack(small_g[n]).reshape(-1) for n in layered]
    vec += [s_final[2], jnp.sum(loss_rows[0]).reshape(1)]
    flat = jnp.concatenate(vec)
    vrows = -(-flat.shape[0] // (8 * PACK_C)) * 8
    slab = jnp.pad(flat, (0, vrows * PACK_C - flat.shape[0])).reshape(vrows, PACK_C)
    slabs = _gather_small("gather_small_grads", slab)
    total = _sum_devices("sum_small_grads", slabs).reshape(-1)
    grads = {}
    grads["ada_b"] = total[:DEPTH * 9 * D].reshape(DEPTH, 9 * D)
    grads["final_ada_b"] = total[DEPTH * 9 * D:n_mod]
    off = n_mod
    for n in layered:
        shp = (DEPTH,) + ((SSM_CONV, D) if n == "conv_w" else W[n].shape[1:])
        cnt = math.prod(shp)
        grads[n] = total[off:off + cnt].reshape(shp)
        off += cnt
    grads["final_norm"] = total[off:off + D]
    loss = total[off + D]
    grads["conv_w"] = lax.dynamic_slice_in_dim(grads["conv_w"], chip * (D // 4), D // 4, axis=2)

    dmod_all = slabs[:, :n_mod // PACK_C, :].reshape(8, n_mod)
    grads["ada_w"] = jnp.stack([
        _mm(f"grad_ada_w{l}", [(cond_all, lax.dynamic_slice_in_dim(dmod_all, l * 9 * D + chip * nmod, nmod, axis=1))],
            "tn", F32, tm=D, tn=768, tk=8) for l in range(DEPTH)])
    grads["final_ada_w"] = _mm(
        "grad_final_ada_w",
        [(cond_all, lax.dynamic_slice_in_dim(dmod_all, DEPTH * 9 * D + chip * (2 * D // 4), 2 * D // 4, axis=1))],
        "tn", F32, tm=D, tn=512, tk=8)

    g0, g1 = [big_g[0][n] for n in BIG], [big_g[1][n] for n in BIG]
    from_sibling = _sibling_swap("reduce_grads_sibling", g0, alt=g1)
    chip_sum = _add_layers("reduce_grads_add", g0, g1, from_sibling, ac)
    from_chips = _chip_exchange("reduce_grads_chips", chip_sum, gather=False)
    reduced = _sum_chips("reduce_grads_sum", from_chips, chip_sum, chip)
    reduced_other = _sibling_swap("reduce_grads_complete", reduced)

    delta, new_m, new_v = {}, {}, {}
    for i, n in enumerate(BIG):
        grads[n], delta[n], new_m[n], new_v[n] = _adamw_layers("adamw_" + n, W[n], reduced[i], reduced_other[i],
                                                               M[n], V[n], ac)
    delta["ada_w"], new_m["ada_w"], new_v["ada_w"] = _adamw("adamw_ada_w", ada_w, grads["ada_w"], m_ada_w, v_ada_w)
    outs = _adamw("adamw_final_ada_w", final_ada_w[None], grads["final_ada_w"][None], m_final_ada_w[None],
                  v_final_ada_w[None])
    delta["final_ada_w"], new_m["final_ada_w"], new_v["final_ada_w"] = [o[0] for o in outs]
    small_shapes = [W[n].shape for n in SMALL]
    n_small = sum(math.prod(s) for s in small_shapes)
    srows = -(-n_small // (8 * LANES)) * 8
    slab_of = lambda T_: jnp.pad(jnp.concatenate([T_[n].reshape(-1) for n in SMALL]),
                                 (0, srows * LANES - n_small)).reshape(1, srows, LANES)
    outs = _adamw("adamw_small", slab_of(W), slab_of(grads), slab_of(M), slab_of(V))
    for res, o in zip((delta, new_m, new_v), outs):
        for n, val in zip(SMALL, _unpack(o[0], small_shapes)):
            res[n] = val

    for n in HELD_TRANSPOSED:
        for res in (grads, delta, new_m, new_v):
            res[n] = jnp.transpose(res[n], (0, 2, 1))
    return (loss, grad_x, *[grads[n] for n in WEIGHTS], *[delta[n] for n in WEIGHTS],
            *[new_m[n] for n in WEIGHTS], *[new_v[n] for n in WEIGHTS])
```

```python
import functools
import math

import jax
import jax.numpy as jnp
from jax import lax
from jax.experimental import pallas as pl
from jax.experimental.pallas import tpu as pltpu

F32 = jnp.float32
BF16 = jnp.bfloat16
MESH = pl.DeviceIdType.MESH

D_MODEL = 1024
DEPTH = 2
D_FF = 2816
RET_HEADS = 4
SB_HEADS = 4
HEAD_DIM = 128
SSM_HEADS = 8
SSM_P = 64
SSM_N = 128
SSM_GROUPS = 2
SSM_CONV = 4
MIX_W = 1536
IN_W = 5128
IN_WP = 5376
DT_COL = 5120
ROPE_BASE = 10000.0
NORM_EPS = 1e-6
ADAM_LR = 0.001
ADAM_B1 = 0.9
ADAM_B2 = 0.999
ADAM_EPS = 1e-08
ADAM_WD = 0.01
ADAM_STEP = 10

LANES = 128
VMEM_LIMIT = 56 * 1024 * 1024
RET_T = 256
RET_HPS = 4
SB_T = 256
SB_HPS = 2
SSD_T = 256
CONV_T = 512
ROW_T = 512
SB_DEAD = -150.0

NN = (((1,), (0,)), ((), ()))
NT = (((1,), (1,)), ((), ()))
TN = (((0,), (0,)), ((), ()))


def _dot(a, b, dims=NN):
    return lax.dot_general(a.astype(BF16), b.astype(BF16), dims, preferred_element_type=F32)


def _params(sem):
    return pltpu.CompilerParams(dimension_semantics=sem, vmem_limit_bytes=VMEM_LIMIT)


def _sigmoid(x):
    return 1.0 / (1.0 + jnp.exp(-x))


def _split_dot(mat01, x, dims=NN, lhs01=True, pieces=3):
    m = mat01.astype(BF16)
    total, rest = None, x
    for _ in range(pieces):
        p = rest.astype(BF16)
        rest = rest - p.astype(F32)
        d = lax.dot_general(m, p, dims, preferred_element_type=F32) if lhs01 else lax.dot_general(
            p, m, dims, preferred_element_type=F32)
        total = d if total is None else total + d
    return total


def _mm(name, terms, mode, out_dtype, tm=512, tn=512, tk=1024, n_outer=False):
    a0, b0 = terms[0]
    if mode == "nn":
        (M, K), N = a0.shape, b0.shape[1]
    elif mode == "nt":
        (M, K), N = a0.shape, b0.shape[0]
    else:
        (K, M), N = a0.shape, b0.shape[1]
    tm, tn, tk = min(tm, M), min(tn, N), min(tk, K)
    assert M % tm == 0 and N % tn == 0 and K % tk == 0, (name, M, N, K, tm, tn, tk)
    nk = K // tk
    nterm = len(terms)
    dims = {"nn": NN, "nt": NT, "tn": TN}[mode]

    def body(*refs):
        o_ref, acc = refs[2 * nterm], refs[2 * nterm + 1]
        part = None
        for t in range(nterm):
            p = _dot(refs[2 * t][...], refs[2 * t + 1][...], dims)
            part = p if part is None else part + p
        if nk == 1:
            o_ref[...] = part.astype(o_ref.dtype)
        else:
            k = pl.program_id(2)

            @pl.when(k == 0)
            def _():
                acc[...] = part

            @pl.when(k > 0)
            def _():
                acc[...] += part

            @pl.when(k == nk - 1)
            def _():
                o_ref[...] = acc[...].astype(o_ref.dtype)

    ij = (lambda g0, g1: (g1, g0)) if n_outer else (lambda g0, g1: (g0, g1))
    if mode == "nn":
        a_spec = pl.BlockSpec((tm, tk), lambda g0, g1, k: (ij(g0, g1)[0], k))
        b_spec = pl.BlockSpec((tk, tn), lambda g0, g1, k: (k, ij(g0, g1)[1]))
    elif mode == "nt":
        a_spec = pl.BlockSpec((tm, tk), lambda g0, g1, k: (ij(g0, g1)[0], k))
        b_spec = pl.BlockSpec((tn, tk), lambda g0, g1, k: (ij(g0, g1)[1], k))
    else:
        a_spec = pl.BlockSpec((tk, tm), lambda g0, g1, k: (k, ij(g0, g1)[0]))
        b_spec = pl.BlockSpec((tk, tn), lambda g0, g1, k: (k, ij(g0, g1)[1]))
    flat = [r for ab in terms for r in ab]
    return pl.pallas_call(
        body, name=name, grid=(N // tn, M // tm, nk) if n_outer else (M // tm, N // tn, nk),
        in_specs=[a_spec, b_spec] * nterm,
        out_specs=pl.BlockSpec((tm, tn), lambda g0, g1, k: ij(g0, g1)),
        out_shape=jax.ShapeDtypeStruct((M, N), out_dtype),
        scratch_shapes=[pltpu.VMEM((tm, tn) if nk > 1 else (8, LANES), F32)],
        compiler_params=_params(("parallel", "parallel", "arbitrary")),
    )(*flat)


def _norm_bwd_rows(d, x, gain, scale, sums_ref):
    r = lax.rsqrt(jnp.mean(x * x, axis=-1, keepdims=True) + NORM_EPS)
    xh = x * r
    dn = d * (1.0 + scale)
    sums_ref[0:1, :] += jnp.sum(d, axis=0, keepdims=True)
    sums_ref[1:2, :] += jnp.sum(d * xh * gain, axis=0, keepdims=True)
    sums_ref[2:3, :] += jnp.sum(dn * xh, axis=0, keepdims=True)
    dxh = dn * gain
    return r * (dxh - xh * jnp.mean(dxh * xh, axis=-1, keepdims=True))


def _rowmm(name, terms, mode, out_dtype=F32, res=None, norm_bwd=None):
    S = terms[0][0].shape[-2]
    N = terms[0][2].shape[-1] if mode == "nn" else terms[0][2].shape[-2]
    nterm = len(terms)
    tm = ROW_T
    dims = NN if mode == "nn" else NT
    nxt = None if (res or norm_bwd) is None else (res or norm_bwd)[-1]

    def body(*refs):
        o = None
        for t in range(nterm):
            p = _dot(refs[2 * t][...], refs[2 * t + 1][...], dims)
            o = p if o is None else o + p
        rest = refs[2 * nterm:]
        if norm_bwd is not None:
            h_ref, g_ref, sc_ref, gr_ref = rest[:4]
            dh_ref, sums_ref = rest[4 + (2 if nxt else 0):][:2]

            @pl.when(pl.program_id(0) == 0)
            def _():
                sums_ref[...] = jnp.zeros_like(sums_ref)

            g = gr_ref[...] + _norm_bwd_rows(o, h_ref[...], g_ref[...], sc_ref[...], sums_ref)
            dh_ref[...] = g
            if nxt:
                _gate_rows(g, rest[4], rest[5], rest[8], rest[9])
        elif res is None:
            rest[0][...] = o.astype(out_dtype)
        else:
            h_ref, gf_ref = rest[:2]
            out_ref, hn_ref = rest[2 + (3 if nxt else 0):][:2]
            out_ref[...] = o.astype(out_ref.dtype)
            hn = h_ref[...] + gf_ref[...] * o
            hn_ref[...] = hn
            if nxt:
                rest[7][...] = _norm_rows(hn, rest[2][...], rest[3][...], rest[4][...]).astype(BF16)

    in_specs, flat = [], []
    for a, ai, w, wi in terms:
        if ai is None:
            in_specs.append(pl.BlockSpec((tm, a.shape[1]), lambda i: (i, 0)))
        else:
            in_specs.append(pl.BlockSpec((None, tm, a.shape[2]), lambda i, g=ai: (g, i, 0)))
        if wi is None:
            in_specs.append(pl.BlockSpec(w.shape, lambda i: (0, 0)))
        else:
            in_specs.append(pl.BlockSpec((None,) + w.shape[1:], lambda i, g=wi: (g, 0, 0)))
        flat += [a, w]
    row = pl.BlockSpec((tm, N), lambda i: (i, 0))
    vec = pl.BlockSpec((1, N), lambda i: (0, 0))
    sums = pl.BlockSpec((8, N), lambda i: (0, 0))
    rows_f32, rows_bf16 = jax.ShapeDtypeStruct((S, N), F32), jax.ShapeDtypeStruct((S, N), BF16)
    sums_f32 = jax.ShapeDtypeStruct((8, N), F32)
    if norm_bwd is not None:
        extra = list(norm_bwd[:4]) + (list(nxt) if nxt else [])
        return pl.pallas_call(
            body, name=name, grid=(S // tm,),
            in_specs=in_specs + [row, vec, vec, row] + ([row, vec] if nxt else []),
            out_specs=[row, sums] + ([row, sums] if nxt else []),
            out_shape=[rows_f32, sums_f32] + ([rows_bf16, sums_f32] if nxt else []),
            compiler_params=_params(("arbitrary",)),
        )(*flat, *extra)
    if res is None:
        return pl.pallas_call(
            body, name=name, grid=(S // tm,), in_specs=in_specs, out_specs=row,
            out_shape=jax.ShapeDtypeStruct((S, N), out_dtype), compiler_params=_params(("parallel",)),
        )(*flat)
    extra = list(res[:2]) + (list(nxt) if nxt else [])
    return pl.pallas_call(
        body, name=name, grid=(S // tm,),
        in_specs=in_specs + [row, vec] + ([vec, vec, vec] if nxt else []),
        out_specs=[row, row] + ([row] if nxt else []),
        out_shape=[rows_bf16, rows_f32] + ([rows_bf16] if nxt else []),
        compiler_params=_params(("parallel",)),
    )(*flat, *extra)


def _mm_tn_groups(name, a, b, out_dtype, tk=2048):
    G = a.shape[0] if a.ndim == 3 else b.shape[0]
    S, M, N = a.shape[-2], a.shape[-1], b.shape[-1]
    tk = min(tk, S)
    nk = S // tk

    def body(a_ref, b_ref, o_ref, acc):
        k = pl.program_id(1)
        part = _dot(a_ref[...], b_ref[...], TN)

        @pl.when(k == 0)
        def _():
            acc[...] = part

        @pl.when(k > 0)
        def _():
            acc[...] += part

        @pl.when(k == nk - 1)
        def _():
            o_ref[...] = acc[...].astype(o_ref.dtype)

    def spec(arr, width):
        if arr.ndim == 3:
            return pl.BlockSpec((None, tk, width), lambda g, k: (g, k, 0))
        return pl.BlockSpec((tk, width), lambda g, k: (k, 0))

    return pl.pallas_call(
        body, name=name, grid=(G, nk), in_specs=[spec(a, M), spec(b, N)],
        out_specs=pl.BlockSpec((None, M, N), lambda g, k: (g, 0, 0)),
        out_shape=jax.ShapeDtypeStruct((G, M, N), out_dtype),
        scratch_shapes=[pltpu.VMEM((M, N), F32)],
        compiler_params=_params(("parallel", "arbitrary")),
    )(a, b)


def _norm_rows(x, gain, shift, scale):
    r = lax.rsqrt(jnp.mean(x * x, axis=-1, keepdims=True) + NORM_EPS)
    return x * r * gain * (1.0 + scale) + shift


def _gate_rows(g, out_ref, gf_ref, d_ref, gsum_ref):
    @pl.when(pl.program_id(0) == 0)
    def _():
        gsum_ref[...] = jnp.zeros_like(gsum_ref)

    d_ref[...] = (g * gf_ref[...]).astype(d_ref.dtype)
    gsum_ref[0:1, :] += jnp.sum(g * out_ref[...].astype(F32), axis=0, keepdims=True)


def _norm_mod(name, h, gain, shift, scale):
    S, D = h.shape
    tm = ROW_T

    def body(h_ref, g_ref, sh_ref, sc_ref, u_ref):
        u_ref[...] = _norm_rows(h_ref[...], g_ref[...], sh_ref[...], sc_ref[...]).astype(u_ref.dtype)

    vec = pl.BlockSpec((1, D), lambda i: (0, 0))
    return pl.pallas_call(
        body, name=name, grid=(S // tm,),
        in_specs=[pl.BlockSpec((tm, D), lambda i: (i, 0)), vec, vec, vec],
        out_specs=pl.BlockSpec((tm, D), lambda i: (i, 0)),
        out_shape=jax.ShapeDtypeStruct((S, D), BF16),
        compiler_params=_params(("parallel",)),
    )(h, gain, shift, scale)


def _norm_mod_bwd(name, du, h, gain, scale, out, gatefac):
    S, D = h.shape
    tm = ROW_T

    def body(du_ref, h_ref, g_ref, sc_ref, o_ref, gf_ref, dh_ref, sums_ref, d_ref, gsum_ref):
        @pl.when(pl.program_id(0) == 0)
        def _():
            sums_ref[...] = jnp.zeros_like(sums_ref)

        g = _norm_bwd_rows(du_ref[...], h_ref[...], g_ref[...], sc_ref[...], sums_ref)
        dh_ref[...] = g
        _gate_rows(g, o_ref, gf_ref, d_ref, gsum_ref)

    row = pl.BlockSpec((tm, D), lambda i: (i, 0))
    vec = pl.BlockSpec((1, D), lambda i: (0, 0))
    sums = pl.BlockSpec((8, D), lambda i: (0, 0))
    return pl.pallas_call(
        body, name=name, grid=(S // tm,),
        in_specs=[row, row, vec, vec, row, vec],
        out_specs=[row, sums, row, sums],
        out_shape=[jax.ShapeDtypeStruct((S, D), F32), jax.ShapeDtypeStruct((8, D), F32),
                   jax.ShapeDtypeStruct((S, D), BF16), jax.ShapeDtypeStruct((8, D), F32)],
        compiler_params=_params(("arbitrary",)),
    )(du, h, gain, scale, out, gatefac)


def _final_loss(name, h, target, gain, shift, scale):
    S, D = h.shape
    tm = ROW_T

    def body(h_ref, t_ref, g_ref, sh_ref, sc_ref, dy_ref, loss_ref):
        @pl.when(pl.program_id(0) == 0)
        def _():
            loss_ref[...] = jnp.zeros_like(loss_ref)

        x = h_ref[...]
        r = lax.rsqrt(jnp.mean(x * x, axis=-1, keepdims=True) + NORM_EPS)
        y = x * r * g_ref[...] * (1.0 + sc_ref[...]) + sh_ref[...]
        e = y - t_ref[...]
        dy_ref[...] = e * (1.0 / D)
        loss_ref[0:1, :] += 0.5 * jnp.sum(e * e, axis=0, keepdims=True) * (1.0 / D)

    row = pl.BlockSpec((tm, D), lambda i: (i, 0))
    vec = pl.BlockSpec((1, D), lambda i: (0, 0))
    return pl.pallas_call(
        body, name=name, grid=(S // tm,),
        in_specs=[row, row, vec, vec, vec],
        out_specs=[row, pl.BlockSpec((8, D), lambda i: (0, 0))],
        out_shape=[jax.ShapeDtypeStruct((S, D), F32), jax.ShapeDtypeStruct((8, D), F32)],
        compiler_params=_params(("arbitrary",)),
    )(h, target, gain, shift, scale)


def _ffn_up(name, u, wg, wu):
    S, D = u.shape
    G, Fg, _ = wg.shape
    tm = ROW_T

    def body(u_ref, wg_ref, wu_ref, a_ref, b_ref, hm_ref):
        uv = u_ref[...]
        a = _dot(uv, wg_ref[...], NT)
        b = _dot(uv, wu_ref[...], NT)
        sg = _sigmoid(a)
        act = a * sg
        a_ref[...] = (b * sg * (1.0 + a * (1.0 - sg))).astype(a_ref.dtype)
        b_ref[...] = act.astype(b_ref.dtype)
        hm_ref[...] = (act * b).astype(hm_ref.dtype)

    w_spec = pl.BlockSpec((None, Fg, D), lambda g, i: (g, 0, 0))
    o_spec = pl.BlockSpec((None, tm, Fg), lambda g, i: (g, i, 0))
    return pl.pallas_call(
        body, name=name, grid=(G, S // tm),
        in_specs=[pl.BlockSpec((tm, D), lambda g, i: (i, 0)), w_spec, w_spec],
        out_specs=[o_spec] * 3,
        out_shape=[jax.ShapeDtypeStruct((G, S, Fg), BF16)] * 3,
        compiler_params=_params(("parallel", "parallel")),
    )(u, wg, wu)


def _ffn_bwd_mid(name, d_out, wd, a, b):
    S, D = d_out.shape
    G, Fg, _ = wd.shape
    tm = ROW_T

    def body(d_ref, wd_ref, a_ref, b_ref, da_ref, db_ref):
        dhm = _dot(d_ref[...], wd_ref[...], NT)
        da_ref[...] = (dhm * a_ref[...].astype(F32)).astype(da_ref.dtype)
        db_ref[...] = (dhm * b_ref[...].astype(F32)).astype(db_ref.dtype)

    t_spec = pl.BlockSpec((None, tm, Fg), lambda g, i: (g, i, 0))
    return pl.pallas_call(
        body, name=name, grid=(G, S // tm),
        in_specs=[pl.BlockSpec((tm, D), lambda g, i: (i, 0)), pl.BlockSpec((None, Fg, D), lambda g, i: (g, 0, 0)),
                  t_spec, t_spec],
        out_specs=[t_spec] * 2,
        out_shape=[jax.ShapeDtypeStruct((G, S, Fg), BF16)] * 2,
        compiler_params=_params(("parallel", "parallel")),
    )(d_out, wd, a, b)


def _ret_tables(S):
    T = RET_T
    half = HEAD_DIM // 2
    inv_freq = ROPE_BASE ** (-jnp.arange(half, dtype=F32) / half)
    ang = jnp.arange(S, dtype=F32)[:, None] * inv_freq[None, :]
    cos, sin = jnp.cos(ang), jnp.sin(ang)
    cosf = jnp.concatenate([cos, cos], axis=-1)
    sinf = jnp.concatenate([-sin, sin], axis=-1)
    log_gamma = jnp.log1p(-(2.0 ** (-5.0 - jnp.arange(RET_HEADS, dtype=F32))))
    idx = jnp.arange(T, dtype=F32)
    chunk = jnp.arange(T) // 64
    vis = (chunk[None, :] <= chunk[:, None]).astype(F32)
    mask = jnp.exp(log_gamma[:, None, None] * jnp.abs(idx[:, None] - idx[None, :])) * vis[None]
    ones = jnp.ones((1, 1, LANES), F32)
    qdec = jnp.exp(log_gamma[:, None] * (idx + 1.0)[None, :])[:, :, None] * ones
    kdec = jnp.exp(log_gamma[:, None] * (T - 1.0 - idx)[None, :])[:, :, None] * ones
    cdec = jnp.exp(log_gamma * T)[:, None, None] * jnp.ones((1, 8, LANES), F32)
    return cosf, sinf, mask, qdec, kdec, cdec


def _rope(x, cosf, sinf):
    return x * cosf + pltpu.roll(x, HEAD_DIM // 2, 1) * sinf


def _rope_bwd(d, cosf, sinf):
    return d * cosf + pltpu.roll(d * sinf, HEAD_DIM // 2, 1)


def _ret_specs(T, rev_nb=None):
    if rev_nb is None:
        blk = lambda b: b
    else:
        blk = lambda b: rev_nb - 1 - b
    E, W = RET_HPS, RET_HPS * HEAD_DIM
    proj = lambda off: pl.BlockSpec((T, W), lambda hp, b: (blk(b), off * (RET_HEADS // E) + hp))
    rows = pl.BlockSpec((T, HEAD_DIM), lambda hp, b: (blk(b), 0))
    per_head = lambda shape: pl.BlockSpec((E,) + shape, lambda hp, b: (hp, 0, 0))
    return ([proj(0), proj(1), proj(2), proj(3), rows, rows,
             per_head((T, T)), per_head((T, LANES)), per_head((T, LANES)), per_head((8, LANES)),
             pl.BlockSpec((1, W), lambda hp, b: (0, hp))], blk)


def _ret_fwd(name, proj, gn, tables):
    S = proj.shape[0]
    T, E = RET_T, RET_HPS
    nb = S // T
    scale = HEAD_DIM ** -0.5
    specs, _ = _ret_specs(T)

    def body(q_ref, k_ref, v_ref, g_ref, cos_ref, sin_ref, m_ref, qd_ref, kd_ref, cd_ref, gn_ref, y_ref, st_ref, st):
        @pl.when(pl.program_id(1) == 0)
        def _():
            st[...] = jnp.zeros_like(st)

        cosf, sinf = cos_ref[...], sin_ref[...]
        for e in range(E):
            qr = _rope(q_ref[:, _lanes(e)], cosf, sinf)
            kr = _rope(k_ref[:, _lanes(e)], cosf, sinf) * scale
            v = v_ref[:, _lanes(e)]
            sp = st[e]
            st_ref[e, 0] = sp
            w = _dot(qr, kr, NT) * m_ref[e]
            y = _dot(w, v) + _dot(qr * qd_ref[e], sp)
            st[e] = cd_ref[e, 0:1, :] * sp + _dot(kr * kd_ref[e], v, TN)
            r = lax.rsqrt(jnp.mean(y * y, axis=-1, keepdims=True) + NORM_EPS)
            g = g_ref[:, _lanes(e)]
            y_ref[:, _lanes(e)] = y * r * gn_ref[:, _lanes(e)] * (g * _sigmoid(g))

    return pl.pallas_call(
        body, name=name, grid=(RET_HEADS // E, nb), in_specs=specs,
        out_specs=[pl.BlockSpec((T, E * HEAD_DIM), lambda hp, b: (b, hp)),
                   pl.BlockSpec((E, 1, HEAD_DIM, HEAD_DIM), lambda hp, b: (hp, b, 0, 0))],
        out_shape=[jax.ShapeDtypeStruct((S, RET_HEADS * HEAD_DIM), F32),
                   jax.ShapeDtypeStruct((RET_HEADS, nb, HEAD_DIM, HEAD_DIM), F32)],
        scratch_shapes=[pltpu.VMEM((E, HEAD_DIM, HEAD_DIM), F32)],
        compiler_params=_params(("parallel", "arbitrary")),
    )(proj, proj, proj, proj, tables[0], tables[1], tables[2], tables[3], tables[4], tables[5], gn)


def _ret_bwd(name, proj, gn, tables, states, dmix):
    S = proj.shape[0]
    T, E = RET_T, RET_HPS
    W = E * HEAD_DIM
    nb = S // T
    scale = HEAD_DIM ** -0.5
    specs, blk = _ret_specs(T, rev_nb=nb)
    specs = specs + [pl.BlockSpec((E, 1, HEAD_DIM, HEAD_DIM), lambda hp, b: (hp, blk(b), 0, 0)),
                     pl.BlockSpec((T, W), lambda hp, b: (blk(b), hp))]

    def body(q_ref, k_ref, v_ref, g_ref, cos_ref, sin_ref, m_ref, qd_ref, kd_ref, cd_ref, gn_ref, st_ref, do_ref,
             d_ref, dgn_ref, dst):
        @pl.when(pl.program_id(1) == 0)
        def _():
            dst[...] = jnp.zeros_like(dst)
            dgn_ref[...] = jnp.zeros_like(dgn_ref)

        cosf, sinf = cos_ref[...], sin_ref[...]
        for e in range(E):
            qr = _rope(q_ref[:, _lanes(e)], cosf, sinf)
            kr = _rope(k_ref[:, _lanes(e)], cosf, sinf) * scale
            v = v_ref[:, _lanes(e)]
            sp = st_ref[e, 0]
            mask, qd, kd = m_ref[e], qd_ref[e], kd_ref[e]
            w = _dot(qr, kr, NT) * mask
            y = _dot(w, v) + _dot(qr * qd, sp)
            r = lax.rsqrt(jnp.mean(y * y, axis=-1, keepdims=True) + NORM_EPS)
            yh = y * r
            gn_v = gn_ref[:, _lanes(e)]
            g = g_ref[:, _lanes(e)]
            sg = _sigmoid(g)
            do = do_ref[:, _lanes(e)]
            dyn = do * g * sg
            dgn_ref[:, _lanes(e)] += jnp.sum(dyn * yh, axis=0, keepdims=True)
            dyh = dyn * gn_v
            dy = r * (dyh - yh * jnp.mean(dyh * yh, axis=-1, keepdims=True))
            dg = do * yh * gn_v * sg * (1.0 + g * (1.0 - sg))
            ds = dst[e]
            dp = _dot(dy, v, NT) * mask
            dqr = _dot(dp, kr) + _dot(dy, sp, NT) * qd
            dkr = _dot(dp, qr, TN) + _dot(v, ds, NT) * kd
            dv = _dot(w, dy, TN) + _dot(kr * kd, ds)
            dst[e] = cd_ref[e, 0:1, :] * ds + _dot(qr * qd, dy, TN)
            d_ref[0, :, _lanes(e)] = _rope_bwd(dqr, cosf, sinf).astype(d_ref.dtype)
            d_ref[1, :, _lanes(e)] = _rope_bwd(dkr * scale, cosf, sinf).astype(d_ref.dtype)
            d_ref[2, :, _lanes(e)] = dv.astype(d_ref.dtype)
            d_ref[3, :, _lanes(e)] = dg.astype(d_ref.dtype)

    return pl.pallas_call(
        body, name=name, grid=(RET_HEADS // E, nb), in_specs=specs,
        out_specs=[pl.BlockSpec((4, T, W), lambda hp, b: (0, blk(b), hp)),
                   pl.BlockSpec((1, W), lambda hp, b: (0, hp))],
        out_shape=[jax.ShapeDtypeStruct((4, S, RET_HEADS * HEAD_DIM), BF16),
                   jax.ShapeDtypeStruct((1, RET_HEADS * HEAD_DIM), F32)],
        scratch_shapes=[pltpu.VMEM((E, HEAD_DIM, HEAD_DIM), F32)],
        compiler_params=_params(("parallel", "arbitrary")),
    )(proj, proj, proj, proj, tables[0], tables[1], tables[2], tables[3], tables[4], tables[5], gn, states, dmix)


def _sb_logits(qb, kb, i, j, scale):
    T = SB_T
    z = lax.dot_general(qb, kb, NT, preferred_element_type=F32) * scale
    row = lax.broadcasted_iota(jnp.int32, (T, T), 0)
    col = lax.broadcasted_iota(jnp.int32, (T, T), 1)
    vis = jnp.logical_or(j < i, col < row)
    lp = jnp.log1p(jnp.exp(-jnp.abs(z)))
    lb = jnp.minimum(z, 0.0) - lp
    lk = jnp.where(vis, -jnp.maximum(z, 0.0) - lp, 0.0)
    return lb, lk, vis


def _sb_weights(lb, lk, vis, tailc):
    T = SB_T
    row = lax.broadcasted_iota(jnp.int32, (T, T), 0)
    col = lax.broadcasted_iota(jnp.int32, (T, T), 1)
    tail = tailc + _split_dot(row > col, lk, lhs01=False, pieces=2)
    return jnp.where(vis, jnp.exp(lb + tail), 0.0)


def _sb_specs(S):
    T, W = SB_T, SB_HPS * HEAD_DIM
    return [pl.BlockSpec((T, W), lambda hp, i: (i, 2048 // W + hp)),
            pl.BlockSpec((S, W), lambda hp, i: (0, hp)),
            pl.BlockSpec((S, W), lambda hp, i: (0, 512 // W + hp))]


def _lanes(e):
    return slice(e * HEAD_DIM, (e + 1) * HEAD_DIM)


def _sb_fwd(name, proj, kv):
    S = proj.shape[0]
    T, E = SB_T, SB_HPS
    nq = S // T
    scale = HEAD_DIM ** -0.5

    def body(q_ref, k_ref, v_ref, y_ref):
        i = pl.program_id(1)
        qs = [q_ref[:, _lanes(e)].astype(BF16) for e in range(E)]

        def cond(c):
            return jnp.logical_and(c[0] >= 0, c[1] == 0)

        def step(c):
            j, _, tails, accs = c
            rows = pl.ds(pl.multiple_of(j * T, T), T)
            new_tails, new_accs, worst = [], [], None
            for e in range(E):
                lb, lk, vis = _sb_logits(qs[e], k_ref[rows, _lanes(e)], i, j, scale)
                w = _sb_weights(lb, lk, vis, tails[e])
                new_accs.append(accs[e] + lax.dot_general(w.astype(BF16), v_ref[rows, _lanes(e)], NN,
                                                          preferred_element_type=F32))
                t = tails[e] + jnp.sum(lk, axis=1, keepdims=True)
                new_tails.append(t)
                worst = jnp.max(t) if worst is None else jnp.maximum(worst, jnp.max(t))
            return j - 1, (worst < SB_DEAD).astype(jnp.int32), tuple(new_tails), tuple(new_accs)

        init = (i, jnp.int32(0), (jnp.zeros((T, 1), F32),) * E, (jnp.zeros((T, HEAD_DIM), F32),) * E)
        accs = lax.while_loop(cond, step, init)[3]
        for e in range(E):
            y_ref[:, _lanes(e)] = accs[e]

    return pl.pallas_call(
        body, name=name, grid=(SB_HEADS // E, nq), in_specs=_sb_specs(S),
        out_specs=pl.BlockSpec((T, E * HEAD_DIM), lambda hp, i: (i, hp)),
        out_shape=jax.ShapeDtypeStruct((S, SB_HEADS * HEAD_DIM), F32),
        compiler_params=_params(("parallel", "arbitrary")),
    )(proj, kv, kv)


def _sb_bwd(name, proj, kv, dmix):
    S = proj.shape[0]
    T, E = SB_T, SB_HPS
    nq = S // T
    scale = HEAD_DIM ** -0.5

    def body(q_ref, k_ref, v_ref, do_ref, dq_ref, dk_ref, dv_ref):
        i = pl.program_id(1)

        @pl.when(i == 0)
        def _():
            dk_ref[...] = jnp.zeros_like(dk_ref)
            dv_ref[...] = jnp.zeros_like(dv_ref)

        qs = [q_ref[:, _lanes(e)].astype(BF16) for e in range(E)]
        dos = [do_ref[:, _lanes(e)].astype(BF16) for e in range(E)]
        row = lax.broadcasted_iota(jnp.int32, (T, T), 0)
        col = lax.broadcasted_iota(jnp.int32, (T, T), 1)
        zcol = (jnp.zeros((T, 1), F32),) * E

        def cond(c):
            return jnp.logical_and(c[0] >= 0, c[1] == 0)

        def walk_left(c):
            j, _, tails = c
            rows = pl.ds(pl.multiple_of(j * T, T), T)
            new_tails, worst = [], None
            for e in range(E):
                _, lk, _ = _sb_logits(qs[e], k_ref[rows, _lanes(e)], i, j, scale)
                t = tails[e] + jnp.sum(lk, axis=1, keepdims=True)
                new_tails.append(t)
                worst = jnp.max(t) if worst is None else jnp.maximum(worst, jnp.max(t))
            return j - 1, (worst < SB_DEAD).astype(jnp.int32), tuple(new_tails)

        j_end, _, totals = lax.while_loop(cond, walk_left, (i, jnp.int32(0), zcol))

        def walk_back(j, c):
            lefts, used, dqs = c
            rows = pl.ds(pl.multiple_of(j * T, T), T)
            new_lefts, new_used, new_dqs = [], [], []
            for e in range(E):
                kb = k_ref[rows, _lanes(e)]
                lb, lk, vis = _sb_logits(qs[e], kb, i, j, scale)
                u = used[e] + jnp.sum(lk, axis=1, keepdims=True)
                w = _sb_weights(lb, lk, vis, totals[e] - u)
                de = lax.dot_general(dos[e], v_ref[rows, _lanes(e)], NT, preferred_element_type=F32) * w
                dlk = jnp.where(vis, lefts[e] + _split_dot(row < col, de, lhs01=False, pieces=2), 0.0)
                sg = jnp.exp(lb)
                dz = ((de * (1.0 - sg) - dlk * sg) * scale).astype(BF16)
                new_dqs.append(dqs[e] + lax.dot_general(dz, kb, NN, preferred_element_type=F32))
                dk_ref[rows, _lanes(e)] += lax.dot_general(dz, qs[e], TN, preferred_element_type=F32)
                dv_ref[rows, _lanes(e)] += lax.dot_general(w.astype(BF16), dos[e], TN, preferred_element_type=F32)
                new_lefts.append(lefts[e] + jnp.sum(de, axis=1, keepdims=True))
                new_used.append(u)
            return tuple(new_lefts), tuple(new_used), tuple(new_dqs)

        init = (zcol, zcol, (jnp.zeros((T, HEAD_DIM), F32),) * E)
        dqs = lax.fori_loop(j_end + 1, i + 1, walk_back, init)[2]
        for e in range(E):
            dq_ref[:, _lanes(e)] = dqs[e].astype(dq_ref.dtype)

    W = E * HEAD_DIM
    blk = pl.BlockSpec((T, W), lambda hp, i: (i, hp))
    full = pl.BlockSpec((S, W), lambda hp, i: (0, hp))
    shp = jax.ShapeDtypeStruct((S, SB_HEADS * HEAD_DIM), F32)
    return pl.pallas_call(
        body, name=name, grid=(SB_HEADS // E, nq),
        in_specs=_sb_specs(S) + [pl.BlockSpec((T, W), lambda hp, i: (i, 512 // W + hp))],
        out_specs=[blk, full, full],
        out_shape=[jax.ShapeDtypeStruct((S, SB_HEADS * HEAD_DIM), BF16), shp, shp],
        compiler_params=_params(("parallel", "arbitrary")),
    )(proj, kv, kv, dmix)


def _conv_fwd(name, proj, conv_w, conv_b):
    S = proj.shape[0]
    T = CONV_T
    C = 1024
    K = SSM_CONV

    def body(x_ref, w_ref, b_ref, o_ref, buf):
        @pl.when(pl.program_id(0) == 0)
        def _():
            buf[0:8, :] = jnp.zeros((8, C), F32)

        buf[8:T + 8, :] = x_ref[...]
        acc = b_ref[...] + w_ref[K - 1:K, :] * buf[8:T + 8, :]
        for k in range(K - 1):
            acc = acc + w_ref[k:k + 1, :] * buf[5 + k:5 + k + T, :]
        o_ref[...] = acc * _sigmoid(acc)
        buf[0:8, :] = buf[T:T + 8, :]

    return pl.pallas_call(
        body, name=name, grid=(S // T,),
        in_specs=[pl.BlockSpec((T, C), lambda i: (i, 4)), pl.BlockSpec((K, C), lambda i: (0, 0)),
                  pl.BlockSpec((1, C), lambda i: (0, 0))],
        out_specs=pl.BlockSpec((T, C), lambda i: (i, 0)),
        out_shape=jax.ShapeDtypeStruct((S, C), F32),
        scratch_shapes=[pltpu.VMEM((T + 8, C), F32)],
        compiler_params=_params(("arbitrary",)),
    )(proj, conv_w, conv_b)


def _conv_bwd_act(name, proj, conv_w, conv_b, dxbc):
    S = proj.shape[0]
    T = CONV_T
    C = 1024
    K = SSM_CONV

    def body(x_ref, w_ref, b_ref, d_ref, dp_ref, sums_ref, buf):
        @pl.when(pl.program_id(0) == 0)
        def _():
            buf[0:8, :] = jnp.zeros((8, C), F32)
            sums_ref[...] = jnp.zeros_like(sums_ref)

        buf[8:T + 8, :] = x_ref[...]
        acc = b_ref[...] + w_ref[K - 1:K, :] * buf[8:T + 8, :]
        for k in range(K - 1):
            acc = acc + w_ref[k:k + 1, :] * buf[5 + k:5 + k + T, :]
        sg = _sigmoid(acc)
        dp = d_ref[...] * sg * (1.0 + acc * (1.0 - sg))
        dp_ref[...] = dp
        for k in range(K):
            sums_ref[k:k + 1, :] += jnp.sum(dp * buf[5 + k:5 + k + T, :], axis=0, keepdims=True)
        sums_ref[4:5, :] += jnp.sum(dp, axis=0, keepdims=True)
        buf[0:8, :] = buf[T:T + 8, :]

    row = pl.BlockSpec((T, C), lambda i: (i, 0))
    return pl.pallas_call(
        body, name=name, grid=(S // T,),
        in_specs=[pl.BlockSpec((T, C), lambda i: (i, 4)), pl.BlockSpec((K, C), lambda i: (0, 0)),
                  pl.BlockSpec((1, C), lambda i: (0, 0)), row],
        out_specs=[row, pl.BlockSpec((8, C), lambda i: (0, 0))],
        out_shape=[jax.ShapeDtypeStruct((S, C), F32), jax.ShapeDtypeStruct((8, C), F32)],
        scratch_shapes=[pltpu.VMEM((T + 8, C), F32)],
        compiler_params=_params(("arbitrary",)),
    )(proj, conv_w, conv_b, dxbc)


def _conv_bwd_in(name, dp, conv_w):
    S, C = dp.shape
    T = CONV_T
    K = SSM_CONV
    nb = S // T

    def body(d_ref, w_ref, o_ref, buf):
        @pl.when(pl.program_id(0) == 0)
        def _():
            buf[T:T + 8, :] = jnp.zeros((8, C), F32)

        buf[0:T, :] = d_ref[...]
        acc = w_ref[K - 1:K, :] * buf[0:T, :]
        for k in range(K - 1):
            acc = acc + w_ref[k:k + 1, :] * buf[3 - k:3 - k + T, :]
        o_ref[...] = acc.astype(o_ref.dtype)
        buf[T:T + 8, :] = buf[0:8, :]

    row = pl.BlockSpec((T, C), lambda i: (nb - 1 - i, 0))
    return pl.pallas_call(
        body, name=name, grid=(nb,),
        in_specs=[row, pl.BlockSpec((K, C), lambda i: (0, 0))],
        out_specs=row,
        out_shape=jax.ShapeDtypeStruct((S, C), BF16),
        scratch_shapes=[pltpu.VMEM((T + 8, C), F32)],
        compiler_params=_params(("arbitrary",)),
    )(dp, conv_w)


def _softplus(x):
    return jnp.maximum(x, 0.0) + jnp.log1p(jnp.exp(-jnp.abs(x)))


def _ssd_common(xbc_ref, dt_ref, dtb_ref, alog_ref):
    T = SSD_T
    dtr = dt_ref[...] + dtb_ref[...]
    dt = _softplus(dtr)
    a = -jnp.exp(alog_ref[...])
    dta = dt * a
    row = lax.broadcasted_iota(jnp.int32, (T, T), 0)
    col = lax.broadcasted_iota(jnp.int32, (T, T), 1)
    causal = col <= row
    acum = _split_dot(causal, dta)
    acum_t = acum.T
    return dtr, dt, a, acum, acum_t, causal, row


def _ssd_head(xbc_ref, h, dt, acum, acum_t, causal, cb):
    P = SSM_P
    ac = acum[:, h:h + 1]
    decay = jnp.exp(jnp.where(causal, ac - acum_t[h:h + 1, :], -1e30))
    dth = dt[:, h:h + 1]
    xs = xbc_ref[:, h * P:(h + 1) * P]
    xdt = xs * dth
    sc = (cb * decay).astype(BF16)
    aend = acum[SSD_T - 1:SSD_T, h:h + 1]
    return ac, dth, xs, xdt, sc, aend


def _ssd_specs(T, blk):
    vec = lambda n: pl.BlockSpec((1, n), lambda b: (0, 0))
    return [pl.BlockSpec((T, 1024), lambda b: (blk(b), 0)),
            pl.BlockSpec((T, LANES), lambda b: (blk(b), DT_COL // LANES)),
            pl.BlockSpec((T, 512), lambda b: (blk(b), 7)),
            vec(LANES), vec(LANES), vec(LANES), vec(512)]


def _ssd_fwd(name, xbc, proj, dt_bias, a_log, d_skip, gain):
    S = xbc.shape[0]
    T = SSD_T
    nb = S // T
    H, P, N = SSM_HEADS, SSM_P, SSM_N

    def body(xbc_ref, dt_ref, z_ref, dtb_ref, alog_ref, dsk_ref, gain_ref, y_ref, st_ref, st, ybuf):
        @pl.when(pl.program_id(0) == 0)
        def _():
            st[...] = jnp.zeros_like(st)

        _, dt, _, acum, acum_t, causal, _ = _ssd_common(xbc_ref, dt_ref, dtb_ref, alog_ref)
        for g in range(SSM_GROUPS):
            bg = xbc_ref[:, 512 + g * N:512 + (g + 1) * N]
            cg = xbc_ref[:, 768 + g * N:768 + (g + 1) * N]
            cb = _dot(cg, bg, NT)
            for hh in range(H // SSM_GROUPS):
                h = g * (H // SSM_GROUPS) + hh
                ac, _, xs, xdt, sc, aend = _ssd_head(xbc_ref, h, dt, acum, acum_t, causal, cb)
                sp = st[h]
                st_ref[0, h] = sp
                y = _dot(sc, xdt) + jnp.exp(ac) * _dot(cg, sp) + xs * dsk_ref[:, h:h + 1]
                st[h] = jnp.exp(aend) * sp + _dot(bg, xdt * jnp.exp(aend - ac), TN)
                ybuf[:, h * P:(h + 1) * P] = y
        z = z_ref[...]
        yg = ybuf[...] * z * _sigmoid(z)
        r = lax.rsqrt(jnp.mean(yg * yg, axis=-1, keepdims=True) + NORM_EPS)
        y_ref[...] = yg * r * gain_ref[...]

    return pl.pallas_call(
        body, name=name, grid=(nb,), in_specs=_ssd_specs(T, lambda b: b),
        out_specs=[pl.BlockSpec((T, 512), lambda b: (b, 0)), pl.BlockSpec((1, H, N, P), lambda b: (b, 0, 0, 0))],
        out_shape=[jax.ShapeDtypeStruct((S, 512), F32), jax.ShapeDtypeStruct((nb, H, N, P), F32)],
        scratch_shapes=[pltpu.VMEM((H, N, P), F32), pltpu.VMEM((T, 512), F32)],
        compiler_params=_params(("arbitrary",)),
    )(xbc, proj, proj, dt_bias, a_log, d_skip, gain)


def _ssd_bwd(name, xbc, proj, dt_bias, a_log, d_skip, gain, states, dmix):
    S = xbc.shape[0]
    T = SSD_T
    nb = S // T
    H, P, N = SSM_HEADS, SSM_P, SSM_N
    blk = lambda b: nb - 1 - b

    def body(xbc_ref, dt_ref, z_ref, dtb_ref, alog_ref, dsk_ref, gain_ref, st_ref, do_ref,
             dx_ref, dz_ref, ddt_ref, sums_ref, dst, ybuf):
        @pl.when(pl.program_id(0) == 0)
        def _():
            dst[...] = jnp.zeros_like(dst)
            sums_ref[...] = jnp.zeros_like(sums_ref)

        dtr, dt, a, acum, acum_t, causal, row = _ssd_common(xbc_ref, dt_ref, dtb_ref, alog_ref)
        cbs = []
        for g in range(SSM_GROUPS):
            bg = xbc_ref[:, 512 + g * N:512 + (g + 1) * N]
            cg = xbc_ref[:, 768 + g * N:768 + (g + 1) * N]
            cb = _dot(cg, bg, NT)
            cbs.append(cb)
            for hh in range(H // SSM_GROUPS):
                h = g * (H // SSM_GROUPS) + hh
                ac, _, xs, xdt, sc, _ = _ssd_head(xbc_ref, h, dt, acum, acum_t, causal, cb)
                ybuf[:, h * P:(h + 1) * P] = (_dot(sc, xdt) + jnp.exp(ac) * _dot(cg, st_ref[0, h])
                                              + xs * dsk_ref[:, h:h + 1])
        z = z_ref[...]
        sg = _sigmoid(z)
        sz = z * sg
        yfull = ybuf[...]
        yg = yfull * sz
        r = lax.rsqrt(jnp.mean(yg * yg, axis=-1, keepdims=True) + NORM_EPS)
        yh = yg * r
        do = do_ref[...]
        sums_ref[0:1, :] += jnp.sum(do * yh, axis=0, keepdims=True)
        dyh = do * gain_ref[...]
        dyg = r * (dyh - yh * jnp.mean(dyh * yh, axis=-1, keepdims=True))
        dz_ref[...] = (dyg * yfull * sg * (1.0 + z * (1.0 - sg))).astype(dz_ref.dtype)
        dyv = dyg * sz

        lane = lax.broadcasted_iota(jnp.int32, (T, LANES), 1)
        rowl = lax.broadcasted_iota(jnp.int32, (T, 1), 0)
        dacum = jnp.zeros((T, LANES), F32)
        dacum_t = jnp.zeros((LANES, T), F32)
        sub = lax.broadcasted_iota(jnp.int32, (LANES, T), 0)
        ddt = jnp.zeros((T, LANES), F32)
        dskp = jnp.zeros((T, LANES), F32)
        for g in range(SSM_GROUPS):
            bg = xbc_ref[:, 512 + g * N:512 + (g + 1) * N]
            cg = xbc_ref[:, 768 + g * N:768 + (g + 1) * N]
            cb = cbs[g]
            dcb = jnp.zeros((T, T), F32)
            dbg = jnp.zeros((T, N), F32)
            dcg = jnp.zeros((T, N), F32)
            for hh in range(H // SSM_GROUPS):
                h = g * (H // SSM_GROUPS) + hh
                ac, dth, xs, xdt, sc, aend = _ssd_head(xbc_ref, h, dt, acum, acum_t, causal, cb)
                decay = jnp.exp(jnp.where(causal, ac - acum_t[h:h + 1, :], -1e30))
                dy = dyv[:, h * P:(h + 1) * P]
                sp = st_ref[0, h]
                ea = jnp.exp(ac)
                de = jnp.exp(aend - ac)
                dec = jnp.exp(aend)
                dskp = dskp + jnp.where(lane == h, jnp.sum(dy * xs, axis=1, keepdims=True), 0.0)
                dxdt = _dot(sc, dy, TN)
                dsd = _dot(dy, xdt, NT) * decay
                dcb = dcb + dsd
                e = dsd * cb
                dacum_t = dacum_t + jnp.where(sub == h, jnp.sum(e, axis=0, keepdims=True), 0.0)
                dac = jnp.sum(e, axis=1, keepdims=True)
                dyea = dy * ea
                dac = dac + jnp.sum(dyea * _dot(cg, sp), axis=1, keepdims=True)
                dcg = dcg + _dot(dyea, sp, NT)
                dsp = _dot(cg, dyea, TN)
                dsn = dst[h]
                xde = xdt * de
                dbg = dbg + _dot(xde, dsn, NT)
                wh = _dot(bg, dsn)
                dxdt = dxdt + wh * de
                r_end = jnp.sum(wh * xde, axis=1, keepdims=True)
                dend = jnp.sum(r_end) + jnp.sum(dsn * sp) * dec
                dst[h] = dsp + dec * dsn
                dac = dac - r_end + jnp.where(rowl == T - 1, dend, 0.0)
                dacum = dacum + jnp.where(lane == h, dac, 0.0)
                ddt = ddt + jnp.where(lane == h, jnp.sum(dxdt * xs, axis=1, keepdims=True), 0.0)
                dx_ref[:, h * P:(h + 1) * P] = dxdt * dth + dy * dsk_ref[:, h:h + 1]
            dx_ref[:, 512 + g * N:512 + (g + 1) * N] = dbg + _dot(dcb, cg, TN)
            dx_ref[:, 768 + g * N:768 + (g + 1) * N] = dcg + _dot(dcb, bg)
        ddta = _split_dot(row <= lax.broadcasted_iota(jnp.int32, (T, T), 1), dacum - dacum_t.T)
        ddt = ddt + ddta * a
        ddtr = ddt * _sigmoid(dtr)
        ddt_ref[...] = ddtr.astype(ddt_ref.dtype)
        sums_ref[1:2, 0:LANES] += jnp.sum(ddtr, axis=0, keepdims=True)
        sums_ref[1:2, LANES:2 * LANES] += jnp.sum(ddta * dt, axis=0, keepdims=True) * a
        sums_ref[1:2, 2 * LANES:3 * LANES] += jnp.sum(dskp, axis=0, keepdims=True)

    specs = _ssd_specs(T, blk) + [pl.BlockSpec((1, H, N, P), lambda b: (blk(b), 0, 0, 0)),
                                  pl.BlockSpec((T, 512), lambda b: (blk(b), 2))]
    return pl.pallas_call(
        body, name=name, grid=(nb,), in_specs=specs,
        out_specs=[pl.BlockSpec((T, 1024), lambda b: (blk(b), 0)), pl.BlockSpec((T, 512), lambda b: (blk(b), 0)),
                   pl.BlockSpec((T, LANES), lambda b: (blk(b), 0)), pl.BlockSpec((8, 512), lambda b: (0, 0))],
        out_shape=[jax.ShapeDtypeStruct((S, 1024), F32), jax.ShapeDtypeStruct((S, 512), BF16),
                   jax.ShapeDtypeStruct((S, LANES), BF16), jax.ShapeDtypeStruct((8, 512), F32)],
        scratch_shapes=[pltpu.VMEM((H, N, P), F32), pltpu.VMEM((T, 512), F32)],
        compiler_params=_params(("arbitrary",)),
    )(xbc, proj, proj, dt_bias, a_log, d_skip, gain, states, dmix)


def _place():
    return lax.axis_index("x"), lax.axis_index("y"), lax.axis_index("c")


def _flip(v, bit):
    return 1 - v if bit else v


def _gather_small(name, v):
    R, C = v.shape

    def body(v_ref, out_ref, send_sems, recv_sems, local_sem):
        x, y, c = _place()
        me = 4 * x + 2 * y + c
        mine = pltpu.make_async_copy(v_ref, out_ref.at[me], local_sem)
        mine.start()
        peers = [(_flip(x, (k >> 2) & 1), _flip(y, (k >> 1) & 1), _flip(c, k & 1)) for k in range(1, 8)]
        sends = []
        for k, peer in enumerate(peers):
            cp = pltpu.make_async_remote_copy(src_ref=v_ref, dst_ref=out_ref.at[me], send_sem=send_sems.at[k],
                                              recv_sem=recv_sems.at[k], device_id=peer, device_id_type=MESH)
            cp.start()
            sends.append(cp)
        for k, (px, py, pc) in enumerate(peers):
            pltpu.make_async_remote_copy(src_ref=v_ref, dst_ref=out_ref.at[4 * px + 2 * py + pc],
                                         send_sem=send_sems.at[k], recv_sem=recv_sems.at[k],
                                         device_id=(px, py, pc), device_id_type=MESH).wait_recv()
        for cp in sends:
            cp.wait_send()
        mine.wait()

    return pl.pallas_call(
        body, name=name, out_shape=jax.ShapeDtypeStruct((8, R, C), v.dtype),
        in_specs=[pl.BlockSpec(memory_space=pltpu.VMEM)], out_specs=pl.BlockSpec(memory_space=pltpu.VMEM),
        scratch_shapes=[pltpu.SemaphoreType.DMA((7,)), pltpu.SemaphoreType.DMA((7,)), pltpu.SemaphoreType.DMA(())],
    )(v)


def _hbm_call(body, name, arrays, out_shapes, n_sems):
    spec = pl.BlockSpec(memory_space=pl.ANY)
    return pl.pallas_call(
        body, name=name, out_shape=out_shapes, in_specs=[spec] * len(arrays), out_specs=[spec] * len(out_shapes),
        scratch_shapes=[pltpu.SemaphoreType.DMA((n_sems,)), pltpu.SemaphoreType.DMA((n_sems,))],
    )(*arrays)


def _chip_exchange(name, arrays, gather):
    n = len(arrays)

    def body(*refs):
        ins, outs, send_sems, recv_sems = refs[:n], refs[n:2 * n], refs[2 * n], refs[2 * n + 1]
        x, y, c = _place()
        chip = 2 * x + y
        peers = [(_flip(x, (k >> 1) & 1), _flip(y, k & 1)) for k in range(1, 4)]

        def copy(a, k, slot):
            px, py = peers[k]
            return pltpu.make_async_remote_copy(
                src_ref=ins[a].at[c] if gather else ins[a].at[2 * px + py], dst_ref=outs[a].at[slot],
                send_sem=send_sems.at[3 * a + k], recv_sem=recv_sems.at[3 * a + k],
                device_id=(px, py, c), device_id_type=MESH)

        sends = [copy(a, k, chip) for a in range(n) for k in range(3)]
        for cp in sends:
            cp.start()
        for a in range(n):
            for k, (px, py) in enumerate(peers):
                copy(a, k, 2 * px + py).wait_recv()
        for cp in sends:
            cp.wait_send()

    shapes = [jax.ShapeDtypeStruct((4,) + a.shape[1:], a.dtype) for a in arrays]
    return _hbm_call(body, name, arrays, shapes, 3 * n)


def _sibling_swap(name, arrays, alt=None):
    n = len(arrays)

    def body(*refs):
        k = 1 if alt is None else 2
        outs, send_sems, recv_sems = refs[k * n:(k + 1) * n], refs[(k + 1) * n], refs[(k + 1) * n + 1]
        x, y, c = _place()

        def exchange(srcs):
            cps = [pltpu.make_async_remote_copy(src_ref=srcs[a], dst_ref=outs[a], send_sem=send_sems.at[a],
                                                recv_sem=recv_sems.at[a], device_id=(x, y, 1 - c),
                                                device_id_type=MESH) for a in range(n)]
            for cp in cps:
                cp.start()
            for cp in cps:
                cp.wait()

        if alt is None:
            exchange(refs[:n])
        else:
            @pl.when(c == 1)
            def _():
                exchange(refs[:n])

            @pl.when(c == 0)
            def _():
                exchange(refs[n:2 * n])

    shapes = [jax.ShapeDtypeStruct(a.shape, a.dtype) for a in arrays]
    return _hbm_call(body, name, list(arrays) + ([] if alt is None else list(alt)), shapes, n)


PACK_C = 1024
SUM_STEPS = 4


def _add_layers(name, own0, own1, other, c):
    n = len(other)

    def body(c_ref, *refs):
        for a in range(n):
            mine = jnp.where(c_ref[0] == 0, refs[a][...].astype(F32), refs[n + a][...].astype(F32))
            refs[3 * n + a][...] = (mine + refs[2 * n + a][...].astype(F32)).astype(BF16)

    specs = [pl.BlockSpec((1, o.shape[1] // SUM_STEPS, o.shape[2]), lambda p, i, c_ref: (p, i, 0)) for o in other]
    grid_spec = pltpu.PrefetchScalarGridSpec(num_scalar_prefetch=1, grid=(4, SUM_STEPS), in_specs=specs * 3,
                                             out_specs=specs)
    return pl.pallas_call(
        body, name=name, grid_spec=grid_spec, out_shape=[jax.ShapeDtypeStruct(o.shape, BF16) for o in other],
        compiler_params=_params(("parallel", "parallel")),
    )(jnp.reshape(c, (1,)).astype(jnp.int32), *own0, *own1, *other)


def _sum_chips(name, received, own, chip):
    n = len(received)

    def body(chip_ref, *refs):
        for a in range(n):
            s = None
            for q in range(4):
                term = jnp.where(chip_ref[0] == q, refs[n + a][0], refs[a][q]).astype(F32)
                s = term if s is None else s + term
            refs[2 * n + a][...] = s

    rec = [pl.BlockSpec((4, r.shape[1] // SUM_STEPS, r.shape[2]), lambda i, chip_ref: (0, i, 0)) for r in received]
    mine = [pl.BlockSpec((1, r.shape[1] // SUM_STEPS, r.shape[2]), lambda i, chip_ref: (chip_ref[0], i, 0))
            for r in received]
    outs = [pl.BlockSpec((r.shape[1] // SUM_STEPS, r.shape[2]), lambda i, chip_ref: (i, 0)) for r in received]
    grid_spec = pltpu.PrefetchScalarGridSpec(num_scalar_prefetch=1, grid=(SUM_STEPS,), in_specs=rec + mine,
                                             out_specs=outs)
    return pl.pallas_call(
        body, name=name, grid_spec=grid_spec,
        out_shape=[jax.ShapeDtypeStruct(r.shape[1:], F32) for r in received],
        compiler_params=_params(("parallel",)),
    )(jnp.reshape(chip, (1,)).astype(jnp.int32), *received, *own)


def _sum_devices(name, parts):
    _, R, C = parts.shape

    def body(p_ref, s_ref):
        s = p_ref[0]
        for q in range(1, 8):
            s = s + p_ref[q]
        s_ref[...] = s

    return pl.pallas_call(body, name=name, out_shape=jax.ShapeDtypeStruct((R, C), F32))(parts)


def _adamw_rule(w, g, m, v):
    nm = ADAM_B1 * m + (1.0 - ADAM_B1) * g
    nv = ADAM_B2 * v + (1.0 - ADAM_B2) * (g * g)
    m_hat = nm / (1.0 - ADAM_B1 ** ADAM_STEP)
    v_hat = nv / (1.0 - ADAM_B2 ** ADAM_STEP)
    return -ADAM_LR * (m_hat / (jnp.sqrt(v_hat) + ADAM_EPS) + ADAM_WD * w), nm, nv


def _adamw(name, w, g, m, v):
    L, R, C = w.shape
    tr = 128 if R % 128 == 0 else R

    def body(w_ref, g_ref, m_ref, v_ref, d_ref, nm_ref, nv_ref):
        d_ref[...], nm_ref[...], nv_ref[...] = _adamw_rule(w_ref[...], g_ref[...], m_ref[...], v_ref[...])

    spec = pl.BlockSpec((None, tr, C), lambda l, i: (l, i, 0))
    return pl.pallas_call(
        body, name=name, grid=(L, R // tr), in_specs=[spec] * 4, out_specs=[spec] * 3,
        out_shape=[jax.ShapeDtypeStruct((L, R, C), F32)] * 3, compiler_params=_params(("parallel", "parallel")),
    )(w, g, m, v)


def _adamw_layers(name, w, g_mine, g_other, m, v, c):
    L, R, C = w.shape
    tr = 128 if R % 128 == 0 else R

    def body(c_ref, w_ref, gm_ref, go_ref, m_ref, v_ref, g_ref, d_ref, nm_ref, nv_ref):
        gv = jnp.where(pl.program_id(0) == c_ref[0], gm_ref[...], go_ref[...])
        g_ref[...] = gv
        d_ref[...], nm_ref[...], nv_ref[...] = _adamw_rule(w_ref[...], gv, m_ref[...], v_ref[...])

    full = pl.BlockSpec((None, tr, C), lambda l, i, c_ref: (l, i, 0))
    part = pl.BlockSpec((tr, C), lambda l, i, c_ref: (i, 0))
    grid_spec = pltpu.PrefetchScalarGridSpec(num_scalar_prefetch=1, grid=(L, R // tr),
                                             in_specs=[full, part, part, full, full], out_specs=[full] * 4)
    return pl.pallas_call(
        body, name=name, grid_spec=grid_spec, out_shape=[jax.ShapeDtypeStruct((L, R, C), F32)] * 4,
        compiler_params=_params(("parallel", "parallel")),
    )(jnp.reshape(c, (1,)).astype(jnp.int32), w, g_mine, g_other, m, v)


BIG = ("ffn1_wg", "ffn1_wu", "ffn1_wd", "w_in", "w_out", "ffn2_wg", "ffn2_wu", "ffn2_wd")
HELD_TRANSPOSED = ("ffn1_wg", "ffn1_wu", "ffn2_wg", "ffn2_wu")
WEIGHTS =("ada_w", "ada_b", "norm_ffn1", "ffn1_wg", "ffn1_wu", "ffn1_wd", "norm_mix", "w_in", "conv_w", "conv_b",
           "dt_bias", "a_log", "d_skip", "ret_gn", "ssm_norm", "w_out", "norm_ffn2", "ffn2_wg", "ffn2_wu", "ffn2_wd",
           "final_ada_w", "final_ada_b", "final_norm")
SMALL = ("ada_b", "norm_ffn1", "norm_mix", "conv_w", "conv_b", "dt_bias", "a_log", "d_skip", "ret_gn", "ssm_norm",
         "norm_ffn2", "final_ada_b", "final_norm")


def _unpack(slab, shapes):
    flat = slab.reshape(-1)
    out, off = [], 0
    for shp in shapes:
        n = math.prod(shp)
        out.append(flat[off:off + n].reshape(shp))
        off += n
    return out


def _pad_lanes(v):
    return jnp.pad(v, (0, LANES - v.shape[0])).reshape(1, LANES)


def _ffn_fwd(tag, h, u, mod3, wg, wu, wd, nxt):
    a, b, hm = _ffn_up(tag + "_up", u, wg, wu)
    gatefac = 0.5 * (1.0 + mod3[2:3])
    out, h_new, *u_next = _rowmm(tag + "_down", [(hm, p, wd, p) for p in range(4)], "nn", res=(h, gatefac, nxt))
    return h_new, (u_next[0] if nxt else None), dict(h=h, u=u, a=a, b=b, hm=hm, out=out, gatefac=gatefac)


def _ffn_bwd(tag, g, d_out, s_gate, saved, gain, mod3, wg, wu, wd, nxt):
    sv = saved
    da, db = _ffn_bwd_mid(tag + "_mid_bwd", d_out, wd, sv["a"], sv["b"])
    dwd = _mm_tn_groups(tag + "_dwd", sv["hm"], d_out, BF16)
    dwg = _mm_tn_groups(tag + "_dwg", da, sv["u"], BF16)
    dwu = _mm_tn_groups(tag + "_dwu", db, sv["u"], BF16)
    g_in, s_norm, *below = _rowmm(
        tag + "_du", [(da, p, wg, p) for p in range(4)] + [(db, p, wu, p) for p in range(4)], "nn",
        norm_bwd=(sv["h"], gain, mod3[1:2], g, nxt))
    dmod3 = jnp.concatenate([s_norm[0:1], s_norm[1:2], 0.5 * s_gate[0:1]], axis=0)
    return g_in, below, (dwg, dwu, dwd), dmod3, s_norm[2]


def _mixer_fwd(tag, h, u, P, mod3, w_in, w_out, tables, nxt):
    proj = _mm(tag + "_in_proj", [(u, w_in)], "nn", F32, tm=512, tn=1792, tk=1024, n_outer=True)
    y_ret, ret_st = _ret_fwd(tag + "_ret", proj, P["ret_gn"], tables)
    kv = proj[:, 2560:3584].astype(BF16)
    y_sb = _sb_fwd(tag + "_sb", proj, kv)
    xbc = _conv_fwd(tag + "_conv", proj, P["conv_w"], P["conv_b"])
    y_ssm, ssm_st = _ssd_fwd(tag + "_ssd", xbc, proj, P["dt_bias"], P["a_log"], P["d_skip"], P["ssm_norm"])
    gatefac = 1.0 + mod3[2:3]
    ys = (y_ret, y_sb, y_ssm)
    w_out3 = w_out.reshape(3, 512, D_MODEL)
    mixed, h_new, *u_next = _rowmm(tag + "_out_proj", [(y, None, w_out3, i) for i, y in enumerate(ys)], "nn",
                                   res=(h, gatefac, nxt))
    return h_new, (u_next[0] if nxt else None), dict(h=h, u=u, proj=proj, kv=kv, ys=ys, ret_st=ret_st, xbc=xbc,
                                                     ssm_st=ssm_st, out=mixed, gatefac=gatefac)


def _mixer_bwd(tag, g, d_mixed, s_gate, saved, P, mod3, w_in, w_out, tables, nxt):
    h, u, proj, kv, ys, ret_st, xbc, ssm_st = (saved[k] for k in ("h", "u", "proj", "kv", "ys", "ret_st", "xbc",
                                                                  "ssm_st"))
    S = h.shape[0]
    dmix = _mm(tag + "_dmix", [(d_mixed, w_out)], "nt", F32, tm=512, tn=512, tk=1024)
    dw_out = jnp.concatenate(
        [_mm(tag + f"_dw_out{i}", [(y, d_mixed)], "tn", BF16, tm=512, tn=1024, tk=2048) for i, y in enumerate(ys)], axis=0)
    d_ret, d_gn = _ret_bwd(tag + "_ret_bwd", proj, P["ret_gn"], tables, ret_st, dmix)
    dq, dk, dv = _sb_bwd(tag + "_sb_bwd", proj, kv, dmix)
    dxbc, dz, ddt, s_ssd = _ssd_bwd(tag + "_ssd_bwd", xbc, proj, P["dt_bias"], P["a_log"], P["d_skip"], P["ssm_norm"],
                                    ssm_st, dmix)
    dp, s_conv = _conv_bwd_act(tag + "_conv_bwd_act", proj, P["conv_w"], P["conv_b"], dxbc)
    dxr = _conv_bwd_in(tag + "_conv_bwd_in", dp, P["conv_w"])
    dproj = jnp.concatenate(
        [d_ret[0], d_ret[1], d_ret[2], d_ret[3], dq, dk.astype(BF16), dv.astype(BF16), dz, dxr, ddt,
         jnp.zeros((S, IN_WP - DT_COL - LANES), BF16)], axis=1)
    dw_in = _mm(tag + "_dw_in", [(u, dproj)], "tn", BF16, tm=1024, tn=768, tk=2048)
    g_in, s_norm, *below = _rowmm(tag + "_du", [(dproj, None, w_in, None)], "nt",
                                  norm_bwd=(h, P["norm_mix"], mod3[1:2], g, nxt))
    dmod3 = jnp.concatenate([s_norm[0:1], s_norm[1:2], s_gate[0:1]], axis=0)
    small = dict(norm_mix=s_norm[2], conv_w=s_conv[0:4], conv_b=s_conv[4], dt_bias=s_ssd[1, 0:8],
                 a_log=s_ssd[1, LANES:LANES + 8], d_skip=s_ssd[1, 2 * LANES:2 * LANES + 8],
                 ret_gn=d_gn[0], ssm_norm=s_ssd[0])
    return g_in, below, dw_in, dw_out, dmod3, small


def kernel(x, c, ada_w, ada_b, norm_ffn1, ffn1_wg, ffn1_wu, ffn1_wd, norm_mix, w_in, conv_w, conv_b, dt_bias, a_log, d_skip, ret_gn, ssm_norm, w_out, norm_ffn2, ffn2_wg, ffn2_wu, ffn2_wd, final_ada_w, final_ada_b, final_norm, loss_target, m_ada_w, m_ada_b, m_norm_ffn1, m_ffn1_wg, m_ffn1_wu, m_ffn1_wd, m_norm_mix, m_w_in, m_conv_w, m_conv_b, m_dt_bias, m_a_log, m_d_skip, m_ret_gn, m_ssm_norm, m_w_out, m_norm_ffn2, m_ffn2_wg, m_ffn2_wu, m_ffn2_wd, m_final_ada_w, m_final_ada_b, m_final_norm, v_ada_w, v_ada_b, v_norm_ffn1, v_ffn1_wg, v_ffn1_wu, v_ffn1_wd, v_norm_mix, v_w_in, v_conv_w, v_conv_b, v_dt_bias, v_a_log, v_d_skip, v_ret_gn, v_ssm_norm, v_w_out, v_norm_ffn2, v_ffn2_wg, v_ffn2_wu, v_ffn2_wd, v_final_ada_w, v_final_ada_b, v_final_norm):
    W = dict(ada_w=ada_w, ada_b=ada_b, norm_ffn1=norm_ffn1, ffn1_wg=ffn1_wg, ffn1_wu=ffn1_wu, ffn1_wd=ffn1_wd,
             norm_mix=norm_mix, w_in=w_in, conv_w=conv_w, conv_b=conv_b, dt_bias=dt_bias, a_log=a_log, d_skip=d_skip,
             ret_gn=ret_gn, ssm_norm=ssm_norm, w_out=w_out, norm_ffn2=norm_ffn2, ffn2_wg=ffn2_wg, ffn2_wu=ffn2_wu,
             ffn2_wd=ffn2_wd, final_ada_w=final_ada_w, final_ada_b=final_ada_b, final_norm=final_norm)
    M = dict(ada_w=m_ada_w, ada_b=m_ada_b, norm_ffn1=m_norm_ffn1, ffn1_wg=m_ffn1_wg, ffn1_wu=m_ffn1_wu,
             ffn1_wd=m_ffn1_wd, norm_mix=m_norm_mix, w_in=m_w_in, conv_w=m_conv_w, conv_b=m_conv_b, dt_bias=m_dt_bias,
             a_log=m_a_log, d_skip=m_d_skip, ret_gn=m_ret_gn, ssm_norm=m_ssm_norm, w_out=m_w_out,
             norm_ffn2=m_norm_ffn2, ffn2_wg=m_ffn2_wg, ffn2_wu=m_ffn2_wu, ffn2_wd=m_ffn2_wd,
             final_ada_w=m_final_ada_w, final_ada_b=m_final_ada_b, final_norm=m_final_norm)
    V = dict(ada_w=v_ada_w, ada_b=v_ada_b, norm_ffn1=v_norm_ffn1, ffn1_wg=v_ffn1_wg, ffn1_wu=v_ffn1_wu,
             ffn1_wd=v_ffn1_wd, norm_mix=v_norm_mix, w_in=v_w_in, conv_w=v_conv_w, conv_b=v_conv_b, dt_bias=v_dt_bias,
             a_log=v_a_log, d_skip=v_d_skip, ret_gn=v_ret_gn, ssm_norm=v_ssm_norm, w_out=v_w_out,
             norm_ffn2=v_norm_ffn2, ffn2_wg=v_ffn2_wg, ffn2_wu=v_ffn2_wu, ffn2_wd=v_ffn2_wd,
             final_ada_w=v_final_ada_w, final_ada_b=v_final_ada_b, final_norm=v_final_norm)
    for n in HELD_TRANSPOSED:
        W[n], M[n], V[n] = (jnp.transpose(t[n], (0, 2, 1)) for t in (W, M, V))
    D = D_MODEL
    S = x.shape[1]
    ax, ay, ac = _place()
    me = 4 * ax + 2 * ay + ac
    chip = 2 * ax + ay
    h0 = x[0]

    c_all = _gather_small("gather_c", jnp.pad(c, ((0, 7), (0, 0))))[:, 0, :]
    cond_all = c_all * jax.nn.sigmoid(c_all)
    nmod = 3 * 3 * D // 4
    mod_part = jnp.concatenate(
        [_mm(f"mod_proj{l}", [(cond_all, ada_w[l])], "nn", F32, tm=8, tn=768, tk=D) for l in range(DEPTH)]
        + [_mm("mod_proj_final", [(cond_all, final_ada_w)], "nn", F32, tm=8, tn=512, tk=D),
           conv_w.reshape(DEPTH * SSM_CONV, -1)], axis=1)
    gathered = _gather_small("gather_mod", mod_part)[0::2]
    mine = lax.dynamic_index_in_dim(gathered, me, axis=1, keepdims=False)
    mods = [(jnp.reshape(mine[:, l * nmod:(l + 1) * nmod], (-1,)) + ada_b[l]).reshape(9, D) for l in range(DEPTH)]
    fmod = (jnp.reshape(mine[:, DEPTH * nmod:DEPTH * nmod + 2 * D // 4], (-1,)) + final_ada_b).reshape(2, D)
    conv_full = jnp.transpose(gathered[:, :, DEPTH * nmod + 2 * D // 4:], (1, 0, 2)).reshape(DEPTH, SSM_CONV, -1)

    shards = [W[n].astype(BF16) for n in BIG]
    fetched = _chip_exchange("gather_weights", shards, gather=True)
    swapped = _sibling_swap("gather_weights_sibling", fetched)
    full = []
    for l in range(DEPTH):
        fw = {}
        for i, n in enumerate(BIG):
            arr = jnp.where(ac == l, fetched[i], swapped[i])
            fw[n] = lax.dynamic_update_index_in_dim(arr, shards[i][l], chip, 0)
        fw["w_in"] = jnp.pad(jnp.transpose(fw["w_in"], (1, 0, 2)).reshape(D, IN_W), ((0, 0), (0, IN_WP - IN_W)))
        fw["w_out"] = fw["w_out"].reshape(MIX_W, D)
        full.append(fw)

    tables = _ret_tables(S)
    small_p = []
    for l in range(DEPTH):
        small_p.append(dict(
            norm_ffn1=norm_ffn1[l:l + 1], norm_mix=norm_mix[l:l + 1], norm_ffn2=norm_ffn2[l:l + 1],
            ret_gn=ret_gn[l:l + 1], ssm_norm=ssm_norm[l:l + 1], conv_w=conv_full[l], conv_b=conv_b[l:l + 1],
            dt_bias=_pad_lanes(dt_bias[l]), a_log=_pad_lanes(a_log[l]), d_skip=_pad_lanes(d_skip[l])))

    def norm_of(l, sub):
        gain = small_p[l][("norm_ffn1", "norm_mix", "norm_ffn2")[sub]]
        return gain, mods[l][3 * sub:3 * sub + 1], mods[l][3 * sub + 1:3 * sub + 2]

    h = h0
    u = _norm_mod("l0_ffn1_norm", h, *norm_of(0, 0))
    saved = []
    for l in range(DEPTH):
        P, fw, mod = small_p[l], full[l], mods[l]
        h, u, s1 = _ffn_fwd(f"l{l}_ffn1", h, u, mod[0:3], fw["ffn1_wg"], fw["ffn1_wu"], fw["ffn1_wd"], norm_of(l, 1))
        h, u, sm = _mixer_fwd(f"l{l}_mix", h, u, P, mod[3:6], fw["w_in"], fw["w_out"], tables, norm_of(l, 2))
        h, u, s2 = _ffn_fwd(f"l{l}_ffn2", h, u, mod[6:9], fw["ffn2_wg"], fw["ffn2_wu"], fw["ffn2_wd"],
                            norm_of(l + 1, 0) if l + 1 < DEPTH else None)
        saved.append((s1, sm, s2))
    fgain = final_norm.reshape(1, D)
    dy, loss_rows = _final_loss("final_loss", h, loss_target[0], fgain, fmod[0:1], fmod[1:2])

    below = lambda sv: (sv["out"], sv["gatefac"])
    g, s_final, d_out, s_gate = _norm_mod_bwd("final_norm_bwd", dy, h, fgain, fmod[1:2], *below(saved[-1][2]))
    dfmod = s_final[0:2]
    small_g = {n: [None] * DEPTH for n in SMALL}
    big_g = [None] * DEPTH
    dmods = [None] * DEPTH
    for l in reversed(range(DEPTH)):
        P, fw, mod = small_p[l], full[l], mods[l]
        s1, sm, s2 = saved[l]
        g, (d_out, s_gate), (dwg2, dwu2, dwd2), dm2, dn2 = _ffn_bwd(
            f"l{l}_ffn2", g, d_out, s_gate, s2, P["norm_ffn2"], mod[6:9], fw["ffn2_wg"], fw["ffn2_wu"], fw["ffn2_wd"],
            below(sm))
        g, (d_out, s_gate), dw_in, dw_out, dmm, sg = _mixer_bwd(
            f"l{l}_mix", g, d_out, s_gate, sm, P, mod[3:6], fw["w_in"], fw["w_out"], tables, below(s1))
        g, nxt_start, (dwg1, dwu1, dwd1), dm1, dn1 = _ffn_bwd(
            f"l{l}_ffn1", g, d_out, s_gate, s1, P["norm_ffn1"], mod[0:3], fw["ffn1_wg"], fw["ffn1_wu"], fw["ffn1_wd"],
            below(saved[l - 1][2]) if l > 0 else None)
        if l > 0:
            d_out, s_gate = nxt_start
        dmods[l] = jnp.concatenate([dm1, dmm, dm2], axis=0)
        big_g[l] = dict(ffn1_wg=dwg1, ffn1_wu=dwu1, ffn1_wd=dwd1,
                        w_in=jnp.transpose(dw_in[:, :IN_W].reshape(D, 4, IN_W // 4), (1, 0, 2)),
                        w_out=dw_out.reshape(4, MIX_W // 4, D), ffn2_wg=dwg2, ffn2_wu=dwu2, ffn2_wd=dwd2)
        sg.update(norm_ffn1=dn1, norm_ffn2=dn2)
        for n, val in sg.items():
            small_g[n][l] = val
    grad_x = g[None]

    n_mod = DEPTH * 9 * D + 2 * D
    vec = [jnp.stack(dmods).reshape(-1), dfmod.reshape(-1)]
    layered = [n for n in SMALL if n not in ("ada_b", "final_ada_b", "final_norm")]
    vec += [jnp.stack(small_g[n]).reshape(-1) for n in layered]
    vec += [s_final[2], jnp.sum(loss_rows[0]).reshape(1)]
    flat = jnp.concatenate(vec)
    vrows = -(-flat.shape[0] // (8 * PACK_C)) * 8
    slab = jnp.pad(flat, (0, vrows * PACK_C - flat.shape[0])).reshape(vrows, PACK_C)
    slabs = _gather_small("gather_small_grads", slab)
    total = _sum_devices("sum_small_grads", slabs).reshape(-1)
    grads = {}
    grads["ada_b"] = total[:DEPTH * 9 * D].reshape(DEPTH, 9 * D)
    grads["final_ada_b"] = total[DEPTH * 9 * D:n_mod]
    off = n_mod
    for n in layered:
        shp = (DEPTH,) + ((SSM_CONV, D) if n == "conv_w" else W[n].shape[1:])
        cnt = math.prod(shp)
        grads[n] = total[off:off + cnt].reshape(shp)
        off += cnt
    grads["final_norm"] = total[off:off + D]
    loss = total[off + D]
    grads["conv_w"] = lax.dynamic_slice_in_dim(grads["conv_w"], chip * (D // 4), D // 4, axis=2)

    dmod_all = slabs[:, :n_mod // PACK_C, :].reshape(8, n_mod)
    grads["ada_w"] = jnp.stack([
        _mm(f"grad_ada_w{l}", [(cond_all, lax.dynamic_slice_in_dim(dmod_all, l * 9 * D + chip * nmod, nmod, axis=1))],
            "tn", F32, tm=D, tn=768, tk=8) for l in range(DEPTH)])
    grads["final_ada_w"] = _mm(
        "grad_final_ada_w",
        [(cond_all, lax.dynamic_slice_in_dim(dmod_all, DEPTH * 9 * D + chip * (2 * D // 4), 2 * D // 4, axis=1))],
        "tn", F32, tm=D, tn=512, tk=8)

    g0, g1 = [big_g[0][n] for n in BIG], [big_g[1][n] for n in BIG]
    from_sibling = _sibling_swap("reduce_grads_sibling", g0, alt=g1)
    chip_sum = _add_layers("reduce_grads_add", g0, g1, from_sibling, ac)
    from_chips = _chip_exchange("reduce_grads_chips", chip_sum, gather=False)
    reduced = _sum_chips("reduce_grads_sum", from_chips, chip_sum, chip)
    reduced_other = _sibling_swap("reduce_grads_complete", reduced)

    delta, new_m, new_v = {}, {}, {}
    for i, n in enumerate(BIG):
        grads[n], delta[n], new_m[n], new_v[n] = _adamw_layers("adamw_" + n, W[n], reduced[i], reduced_other[i],
                                                               M[n], V[n], ac)
    delta["ada_w"], new_m["ada_w"], new_v["ada_w"] = _adamw("adamw_ada_w", ada_w, grads["ada_w"], m_ada_w, v_ada_w)
    outs = _adamw("adamw_final_ada_w", final_ada_w[None], grads["final_ada_w"][None], m_final_ada_w[None],
                  v_final_ada_w[None])
    delta["final_ada_w"], new_m["final_ada_w"], new_v["final_ada_w"] = [o[0] for o in outs]
    small_shapes = [W[n].shape for n in SMALL]
    n_small = sum(math.prod(s) for s in small_shapes)
    srows = -(-n_small // (8 * LANES)) * 8
    slab_of = lambda T_: jnp.pad(jnp.concatenate([T_[n].reshape(-1) for n in SMALL]),
                                 (0, srows * LANES - n_small)).reshape(1, srows, LANES)
    outs = _adamw("adamw_small", slab_of(W), slab_of(grads), slab_of(M), slab_of(V))
    for res, o in zip((delta, new_m, new_v), outs):
        for n, val in zip(SMALL, _unpack(o[0], small_shapes)):
            res[n] = val

    for n in HELD_TRANSPOSED:
        for res in (grads, delta, new_m, new_v):
            res[n] = jnp.transpose(res[n], (0, 2, 1))
    return (loss, grad_x, *[grads[n] for n in WEIGHTS], *[delta[n] for n in WEIGHTS],
            *[new_m[n] for n in WEIGHTS], *[new_v[n] for n in WEIGHTS])
```

```python
import functools
import math

import jax
import jax.numpy as jnp
from jax import lax
from jax.experimental import pallas as pl
from jax.experimental.pallas import tpu as pltpu

F32 = jnp.float32
BF16 = jnp.bfloat16
MESH = pl.DeviceIdType.MESH

D_MODEL = 1024
DEPTH = 2
D_FF = 2816
RET_HEADS = 4
SB_HEADS = 4
HEAD_DIM = 128
SSM_HEADS = 8
SSM_P = 64
SSM_N = 128
SSM_GROUPS = 2
SSM_CONV = 4
MIX_W = 1536
IN_W = 5128
IN_WP = 5376
DT_COL = 5120
ROPE_BASE = 10000.0
NORM_EPS = 1e-6
ADAM_LR = 0.001
ADAM_B1 = 0.9
ADAM_B2 = 0.999
ADAM_EPS = 1e-08
ADAM_WD = 0.01
ADAM_STEP = 10

LANES = 128
VMEM_LIMIT = 56 * 1024 * 1024
RET_T = 256
RET_HPS = 4
SB_T = 256
SB_HPS = 2
SSD_T = 256
CONV_T = 512
ROW_T = 512
SB_DEAD = -150.0

NN = (((1,), (0,)), ((), ()))
NT = (((1,), (1,)), ((), ()))
TN = (((0,), (0,)), ((), ()))


def _dot(a, b, dims=NN):
    return lax.dot_general(a.astype(BF16), b.astype(BF16), dims, preferred_element_type=F32)


def _params(sem):
    return pltpu.CompilerParams(dimension_semantics=sem, vmem_limit_bytes=VMEM_LIMIT)


def _sigmoid(x):
    return 1.0 / (1.0 + jnp.exp(-x))


def _split_dot(mat01, x, dims=NN, lhs01=True, pieces=3):
    m = mat01.astype(BF16)
    total, rest = None, x
    for _ in range(pieces):
        p = rest.astype(BF16)
        rest = rest - p.astype(F32)
        d = lax.dot_general(m, p, dims, preferred_element_type=F32) if lhs01 else lax.dot_general(
            p, m, dims, preferred_element_type=F32)
        total = d if total is None else total + d
    return total


def _mm(name, terms, mode, out_dtype, tm=512, tn=512, tk=1024, n_outer=False):
    a0, b0 = terms[0]
    if mode == "nn":
        (M, K), N = a0.shape, b0.shape[1]
    elif mode == "nt":
        (M, K), N = a0.shape, b0.shape[0]
    else:
        (K, M), N = a0.shape, b0.shape[1]
    tm, tn, tk = min(tm, M), min(tn, N), min(tk, K)
    assert M % tm == 0 and N % tn == 0 and K % tk == 0, (name, M, N, K, tm, tn, tk)
    nk = K // tk
    nterm = len(terms)
    dims = {"nn": NN, "nt": NT, "tn": TN}[mode]

    def body(*refs):
        o_ref, acc = refs[2 * nterm], refs[2 * nterm + 1]
        part = None
        for t in range(nterm):
            p = _dot(refs[2 * t][...], refs[2 * t + 1][...], dims)
            part = p if part is None else part + p
        if nk == 1:
            o_ref[...] = part.astype(o_ref.dtype)
        else:
            k = pl.program_id(2)

            @pl.when(k == 0)
            def _():
                acc[...] = part

            @pl.when(k > 0)
            def _():
                acc[...] += part

            @pl.when(k == nk - 1)
            def _():
                o_ref[...] = acc[...].astype(o_ref.dtype)

    ij = (lambda g0, g1: (g1, g0)) if n_outer else (lambda g0, g1: (g0, g1))
    if mode == "nn":
        a_spec = pl.BlockSpec((tm, tk), lambda g0, g1, k: (ij(g0, g1)[0], k))
        b_spec = pl.BlockSpec((tk, tn), lambda g0, g1, k: (k, ij(g0, g1)[1]))
    elif mode == "nt":
        a_spec = pl.BlockSpec((tm, tk), lambda g0, g1, k: (ij(g0, g1)[0], k))
        b_spec = pl.BlockSpec((tn, tk), lambda g0, g1, k: (ij(g0, g1)[1], k))
    else:
        a_spec = pl.BlockSpec((tk, tm), lambda g0, g1, k: (k, ij(g0, g1)[0]))
        b_spec = pl.BlockSpec((tk, tn), lambda g0, g1, k: (k, ij(g0, g1)[1]))
    flat = [r for ab in terms for r in ab]
    return pl.pallas_call(
        body, name=name, grid=(N // tn, M // tm, nk) if n_outer else (M // tm, N // tn, nk),
        in_specs=[a_spec, b_spec] * nterm,
        out_specs=pl.BlockSpec((tm, tn), lambda g0, g1, k: ij(g0, g1)),
        out_shape=jax.ShapeDtypeStruct((M, N), out_dtype),
        scratch_shapes=[pltpu.VMEM((tm, tn) if nk > 1 else (8, LANES), F32)],
        compiler_params=_params(("parallel", "parallel", "arbitrary")),
    )(*flat)


def _norm_bwd_rows(d, x, gain, scale, sums_ref):
    r = lax.rsqrt(jnp.mean(x * x, axis=-1, keepdims=True) + NORM_EPS)
    xh = x * r
    dn = d * (1.0 + scale)
    sums_ref[0:1, :] += jnp.sum(d, axis=0, keepdims=True)
    sums_ref[1:2, :] += jnp.sum(d * xh * gain, axis=0, keepdims=True)
    sums_ref[2:3, :] += jnp.sum(dn * xh, axis=0, keepdims=True)
    dxh = dn * gain
    return r * (dxh - xh * jnp.mean(dxh * xh, axis=-1, keepdims=True))


def _rowmm(name, terms, mode, out_dtype=F32, res=None, norm_bwd=None):
    S = terms[0][0].shape[-2]
    N = terms[0][2].shape[-1] if mode == "nn" else terms[0][2].shape[-2]
    nterm = len(terms)
    tm = ROW_T
    dims = NN if mode == "nn" else NT
    nxt = None if (res or norm_bwd) is None else (res or norm_bwd)[-1]

    def body(*refs):
        o = None
        for t in range(nterm):
            p = _dot(refs[2 * t][...], refs[2 * t + 1][...], dims)
            o = p if o is None else o + p
        rest = refs[2 * nterm:]
        if norm_bwd is not None:
            h_ref, g_ref, sc_ref, gr_ref = rest[:4]
            dh_ref, sums_ref = rest[4 + (2 if nxt else 0):][:2]

            @pl.when(pl.program_id(0) == 0)
            def _():
                sums_ref[...] = jnp.zeros_like(sums_ref)

            g = gr_ref[...] + _norm_bwd_rows(o, h_ref[...], g_ref[...], sc_ref[...], sums_ref)
            dh_ref[...] = g
            if nxt:
                _gate_rows(g, rest[4], rest[5], rest[8], rest[9])
        elif res is None:
            rest[0][...] = o.astype(out_dtype)
        else:
            h_ref, gf_ref = rest[:2]
            out_ref, hn_ref = rest[2 + (3 if nxt else 0):][:2]
            out_ref[...] = o.astype(out_ref.dtype)
            hn = h_ref[...] + gf_ref[...] * o
            hn_ref[...] = hn
            if nxt:
                rest[7][...] = _norm_rows(hn, rest[2][...], rest[3][...], rest[4][...]).astype(BF16)

    in_specs, flat = [], []
    for a, ai, w, wi in terms:
        if ai is None:
            in_specs.append(pl.BlockSpec((tm, a.shape[1]), lambda i: (i, 0)))
        else:
            in_specs.append(pl.BlockSpec((None, tm, a.shape[2]), lambda i, g=ai: (g, i, 0)))
        if wi is None:
            in_specs.append(pl.BlockSpec(w.shape, lambda i: (0, 0)))
        else:
            in_specs.append(pl.BlockSpec((None,) + w.shape[1:], lambda i, g=wi: (g, 0, 0)))
        flat += [a, w]
    row = pl.BlockSpec((tm, N), lambda i: (i, 0))
    vec = pl.BlockSpec((1, N), lambda i: (0, 0))
    sums = pl.BlockSpec((8, N), lambda i: (0, 0))
    rows_f32, rows_bf16 = jax.ShapeDtypeStruct((S, N), F32), jax.ShapeDtypeStruct((S, N), BF16)
    sums_f32 = jax.ShapeDtypeStruct((8, N), F32)
    if norm_bwd is not None:
        extra = list(norm_bwd[:4]) + (list(nxt) if nxt else [])
        return pl.pallas_call(
            body, name=name, grid=(S // tm,),
            in_specs=in_specs + [row, vec, vec, row] + ([row, vec] if nxt else []),
            out_specs=[row, sums] + ([row, sums] if nxt else []),
            out_shape=[rows_f32, sums_f32] + ([rows_bf16, sums_f32] if nxt else []),
            compiler_params=_params(("arbitrary",)),
        )(*flat, *extra)
    if res is None:
        return pl.pallas_call(
            body, name=name, grid=(S // tm,), in_specs=in_specs, out_specs=row,
            out_shape=jax.ShapeDtypeStruct((S, N), out_dtype), compiler_params=_params(("parallel",)),
        )(*flat)
    extra = list(res[:2]) + (list(nxt) if nxt else [])
    return pl.pallas_call(
        body, name=name, grid=(S // tm,),
        in_specs=in_specs + [row, vec] + ([vec, vec, vec] if nxt else []),
        out_specs=[row, row] + ([row] if nxt else []),
        out_shape=[rows_bf16, rows_f32] + ([rows_bf16] if nxt else []),
        compiler_params=_params(("parallel",)),
    )(*flat, *extra)


def _mm_tn_groups(name, a, b, out_dtype, tk=2048):
    G = a.shape[0] if a.ndim == 3 else b.shape[0]
    S, M, N = a.shape[-2], a.shape[-1], b.shape[-1]
    tk = min(tk, S)
    nk = S // tk

    def body(a_ref, b_ref, o_ref, acc):
        k = pl.program_id(1)
        part = _dot(a_ref[...], b_ref[...], TN)

        @pl.when(k == 0)
        def _():
            acc[...] = part

        @pl.when(k > 0)
        def _():
            acc[...] += part

        @pl.when(k == nk - 1)
        def _():
            o_ref[...] = acc[...].astype(o_ref.dtype)

    def spec(arr, width):
        if arr.ndim == 3:
            return pl.BlockSpec((None, tk, width), lambda g, k: (g, k, 0))
        return pl.BlockSpec((tk, width), lambda g, k: (k, 0))

    return pl.pallas_call(
        body, name=name, grid=(G, nk), in_specs=[spec(a, M), spec(b, N)],
        out_specs=pl.BlockSpec((None, M, N), lambda g, k: (g, 0, 0)),
        out_shape=jax.ShapeDtypeStruct((G, M, N), out_dtype),
        scratch_shapes=[pltpu.VMEM((M, N), F32)],
        compiler_params=_params(("parallel", "arbitrary")),
    )(a, b)


def _norm_rows(x, gain, shift, scale):
    r = lax.rsqrt(jnp.mean(x * x, axis=-1, keepdims=True) + NORM_EPS)
    return x * r * gain * (1.0 + scale) + shift


def _gate_rows(g, out_ref, gf_ref, d_ref, gsum_ref):
    @pl.when(pl.program_id(0) == 0)
    def _():
        gsum_ref[...] = jnp.zeros_like(gsum_ref)

    d_ref[...] = (g * gf_ref[...]).astype(d_ref.dtype)
    gsum_ref[0:1, :] += jnp.sum(g * out_ref[...].astype(F32), axis=0, keepdims=True)


def _norm_mod(name, h, gain, shift, scale):
    S, D = h.shape
    tm = ROW_T

    def body(h_ref, g_ref, sh_ref, sc_ref, u_ref):
        u_ref[...] = _norm_rows(h_ref[...], g_ref[...], sh_ref[...], sc_ref[...]).astype(u_ref.dtype)

    vec = pl.BlockSpec((1, D), lambda i: (0, 0))
    return pl.pallas_call(
        body, name=name, grid=(S // tm,),
        in_specs=[pl.BlockSpec((tm, D), lambda i: (i, 0)), vec, vec, vec],
        out_specs=pl.BlockSpec((tm, D), lambda i: (i, 0)),
        out_shape=jax.ShapeDtypeStruct((S, D), BF16),
        compiler_params=_params(("parallel",)),
    )(h, gain, shift, scale)


def _norm_mod_bwd(name, du, h, gain, scale, out, gatefac):
    S, D = h.shape
    tm = ROW_T

    def body(du_ref, h_ref, g_ref, sc_ref, o_ref, gf_ref, dh_ref, sums_ref, d_ref, gsum_ref):
        @pl.when(pl.program_id(0) == 0)
        def _():
            sums_ref[...] = jnp.zeros_like(sums_ref)

        g = _norm_bwd_rows(du_ref[...], h_ref[...], g_ref[...], sc_ref[...], sums_ref)
        dh_ref[...] = g
        _gate_rows(g, o_ref, gf_ref, d_ref, gsum_ref)

    row = pl.BlockSpec((tm, D), lambda i: (i, 0))
    vec = pl.BlockSpec((1, D), lambda i: (0, 0))
    sums = pl.BlockSpec((8, D), lambda i: (0, 0))
    return pl.pallas_call(
        body, name=name, grid=(S // tm,),
        in_specs=[row, row, vec, vec, row, vec],
        out_specs=[row, sums, row, sums],
        out_shape=[jax.ShapeDtypeStruct((S, D), F32), jax.ShapeDtypeStruct((8, D), F32),
                   jax.ShapeDtypeStruct((S, D), BF16), jax.ShapeDtypeStruct((8, D), F32)],
        compiler_params=_params(("arbitrary",)),
    )(du, h, gain, scale, out, gatefac)


def _final_loss(name, h, target, gain, shift, scale):
    S, D = h.shape
    tm = ROW_T

    def body(h_ref, t_ref, g_ref, sh_ref, sc_ref, dy_ref, loss_ref):
        @pl.when(pl.program_id(0) == 0)
        def _():
            loss_ref[...] = jnp.zeros_like(loss_ref)

        x = h_ref[...]
        r = lax.rsqrt(jnp.mean(x * x, axis=-1, keepdims=True) + NORM_EPS)
        y = x * r * g_ref[...] * (1.0 + sc_ref[...]) + sh_ref[...]
        e = y - t_ref[...]
        dy_ref[...] = e * (1.0 / D)
        loss_ref[0:1, :] += 0.5 * jnp.sum(e * e, axis=0, keepdims=True) * (1.0 / D)

    row = pl.BlockSpec((tm, D), lambda i: (i, 0))
    vec = pl.BlockSpec((1, D), lambda i: (0, 0))
    return pl.pallas_call(
        body, name=name, grid=(S // tm,),
        in_specs=[row, row, vec, vec, vec],
        out_specs=[row, pl.BlockSpec((8, D), lambda i: (0, 0))],
        out_shape=[jax.ShapeDtypeStruct((S, D), F32), jax.ShapeDtypeStruct((8, D), F32)],
        compiler_params=_params(("arbitrary",)),
    )(h, target, gain, shift, scale)


def _ffn_up(name, u, wg, wu, side=None):
    S, D = u.shape
    G, Fg, _ = wg.shape
    tm = ROW_T

    def body(u_ref, wg_ref, wu_ref, a_ref, b_ref, hm_ref):
        uv = u_ref[...]
        a = _dot(uv, wg_ref[...], NT)
        b = _dot(uv, wu_ref[...], NT)
        sg = _sigmoid(a)
        act = a * sg
        a_ref[...] = (b * sg * (1.0 + a * (1.0 - sg))).astype(a_ref.dtype)
        b_ref[...] = act.astype(b_ref.dtype)
        hm_ref[...] = (act * b).astype(hm_ref.dtype)

    w_spec = pl.BlockSpec((None, Fg, D), lambda g, i: (g, 0, 0))
    o_spec = pl.BlockSpec((None, tm, Fg), lambda g, i: (g, i, 0))
    return _host_call(
        body, side, name, (G, S // tm), [pl.BlockSpec((tm, D), lambda g, i: (i, 0)), w_spec, w_spec], [o_spec] * 3,
        [jax.ShapeDtypeStruct((G, S, Fg), BF16)] * 3, [], (u, wg, wu))


def _ffn_bwd_mid(name, d_out, wd, a, b):
    S, D = d_out.shape
    G, Fg, _ = wd.shape
    tm = ROW_T

    def body(d_ref, wd_ref, a_ref, b_ref, da_ref, db_ref):
        dhm = _dot(d_ref[...], wd_ref[...], NT)
        da_ref[...] = (dhm * a_ref[...].astype(F32)).astype(da_ref.dtype)
        db_ref[...] = (dhm * b_ref[...].astype(F32)).astype(db_ref.dtype)

    t_spec = pl.BlockSpec((None, tm, Fg), lambda g, i: (g, i, 0))
    return pl.pallas_call(
        body, name=name, grid=(G, S // tm),
        in_specs=[pl.BlockSpec((tm, D), lambda g, i: (i, 0)), pl.BlockSpec((None, Fg, D), lambda g, i: (g, 0, 0)),
                  t_spec, t_spec],
        out_specs=[t_spec] * 2,
        out_shape=[jax.ShapeDtypeStruct((G, S, Fg), BF16)] * 2,
        compiler_params=_params(("parallel", "parallel")),
    )(d_out, wd, a, b)


def _ret_tables(S):
    T = RET_T
    half = HEAD_DIM // 2
    inv_freq = ROPE_BASE ** (-jnp.arange(half, dtype=F32) / half)
    ang = jnp.arange(S, dtype=F32)[:, None] * inv_freq[None, :]
    cos, sin = jnp.cos(ang), jnp.sin(ang)
    cosf = jnp.concatenate([cos, cos], axis=-1)
    sinf = jnp.concatenate([-sin, sin], axis=-1)
    log_gamma = jnp.log1p(-(2.0 ** (-5.0 - jnp.arange(RET_HEADS, dtype=F32))))
    idx = jnp.arange(T, dtype=F32)
    chunk = jnp.arange(T) // 64
    vis = (chunk[None, :] <= chunk[:, None]).astype(F32)
    mask = jnp.exp(log_gamma[:, None, None] * jnp.abs(idx[:, None] - idx[None, :])) * vis[None]
    ones = jnp.ones((1, 1, LANES), F32)
    qdec = jnp.exp(log_gamma[:, None] * (idx + 1.0)[None, :])[:, :, None] * ones
    kdec = jnp.exp(log_gamma[:, None] * (T - 1.0 - idx)[None, :])[:, :, None] * ones
    cdec = jnp.exp(log_gamma * T)[:, None, None] * jnp.ones((1, 8, LANES), F32)
    return cosf, sinf, mask, qdec, kdec, cdec


def _rope(x, cosf, sinf):
    return x * cosf + pltpu.roll(x, HEAD_DIM // 2, 1) * sinf


def _rope_bwd(d, cosf, sinf):
    return d * cosf + pltpu.roll(d * sinf, HEAD_DIM // 2, 1)


def _ret_specs(T, rev_nb=None):
    if rev_nb is None:
        blk = lambda b: b
    else:
        blk = lambda b: rev_nb - 1 - b
    E, W = RET_HPS, RET_HPS * HEAD_DIM
    proj = lambda off: pl.BlockSpec((T, W), lambda hp, b: (blk(b), off * (RET_HEADS // E) + hp))
    rows = pl.BlockSpec((T, HEAD_DIM), lambda hp, b: (blk(b), 0))
    per_head = lambda shape: pl.BlockSpec((E,) + shape, lambda hp, b: (hp, 0, 0))
    return ([proj(0), proj(1), proj(2), proj(3), rows, rows,
             per_head((T, T)), per_head((T, LANES)), per_head((T, LANES)), per_head((8, LANES)),
             pl.BlockSpec((1, W), lambda hp, b: (0, hp))], blk)


def _ret_fwd(name, proj, gn, tables):
    S = proj.shape[0]
    T, E = RET_T, RET_HPS
    nb = S // T
    scale = HEAD_DIM ** -0.5
    specs, _ = _ret_specs(T)

    def body(q_ref, k_ref, v_ref, g_ref, cos_ref, sin_ref, m_ref, qd_ref, kd_ref, cd_ref, gn_ref, y_ref, st_ref, st):
        @pl.when(pl.program_id(1) == 0)
        def _():
            st[...] = jnp.zeros_like(st)

        cosf, sinf = cos_ref[...], sin_ref[...]
        for e in range(E):
            qr = _rope(q_ref[:, _lanes(e)], cosf, sinf)
            kr = _rope(k_ref[:, _lanes(e)], cosf, sinf) * scale
            v = v_ref[:, _lanes(e)]
            sp = st[e]
            st_ref[e, 0] = sp
            w = _dot(qr, kr, NT) * m_ref[e]
            y = _dot(w, v) + _dot(qr * qd_ref[e], sp)
            st[e] = cd_ref[e, 0:1, :] * sp + _dot(kr * kd_ref[e], v, TN)
            r = lax.rsqrt(jnp.mean(y * y, axis=-1, keepdims=True) + NORM_EPS)
            g = g_ref[:, _lanes(e)]
            y_ref[:, _lanes(e)] = y * r * gn_ref[:, _lanes(e)] * (g * _sigmoid(g))

    return pl.pallas_call(
        body, name=name, grid=(RET_HEADS // E, nb), in_specs=specs,
        out_specs=[pl.BlockSpec((T, E * HEAD_DIM), lambda hp, b: (b, hp)),
                   pl.BlockSpec((E, 1, HEAD_DIM, HEAD_DIM), lambda hp, b: (hp, b, 0, 0))],
        out_shape=[jax.ShapeDtypeStruct((S, RET_HEADS * HEAD_DIM), F32),
                   jax.ShapeDtypeStruct((RET_HEADS, nb, HEAD_DIM, HEAD_DIM), F32)],
        scratch_shapes=[pltpu.VMEM((E, HEAD_DIM, HEAD_DIM), F32)],
        compiler_params=_params(("parallel", "arbitrary")),
    )(proj, proj, proj, proj, tables[0], tables[1], tables[2], tables[3], tables[4], tables[5], gn)


def _ret_bwd(name, proj, gn, tables, states, dmix):
    S = proj.shape[0]
    T, E = RET_T, RET_HPS
    W = E * HEAD_DIM
    nb = S // T
    scale = HEAD_DIM ** -0.5
    specs, blk = _ret_specs(T, rev_nb=nb)
    specs = specs + [pl.BlockSpec((E, 1, HEAD_DIM, HEAD_DIM), lambda hp, b: (hp, blk(b), 0, 0)),
                     pl.BlockSpec((T, W), lambda hp, b: (blk(b), hp))]

    def body(q_ref, k_ref, v_ref, g_ref, cos_ref, sin_ref, m_ref, qd_ref, kd_ref, cd_ref, gn_ref, st_ref, do_ref,
             d_ref, dgn_ref, dst):
        @pl.when(pl.program_id(1) == 0)
        def _():
            dst[...] = jnp.zeros_like(dst)
            dgn_ref[...] = jnp.zeros_like(dgn_ref)

        cosf, sinf = cos_ref[...], sin_ref[...]
        for e in range(E):
            qr = _rope(q_ref[:, _lanes(e)], cosf, sinf)
            kr = _rope(k_ref[:, _lanes(e)], cosf, sinf) * scale
            v = v_ref[:, _lanes(e)]
            sp = st_ref[e, 0]
            mask, qd, kd = m_ref[e], qd_ref[e], kd_ref[e]
            w = _dot(qr, kr, NT) * mask
            y = _dot(w, v) + _dot(qr * qd, sp)
            r = lax.rsqrt(jnp.mean(y * y, axis=-1, keepdims=True) + NORM_EPS)
            yh = y * r
            gn_v = gn_ref[:, _lanes(e)]
            g = g_ref[:, _lanes(e)]
            sg = _sigmoid(g)
            do = do_ref[:, _lanes(e)]
            dyn = do * g * sg
            dgn_ref[:, _lanes(e)] += jnp.sum(dyn * yh, axis=0, keepdims=True)
            dyh = dyn * gn_v
            dy = r * (dyh - yh * jnp.mean(dyh * yh, axis=-1, keepdims=True))
            dg = do * yh * gn_v * sg * (1.0 + g * (1.0 - sg))
            ds = dst[e]
            dp = _dot(dy, v, NT) * mask
            dqr = _dot(dp, kr) + _dot(dy, sp, NT) * qd
            dkr = _dot(dp, qr, TN) + _dot(v, ds, NT) * kd
            dv = _dot(w, dy, TN) + _dot(kr * kd, ds)
            dst[e] = cd_ref[e, 0:1, :] * ds + _dot(qr * qd, dy, TN)
            d_ref[0, :, _lanes(e)] = _rope_bwd(dqr, cosf, sinf).astype(d_ref.dtype)
            d_ref[1, :, _lanes(e)] = _rope_bwd(dkr * scale, cosf, sinf).astype(d_ref.dtype)
            d_ref[2, :, _lanes(e)] = dv.astype(d_ref.dtype)
            d_ref[3, :, _lanes(e)] = dg.astype(d_ref.dtype)

    return pl.pallas_call(
        body, name=name, grid=(RET_HEADS // E, nb), in_specs=specs,
        out_specs=[pl.BlockSpec((4, T, W), lambda hp, b: (0, blk(b), hp)),
                   pl.BlockSpec((1, W), lambda hp, b: (0, hp))],
        out_shape=[jax.ShapeDtypeStruct((4, S, RET_HEADS * HEAD_DIM), BF16),
                   jax.ShapeDtypeStruct((1, RET_HEADS * HEAD_DIM), F32)],
        scratch_shapes=[pltpu.VMEM((E, HEAD_DIM, HEAD_DIM), F32)],
        compiler_params=_params(("parallel", "arbitrary")),
    )(proj, proj, proj, proj, tables[0], tables[1], tables[2], tables[3], tables[4], tables[5], gn, states, dmix)


def _sb_logits(qb, kb, i, j, scale):
    T = SB_T
    z = lax.dot_general(qb, kb, NT, preferred_element_type=F32) * scale
    row = lax.broadcasted_iota(jnp.int32, (T, T), 0)
    col = lax.broadcasted_iota(jnp.int32, (T, T), 1)
    vis = jnp.logical_or(j < i, col < row)
    lp = jnp.log1p(jnp.exp(-jnp.abs(z)))
    lb = jnp.minimum(z, 0.0) - lp
    lk = jnp.where(vis, -jnp.maximum(z, 0.0) - lp, 0.0)
    return lb, lk, vis


def _sb_weights(lb, lk, vis, tailc):
    T = SB_T
    row = lax.broadcasted_iota(jnp.int32, (T, T), 0)
    col = lax.broadcasted_iota(jnp.int32, (T, T), 1)
    tail = tailc + _split_dot(row > col, lk, lhs01=False, pieces=2)
    return jnp.where(vis, jnp.exp(lb + tail), 0.0)


def _sb_specs(S):
    T, W = SB_T, SB_HPS * HEAD_DIM
    return [pl.BlockSpec((T, W), lambda hp, i: (i, 2048 // W + hp)),
            pl.BlockSpec((S, W), lambda hp, i: (0, hp)),
            pl.BlockSpec((S, W), lambda hp, i: (0, 512 // W + hp))]


def _lanes(e):
    return slice(e * HEAD_DIM, (e + 1) * HEAD_DIM)


def _sb_fwd(name, proj, kv, side=None):
    S = proj.shape[0]
    T, E = SB_T, SB_HPS
    nq = S // T
    scale = HEAD_DIM ** -0.5

    def body(q_ref, k_ref, v_ref, y_ref):
        i = pl.program_id(1)
        qs = [q_ref[:, _lanes(e)].astype(BF16) for e in range(E)]

        def cond(c):
            return jnp.logical_and(c[0] >= 0, c[1] == 0)

        def step(c):
            j, _, tails, accs = c
            rows = pl.ds(pl.multiple_of(j * T, T), T)
            new_tails, new_accs, worst = [], [], None
            for e in range(E):
                lb, lk, vis = _sb_logits(qs[e], k_ref[rows, _lanes(e)], i, j, scale)
                w = _sb_weights(lb, lk, vis, tails[e])
                new_accs.append(accs[e] + lax.dot_general(w.astype(BF16), v_ref[rows, _lanes(e)], NN,
                                                          preferred_element_type=F32))
                t = tails[e] + jnp.sum(lk, axis=1, keepdims=True)
                new_tails.append(t)
                worst = jnp.max(t) if worst is None else jnp.maximum(worst, jnp.max(t))
            return j - 1, (worst < SB_DEAD).astype(jnp.int32), tuple(new_tails), tuple(new_accs)

        init = (i, jnp.int32(0), (jnp.zeros((T, 1), F32),) * E, (jnp.zeros((T, HEAD_DIM), F32),) * E)
        accs = lax.while_loop(cond, step, init)[3]
        for e in range(E):
            y_ref[:, _lanes(e)] = accs[e]

    (y,), fetched = _host_call(
        body, side, name, (SB_HEADS // E, nq), _sb_specs(S), [pl.BlockSpec((T, E * HEAD_DIM), lambda hp, i: (i, hp))],
        [jax.ShapeDtypeStruct((S, SB_HEADS * HEAD_DIM), F32)], [], (proj, kv, kv))
    return y, fetched


def _sb_bwd(name, proj, kv, dmix):
    S = proj.shape[0]
    T, E = SB_T, SB_HPS
    nq = S // T
    scale = HEAD_DIM ** -0.5

    def body(q_ref, k_ref, v_ref, do_ref, dq_ref, dk_ref, dv_ref):
        i = pl.program_id(1)

        @pl.when(i == 0)
        def _():
            dk_ref[...] = jnp.zeros_like(dk_ref)
            dv_ref[...] = jnp.zeros_like(dv_ref)

        qs = [q_ref[:, _lanes(e)].astype(BF16) for e in range(E)]
        dos = [do_ref[:, _lanes(e)].astype(BF16) for e in range(E)]
        row = lax.broadcasted_iota(jnp.int32, (T, T), 0)
        col = lax.broadcasted_iota(jnp.int32, (T, T), 1)
        zcol = (jnp.zeros((T, 1), F32),) * E

        def cond(c):
            return jnp.logical_and(c[0] >= 0, c[1] == 0)

        def walk_left(c):
            j, _, tails = c
            rows = pl.ds(pl.multiple_of(j * T, T), T)
            new_tails, worst = [], None
            for e in range(E):
                _, lk, _ = _sb_logits(qs[e], k_ref[rows, _lanes(e)], i, j, scale)
                t = tails[e] + jnp.sum(lk, axis=1, keepdims=True)
                new_tails.append(t)
                worst = jnp.max(t) if worst is None else jnp.maximum(worst, jnp.max(t))
            return j - 1, (worst < SB_DEAD).astype(jnp.int32), tuple(new_tails)

        j_end, _, totals = lax.while_loop(cond, walk_left, (i, jnp.int32(0), zcol))

        def walk_back(j, c):
            lefts, used, dqs = c
            rows = pl.ds(pl.multiple_of(j * T, T), T)
            new_lefts, new_used, new_dqs = [], [], []
            for e in range(E):
                kb = k_ref[rows, _lanes(e)]
                lb, lk, vis = _sb_logits(qs[e], kb, i, j, scale)
                u = used[e] + jnp.sum(lk, axis=1, keepdims=True)
                w = _sb_weights(lb, lk, vis, totals[e] - u)
                de = lax.dot_general(dos[e], v_ref[rows, _lanes(e)], NT, preferred_element_type=F32) * w
                dlk = jnp.where(vis, lefts[e] + _split_dot(row < col, de, lhs01=False, pieces=2), 0.0)
                sg = jnp.exp(lb)
                dz = ((de * (1.0 - sg) - dlk * sg) * scale).astype(BF16)
                new_dqs.append(dqs[e] + lax.dot_general(dz, kb, NN, preferred_element_type=F32))
                dk_ref[rows, _lanes(e)] += lax.dot_general(dz, qs[e], TN, preferred_element_type=F32)
                dv_ref[rows, _lanes(e)] += lax.dot_general(w.astype(BF16), dos[e], TN, preferred_element_type=F32)
                new_lefts.append(lefts[e] + jnp.sum(de, axis=1, keepdims=True))
                new_used.append(u)
            return tuple(new_lefts), tuple(new_used), tuple(new_dqs)

        init = (zcol, zcol, (jnp.zeros((T, HEAD_DIM), F32),) * E)
        dqs = lax.fori_loop(j_end + 1, i + 1, walk_back, init)[2]
        for e in range(E):
            dq_ref[:, _lanes(e)] = dqs[e].astype(dq_ref.dtype)

    W = E * HEAD_DIM
    blk = pl.BlockSpec((T, W), lambda hp, i: (i, hp))
    full = pl.BlockSpec((S, W), lambda hp, i: (0, hp))
    shp = jax.ShapeDtypeStruct((S, SB_HEADS * HEAD_DIM), F32)
    return pl.pallas_call(
        body, name=name, grid=(SB_HEADS // E, nq),
        in_specs=_sb_specs(S) + [pl.BlockSpec((T, W), lambda hp, i: (i, 512 // W + hp))],
        out_specs=[blk, full, full],
        out_shape=[jax.ShapeDtypeStruct((S, SB_HEADS * HEAD_DIM), BF16), shp, shp],
        compiler_params=_params(("parallel", "arbitrary")),
    )(proj, kv, kv, dmix)


def _conv_fwd(name, proj, conv_w, conv_b):
    S = proj.shape[0]
    T = CONV_T
    C = 1024
    K = SSM_CONV

    def body(x_ref, w_ref, b_ref, o_ref, buf):
        @pl.when(pl.program_id(0) == 0)
        def _():
            buf[0:8, :] = jnp.zeros((8, C), F32)

        buf[8:T + 8, :] = x_ref[...]
        acc = b_ref[...] + w_ref[K - 1:K, :] * buf[8:T + 8, :]
        for k in range(K - 1):
            acc = acc + w_ref[k:k + 1, :] * buf[5 + k:5 + k + T, :]
        o_ref[...] = acc * _sigmoid(acc)
        buf[0:8, :] = buf[T:T + 8, :]

    return pl.pallas_call(
        body, name=name, grid=(S // T,),
        in_specs=[pl.BlockSpec((T, C), lambda i: (i, 4)), pl.BlockSpec((K, C), lambda i: (0, 0)),
                  pl.BlockSpec((1, C), lambda i: (0, 0))],
        out_specs=pl.BlockSpec((T, C), lambda i: (i, 0)),
        out_shape=jax.ShapeDtypeStruct((S, C), F32),
        scratch_shapes=[pltpu.VMEM((T + 8, C), F32)],
        compiler_params=_params(("arbitrary",)),
    )(proj, conv_w, conv_b)


def _conv_bwd_act(name, proj, conv_w, conv_b, dxbc):
    S = proj.shape[0]
    T = CONV_T
    C = 1024
    K = SSM_CONV

    def body(x_ref, w_ref, b_ref, d_ref, dp_ref, sums_ref, buf):
        @pl.when(pl.program_id(0) == 0)
        def _():
            buf[0:8, :] = jnp.zeros((8, C), F32)
            sums_ref[...] = jnp.zeros_like(sums_ref)

        buf[8:T + 8, :] = x_ref[...]
        acc = b_ref[...] + w_ref[K - 1:K, :] * buf[8:T + 8, :]
        for k in range(K - 1):
            acc = acc + w_ref[k:k + 1, :] * buf[5 + k:5 + k + T, :]
        sg = _sigmoid(acc)
        dp = d_ref[...] * sg * (1.0 + acc * (1.0 - sg))
        dp_ref[...] = dp
        for k in range(K):
            sums_ref[k:k + 1, :] += jnp.sum(dp * buf[5 + k:5 + k + T, :], axis=0, keepdims=True)
        sums_ref[4:5, :] += jnp.sum(dp, axis=0, keepdims=True)
        buf[0:8, :] = buf[T:T + 8, :]

    row = pl.BlockSpec((T, C), lambda i: (i, 0))
    return pl.pallas_call(
        body, name=name, grid=(S // T,),
        in_specs=[pl.BlockSpec((T, C), lambda i: (i, 4)), pl.BlockSpec((K, C), lambda i: (0, 0)),
                  pl.BlockSpec((1, C), lambda i: (0, 0)), row],
        out_specs=[row, pl.BlockSpec((8, C), lambda i: (0, 0))],
        out_shape=[jax.ShapeDtypeStruct((S, C), F32), jax.ShapeDtypeStruct((8, C), F32)],
        scratch_shapes=[pltpu.VMEM((T + 8, C), F32)],
        compiler_params=_params(("arbitrary",)),
    )(proj, conv_w, conv_b, dxbc)


def _conv_bwd_in(name, dp, conv_w):
    S, C = dp.shape
    T = CONV_T
    K = SSM_CONV
    nb = S // T

    def body(d_ref, w_ref, o_ref, buf):
        @pl.when(pl.program_id(0) == 0)
        def _():
            buf[T:T + 8, :] = jnp.zeros((8, C), F32)

        buf[0:T, :] = d_ref[...]
        acc = w_ref[K - 1:K, :] * buf[0:T, :]
        for k in range(K - 1):
            acc = acc + w_ref[k:k + 1, :] * buf[3 - k:3 - k + T, :]
        o_ref[...] = acc.astype(o_ref.dtype)
        buf[T:T + 8, :] = buf[0:8, :]

    row = pl.BlockSpec((T, C), lambda i: (nb - 1 - i, 0))
    return pl.pallas_call(
        body, name=name, grid=(nb,),
        in_specs=[row, pl.BlockSpec((K, C), lambda i: (0, 0))],
        out_specs=row,
        out_shape=jax.ShapeDtypeStruct((S, C), BF16),
        scratch_shapes=[pltpu.VMEM((T + 8, C), F32)],
        compiler_params=_params(("arbitrary",)),
    )(dp, conv_w)


def _softplus(x):
    return jnp.maximum(x, 0.0) + jnp.log1p(jnp.exp(-jnp.abs(x)))


def _ssd_common(xbc_ref, dt_ref, dtb_ref, alog_ref):
    T = SSD_T
    dtr = dt_ref[...] + dtb_ref[...]
    dt = _softplus(dtr)
    a = -jnp.exp(alog_ref[...])
    dta = dt * a
    row = lax.broadcasted_iota(jnp.int32, (T, T), 0)
    col = lax.broadcasted_iota(jnp.int32, (T, T), 1)
    causal = col <= row
    acum = _split_dot(causal, dta)
    acum_t = acum.T
    return dtr, dt, a, acum, acum_t, causal, row


def _ssd_head(xbc_ref, h, dt, acum, acum_t, causal, cb):
    P = SSM_P
    ac = acum[:, h:h + 1]
    decay = jnp.exp(jnp.where(causal, ac - acum_t[h:h + 1, :], -1e30))
    dth = dt[:, h:h + 1]
    xs = xbc_ref[:, h * P:(h + 1) * P]
    xdt = xs * dth
    sc = (cb * decay).astype(BF16)
    aend = acum[SSD_T - 1:SSD_T, h:h + 1]
    return ac, dth, xs, xdt, sc, aend


def _ssd_specs(T, blk):
    vec = lambda n: pl.BlockSpec((1, n), lambda b: (0, 0))
    return [pl.BlockSpec((T, 1024), lambda b: (blk(b), 0)),
            pl.BlockSpec((T, LANES), lambda b: (blk(b), DT_COL // LANES)),
            pl.BlockSpec((T, 512), lambda b: (blk(b), 7)),
            vec(LANES), vec(LANES), vec(LANES), vec(512)]


def _ssd_fwd(name, xbc, proj, dt_bias, a_log, d_skip, gain):
    S = xbc.shape[0]
    T = SSD_T
    nb = S // T
    H, P, N = SSM_HEADS, SSM_P, SSM_N

    def body(xbc_ref, dt_ref, z_ref, dtb_ref, alog_ref, dsk_ref, gain_ref, y_ref, st_ref, st, ybuf):
        @pl.when(pl.program_id(0) == 0)
        def _():
            st[...] = jnp.zeros_like(st)

        _, dt, _, acum, acum_t, causal, _ = _ssd_common(xbc_ref, dt_ref, dtb_ref, alog_ref)
        for g in range(SSM_GROUPS):
            bg = xbc_ref[:, 512 + g * N:512 + (g + 1) * N]
            cg = xbc_ref[:, 768 + g * N:768 + (g + 1) * N]
            cb = _dot(cg, bg, NT)
            for hh in range(H // SSM_GROUPS):
                h = g * (H // SSM_GROUPS) + hh
                ac, _, xs, xdt, sc, aend = _ssd_head(xbc_ref, h, dt, acum, acum_t, causal, cb)
                sp = st[h]
                st_ref[0, h] = sp
                y = _dot(sc, xdt) + jnp.exp(ac) * _dot(cg, sp) + xs * dsk_ref[:, h:h + 1]
                st[h] = jnp.exp(aend) * sp + _dot(bg, xdt * jnp.exp(aend - ac), TN)
                ybuf[:, h * P:(h + 1) * P] = y
        z = z_ref[...]
        yg = ybuf[...] * z * _sigmoid(z)
        r = lax.rsqrt(jnp.mean(yg * yg, axis=-1, keepdims=True) + NORM_EPS)
        y_ref[...] = yg * r * gain_ref[...]

    return pl.pallas_call(
        body, name=name, grid=(nb,), in_specs=_ssd_specs(T, lambda b: b),
        out_specs=[pl.BlockSpec((T, 512), lambda b: (b, 0)), pl.BlockSpec((1, H, N, P), lambda b: (b, 0, 0, 0))],
        out_shape=[jax.ShapeDtypeStruct((S, 512), F32), jax.ShapeDtypeStruct((nb, H, N, P), F32)],
        scratch_shapes=[pltpu.VMEM((H, N, P), F32), pltpu.VMEM((T, 512), F32)],
        compiler_params=_params(("arbitrary",)),
    )(xbc, proj, proj, dt_bias, a_log, d_skip, gain)


def _ssd_bwd(name, xbc, proj, dt_bias, a_log, d_skip, gain, states, dmix, side=None):
    S = xbc.shape[0]
    T = SSD_T
    nb = S // T
    H, P, N = SSM_HEADS, SSM_P, SSM_N
    blk = lambda b: nb - 1 - b

    def body(xbc_ref, dt_ref, z_ref, dtb_ref, alog_ref, dsk_ref, gain_ref, st_ref, do_ref,
             dx_ref, dz_ref, ddt_ref, sums_ref, dst, ybuf):
        @pl.when(pl.program_id(0) == 0)
        def _():
            dst[...] = jnp.zeros_like(dst)
            sums_ref[...] = jnp.zeros_like(sums_ref)

        dtr, dt, a, acum, acum_t, causal, row = _ssd_common(xbc_ref, dt_ref, dtb_ref, alog_ref)
        cbs = []
        for g in range(SSM_GROUPS):
            bg = xbc_ref[:, 512 + g * N:512 + (g + 1) * N]
            cg = xbc_ref[:, 768 + g * N:768 + (g + 1) * N]
            cb = _dot(cg, bg, NT)
            cbs.append(cb)
            for hh in range(H // SSM_GROUPS):
                h = g * (H // SSM_GROUPS) + hh
                ac, _, xs, xdt, sc, _ = _ssd_head(xbc_ref, h, dt, acum, acum_t, causal, cb)
                ybuf[:, h * P:(h + 1) * P] = (_dot(sc, xdt) + jnp.exp(ac) * _dot(cg, st_ref[0, h])
                                              + xs * dsk_ref[:, h:h + 1])
        z = z_ref[...]
        sg = _sigmoid(z)
        sz = z * sg
        yfull = ybuf[...]
        yg = yfull * sz
        r = lax.rsqrt(jnp.mean(yg * yg, axis=-1, keepdims=True) + NORM_EPS)
        yh = yg * r
        do = do_ref[...]
        sums_ref[0:1, :] += jnp.sum(do * yh, axis=0, keepdims=True)
        dyh = do * gain_ref[...]
        dyg = r * (dyh - yh * jnp.mean(dyh * yh, axis=-1, keepdims=True))
        dz_ref[...] = (dyg * yfull * sg * (1.0 + z * (1.0 - sg))).astype(dz_ref.dtype)
        dyv = dyg * sz

        lane = lax.broadcasted_iota(jnp.int32, (T, LANES), 1)
        rowl = lax.broadcasted_iota(jnp.int32, (T, 1), 0)
        dacum = jnp.zeros((T, LANES), F32)
        dacum_t = jnp.zeros((LANES, T), F32)
        sub = lax.broadcasted_iota(jnp.int32, (LANES, T), 0)
        ddt = jnp.zeros((T, LANES), F32)
        dskp = jnp.zeros((T, LANES), F32)
        for g in range(SSM_GROUPS):
            bg = xbc_ref[:, 512 + g * N:512 + (g + 1) * N]
            cg = xbc_ref[:, 768 + g * N:768 + (g + 1) * N]
            cb = cbs[g]
            dcb = jnp.zeros((T, T), F32)
            dbg = jnp.zeros((T, N), F32)
            dcg = jnp.zeros((T, N), F32)
            for hh in range(H // SSM_GROUPS):
                h = g * (H // SSM_GROUPS) + hh
                ac, dth, xs, xdt, sc, aend = _ssd_head(xbc_ref, h, dt, acum, acum_t, causal, cb)
                decay = jnp.exp(jnp.where(causal, ac - acum_t[h:h + 1, :], -1e30))
                dy = dyv[:, h * P:(h + 1) * P]
                sp = st_ref[0, h]
                ea = jnp.exp(ac)
                de = jnp.exp(aend - ac)
                dec = jnp.exp(aend)
                dskp = dskp + jnp.where(lane == h, jnp.sum(dy * xs, axis=1, keepdims=True), 0.0)
                dxdt = _dot(sc, dy, TN)
                dsd = _dot(dy, xdt, NT) * decay
                dcb = dcb + dsd
                e = dsd * cb
                dacum_t = dacum_t + jnp.where(sub == h, jnp.sum(e, axis=0, keepdims=True), 0.0)
                dac = jnp.sum(e, axis=1, keepdims=True)
                dyea = dy * ea
                dac = dac + jnp.sum(dyea * _dot(cg, sp), axis=1, keepdims=True)
                dcg = dcg + _dot(dyea, sp, NT)
                dsp = _dot(cg, dyea, TN)
                dsn = dst[h]
                xde = xdt * de
                dbg = dbg + _dot(xde, dsn, NT)
                wh = _dot(bg, dsn)
                dxdt = dxdt + wh * de
                r_end = jnp.sum(wh * xde, axis=1, keepdims=True)
                dend = jnp.sum(r_end) + jnp.sum(dsn * sp) * dec
                dst[h] = dsp + dec * dsn
                dac = dac - r_end + jnp.where(rowl == T - 1, dend, 0.0)
                dacum = dacum + jnp.where(lane == h, dac, 0.0)
                ddt = ddt + jnp.where(lane == h, jnp.sum(dxdt * xs, axis=1, keepdims=True), 0.0)
                dx_ref[:, h * P:(h + 1) * P] = dxdt * dth + dy * dsk_ref[:, h:h + 1]
            dx_ref[:, 512 + g * N:512 + (g + 1) * N] = dbg + _dot(dcb, cg, TN)
            dx_ref[:, 768 + g * N:768 + (g + 1) * N] = dcg + _dot(dcb, bg)
        ddta = _split_dot(row <= lax.broadcasted_iota(jnp.int32, (T, T), 1), dacum - dacum_t.T)
        ddt = ddt + ddta * a
        ddtr = ddt * _sigmoid(dtr)
        ddt_ref[...] = ddtr.astype(ddt_ref.dtype)
        sums_ref[1:2, 0:LANES] += jnp.sum(ddtr, axis=0, keepdims=True)
        sums_ref[1:2, LANES:2 * LANES] += jnp.sum(ddta * dt, axis=0, keepdims=True) * a
        sums_ref[1:2, 2 * LANES:3 * LANES] += jnp.sum(dskp, axis=0, keepdims=True)

    specs = _ssd_specs(T, blk) + [pl.BlockSpec((1, H, N, P), lambda b: (blk(b), 0, 0, 0)),
                                  pl.BlockSpec((T, 512), lambda b: (blk(b), 2))]
    return _host_call(
        body, side, name, (nb,), specs,
        [pl.BlockSpec((T, 1024), lambda b: (blk(b), 0)), pl.BlockSpec((T, 512), lambda b: (blk(b), 0)),
         pl.BlockSpec((T, LANES), lambda b: (blk(b), 0)), pl.BlockSpec((8, 512), lambda b: (0, 0))],
        [jax.ShapeDtypeStruct((S, 1024), F32), jax.ShapeDtypeStruct((S, 512), BF16),
         jax.ShapeDtypeStruct((S, LANES), BF16), jax.ShapeDtypeStruct((8, 512), F32)],
        [pltpu.VMEM((H, N, P), F32), pltpu.VMEM((T, 512), F32)],
        (xbc, proj, proj, dt_bias, a_log, d_skip, gain, states, dmix))


def _place():
    return lax.axis_index("x"), lax.axis_index("y"), lax.axis_index("c")


def _flip(v, bit):
    return 1 - v if bit else v


def _gather_small(name, v):
    R, C = v.shape

    def body(v_ref, out_ref, send_sems, recv_sems, local_sem):
        x, y, c = _place()
        me = 4 * x + 2 * y + c
        mine = pltpu.make_async_copy(v_ref, out_ref.at[me], local_sem)
        mine.start()
        peers = [(_flip(x, (k >> 2) & 1), _flip(y, (k >> 1) & 1), _flip(c, k & 1)) for k in range(1, 8)]
        sends = []
        for k, peer in enumerate(peers):
            cp = pltpu.make_async_remote_copy(src_ref=v_ref, dst_ref=out_ref.at[me], send_sem=send_sems.at[k],
                                              recv_sem=recv_sems.at[k], device_id=peer, device_id_type=MESH)
            cp.start()
            sends.append(cp)
        for k, (px, py, pc) in enumerate(peers):
            pltpu.make_async_remote_copy(src_ref=v_ref, dst_ref=out_ref.at[4 * px + 2 * py + pc],
                                         send_sem=send_sems.at[k], recv_sem=recv_sems.at[k],
                                         device_id=(px, py, pc), device_id_type=MESH).wait_recv()
        for cp in sends:
            cp.wait_send()
        mine.wait()

    return pl.pallas_call(
        body, name=name, out_shape=jax.ShapeDtypeStruct((8, R, C), v.dtype),
        in_specs=[pl.BlockSpec(memory_space=pltpu.VMEM)], out_specs=pl.BlockSpec(memory_space=pltpu.VMEM),
        scratch_shapes=[pltpu.SemaphoreType.DMA((7,)), pltpu.SemaphoreType.DMA((7,)), pltpu.SemaphoreType.DMA(())],
    )(v)


def _hbm_call(body, name, arrays, out_shapes, n_sems):
    spec = pl.BlockSpec(memory_space=pl.ANY)
    return pl.pallas_call(
        body, name=name, out_shape=out_shapes, in_specs=[spec] * len(arrays), out_specs=[spec] * len(out_shapes),
        scratch_shapes=[pltpu.SemaphoreType.DMA((n_sems,)), pltpu.SemaphoreType.DMA((n_sems,))],
    )(*arrays)


class _ChipExchange:
    def __init__(self, arrays, core, layer=None):
        self.arrays, self.core, self.layer, self.n = list(arrays), core, layer, len(arrays)
        self.out_shapes = [jax.ShapeDtypeStruct((4,) + a.shape[1:], a.dtype) for a in arrays]
        self.scratch = [pltpu.SemaphoreType.DMA((3 * self.n,)), pltpu.SemaphoreType.DMA((3 * self.n,))]

    def _copies(self, ins, outs, send_sems, recv_sems):
        x, y, c = _place()
        chip = 2 * x + y
        peers = [(_flip(x, (k >> 1) & 1), _flip(y, k & 1)) for k in range(1, 4)]

        def copy(a, k, slot):
            px, py = peers[k]
            src = ins[a].at[self.layer] if self.layer is not None else ins[a].at[2 * px + py]
            return pltpu.make_async_remote_copy(
                src_ref=src, dst_ref=outs[a].at[slot], send_sem=send_sems.at[3 * a + k],
                recv_sem=recv_sems.at[3 * a + k], device_id=(px, py, c), device_id_type=MESH)

        sends = [copy(a, k, chip) for a in range(self.n) for k in range(3)]
        recvs = [copy(a, k, 2 * px + py) for a in range(self.n) for k, (px, py) in enumerate(peers)]
        return c == self.core, sends, recvs

    def start(self, refs, when=True):
        mine, sends, _ = self._copies(*refs)

        @pl.when(jnp.logical_and(mine, when))
        def _():
            for cp in sends:
                cp.start()

    def wait(self, refs, when=True):
        mine, sends, recvs = self._copies(*refs)

        @pl.when(jnp.logical_and(mine, when))
        def _():
            for cp in recvs:
                cp.wait_recv()
            for cp in sends:
                cp.wait_send()


def _chip_exchange(name, arrays, core, layer=None):
    ex = _ChipExchange(arrays, core, layer)

    def body(*refs):
        parts = (refs[:ex.n], refs[ex.n:2 * ex.n], refs[2 * ex.n], refs[2 * ex.n + 1])
        ex.start(parts)
        ex.wait(parts)

    return _hbm_call(body, name, ex.arrays, ex.out_shapes, 3 * ex.n)


def _host_call(body, side, name, grid, in_specs, out_specs, out_shape, scratch_shapes, operands):
    params = _params(("arbitrary",) * len(grid))
    if side is None:
        return pl.pallas_call(body, name=name, grid=grid, in_specs=in_specs, out_specs=out_specs, out_shape=out_shape,
                              scratch_shapes=scratch_shapes, compiler_params=params)(*operands), None
    n_in, n_out, n_scr, n = len(in_specs), len(out_specs), len(scratch_shapes), side.n
    hbm = pl.BlockSpec(memory_space=pl.ANY)

    def wrapped(*refs):
        i1 = n_in + n
        i2 = i1 + n_out
        i3 = i2 + n
        i4 = i3 + n_scr
        parts = (refs[n_in:i1], refs[i2:i3], refs[i4], refs[i4 + 1])
        first, last = True, True
        for ax, size in enumerate(grid):
            first = jnp.logical_and(first, pl.program_id(ax) == 0)
            last = jnp.logical_and(last, pl.program_id(ax) == size - 1)
        side.start(parts, first)
        body(*refs[:n_in], *refs[i1:i2], *refs[i3:i4])
        side.wait(parts, last)

    outs = pl.pallas_call(
        wrapped, name=name, grid=grid, in_specs=list(in_specs) + [hbm] * n, out_specs=list(out_specs) + [hbm] * n,
        out_shape=list(out_shape) + side.out_shapes, scratch_shapes=list(scratch_shapes) + side.scratch,
        compiler_params=params)(*operands, *side.arrays)
    return outs[:n_out], outs[n_out:]


def _sibling_send(name, arrays, sender):
    n = len(arrays)

    def body(*refs):
        ins, outs, send_sems, recv_sems = refs[:n], refs[n:2 * n], refs[2 * n], refs[2 * n + 1]
        x, y, c = _place()
        cps = [pltpu.make_async_remote_copy(src_ref=ins[a], dst_ref=outs[a], send_sem=send_sems.at[a],
                                            recv_sem=recv_sems.at[a], device_id=(x, y, 1 - c), device_id_type=MESH)
               for a in range(n)]

        @pl.when(c == sender)
        def _():
            for cp in cps:
                cp.start()
            for cp in cps:
                cp.wait_send()

        @pl.when(c != sender)
        def _():
            for cp in cps:
                cp.wait_recv()

    shapes = [jax.ShapeDtypeStruct(a.shape, a.dtype) for a in arrays]
    return _hbm_call(body, name, list(arrays), shapes, n)


def _sibling_swap(name, arrays, alt=None):
    n = len(arrays)

    def body(*refs):
        k = 1 if alt is None else 2
        outs, send_sems, recv_sems = refs[k * n:(k + 1) * n], refs[(k + 1) * n], refs[(k + 1) * n + 1]
        x, y, c = _place()

        def exchange(srcs):
            cps = [pltpu.make_async_remote_copy(src_ref=srcs[a], dst_ref=outs[a], send_sem=send_sems.at[a],
                                                recv_sem=recv_sems.at[a], device_id=(x, y, 1 - c),
                                                device_id_type=MESH) for a in range(n)]
            for cp in cps:
                cp.start()
            for cp in cps:
                cp.wait()

        if alt is None:
            exchange(refs[:n])
        else:
            @pl.when(c == 1)
            def _():
                exchange(refs[:n])

            @pl.when(c == 0)
            def _():
                exchange(refs[n:2 * n])

    shapes = [jax.ShapeDtypeStruct(a.shape, a.dtype) for a in arrays]
    return _hbm_call(body, name, list(arrays) + ([] if alt is None else list(alt)), shapes, n)


PACK_C = 1024
SUM_STEPS = 4


def _add_lists(name, own, other):
    n = len(other)

    def body(*refs):
        for a in range(n):
            refs[2 * n + a][...] = (refs[a][...].astype(F32) + refs[n + a][...].astype(F32)).astype(BF16)

    specs = [pl.BlockSpec((1, o.shape[1] // SUM_STEPS, o.shape[2]), lambda p, i: (p, i, 0)) for o in other]
    return pl.pallas_call(
        body, name=name, grid=(4, SUM_STEPS), in_specs=specs * 2, out_specs=specs,
        out_shape=[jax.ShapeDtypeStruct(o.shape, BF16) for o in other],
        compiler_params=_params(("parallel", "parallel")),
    )(*own, *other)


def _sum_chips(name, received, own, chip):
    n = len(received)

    def body(chip_ref, *refs):
        for a in range(n):
            s = None
            for q in range(4):
                term = jnp.where(chip_ref[0] == q, refs[n + a][0], refs[a][q]).astype(F32)
                s = term if s is None else s + term
            refs[2 * n + a][...] = s

    rec = [pl.BlockSpec((4, r.shape[1] // SUM_STEPS, r.shape[2]), lambda i, chip_ref: (0, i, 0)) for r in received]
    mine = [pl.BlockSpec((1, r.shape[1] // SUM_STEPS, r.shape[2]), lambda i, chip_ref: (chip_ref[0], i, 0))
            for r in received]
    outs = [pl.BlockSpec((r.shape[1] // SUM_STEPS, r.shape[2]), lambda i, chip_ref: (i, 0)) for r in received]
    grid_spec = pltpu.PrefetchScalarGridSpec(num_scalar_prefetch=1, grid=(SUM_STEPS,), in_specs=rec + mine,
                                             out_specs=outs)
    return pl.pallas_call(
        body, name=name, grid_spec=grid_spec,
        out_shape=[jax.ShapeDtypeStruct(r.shape[1:], F32) for r in received],
        compiler_params=_params(("parallel",)),
    )(jnp.reshape(chip, (1,)).astype(jnp.int32), *received, *own)


def _sum_devices(name, parts):
    _, R, C = parts.shape

    def body(p_ref, s_ref):
        s = p_ref[0]
        for q in range(1, 8):
            s = s + p_ref[q]
        s_ref[...] = s

    return pl.pallas_call(body, name=name, out_shape=jax.ShapeDtypeStruct((R, C), F32))(parts)


def _adamw_rule(w, g, m, v):
    nm = ADAM_B1 * m + (1.0 - ADAM_B1) * g
    nv = ADAM_B2 * v + (1.0 - ADAM_B2) * (g * g)
    m_hat = nm / (1.0 - ADAM_B1 ** ADAM_STEP)
    v_hat = nv / (1.0 - ADAM_B2 ** ADAM_STEP)
    return -ADAM_LR * (m_hat / (jnp.sqrt(v_hat) + ADAM_EPS) + ADAM_WD * w), nm, nv


def _adamw(name, w, g, m, v):
    L, R, C = w.shape
    tr = 128 if R % 128 == 0 else R

    def body(w_ref, g_ref, m_ref, v_ref, d_ref, nm_ref, nv_ref):
        d_ref[...], nm_ref[...], nv_ref[...] = _adamw_rule(w_ref[...], g_ref[...], m_ref[...], v_ref[...])

    spec = pl.BlockSpec((None, tr, C), lambda l, i: (l, i, 0))
    return pl.pallas_call(
        body, name=name, grid=(L, R // tr), in_specs=[spec] * 4, out_specs=[spec] * 3,
        out_shape=[jax.ShapeDtypeStruct((L, R, C), F32)] * 3, compiler_params=_params(("parallel", "parallel")),
    )(w, g, m, v)


def _adamw_layers(name, w, g_own, g_other, m, v, c):
    L, R, C = w.shape
    tr = 128 if R % 128 == 0 else R

    def body(c_ref, w_ref, g0_ref, g1_ref, go_ref, m_ref, v_ref, g_ref, d_ref, nm_ref, nv_ref):
        l = pl.program_id(0)
        own = jnp.where(l == 0, g0_ref[...], g1_ref[...])
        gv = jnp.where(l == c_ref[0], own, go_ref[...])
        g_ref[...] = gv
        d_ref[...], nm_ref[...], nv_ref[...] = _adamw_rule(w_ref[...], gv, m_ref[...], v_ref[...])

    full = pl.BlockSpec((None, tr, C), lambda l, i, c_ref: (l, i, 0))
    part = pl.BlockSpec((tr, C), lambda l, i, c_ref: (i, 0))
    grid_spec = pltpu.PrefetchScalarGridSpec(num_scalar_prefetch=1, grid=(L, R // tr),
                                             in_specs=[full, part, part, part, full, full], out_specs=[full] * 4)
    return pl.pallas_call(
        body, name=name, grid_spec=grid_spec, out_shape=[jax.ShapeDtypeStruct((L, R, C), F32)] * 4,
        compiler_params=_params(("parallel", "parallel")),
    )(jnp.reshape(c, (1,)).astype(jnp.int32), w, g_own[0], g_own[1], g_other, m, v)


BIG = ("ffn1_wg", "ffn1_wu", "ffn1_wd", "w_in", "w_out", "ffn2_wg", "ffn2_wu", "ffn2_wd")
HELD_TRANSPOSED = ("ffn1_wg", "ffn1_wu", "ffn2_wg", "ffn2_wu")
WEIGHTS =("ada_w", "ada_b", "norm_ffn1", "ffn1_wg", "ffn1_wu", "ffn1_wd", "norm_mix", "w_in", "conv_w", "conv_b",
           "dt_bias", "a_log", "d_skip", "ret_gn", "ssm_norm", "w_out", "norm_ffn2", "ffn2_wg", "ffn2_wu", "ffn2_wd",
           "final_ada_w", "final_ada_b", "final_norm")
SMALL = ("ada_b", "norm_ffn1", "norm_mix", "conv_w", "conv_b", "dt_bias", "a_log", "d_skip", "ret_gn", "ssm_norm",
         "norm_ffn2", "final_ada_b", "final_norm")


def _unpack(slab, shapes):
    flat = slab.reshape(-1)
    out, off = [], 0
    for shp in shapes:
        n = math.prod(shp)
        out.append(flat[off:off + n].reshape(shp))
        off += n
    return out


def _pad_lanes(v):
    return jnp.pad(v, (0, LANES - v.shape[0])).reshape(1, LANES)


def _ffn_fwd(tag, h, u, mod3, wg, wu, wd, nxt, side=None):
    (a, b, hm), fetched = _ffn_up(tag + "_up", u, wg, wu, side)
    gatefac = 0.5 * (1.0 + mod3[2:3])
    out, h_new, *u_next = _rowmm(tag + "_down", [(hm, p, wd, p) for p in range(4)], "nn", res=(h, gatefac, nxt))
    saved = dict(h=h, u=u, a=a, b=b, hm=hm, out=out, gatefac=gatefac)
    return h_new, (u_next[0] if nxt else None), saved, fetched


def _ffn_bwd(tag, g, d_out, s_gate, saved, gain, mod3, wg, wu, wd, nxt):
    sv = saved
    da, db = _ffn_bwd_mid(tag + "_mid_bwd", d_out, wd, sv["a"], sv["b"])
    dwd = _mm_tn_groups(tag + "_dwd", sv["hm"], d_out, BF16)
    dwg = _mm_tn_groups(tag + "_dwg", da, sv["u"], BF16)
    dwu = _mm_tn_groups(tag + "_dwu", db, sv["u"], BF16)
    g_in, s_norm, *below = _rowmm(
        tag + "_du", [(da, p, wg, p) for p in range(4)] + [(db, p, wu, p) for p in range(4)], "nn",
        norm_bwd=(sv["h"], gain, mod3[1:2], g, nxt))
    dmod3 = jnp.concatenate([s_norm[0:1], s_norm[1:2], 0.5 * s_gate[0:1]], axis=0)
    return g_in, below, (dwg, dwu, dwd), dmod3, s_norm[2]


def _mixer_fwd(tag, h, u, P, mod3, w_in, w_out, tables, nxt, side=None):
    proj = _mm(tag + "_in_proj", [(u, w_in)], "nn", F32, tm=512, tn=1792, tk=1024, n_outer=True)
    y_ret, ret_st = _ret_fwd(tag + "_ret", proj, P["ret_gn"], tables)
    kv = proj[:, 2560:3584].astype(BF16)
    y_sb, fetched = _sb_fwd(tag + "_sb", proj, kv, side)
    xbc = _conv_fwd(tag + "_conv", proj, P["conv_w"], P["conv_b"])
    y_ssm, ssm_st = _ssd_fwd(tag + "_ssd", xbc, proj, P["dt_bias"], P["a_log"], P["d_skip"], P["ssm_norm"])
    gatefac = 1.0 + mod3[2:3]
    ys = (y_ret, y_sb, y_ssm)
    w_out3 = w_out.reshape(3, 512, D_MODEL)
    mixed, h_new, *u_next = _rowmm(tag + "_out_proj", [(y, None, w_out3, i) for i, y in enumerate(ys)], "nn",
                                   res=(h, gatefac, nxt))
    saved = dict(h=h, u=u, proj=proj, kv=kv, ys=ys, ret_st=ret_st, xbc=xbc, ssm_st=ssm_st, out=mixed, gatefac=gatefac)
    return h_new, (u_next[0] if nxt else None), saved, fetched


def _mixer_bwd(tag, g, d_mixed, s_gate, saved, P, mod3, w_in, w_out, tables, nxt, side=None):
    h, u, proj, kv, ys, ret_st, xbc, ssm_st = (saved[k] for k in ("h", "u", "proj", "kv", "ys", "ret_st", "xbc",
                                                                  "ssm_st"))
    S = h.shape[0]
    dmix = _mm(tag + "_dmix", [(d_mixed, w_out)], "nt", F32, tm=512, tn=512, tk=1024)
    dw_out = jnp.concatenate(
        [_mm(tag + f"_dw_out{i}", [(y, d_mixed)], "tn", BF16, tm=512, tn=1024, tk=2048) for i, y in enumerate(ys)], axis=0)
    d_ret, d_gn = _ret_bwd(tag + "_ret_bwd", proj, P["ret_gn"], tables, ret_st, dmix)
    dq, dk, dv = _sb_bwd(tag + "_sb_bwd", proj, kv, dmix)
    (dxbc, dz, ddt, s_ssd), exchanged = _ssd_bwd(tag + "_ssd_bwd", xbc, proj, P["dt_bias"], P["a_log"], P["d_skip"],
                                                 P["ssm_norm"], ssm_st, dmix, side)
    dp, s_conv = _conv_bwd_act(tag + "_conv_bwd_act", proj, P["conv_w"], P["conv_b"], dxbc)
    dxr = _conv_bwd_in(tag + "_conv_bwd_in", dp, P["conv_w"])
    dproj = jnp.concatenate(
        [d_ret[0], d_ret[1], d_ret[2], d_ret[3], dq, dk.astype(BF16), dv.astype(BF16), dz, dxr, ddt,
         jnp.zeros((S, IN_WP - DT_COL - LANES), BF16)], axis=1)
    dw_in = _mm(tag + "_dw_in", [(u, dproj)], "tn", BF16, tm=1024, tn=768, tk=2048)
    g_in, s_norm, *below = _rowmm(tag + "_du", [(dproj, None, w_in, None)], "nt",
                                  norm_bwd=(h, P["norm_mix"], mod3[1:2], g, nxt))
    dmod3 = jnp.concatenate([s_norm[0:1], s_norm[1:2], s_gate[0:1]], axis=0)
    small = dict(norm_mix=s_norm[2], conv_w=s_conv[0:4], conv_b=s_conv[4], dt_bias=s_ssd[1, 0:8],
                 a_log=s_ssd[1, LANES:LANES + 8], d_skip=s_ssd[1, 2 * LANES:2 * LANES + 8],
                 ret_gn=d_gn[0], ssm_norm=s_ssd[0])
    return g_in, below, dw_in, dw_out, dmod3, small, exchanged


def kernel(x, c, ada_w, ada_b, norm_ffn1, ffn1_wg, ffn1_wu, ffn1_wd, norm_mix, w_in, conv_w, conv_b, dt_bias, a_log, d_skip, ret_gn, ssm_norm, w_out, norm_ffn2, ffn2_wg, ffn2_wu, ffn2_wd, final_ada_w, final_ada_b, final_norm, loss_target, m_ada_w, m_ada_b, m_norm_ffn1, m_ffn1_wg, m_ffn1_wu, m_ffn1_wd, m_norm_mix, m_w_in, m_conv_w, m_conv_b, m_dt_bias, m_a_log, m_d_skip, m_ret_gn, m_ssm_norm, m_w_out, m_norm_ffn2, m_ffn2_wg, m_ffn2_wu, m_ffn2_wd, m_final_ada_w, m_final_ada_b, m_final_norm, v_ada_w, v_ada_b, v_norm_ffn1, v_ffn1_wg, v_ffn1_wu, v_ffn1_wd, v_norm_mix, v_w_in, v_conv_w, v_conv_b, v_dt_bias, v_a_log, v_d_skip, v_ret_gn, v_ssm_norm, v_w_out, v_norm_ffn2, v_ffn2_wg, v_ffn2_wu, v_ffn2_wd, v_final_ada_w, v_final_ada_b, v_final_norm):
    W = dict(ada_w=ada_w, ada_b=ada_b, norm_ffn1=norm_ffn1, ffn1_wg=ffn1_wg, ffn1_wu=ffn1_wu, ffn1_wd=ffn1_wd,
             norm_mix=norm_mix, w_in=w_in, conv_w=conv_w, conv_b=conv_b, dt_bias=dt_bias, a_log=a_log, d_skip=d_skip,
             ret_gn=ret_gn, ssm_norm=ssm_norm, w_out=w_out, norm_ffn2=norm_ffn2, ffn2_wg=ffn2_wg, ffn2_wu=ffn2_wu,
             ffn2_wd=ffn2_wd, final_ada_w=final_ada_w, final_ada_b=final_ada_b, final_norm=final_norm)
    M = dict(ada_w=m_ada_w, ada_b=m_ada_b, norm_ffn1=m_norm_ffn1, ffn1_wg=m_ffn1_wg, ffn1_wu=m_ffn1_wu,
             ffn1_wd=m_ffn1_wd, norm_mix=m_norm_mix, w_in=m_w_in, conv_w=m_conv_w, conv_b=m_conv_b, dt_bias=m_dt_bias,
             a_log=m_a_log, d_skip=m_d_skip, ret_gn=m_ret_gn, ssm_norm=m_ssm_norm, w_out=m_w_out,
             norm_ffn2=m_norm_ffn2, ffn2_wg=m_ffn2_wg, ffn2_wu=m_ffn2_wu, ffn2_wd=m_ffn2_wd,
             final_ada_w=m_final_ada_w, final_ada_b=m_final_ada_b, final_norm=m_final_norm)
    V = dict(ada_w=v_ada_w, ada_b=v_ada_b, norm_ffn1=v_norm_ffn1, ffn1_wg=v_ffn1_wg, ffn1_wu=v_ffn1_wu,
             ffn1_wd=v_ffn1_wd, norm_mix=v_norm_mix, w_in=v_w_in, conv_w=v_conv_w, conv_b=v_conv_b, dt_bias=v_dt_bias,
             a_log=v_a_log, d_skip=v_d_skip, ret_gn=v_ret_gn, ssm_norm=v_ssm_norm, w_out=v_w_out,
             norm_ffn2=v_norm_ffn2, ffn2_wg=v_ffn2_wg, ffn2_wu=v_ffn2_wu, ffn2_wd=v_ffn2_wd,
             final_ada_w=v_final_ada_w, final_ada_b=v_final_ada_b, final_norm=v_final_norm)
    for n in HELD_TRANSPOSED:
        W[n], M[n], V[n] = (jnp.transpose(t[n], (0, 2, 1)) for t in (W, M, V))
    D = D_MODEL
    S = x.shape[1]
    ax, ay, ac = _place()
    me = 4 * ax + 2 * ay + ac
    chip = 2 * ax + ay
    h0 = x[0]

    c_all = _gather_small("gather_c", jnp.pad(c, ((0, 7), (0, 0))))[:, 0, :]
    cond_all = c_all * jax.nn.sigmoid(c_all)
    nmod = 3 * 3 * D // 4
    mod_part = jnp.concatenate(
        [_mm(f"mod_proj{l}", [(cond_all, ada_w[l])], "nn", F32, tm=8, tn=768, tk=D) for l in range(DEPTH)]
        + [_mm("mod_proj_final", [(cond_all, final_ada_w)], "nn", F32, tm=8, tn=512, tk=D),
           conv_w.reshape(DEPTH * SSM_CONV, -1)], axis=1)
    gathered = _gather_small("gather_mod", mod_part)[0::2]
    mine = lax.dynamic_index_in_dim(gathered, me, axis=1, keepdims=False)
    mods = [(jnp.reshape(mine[:, l * nmod:(l + 1) * nmod], (-1,)) + ada_b[l]).reshape(9, D) for l in range(DEPTH)]
    fmod = (jnp.reshape(mine[:, DEPTH * nmod:DEPTH * nmod + 2 * D // 4], (-1,)) + final_ada_b).reshape(2, D)
    conv_full = jnp.transpose(gathered[:, :, DEPTH * nmod + 2 * D // 4:], (1, 0, 2)).reshape(DEPTH, SSM_CONV, -1)

    shards = [W[n].astype(BF16) for n in BIG]

    def layer_weights(l, fetched):
        handed = _sibling_send(f"gather_weights_l{l}_sibling", fetched, sender=l)
        fw = {}
        for i, n in enumerate(BIG):
            arr = jnp.where(ac == l, fetched[i], handed[i])
            fw[n] = lax.dynamic_update_index_in_dim(arr, shards[i][l], chip, 0)
        fw["w_in"] = jnp.pad(jnp.transpose(fw["w_in"], (1, 0, 2)).reshape(D, IN_W), ((0, 0), (0, IN_WP - IN_W)))
        fw["w_out"] = fw["w_out"].reshape(MIX_W, D)
        return fw

    full = [layer_weights(0, _chip_exchange("gather_weights_l0", shards, core=0, layer=0))]
    n_first = len(BIG) // 2
    fetch_sides = (_ChipExchange(shards[:n_first], core=1, layer=1), _ChipExchange(shards[n_first:], core=1, layer=1))

    tables = _ret_tables(S)
    small_p = []
    for l in range(DEPTH):
        small_p.append(dict(
            norm_ffn1=norm_ffn1[l:l + 1], norm_mix=norm_mix[l:l + 1], norm_ffn2=norm_ffn2[l:l + 1],
            ret_gn=ret_gn[l:l + 1], ssm_norm=ssm_norm[l:l + 1], conv_w=conv_full[l], conv_b=conv_b[l:l + 1],
            dt_bias=_pad_lanes(dt_bias[l]), a_log=_pad_lanes(a_log[l]), d_skip=_pad_lanes(d_skip[l])))

    def norm_of(l, sub):
        gain = small_p[l][("norm_ffn1", "norm_mix", "norm_ffn2")[sub]]
        return gain, mods[l][3 * sub:3 * sub + 1], mods[l][3 * sub + 1:3 * sub + 2]

    h = h0
    u = _norm_mod("l0_ffn1_norm", h, *norm_of(0, 0))
    saved = []
    for l in range(DEPTH):
        P, fw, mod = small_p[l], full[l], mods[l]
        sides = fetch_sides if l == 0 else (None, None)
        h, u, s1, got_a = _ffn_fwd(f"l{l}_ffn1", h, u, mod[0:3], fw["ffn1_wg"], fw["ffn1_wu"], fw["ffn1_wd"],
                                   norm_of(l, 1), sides[0])
        h, u, sm, got_b = _mixer_fwd(f"l{l}_mix", h, u, P, mod[3:6], fw["w_in"], fw["w_out"], tables, norm_of(l, 2),
                                     sides[1])
        h, u, s2, _ = _ffn_fwd(f"l{l}_ffn2", h, u, mod[6:9], fw["ffn2_wg"], fw["ffn2_wu"], fw["ffn2_wd"],
                               norm_of(l + 1, 0) if l + 1 < DEPTH else None)
        saved.append((s1, sm, s2))
        if l == 0:
            full.append(layer_weights(1, list(got_a) + list(got_b)))
    fgain = final_norm.reshape(1, D)
    dy, loss_rows = _final_loss("final_loss", h, loss_target[0], fgain, fmod[0:1], fmod[1:2])

    below = lambda sv: (sv["out"], sv["gatefac"])
    g, s_final, d_out, s_gate = _norm_mod_bwd("final_norm_bwd", dy, h, fgain, fmod[1:2], *below(saved[-1][2]))
    dfmod = s_final[0:2]
    small_g = {n: [None] * DEPTH for n in SMALL}
    big_g = [None] * DEPTH
    dmods = [None] * DEPTH
    chip_sum = [None] * DEPTH
    reduce_side = None
    for l in reversed(range(DEPTH)):
        P, fw, mod = small_p[l], full[l], mods[l]
        s1, sm, s2 = saved[l]
        g, (d_out, s_gate), (dwg2, dwu2, dwd2), dm2, dn2 = _ffn_bwd(
            f"l{l}_ffn2", g, d_out, s_gate, s2, P["norm_ffn2"], mod[6:9], fw["ffn2_wg"], fw["ffn2_wu"], fw["ffn2_wd"],
            below(sm))
        g, (d_out, s_gate), dw_in, dw_out, dmm, sg, exchanged = _mixer_bwd(
            f"l{l}_mix", g, d_out, s_gate, sm, P, mod[3:6], fw["w_in"], fw["w_out"], tables, below(s1),
            reduce_side if l == 0 else None)
        g, nxt_start, (dwg1, dwu1, dwd1), dm1, dn1 = _ffn_bwd(
            f"l{l}_ffn1", g, d_out, s_gate, s1, P["norm_ffn1"], mod[0:3], fw["ffn1_wg"], fw["ffn1_wu"], fw["ffn1_wd"],
            below(saved[l - 1][2]) if l > 0 else None)
        if l > 0:
            d_out, s_gate = nxt_start
        dmods[l] = jnp.concatenate([dm1, dmm, dm2], axis=0)
        big_g[l] = dict(ffn1_wg=dwg1, ffn1_wu=dwu1, ffn1_wd=dwd1,
                        w_in=jnp.transpose(dw_in[:, :IN_W].reshape(D, 4, IN_W // 4), (1, 0, 2)),
                        w_out=dw_out.reshape(4, MIX_W // 4, D), ffn2_wg=dwg2, ffn2_wu=dwu2, ffn2_wd=dwd2)
        sg.update(norm_ffn1=dn1, norm_ffn2=dn2)
        for n, val in sg.items():
            small_g[n][l] = val
        g_l = [big_g[l][n] for n in BIG]
        chip_sum[l] = _add_lists(f"reduce_l{l}_add", g_l, _sibling_send(f"reduce_l{l}_sibling", g_l, sender=1 - l))
        if l == 1:
            reduce_side = _ChipExchange(chip_sum[1], core=1)
        else:
            from_chips = [_chip_exchange("reduce_l0_chips", chip_sum[0], core=0), exchanged]
    grad_x = g[None]

    n_mod = DEPTH * 9 * D + 2 * D
    vec = [jnp.stack(dmods).reshape(-1), dfmod.reshape(-1)]
    layered = [n for n in SMALL if n not in ("ada_b", "final_ada_b", "final_norm")]
    vec += [jnp.stack(small_g[n]).reshape(-1) for n in layered]
    vec += [s_final[2], jnp.sum(loss_rows[0]).reshape(1)]
    flat = jnp.concatenate(vec)
    vrows = -(-flat.shape[0] // (8 * PACK_C)) * 8
    slab = jnp.pad(flat, (0, vrows * PACK_C - flat.shape[0])).reshape(vrows, PACK_C)
    slabs = _gather_small("gather_small_grads", slab)
    total = _sum_devices("sum_small_grads", slabs).reshape(-1)
    grads = {}
    grads["ada_b"] = total[:DEPTH * 9 * D].reshape(DEPTH, 9 * D)
    grads["final_ada_b"] = total[DEPTH * 9 * D:n_mod]
    off = n_mod
    for n in layered:
        shp = (DEPTH,) + ((SSM_CONV, D) if n == "conv_w" else W[n].shape[1:])
        cnt = math.prod(shp)
        grads[n] = total[off:off + cnt].reshape(shp)
        off += cnt
    grads["final_norm"] = total[off:off + D]
    loss = total[off + D]
    grads["conv_w"] = lax.dynamic_slice_in_dim(grads["conv_w"], chip * (D // 4), D // 4, axis=2)

    dmod_all = slabs[:, :n_mod // PACK_C, :].reshape(8, n_mod)
    grads["ada_w"] = jnp.stack([
        _mm(f"grad_ada_w{l}", [(cond_all, lax.dynamic_slice_in_dim(dmod_all, l * 9 * D + chip * nmod, nmod, axis=1))],
            "tn", F32, tm=D, tn=768, tk=8) for l in range(DEPTH)])
    grads["final_ada_w"] = _mm(
        "grad_final_ada_w",
        [(cond_all, lax.dynamic_slice_in_dim(dmod_all, DEPTH * 9 * D + chip * (2 * D // 4), 2 * D // 4, axis=1))],
        "tn", F32, tm=D, tn=512, tk=8)

    reduced = [_sum_chips(f"reduce_l{l}_sum", from_chips[l], chip_sum[l], chip) for l in range(DEPTH)]
    reduced_other = _sibling_swap("reduce_grads_complete", reduced[1], alt=reduced[0])

    delta, new_m, new_v = {}, {}, {}
    for i, n in enumerate(BIG):
        grads[n], delta[n], new_m[n], new_v[n] = _adamw_layers(
            "adamw_" + n, W[n], (reduced[0][i], reduced[1][i]), reduced_other[i], M[n], V[n], ac)
    delta["ada_w"], new_m["ada_w"], new_v["ada_w"] = _adamw("adamw_ada_w", ada_w, grads["ada_w"], m_ada_w, v_ada_w)
    outs = _adamw("adamw_final_ada_w", final_ada_w[None], grads["final_ada_w"][None], m_final_ada_w[None],
                  v_final_ada_w[None])
    delta["final_ada_w"], new_m["final_ada_w"], new_v["final_ada_w"] = [o[0] for o in outs]
    small_shapes = [W[n].shape for n in SMALL]
    n_small = sum(math.prod(s) for s in small_shapes)
    srows = -(-n_small // (8 * LANES)) * 8
    slab_of = lambda T_: jnp.pad(jnp.concatenate([T_[n].reshape(-1) for n in SMALL]),
                                 (0, srows * LANES - n_small)).reshape(1, srows, LANES)
    outs = _adamw("adamw_small", slab_of(W), slab_of(grads), slab_of(M), slab_of(V))
    for res, o in zip((delta, new_m, new_v), outs):
        for n, val in zip(SMALL, _unpack(o[0], small_shapes)):
            res[n] = val

    for n in HELD_TRANSPOSED:
        for res in (grads, delta, new_m, new_v):
            res[n] = jnp.transpose(res[n], (0, 2, 1))
    return (loss, grad_x, *[grads[n] for n in WEIGHTS], *[delta[n] for n in WEIGHTS],
            *[new_m[n] for n in WEIGHTS], *[new_v[n] for n in WEIGHTS])
```

```python
import functools
import math

import jax
import jax.numpy as jnp
from jax import lax
from jax.experimental import pallas as pl
from jax.experimental.pallas import tpu as pltpu

F32 = jnp.float32
BF16 = jnp.bfloat16
MESH = pl.DeviceIdType.MESH

D_MODEL = 1024
DEPTH = 2
D_FF = 2816
RET_HEADS = 4
SB_HEADS = 4
HEAD_DIM = 128
SSM_HEADS = 8
SSM_P = 64
SSM_N = 128
SSM_GROUPS = 2
SSM_CONV = 4
MIX_W = 1536
IN_W = 5128
IN_WP = 5376
DT_COL = 5120
ROPE_BASE = 10000.0
NORM_EPS = 1e-6
ADAM_LR = 0.001
ADAM_B1 = 0.9
ADAM_B2 = 0.999
ADAM_EPS = 1e-08
ADAM_WD = 0.01
ADAM_STEP = 10

LANES = 128
VMEM_LIMIT = 56 * 1024 * 1024
RET_T = 256
RET_HPS = 4
SB_T = 256
SB_HPS = 2
SSD_T = 256
CONV_T = 512
ROW_T = 512
SB_DEAD = -150.0

NN = (((1,), (0,)), ((), ()))
NT = (((1,), (1,)), ((), ()))
TN = (((0,), (0,)), ((), ()))


def _dot(a, b, dims=NN):
    return lax.dot_general(a.astype(BF16), b.astype(BF16), dims, preferred_element_type=F32)


def _params(sem):
    return pltpu.CompilerParams(dimension_semantics=sem, vmem_limit_bytes=VMEM_LIMIT)


def _sigmoid(x):
    return 1.0 / (1.0 + jnp.exp(-x))


def _split_dot(mat01, x, dims=NN, lhs01=True, pieces=3):
    m = mat01.astype(BF16)
    total, rest = None, x
    for _ in range(pieces):
        p = rest.astype(BF16)
        rest = rest - p.astype(F32)
        d = lax.dot_general(m, p, dims, preferred_element_type=F32) if lhs01 else lax.dot_general(
            p, m, dims, preferred_element_type=F32)
        total = d if total is None else total + d
    return total


def _mm(name, terms, mode, out_dtype, tm=512, tn=512, tk=1024, n_outer=False):
    a0, b0 = terms[0]
    if mode == "nn":
        (M, K), N = a0.shape, b0.shape[1]
    elif mode == "nt":
        (M, K), N = a0.shape, b0.shape[0]
    else:
        (K, M), N = a0.shape, b0.shape[1]
    tm, tn, tk = min(tm, M), min(tn, N), min(tk, K)
    assert M % tm == 0 and N % tn == 0 and K % tk == 0, (name, M, N, K, tm, tn, tk)
    nk = K // tk
    nterm = len(terms)
    dims = {"nn": NN, "nt": NT, "tn": TN}[mode]

    def body(*refs):
        o_ref, acc = refs[2 * nterm], refs[2 * nterm + 1]
        part = None
        for t in range(nterm):
            p = _dot(refs[2 * t][...], refs[2 * t + 1][...], dims)
            part = p if part is None else part + p
        if nk == 1:
            o_ref[...] = part.astype(o_ref.dtype)
        else:
            k = pl.program_id(2)

            @pl.when(k == 0)
            def _():
                acc[...] = part

            @pl.when(k > 0)
            def _():
                acc[...] += part

            @pl.when(k == nk - 1)
            def _():
                o_ref[...] = acc[...].astype(o_ref.dtype)

    ij = (lambda g0, g1: (g1, g0)) if n_outer else (lambda g0, g1: (g0, g1))
    if mode == "nn":
        a_spec = pl.BlockSpec((tm, tk), lambda g0, g1, k: (ij(g0, g1)[0], k))
        b_spec = pl.BlockSpec((tk, tn), lambda g0, g1, k: (k, ij(g0, g1)[1]))
    elif mode == "nt":
        a_spec = pl.BlockSpec((tm, tk), lambda g0, g1, k: (ij(g0, g1)[0], k))
        b_spec = pl.BlockSpec((tn, tk), lambda g0, g1, k: (ij(g0, g1)[1], k))
    else:
        a_spec = pl.BlockSpec((tk, tm), lambda g0, g1, k: (k, ij(g0, g1)[0]))
        b_spec = pl.BlockSpec((tk, tn), lambda g0, g1, k: (k, ij(g0, g1)[1]))
    flat = [r for ab in terms for r in ab]
    return pl.pallas_call(
        body, name=name, grid=(N // tn, M // tm, nk) if n_outer else (M // tm, N // tn, nk),
        in_specs=[a_spec, b_spec] * nterm,
        out_specs=pl.BlockSpec((tm, tn), lambda g0, g1, k: ij(g0, g1)),
        out_shape=jax.ShapeDtypeStruct((M, N), out_dtype),
        scratch_shapes=[pltpu.VMEM((tm, tn) if nk > 1 else (8, LANES), F32)],
        compiler_params=_params(("parallel", "parallel", "arbitrary")),
    )(*flat)


def _norm_bwd_rows(d, x, gain, scale, sums_ref):
    r = lax.rsqrt(jnp.mean(x * x, axis=-1, keepdims=True) + NORM_EPS)
    xh = x * r
    dn = d * (1.0 + scale)
    sums_ref[0:1, :] += jnp.sum(d, axis=0, keepdims=True)
    sums_ref[1:2, :] += jnp.sum(d * xh * gain, axis=0, keepdims=True)
    sums_ref[2:3, :] += jnp.sum(dn * xh, axis=0, keepdims=True)
    dxh = dn * gain
    return r * (dxh - xh * jnp.mean(dxh * xh, axis=-1, keepdims=True))


def _rowmm(name, terms, mode, out_dtype=F32, res=None, norm_bwd=None):
    S = terms[0][0].shape[-2]
    N = terms[0][2].shape[-1] if mode == "nn" else terms[0][2].shape[-2]
    nterm = len(terms)
    tm = ROW_T
    dims = NN if mode == "nn" else NT
    nxt = None if (res or norm_bwd) is None else (res or norm_bwd)[-1]

    def body(*refs):
        o = None
        for t in range(nterm):
            p = _dot(refs[2 * t][...], refs[2 * t + 1][...], dims)
            o = p if o is None else o + p
        rest = refs[2 * nterm:]
        if norm_bwd is not None:
            h_ref, g_ref, sc_ref, gr_ref = rest[:4]
            dh_ref, sums_ref = rest[4 + (2 if nxt else 0):][:2]

            @pl.when(pl.program_id(0) == 0)
            def _():
                sums_ref[...] = jnp.zeros_like(sums_ref)

            g = gr_ref[...] + _norm_bwd_rows(o, h_ref[...], g_ref[...], sc_ref[...], sums_ref)
            dh_ref[...] = g
            if nxt:
                _gate_rows(g, rest[4], rest[5], rest[8], rest[9])
        elif res is None:
            rest[0][...] = o.astype(out_dtype)
        else:
            h_ref, gf_ref = rest[:2]
            out_ref, hn_ref = rest[2 + (3 if nxt else 0):][:2]
            out_ref[...] = o.astype(out_ref.dtype)
            hn = h_ref[...] + gf_ref[...] * o
            hn_ref[...] = hn
            if nxt:
                rest[7][...] = _norm_rows(hn, rest[2][...], rest[3][...], rest[4][...]).astype(BF16)

    in_specs, flat = [], []
    for a, ai, w, wi in terms:
        if ai is None:
            in_specs.append(pl.BlockSpec((tm, a.shape[1]), lambda i: (i, 0)))
        else:
            in_specs.append(pl.BlockSpec((None, tm, a.shape[2]), lambda i, g=ai: (g, i, 0)))
        if wi is None:
            in_specs.append(pl.BlockSpec(w.shape, lambda i: (0, 0)))
        else:
            in_specs.append(pl.BlockSpec((None,) + w.shape[1:], lambda i, g=wi: (g, 0, 0)))
        flat += [a, w]
    row = pl.BlockSpec((tm, N), lambda i: (i, 0))
    vec = pl.BlockSpec((1, N), lambda i: (0, 0))
    sums = pl.BlockSpec((8, N), lambda i: (0, 0))
    rows_f32, rows_bf16 = jax.ShapeDtypeStruct((S, N), F32), jax.ShapeDtypeStruct((S, N), BF16)
    sums_f32 = jax.ShapeDtypeStruct((8, N), F32)
    if norm_bwd is not None:
        extra = list(norm_bwd[:4]) + (list(nxt) if nxt else [])
        return pl.pallas_call(
            body, name=name, grid=(S // tm,),
            in_specs=in_specs + [row, vec, vec, row] + ([row, vec] if nxt else []),
            out_specs=[row, sums] + ([row, sums] if nxt else []),
            out_shape=[rows_f32, sums_f32] + ([rows_bf16, sums_f32] if nxt else []),
            compiler_params=_params(("arbitrary",)),
        )(*flat, *extra)
    if res is None:
        return pl.pallas_call(
            body, name=name, grid=(S // tm,), in_specs=in_specs, out_specs=row,
            out_shape=jax.ShapeDtypeStruct((S, N), out_dtype), compiler_params=_params(("parallel",)),
        )(*flat)
    extra = list(res[:2]) + (list(nxt) if nxt else [])
    return pl.pallas_call(
        body, name=name, grid=(S // tm,),
        in_specs=in_specs + [row, vec] + ([vec, vec, vec] if nxt else []),
        out_specs=[row, row] + ([row] if nxt else []),
        out_shape=[rows_bf16, rows_f32] + ([rows_bf16] if nxt else []),
        compiler_params=_params(("parallel",)),
    )(*flat, *extra)


def _mm_tn_groups(name, a, b, out_dtype, tk=2048):
    G = a.shape[0] if a.ndim == 3 else b.shape[0]
    S, M, N = a.shape[-2], a.shape[-1], b.shape[-1]
    tk = min(tk, S)
    nk = S // tk

    def body(a_ref, b_ref, o_ref, acc):
        k = pl.program_id(1)
        part = _dot(a_ref[...], b_ref[...], TN)

        @pl.when(k == 0)
        def _():
            acc[...] = part

        @pl.when(k > 0)
        def _():
            acc[...] += part

        @pl.when(k == nk - 1)
        def _():
            o_ref[...] = acc[...].astype(o_ref.dtype)

    def spec(arr, width):
        if arr.ndim == 3:
            return pl.BlockSpec((None, tk, width), lambda g, k: (g, k, 0))
        return pl.BlockSpec((tk, width), lambda g, k: (k, 0))

    return pl.pallas_call(
        body, name=name, grid=(G, nk), in_specs=[spec(a, M), spec(b, N)],
        out_specs=pl.BlockSpec((None, M, N), lambda g, k: (g, 0, 0)),
        out_shape=jax.ShapeDtypeStruct((G, M, N), out_dtype),
        scratch_shapes=[pltpu.VMEM((M, N), F32)],
        compiler_params=_params(("parallel", "arbitrary")),
    )(a, b)


def _norm_rows(x, gain, shift, scale):
    r = lax.rsqrt(jnp.mean(x * x, axis=-1, keepdims=True) + NORM_EPS)
    return x * r * gain * (1.0 + scale) + shift


def _gate_rows(g, out_ref, gf_ref, d_ref, gsum_ref):
    @pl.when(pl.program_id(0) == 0)
    def _():
        gsum_ref[...] = jnp.zeros_like(gsum_ref)

    d_ref[...] = (g * gf_ref[...]).astype(d_ref.dtype)
    gsum_ref[0:1, :] += jnp.sum(g * out_ref[...].astype(F32), axis=0, keepdims=True)


def _norm_mod(name, h, gain, shift, scale):
    S, D = h.shape
    tm = ROW_T

    def body(h_ref, g_ref, sh_ref, sc_ref, u_ref):
        u_ref[...] = _norm_rows(h_ref[...], g_ref[...], sh_ref[...], sc_ref[...]).astype(u_ref.dtype)

    vec = pl.BlockSpec((1, D), lambda i: (0, 0))
    return pl.pallas_call(
        body, name=name, grid=(S // tm,),
        in_specs=[pl.BlockSpec((tm, D), lambda i: (i, 0)), vec, vec, vec],
        out_specs=pl.BlockSpec((tm, D), lambda i: (i, 0)),
        out_shape=jax.ShapeDtypeStruct((S, D), BF16),
        compiler_params=_params(("parallel",)),
    )(h, gain, shift, scale)


def _norm_mod_bwd(name, du, h, gain, scale, out, gatefac):
    S, D = h.shape
    tm = ROW_T

    def body(du_ref, h_ref, g_ref, sc_ref, o_ref, gf_ref, dh_ref, sums_ref, d_ref, gsum_ref):
        @pl.when(pl.program_id(0) == 0)
        def _():
            sums_ref[...] = jnp.zeros_like(sums_ref)

        g = _norm_bwd_rows(du_ref[...], h_ref[...], g_ref[...], sc_ref[...], sums_ref)
        dh_ref[...] = g
        _gate_rows(g, o_ref, gf_ref, d_ref, gsum_ref)

    row = pl.BlockSpec((tm, D), lambda i: (i, 0))
    vec = pl.BlockSpec((1, D), lambda i: (0, 0))
    sums = pl.BlockSpec((8, D), lambda i: (0, 0))
    return pl.pallas_call(
        body, name=name, grid=(S // tm,),
        in_specs=[row, row, vec, vec, row, vec],
        out_specs=[row, sums, row, sums],
        out_shape=[jax.ShapeDtypeStruct((S, D), F32), jax.ShapeDtypeStruct((8, D), F32),
                   jax.ShapeDtypeStruct((S, D), BF16), jax.ShapeDtypeStruct((8, D), F32)],
        compiler_params=_params(("arbitrary",)),
    )(du, h, gain, scale, out, gatefac)


def _final_loss(name, h, target, gain, shift, scale):
    S, D = h.shape
    tm = ROW_T

    def body(h_ref, t_ref, g_ref, sh_ref, sc_ref, dy_ref, loss_ref):
        @pl.when(pl.program_id(0) == 0)
        def _():
            loss_ref[...] = jnp.zeros_like(loss_ref)

        x = h_ref[...]
        r = lax.rsqrt(jnp.mean(x * x, axis=-1, keepdims=True) + NORM_EPS)
        y = x * r * g_ref[...] * (1.0 + sc_ref[...]) + sh_ref[...]
        e = y - t_ref[...]
        dy_ref[...] = e * (1.0 / D)
        loss_ref[0:1, :] += 0.5 * jnp.sum(e * e, axis=0, keepdims=True) * (1.0 / D)

    row = pl.BlockSpec((tm, D), lambda i: (i, 0))
    vec = pl.BlockSpec((1, D), lambda i: (0, 0))
    return pl.pallas_call(
        body, name=name, grid=(S // tm,),
        in_specs=[row, row, vec, vec, vec],
        out_specs=[row, pl.BlockSpec((8, D), lambda i: (0, 0))],
        out_shape=[jax.ShapeDtypeStruct((S, D), F32), jax.ShapeDtypeStruct((8, D), F32)],
        compiler_params=_params(("arbitrary",)),
    )(h, target, gain, shift, scale)


def _ffn_up(name, u, wg, wu, side=None):
    S, D = u.shape
    G, Fg, _ = wg.shape
    tm = ROW_T

    def body(u_ref, wg_ref, wu_ref, a_ref, b_ref, hm_ref):
        uv = u_ref[...]
        a = _dot(uv, wg_ref[...], NT)
        b = _dot(uv, wu_ref[...], NT)
        sg = _sigmoid(a)
        act = a * sg
        a_ref[...] = (b * sg * (1.0 + a * (1.0 - sg))).astype(a_ref.dtype)
        b_ref[...] = act.astype(b_ref.dtype)
        hm_ref[...] = (act * b).astype(hm_ref.dtype)

    w_spec = pl.BlockSpec((None, Fg, D), lambda g, i: (g, 0, 0))
    o_spec = pl.BlockSpec((None, tm, Fg), lambda g, i: (g, i, 0))
    return _host_call(
        body, side, name, (G, S // tm), [pl.BlockSpec((tm, D), lambda g, i: (i, 0)), w_spec, w_spec], [o_spec] * 3,
        [jax.ShapeDtypeStruct((G, S, Fg), BF16)] * 3, [], (u, wg, wu))


def _ffn_bwd_mid(name, d_out, wd, a, b, side=None):
    S, D = d_out.shape
    G, Fg, _ = wd.shape
    tm = ROW_T

    def body(d_ref, wd_ref, a_ref, b_ref, da_ref, db_ref):
        dhm = _dot(d_ref[...], wd_ref[...], NT)
        da_ref[...] = (dhm * a_ref[...].astype(F32)).astype(da_ref.dtype)
        db_ref[...] = (dhm * b_ref[...].astype(F32)).astype(db_ref.dtype)

    t_spec = pl.BlockSpec((None, tm, Fg), lambda g, i: (g, i, 0))
    return _host_call(
        body, side, name, (G, S // tm),
        [pl.BlockSpec((tm, D), lambda g, i: (i, 0)), pl.BlockSpec((None, Fg, D), lambda g, i: (g, 0, 0)),
         t_spec, t_spec],
        [t_spec] * 2, [jax.ShapeDtypeStruct((G, S, Fg), BF16)] * 2, [], (d_out, wd, a, b))


def _ret_tables(S):
    T = RET_T
    half = HEAD_DIM // 2
    inv_freq = ROPE_BASE ** (-jnp.arange(half, dtype=F32) / half)
    ang = jnp.arange(S, dtype=F32)[:, None] * inv_freq[None, :]
    cos, sin = jnp.cos(ang), jnp.sin(ang)
    cosf = jnp.concatenate([cos, cos], axis=-1)
    sinf = jnp.concatenate([-sin, sin], axis=-1)
    log_gamma = jnp.log1p(-(2.0 ** (-5.0 - jnp.arange(RET_HEADS, dtype=F32))))
    idx = jnp.arange(T, dtype=F32)
    chunk = jnp.arange(T) // 64
    vis = (chunk[None, :] <= chunk[:, None]).astype(F32)
    mask = jnp.exp(log_gamma[:, None, None] * jnp.abs(idx[:, None] - idx[None, :])) * vis[None]
    ones = jnp.ones((1, 1, LANES), F32)
    qdec = jnp.exp(log_gamma[:, None] * (idx + 1.0)[None, :])[:, :, None] * ones
    kdec = jnp.exp(log_gamma[:, None] * (T - 1.0 - idx)[None, :])[:, :, None] * ones
    cdec = jnp.exp(log_gamma * T)[:, None, None] * jnp.ones((1, 8, LANES), F32)
    return cosf, sinf, mask, qdec, kdec, cdec


def _rope(x, cosf, sinf):
    return x * cosf + pltpu.roll(x, HEAD_DIM // 2, 1) * sinf


def _rope_bwd(d, cosf, sinf):
    return d * cosf + pltpu.roll(d * sinf, HEAD_DIM // 2, 1)


def _ret_specs(T, rev_nb=None):
    if rev_nb is None:
        blk = lambda b: b
    else:
        blk = lambda b: rev_nb - 1 - b
    E, W = RET_HPS, RET_HPS * HEAD_DIM
    proj = lambda off: pl.BlockSpec((T, W), lambda hp, b: (blk(b), off * (RET_HEADS // E) + hp))
    rows = pl.BlockSpec((T, HEAD_DIM), lambda hp, b: (blk(b), 0))
    per_head = lambda shape: pl.BlockSpec((E,) + shape, lambda hp, b: (hp, 0, 0))
    return ([proj(0), proj(1), proj(2), proj(3), rows, rows,
             per_head((T, T)), per_head((T, LANES)), per_head((T, LANES)), per_head((8, LANES)),
             pl.BlockSpec((1, W), lambda hp, b: (0, hp))], blk)


def _ret_fwd(name, proj, gn, tables):
    S = proj.shape[0]
    T, E = RET_T, RET_HPS
    nb = S // T
    scale = HEAD_DIM ** -0.5
    specs, _ = _ret_specs(T)

    def body(q_ref, k_ref, v_ref, g_ref, cos_ref, sin_ref, m_ref, qd_ref, kd_ref, cd_ref, gn_ref, y_ref, st_ref, st):
        @pl.when(pl.program_id(1) == 0)
        def _():
            st[...] = jnp.zeros_like(st)

        cosf, sinf = cos_ref[...], sin_ref[...]
        for e in range(E):
            qr = _rope(q_ref[:, _lanes(e)], cosf, sinf)
            kr = _rope(k_ref[:, _lanes(e)], cosf, sinf) * scale
            v = v_ref[:, _lanes(e)]
            sp = st[e]
            st_ref[e, 0] = sp
            w = _dot(qr, kr, NT) * m_ref[e]
            y = _dot(w, v) + _dot(qr * qd_ref[e], sp)
            st[e] = cd_ref[e, 0:1, :] * sp + _dot(kr * kd_ref[e], v, TN)
            r = lax.rsqrt(jnp.mean(y * y, axis=-1, keepdims=True) + NORM_EPS)
            g = g_ref[:, _lanes(e)]
            y_ref[:, _lanes(e)] = y * r * gn_ref[:, _lanes(e)] * (g * _sigmoid(g))

    return pl.pallas_call(
        body, name=name, grid=(RET_HEADS // E, nb), in_specs=specs,
        out_specs=[pl.BlockSpec((T, E * HEAD_DIM), lambda hp, b: (b, hp)),
                   pl.BlockSpec((E, 1, HEAD_DIM, HEAD_DIM), lambda hp, b: (hp, b, 0, 0))],
        out_shape=[jax.ShapeDtypeStruct((S, RET_HEADS * HEAD_DIM), F32),
                   jax.ShapeDtypeStruct((RET_HEADS, nb, HEAD_DIM, HEAD_DIM), F32)],
        scratch_shapes=[pltpu.VMEM((E, HEAD_DIM, HEAD_DIM), F32)],
        compiler_params=_params(("parallel", "arbitrary")),
    )(proj, proj, proj, proj, tables[0], tables[1], tables[2], tables[3], tables[4], tables[5], gn)


def _ret_bwd(name, proj, gn, tables, states, dmix):
    S = proj.shape[0]
    T, E = RET_T, RET_HPS
    W = E * HEAD_DIM
    nb = S // T
    scale = HEAD_DIM ** -0.5
    specs, blk = _ret_specs(T, rev_nb=nb)
    specs = specs + [pl.BlockSpec((E, 1, HEAD_DIM, HEAD_DIM), lambda hp, b: (hp, blk(b), 0, 0)),
                     pl.BlockSpec((T, W), lambda hp, b: (blk(b), hp))]

    def body(q_ref, k_ref, v_ref, g_ref, cos_ref, sin_ref, m_ref, qd_ref, kd_ref, cd_ref, gn_ref, st_ref, do_ref,
             d_ref, dgn_ref, dst):
        @pl.when(pl.program_id(1) == 0)
        def _():
            dst[...] = jnp.zeros_like(dst)
            dgn_ref[...] = jnp.zeros_like(dgn_ref)

        cosf, sinf = cos_ref[...], sin_ref[...]
        for e in range(E):
            qr = _rope(q_ref[:, _lanes(e)], cosf, sinf)
            kr = _rope(k_ref[:, _lanes(e)], cosf, sinf) * scale
            v = v_ref[:, _lanes(e)]
            sp = st_ref[e, 0]
            mask, qd, kd = m_ref[e], qd_ref[e], kd_ref[e]
            w = _dot(qr, kr, NT) * mask
            y = _dot(w, v) + _dot(qr * qd, sp)
            r = lax.rsqrt(jnp.mean(y * y, axis=-1, keepdims=True) + NORM_EPS)
            yh = y * r
            gn_v = gn_ref[:, _lanes(e)]
            g = g_ref[:, _lanes(e)]
            sg = _sigmoid(g)
            do = do_ref[:, _lanes(e)]
            dyn = do * g * sg
            dgn_ref[:, _lanes(e)] += jnp.sum(dyn * yh, axis=0, keepdims=True)
            dyh = dyn * gn_v
            dy = r * (dyh - yh * jnp.mean(dyh * yh, axis=-1, keepdims=True))
            dg = do * yh * gn_v * sg * (1.0 + g * (1.0 - sg))
            ds = dst[e]
            dp = _dot(dy, v, NT) * mask
            dqr = _dot(dp, kr) + _dot(dy, sp, NT) * qd
            dkr = _dot(dp, qr, TN) + _dot(v, ds, NT) * kd
            dv = _dot(w, dy, TN) + _dot(kr * kd, ds)
            dst[e] = cd_ref[e, 0:1, :] * ds + _dot(qr * qd, dy, TN)
            d_ref[0, :, _lanes(e)] = _rope_bwd(dqr, cosf, sinf).astype(d_ref.dtype)
            d_ref[1, :, _lanes(e)] = _rope_bwd(dkr * scale, cosf, sinf).astype(d_ref.dtype)
            d_ref[2, :, _lanes(e)] = dv.astype(d_ref.dtype)
            d_ref[3, :, _lanes(e)] = dg.astype(d_ref.dtype)

    return pl.pallas_call(
        body, name=name, grid=(RET_HEADS // E, nb), in_specs=specs,
        out_specs=[pl.BlockSpec((4, T, W), lambda hp, b: (0, blk(b), hp)),
                   pl.BlockSpec((1, W), lambda hp, b: (0, hp))],
        out_shape=[jax.ShapeDtypeStruct((4, S, RET_HEADS * HEAD_DIM), BF16),
                   jax.ShapeDtypeStruct((1, RET_HEADS * HEAD_DIM), F32)],
        scratch_shapes=[pltpu.VMEM((E, HEAD_DIM, HEAD_DIM), F32)],
        compiler_params=_params(("parallel", "arbitrary")),
    )(proj, proj, proj, proj, tables[0], tables[1], tables[2], tables[3], tables[4], tables[5], gn, states, dmix)


def _sb_logits(qb, kb, i, j, scale):
    T = SB_T
    z = lax.dot_general(qb, kb, NT, preferred_element_type=F32) * scale
    row = lax.broadcasted_iota(jnp.int32, (T, T), 0)
    col = lax.broadcasted_iota(jnp.int32, (T, T), 1)
    vis = jnp.logical_or(j < i, col < row)
    lp = jnp.log1p(jnp.exp(-jnp.abs(z)))
    lb = jnp.minimum(z, 0.0) - lp
    lk = jnp.where(vis, -jnp.maximum(z, 0.0) - lp, 0.0)
    return lb, lk, vis


def _sb_weights(lb, lk, vis, tailc):
    T = SB_T
    row = lax.broadcasted_iota(jnp.int32, (T, T), 0)
    col = lax.broadcasted_iota(jnp.int32, (T, T), 1)
    tail = tailc + _split_dot(row > col, lk, lhs01=False, pieces=2)
    return jnp.where(vis, jnp.exp(lb + tail), 0.0)


def _sb_specs(S):
    T, W = SB_T, SB_HPS * HEAD_DIM
    return [pl.BlockSpec((T, W), lambda hp, i: (i, 2048 // W + hp)),
            pl.BlockSpec((S, W), lambda hp, i: (0, hp)),
            pl.BlockSpec((S, W), lambda hp, i: (0, 512 // W + hp))]


def _lanes(e):
    return slice(e * HEAD_DIM, (e + 1) * HEAD_DIM)


def _sb_fwd(name, proj, kv, side=None):
    S = proj.shape[0]
    T, E = SB_T, SB_HPS
    nq = S // T
    scale = HEAD_DIM ** -0.5

    def body(q_ref, k_ref, v_ref, y_ref):
        i = pl.program_id(1)
        qs = [q_ref[:, _lanes(e)].astype(BF16) for e in range(E)]

        def cond(c):
            return jnp.logical_and(c[0] >= 0, c[1] == 0)

        def step(c):
            j, _, tails, accs = c
            rows = pl.ds(pl.multiple_of(j * T, T), T)
            new_tails, new_accs, worst = [], [], None
            for e in range(E):
                lb, lk, vis = _sb_logits(qs[e], k_ref[rows, _lanes(e)], i, j, scale)
                w = _sb_weights(lb, lk, vis, tails[e])
                new_accs.append(accs[e] + lax.dot_general(w.astype(BF16), v_ref[rows, _lanes(e)], NN,
                                                          preferred_element_type=F32))
                t = tails[e] + jnp.sum(lk, axis=1, keepdims=True)
                new_tails.append(t)
                worst = jnp.max(t) if worst is None else jnp.maximum(worst, jnp.max(t))
            return j - 1, (worst < SB_DEAD).astype(jnp.int32), tuple(new_tails), tuple(new_accs)

        init = (i, jnp.int32(0), (jnp.zeros((T, 1), F32),) * E, (jnp.zeros((T, HEAD_DIM), F32),) * E)
        accs = lax.while_loop(cond, step, init)[3]
        for e in range(E):
            y_ref[:, _lanes(e)] = accs[e]

    (y,), fetched = _host_call(
        body, side, name, (SB_HEADS // E, nq), _sb_specs(S), [pl.BlockSpec((T, E * HEAD_DIM), lambda hp, i: (i, hp))],
        [jax.ShapeDtypeStruct((S, SB_HEADS * HEAD_DIM), F32)], [], (proj, kv, kv))
    return y, fetched


def _sb_bwd(name, proj, kv, dmix):
    S = proj.shape[0]
    T, E = SB_T, SB_HPS
    nq = S // T
    scale = HEAD_DIM ** -0.5

    def body(q_ref, k_ref, v_ref, do_ref, dq_ref, dk_ref, dv_ref):
        i = pl.program_id(1)

        @pl.when(i == 0)
        def _():
            dk_ref[...] = jnp.zeros_like(dk_ref)
            dv_ref[...] = jnp.zeros_like(dv_ref)

        qs = [q_ref[:, _lanes(e)].astype(BF16) for e in range(E)]
        dos = [do_ref[:, _lanes(e)].astype(BF16) for e in range(E)]
        row = lax.broadcasted_iota(jnp.int32, (T, T), 0)
        col = lax.broadcasted_iota(jnp.int32, (T, T), 1)
        zcol = (jnp.zeros((T, 1), F32),) * E

        def cond(c):
            return jnp.logical_and(c[0] >= 0, c[1] == 0)

        def walk_left(c):
            j, _, tails = c
            rows = pl.ds(pl.multiple_of(j * T, T), T)
            new_tails, worst = [], None
            for e in range(E):
                _, lk, _ = _sb_logits(qs[e], k_ref[rows, _lanes(e)], i, j, scale)
                t = tails[e] + jnp.sum(lk, axis=1, keepdims=True)
                new_tails.append(t)
                worst = jnp.max(t) if worst is None else jnp.maximum(worst, jnp.max(t))
            return j - 1, (worst < SB_DEAD).astype(jnp.int32), tuple(new_tails)

        j_end, _, totals = lax.while_loop(cond, walk_left, (i, jnp.int32(0), zcol))

        def walk_back(j, c):
            lefts, used, dqs = c
            rows = pl.ds(pl.multiple_of(j * T, T), T)
            new_lefts, new_used, new_dqs = [], [], []
            for e in range(E):
                kb = k_ref[rows, _lanes(e)]
                lb, lk, vis = _sb_logits(qs[e], kb, i, j, scale)
                u = used[e] + jnp.sum(lk, axis=1, keepdims=True)
                w = _sb_weights(lb, lk, vis, totals[e] - u)
                de = lax.dot_general(dos[e], v_ref[rows, _lanes(e)], NT, preferred_element_type=F32) * w
                dlk = jnp.where(vis, lefts[e] + _split_dot(row < col, de, lhs01=False, pieces=2), 0.0)
                sg = jnp.exp(lb)
                dz = ((de * (1.0 - sg) - dlk * sg) * scale).astype(BF16)
                new_dqs.append(dqs[e] + lax.dot_general(dz, kb, NN, preferred_element_type=F32))
                dk_ref[rows, _lanes(e)] += lax.dot_general(dz, qs[e], TN, preferred_element_type=F32)
                dv_ref[rows, _lanes(e)] += lax.dot_general(w.astype(BF16), dos[e], TN, preferred_element_type=F32)
                new_lefts.append(lefts[e] + jnp.sum(de, axis=1, keepdims=True))
                new_used.append(u)
            return tuple(new_lefts), tuple(new_used), tuple(new_dqs)

        init = (zcol, zcol, (jnp.zeros((T, HEAD_DIM), F32),) * E)
        dqs = lax.fori_loop(j_end + 1, i + 1, walk_back, init)[2]
        for e in range(E):
            dq_ref[:, _lanes(e)] = dqs[e].astype(dq_ref.dtype)

    W = E * HEAD_DIM
    blk = pl.BlockSpec((T, W), lambda hp, i: (i, hp))
    full = pl.BlockSpec((S, W), lambda hp, i: (0, hp))
    shp = jax.ShapeDtypeStruct((S, SB_HEADS * HEAD_DIM), F32)
    return pl.pallas_call(
        body, name=name, grid=(SB_HEADS // E, nq),
        in_specs=_sb_specs(S) + [pl.BlockSpec((T, W), lambda hp, i: (i, 512 // W + hp))],
        out_specs=[blk, full, full],
        out_shape=[jax.ShapeDtypeStruct((S, SB_HEADS * HEAD_DIM), BF16), shp, shp],
        compiler_params=_params(("parallel", "arbitrary")),
    )(proj, kv, kv, dmix)


def _conv_fwd(name, proj, conv_w, conv_b):
    S = proj.shape[0]
    T = CONV_T
    C = 1024
    K = SSM_CONV

    def body(x_ref, w_ref, b_ref, o_ref, buf):
        @pl.when(pl.program_id(0) == 0)
        def _():
            buf[0:8, :] = jnp.zeros((8, C), F32)

        buf[8:T + 8, :] = x_ref[...]
        acc = b_ref[...] + w_ref[K - 1:K, :] * buf[8:T + 8, :]
        for k in range(K - 1):
            acc = acc + w_ref[k:k + 1, :] * buf[5 + k:5 + k + T, :]
        o_ref[...] = acc * _sigmoid(acc)
        buf[0:8, :] = buf[T:T + 8, :]

    return pl.pallas_call(
        body, name=name, grid=(S // T,),
        in_specs=[pl.BlockSpec((T, C), lambda i: (i, 4)), pl.BlockSpec((K, C), lambda i: (0, 0)),
                  pl.BlockSpec((1, C), lambda i: (0, 0))],
        out_specs=pl.BlockSpec((T, C), lambda i: (i, 0)),
        out_shape=jax.ShapeDtypeStruct((S, C), F32),
        scratch_shapes=[pltpu.VMEM((T + 8, C), F32)],
        compiler_params=_params(("arbitrary",)),
    )(proj, conv_w, conv_b)


def _conv_bwd_act(name, proj, conv_w, conv_b, dxbc):
    S = proj.shape[0]
    T = CONV_T
    C = 1024
    K = SSM_CONV

    def body(x_ref, w_ref, b_ref, d_ref, dp_ref, sums_ref, buf):
        @pl.when(pl.program_id(0) == 0)
        def _():
            buf[0:8, :] = jnp.zeros((8, C), F32)
            sums_ref[...] = jnp.zeros_like(sums_ref)

        buf[8:T + 8, :] = x_ref[...]
        acc = b_ref[...] + w_ref[K - 1:K, :] * buf[8:T + 8, :]
        for k in range(K - 1):
            acc = acc + w_ref[k:k + 1, :] * buf[5 + k:5 + k + T, :]
        sg = _sigmoid(acc)
        dp = d_ref[...] * sg * (1.0 + acc * (1.0 - sg))
        dp_ref[...] = dp
        for k in range(K):
            sums_ref[k:k + 1, :] += jnp.sum(dp * buf[5 + k:5 + k + T, :], axis=0, keepdims=True)
        sums_ref[4:5, :] += jnp.sum(dp, axis=0, keepdims=True)
        buf[0:8, :] = buf[T:T + 8, :]

    row = pl.BlockSpec((T, C), lambda i: (i, 0))
    return pl.pallas_call(
        body, name=name, grid=(S // T,),
        in_specs=[pl.BlockSpec((T, C), lambda i: (i, 4)), pl.BlockSpec((K, C), lambda i: (0, 0)),
                  pl.BlockSpec((1, C), lambda i: (0, 0)), row],
        out_specs=[row, pl.BlockSpec((8, C), lambda i: (0, 0))],
        out_shape=[jax.ShapeDtypeStruct((S, C), F32), jax.ShapeDtypeStruct((8, C), F32)],
        scratch_shapes=[pltpu.VMEM((T + 8, C), F32)],
        compiler_params=_params(("arbitrary",)),
    )(proj, conv_w, conv_b, dxbc)


def _conv_bwd_in(name, dp, conv_w):
    S, C = dp.shape
    T = CONV_T
    K = SSM_CONV
    nb = S // T

    def body(d_ref, w_ref, o_ref, buf):
        @pl.when(pl.program_id(0) == 0)
        def _():
            buf[T:T + 8, :] = jnp.zeros((8, C), F32)

        buf[0:T, :] = d_ref[...]
        acc = w_ref[K - 1:K, :] * buf[0:T, :]
        for k in range(K - 1):
            acc = acc + w_ref[k:k + 1, :] * buf[3 - k:3 - k + T, :]
        o_ref[...] = acc.astype(o_ref.dtype)
        buf[T:T + 8, :] = buf[0:8, :]

    row = pl.BlockSpec((T, C), lambda i: (nb - 1 - i, 0))
    return pl.pallas_call(
        body, name=name, grid=(nb,),
        in_specs=[row, pl.BlockSpec((K, C), lambda i: (0, 0))],
        out_specs=row,
        out_shape=jax.ShapeDtypeStruct((S, C), BF16),
        scratch_shapes=[pltpu.VMEM((T + 8, C), F32)],
        compiler_params=_params(("arbitrary",)),
    )(dp, conv_w)


def _softplus(x):
    return jnp.maximum(x, 0.0) + jnp.log1p(jnp.exp(-jnp.abs(x)))


def _ssd_common(xbc_ref, dt_ref, dtb_ref, alog_ref):
    T = SSD_T
    dtr = dt_ref[...] + dtb_ref[...]
    dt = _softplus(dtr)
    a = -jnp.exp(alog_ref[...])
    dta = dt * a
    row = lax.broadcasted_iota(jnp.int32, (T, T), 0)
    col = lax.broadcasted_iota(jnp.int32, (T, T), 1)
    causal = col <= row
    acum = _split_dot(causal, dta)
    acum_t = acum.T
    return dtr, dt, a, acum, acum_t, causal, row


def _ssd_head(xbc_ref, h, dt, acum, acum_t, causal, cb):
    P = SSM_P
    ac = acum[:, h:h + 1]
    decay = jnp.exp(jnp.where(causal, ac - acum_t[h:h + 1, :], -1e30))
    dth = dt[:, h:h + 1]
    xs = xbc_ref[:, h * P:(h + 1) * P]
    xdt = xs * dth
    sc = (cb * decay).astype(BF16)
    aend = acum[SSD_T - 1:SSD_T, h:h + 1]
    return ac, dth, xs, xdt, sc, aend


def _ssd_specs(T, blk):
    vec = lambda n: pl.BlockSpec((1, n), lambda b: (0, 0))
    return [pl.BlockSpec((T, 1024), lambda b: (blk(b), 0)),
            pl.BlockSpec((T, LANES), lambda b: (blk(b), DT_COL // LANES)),
            pl.BlockSpec((T, 512), lambda b: (blk(b), 7)),
            vec(LANES), vec(LANES), vec(LANES), vec(512)]


def _ssd_fwd(name, xbc, proj, dt_bias, a_log, d_skip, gain):
    S = xbc.shape[0]
    T = SSD_T
    nb = S // T
    H, P, N = SSM_HEADS, SSM_P, SSM_N

    def body(xbc_ref, dt_ref, z_ref, dtb_ref, alog_ref, dsk_ref, gain_ref, y_ref, st_ref, st, ybuf):
        @pl.when(pl.program_id(0) == 0)
        def _():
            st[...] = jnp.zeros_like(st)

        _, dt, _, acum, acum_t, causal, _ = _ssd_common(xbc_ref, dt_ref, dtb_ref, alog_ref)
        for g in range(SSM_GROUPS):
            bg = xbc_ref[:, 512 + g * N:512 + (g + 1) * N]
            cg = xbc_ref[:, 768 + g * N:768 + (g + 1) * N]
            cb = _dot(cg, bg, NT)
            for hh in range(H // SSM_GROUPS):
                h = g * (H // SSM_GROUPS) + hh
                ac, _, xs, xdt, sc, aend = _ssd_head(xbc_ref, h, dt, acum, acum_t, causal, cb)
                sp = st[h]
                st_ref[0, h] = sp
                y = _dot(sc, xdt) + jnp.exp(ac) * _dot(cg, sp) + xs * dsk_ref[:, h:h + 1]
                st[h] = jnp.exp(aend) * sp + _dot(bg, xdt * jnp.exp(aend - ac), TN)
                ybuf[:, h * P:(h + 1) * P] = y
        z = z_ref[...]
        yg = ybuf[...] * z * _sigmoid(z)
        r = lax.rsqrt(jnp.mean(yg * yg, axis=-1, keepdims=True) + NORM_EPS)
        y_ref[...] = yg * r * gain_ref[...]

    return pl.pallas_call(
        body, name=name, grid=(nb,), in_specs=_ssd_specs(T, lambda b: b),
        out_specs=[pl.BlockSpec((T, 512), lambda b: (b, 0)), pl.BlockSpec((1, H, N, P), lambda b: (b, 0, 0, 0))],
        out_shape=[jax.ShapeDtypeStruct((S, 512), F32), jax.ShapeDtypeStruct((nb, H, N, P), F32)],
        scratch_shapes=[pltpu.VMEM((H, N, P), F32), pltpu.VMEM((T, 512), F32)],
        compiler_params=_params(("arbitrary",)),
    )(xbc, proj, proj, dt_bias, a_log, d_skip, gain)


def _ssd_bwd(name, xbc, proj, dt_bias, a_log, d_skip, gain, states, dmix, side=None):
    S = xbc.shape[0]
    T = SSD_T
    nb = S // T
    H, P, N = SSM_HEADS, SSM_P, SSM_N
    blk = lambda b: nb - 1 - b

    def body(xbc_ref, dt_ref, z_ref, dtb_ref, alog_ref, dsk_ref, gain_ref, st_ref, do_ref,
             dx_ref, dz_ref, ddt_ref, sums_ref, dst, ybuf):
        @pl.when(pl.program_id(0) == 0)
        def _():
            dst[...] = jnp.zeros_like(dst)
            sums_ref[...] = jnp.zeros_like(sums_ref)

        dtr, dt, a, acum, acum_t, causal, row = _ssd_common(xbc_ref, dt_ref, dtb_ref, alog_ref)
        cbs = []
        for g in range(SSM_GROUPS):
            bg = xbc_ref[:, 512 + g * N:512 + (g + 1) * N]
            cg = xbc_ref[:, 768 + g * N:768 + (g + 1) * N]
            cb = _dot(cg, bg, NT)
            cbs.append(cb)
            for hh in range(H // SSM_GROUPS):
                h = g * (H // SSM_GROUPS) + hh
                ac, _, xs, xdt, sc, _ = _ssd_head(xbc_ref, h, dt, acum, acum_t, causal, cb)
                ybuf[:, h * P:(h + 1) * P] = (_dot(sc, xdt) + jnp.exp(ac) * _dot(cg, st_ref[0, h])
                                              + xs * dsk_ref[:, h:h + 1])
        z = z_ref[...]
        sg = _sigmoid(z)
        sz = z * sg
        yfull = ybuf[...]
        yg = yfull * sz
        r = lax.rsqrt(jnp.mean(yg * yg, axis=-1, keepdims=True) + NORM_EPS)
        yh = yg * r
        do = do_ref[...]
        sums_ref[0:1, :] += jnp.sum(do * yh, axis=0, keepdims=True)
        dyh = do * gain_ref[...]
        dyg = r * (dyh - yh * jnp.mean(dyh * yh, axis=-1, keepdims=True))
        dz_ref[...] = (dyg * yfull * sg * (1.0 + z * (1.0 - sg))).astype(dz_ref.dtype)
        dyv = dyg * sz

        lane = lax.broadcasted_iota(jnp.int32, (T, LANES), 1)
        rowl = lax.broadcasted_iota(jnp.int32, (T, 1), 0)
        dacum = jnp.zeros((T, LANES), F32)
        dacum_t = jnp.zeros((LANES, T), F32)
        sub = lax.broadcasted_iota(jnp.int32, (LANES, T), 0)
        ddt = jnp.zeros((T, LANES), F32)
        dskp = jnp.zeros((T, LANES), F32)
        for g in range(SSM_GROUPS):
            bg = xbc_ref[:, 512 + g * N:512 + (g + 1) * N]
            cg = xbc_ref[:, 768 + g * N:768 + (g + 1) * N]
            cb = cbs[g]
            dcb = jnp.zeros((T, T), F32)
            dbg = jnp.zeros((T, N), F32)
            dcg = jnp.zeros((T, N), F32)
            for hh in range(H // SSM_GROUPS):
                h = g * (H // SSM_GROUPS) + hh
                ac, dth, xs, xdt, sc, aend = _ssd_head(xbc_ref, h, dt, acum, acum_t, causal, cb)
                decay = jnp.exp(jnp.where(causal, ac - acum_t[h:h + 1, :], -1e30))
                dy = dyv[:, h * P:(h + 1) * P]
                sp = st_ref[0, h]
                ea = jnp.exp(ac)
                de = jnp.exp(aend - ac)
                dec = jnp.exp(aend)
                dskp = dskp + jnp.where(lane == h, jnp.sum(dy * xs, axis=1, keepdims=True), 0.0)
                dxdt = _dot(sc, dy, TN)
                dsd = _dot(dy, xdt, NT) * decay
                dcb = dcb + dsd
                e = dsd * cb
                dacum_t = dacum_t + jnp.where(sub == h, jnp.sum(e, axis=0, keepdims=True), 0.0)
                dac = jnp.sum(e, axis=1, keepdims=True)
                dyea = dy * ea
                dac = dac + jnp.sum(dyea * _dot(cg, sp), axis=1, keepdims=True)
                dcg = dcg + _dot(dyea, sp, NT)
                dsp = _dot(cg, dyea, TN)
                dsn = dst[h]
                xde = xdt * de
                dbg = dbg + _dot(xde, dsn, NT)
                wh = _dot(bg, dsn)
                dxdt = dxdt + wh * de
                r_end = jnp.sum(wh * xde, axis=1, keepdims=True)
                dend = jnp.sum(r_end) + jnp.sum(dsn * sp) * dec
                dst[h] = dsp + dec * dsn
                dac = dac - r_end + jnp.where(rowl == T - 1, dend, 0.0)
                dacum = dacum + jnp.where(lane == h, dac, 0.0)
                ddt = ddt + jnp.where(lane == h, jnp.sum(dxdt * xs, axis=1, keepdims=True), 0.0)
                dx_ref[:, h * P:(h + 1) * P] = dxdt * dth + dy * dsk_ref[:, h:h + 1]
            dx_ref[:, 512 + g * N:512 + (g + 1) * N] = dbg + _dot(dcb, cg, TN)
            dx_ref[:, 768 + g * N:768 + (g + 1) * N] = dcg + _dot(dcb, bg)
        ddta = _split_dot(row <= lax.broadcasted_iota(jnp.int32, (T, T), 1), dacum - dacum_t.T)
        ddt = ddt + ddta * a
        ddtr = ddt * _sigmoid(dtr)
        ddt_ref[...] = ddtr.astype(ddt_ref.dtype)
        sums_ref[1:2, 0:LANES] += jnp.sum(ddtr, axis=0, keepdims=True)
        sums_ref[1:2, LANES:2 * LANES] += jnp.sum(ddta * dt, axis=0, keepdims=True) * a
        sums_ref[1:2, 2 * LANES:3 * LANES] += jnp.sum(dskp, axis=0, keepdims=True)

    specs = _ssd_specs(T, blk) + [pl.BlockSpec((1, H, N, P), lambda b: (blk(b), 0, 0, 0)),
                                  pl.BlockSpec((T, 512), lambda b: (blk(b), 2))]
    return _host_call(
        body, side, name, (nb,), specs,
        [pl.BlockSpec((T, 1024), lambda b: (blk(b), 0)), pl.BlockSpec((T, 512), lambda b: (blk(b), 0)),
         pl.BlockSpec((T, LANES), lambda b: (blk(b), 0)), pl.BlockSpec((8, 512), lambda b: (0, 0))],
        [jax.ShapeDtypeStruct((S, 1024), F32), jax.ShapeDtypeStruct((S, 512), BF16),
         jax.ShapeDtypeStruct((S, LANES), BF16), jax.ShapeDtypeStruct((8, 512), F32)],
        [pltpu.VMEM((H, N, P), F32), pltpu.VMEM((T, 512), F32)],
        (xbc, proj, proj, dt_bias, a_log, d_skip, gain, states, dmix))


def _place():
    return lax.axis_index("x"), lax.axis_index("y"), lax.axis_index("c")


def _flip(v, bit):
    return 1 - v if bit else v


def _gather_small(name, v):
    R, C = v.shape

    def body(v_ref, out_ref, send_sems, recv_sems, local_sem):
        x, y, c = _place()
        me = 4 * x + 2 * y + c
        mine = pltpu.make_async_copy(v_ref, out_ref.at[me], local_sem)
        mine.start()
        peers = [(_flip(x, (k >> 2) & 1), _flip(y, (k >> 1) & 1), _flip(c, k & 1)) for k in range(1, 8)]
        sends = []
        for k, peer in enumerate(peers):
            cp = pltpu.make_async_remote_copy(src_ref=v_ref, dst_ref=out_ref.at[me], send_sem=send_sems.at[k],
                                              recv_sem=recv_sems.at[k], device_id=peer, device_id_type=MESH)
            cp.start()
            sends.append(cp)
        for k, (px, py, pc) in enumerate(peers):
            pltpu.make_async_remote_copy(src_ref=v_ref, dst_ref=out_ref.at[4 * px + 2 * py + pc],
                                         send_sem=send_sems.at[k], recv_sem=recv_sems.at[k],
                                         device_id=(px, py, pc), device_id_type=MESH).wait_recv()
        for cp in sends:
            cp.wait_send()
        mine.wait()

    return pl.pallas_call(
        body, name=name, out_shape=jax.ShapeDtypeStruct((8, R, C), v.dtype),
        in_specs=[pl.BlockSpec(memory_space=pltpu.VMEM)], out_specs=pl.BlockSpec(memory_space=pltpu.VMEM),
        scratch_shapes=[pltpu.SemaphoreType.DMA((7,)), pltpu.SemaphoreType.DMA((7,)), pltpu.SemaphoreType.DMA(())],
    )(v)


def _hbm_call(body, name, arrays, out_shapes, n_sems):
    spec = pl.BlockSpec(memory_space=pl.ANY)
    return pl.pallas_call(
        body, name=name, out_shape=out_shapes, in_specs=[spec] * len(arrays), out_specs=[spec] * len(out_shapes),
        scratch_shapes=[pltpu.SemaphoreType.DMA((n_sems,)), pltpu.SemaphoreType.DMA((n_sems,))],
    )(*arrays)


def _exchange_call(ex, name):
    spec = pl.BlockSpec(memory_space=pl.ANY)

    def body(*refs):
        parts = (refs[:ex.n], refs[ex.n:2 * ex.n], refs[2 * ex.n:])
        ex.start(parts)
        ex.wait(parts)

    return pl.pallas_call(body, name=name, out_shape=ex.out_shapes, in_specs=[spec] * ex.n, out_specs=[spec] * ex.n,
                          scratch_shapes=ex.scratch)(*ex.arrays)


class _ChipExchange:
    def __init__(self, arrays, core, layer=None, forward=False):
        self.arrays, self.core, self.layer, self.n, self.forward = list(arrays), core, layer, len(arrays), forward
        self.out_shapes = [jax.ShapeDtypeStruct((4,) + a.shape[1:], a.dtype) for a in arrays]
        self.scratch = [pltpu.SemaphoreType.DMA((3 * self.n,))] * (4 if forward else 2)

    def _copies(self, ins, outs, sems):
        x, y, c = _place()
        chip = 2 * x + y
        peers = [(_flip(x, (k >> 1) & 1), _flip(y, k & 1)) for k in range(1, 4)]

        def copy(a, k, slot):
            px, py = peers[k]
            src = ins[a].at[self.layer] if self.layer is not None else ins[a].at[2 * px + py]
            return pltpu.make_async_remote_copy(
                src_ref=src, dst_ref=outs[a].at[slot], send_sem=sems[0].at[3 * a + k],
                recv_sem=sems[1].at[3 * a + k], device_id=(px, py, c), device_id_type=MESH)

        def passed_on(a, k):
            px, py = peers[k]
            slot = outs[a].at[2 * px + py]
            return pltpu.make_async_remote_copy(
                src_ref=slot, dst_ref=slot, send_sem=sems[2].at[3 * a + k], recv_sem=sems[3].at[3 * a + k],
                device_id=(x, y, 1 - c), device_id_type=MESH)

        pairs = [(a, k) for a in range(self.n) for k in range(3)]
        sends = [copy(a, k, chip) for a, k in pairs]
        recvs = [copy(a, k, 2 * peers[k][0] + peers[k][1]) for a, k in pairs]
        onward = [passed_on(a, k) for a, k in pairs] if self.forward else []
        return c == self.core, sends, recvs, onward

    def start(self, refs, when=True):
        mine, sends, _, _ = self._copies(*refs)

        @pl.when(jnp.logical_and(mine, when))
        def _():
            for cp in sends:
                cp.start()

    def wait(self, refs, when=True):
        mine, sends, recvs, onward = self._copies(*refs)

        @pl.when(jnp.logical_and(mine, when))
        def _():
            for i, cp in enumerate(recvs):
                cp.wait_recv()
                if onward:
                    onward[i].start()
            for cp in sends + onward:
                cp.wait_send()

        if onward:
            @pl.when(jnp.logical_and(jnp.logical_not(mine), when))
            def _():
                for cp in onward:
                    cp.wait_recv()


class _SiblingSend:
    def __init__(self, arrays, sender):
        self.arrays, self.sender, self.n = list(arrays), sender, len(arrays)
        self.out_shapes = [jax.ShapeDtypeStruct(a.shape, a.dtype) for a in arrays]
        self.scratch = [pltpu.SemaphoreType.DMA((self.n,))] * 2

    def _copies(self, ins, outs, sems):
        x, y, c = _place()
        cps = [pltpu.make_async_remote_copy(src_ref=ins[a], dst_ref=outs[a], send_sem=sems[0].at[a],
                                            recv_sem=sems[1].at[a], device_id=(x, y, 1 - c), device_id_type=MESH)
               for a in range(self.n)]
        return c == self.sender, cps

    def start(self, refs, when=True):
        mine, cps = self._copies(*refs)

        @pl.when(jnp.logical_and(mine, when))
        def _():
            for cp in cps:
                cp.start()

    def wait(self, refs, when=True):
        mine, cps = self._copies(*refs)

        @pl.when(jnp.logical_and(mine, when))
        def _():
            for cp in cps:
                cp.wait_send()

        @pl.when(jnp.logical_and(jnp.logical_not(mine), when))
        def _():
            for cp in cps:
                cp.wait_recv()


def _host_call(body, side, name, grid, in_specs, out_specs, out_shape, scratch_shapes, operands):
    params = _params(("arbitrary",) * len(grid))
    if side is None:
        return pl.pallas_call(body, name=name, grid=grid, in_specs=in_specs, out_specs=out_specs, out_shape=out_shape,
                              scratch_shapes=scratch_shapes, compiler_params=params)(*operands), None
    n_in, n_out, n_scr, n = len(in_specs), len(out_specs), len(scratch_shapes), side.n
    hbm = pl.BlockSpec(memory_space=pl.ANY)

    def wrapped(*refs):
        i1 = n_in + n
        i2 = i1 + n_out
        i3 = i2 + n
        i4 = i3 + n_scr
        parts = (refs[n_in:i1], refs[i2:i3], refs[i4:])
        first, last = True, True
        for ax, size in enumerate(grid):
            first = jnp.logical_and(first, pl.program_id(ax) == 0)
            last = jnp.logical_and(last, pl.program_id(ax) == size - 1)
        side.start(parts, first)
        body(*refs[:n_in], *refs[i1:i2], *refs[i3:i4])
        side.wait(parts, last)

    outs = pl.pallas_call(
        wrapped, name=name, grid=grid, in_specs=list(in_specs) + [hbm] * n, out_specs=list(out_specs) + [hbm] * n,
        out_shape=list(out_shape) + side.out_shapes, scratch_shapes=list(scratch_shapes) + side.scratch,
        compiler_params=params)(*operands, *side.arrays)
    return outs[:n_out], outs[n_out:]


def _sibling_swap(name, arrays, alt=None):
    n = len(arrays)

    def body(*refs):
        k = 1 if alt is None else 2
        outs, send_sems, recv_sems = refs[k * n:(k + 1) * n], refs[(k + 1) * n], refs[(k + 1) * n + 1]
        x, y, c = _place()

        def exchange(srcs):
            cps = [pltpu.make_async_remote_copy(src_ref=srcs[a], dst_ref=outs[a], send_sem=send_sems.at[a],
                                                recv_sem=recv_sems.at[a], device_id=(x, y, 1 - c),
                                                device_id_type=MESH) for a in range(n)]
            for cp in cps:
                cp.start()
            for cp in cps:
                cp.wait()

        if alt is None:
            exchange(refs[:n])
        else:
            @pl.when(c == 1)
            def _():
                exchange(refs[:n])

            @pl.when(c == 0)
            def _():
                exchange(refs[n:2 * n])

    shapes = [jax.ShapeDtypeStruct(a.shape, a.dtype) for a in arrays]
    return _hbm_call(body, name, list(arrays) + ([] if alt is None else list(alt)), shapes, n)


PACK_C = 1024
SUM_STEPS = 4


def _add_lists(name, own, other):
    n = len(other)

    def body(*refs):
        for a in range(n):
            refs[2 * n + a][...] = (refs[a][...].astype(F32) + refs[n + a][...].astype(F32)).astype(BF16)

    specs = [pl.BlockSpec((1, o.shape[1] // SUM_STEPS, o.shape[2]), lambda p, i: (p, i, 0)) for o in other]
    return pl.pallas_call(
        body, name=name, grid=(4, SUM_STEPS), in_specs=specs * 2, out_specs=specs,
        out_shape=[jax.ShapeDtypeStruct(o.shape, BF16) for o in other],
        compiler_params=_params(("parallel", "parallel")),
    )(*own, *other)


def _sum_chips(name, received, own, chip):
    n = len(received)

    def body(chip_ref, *refs):
        for a in range(n):
            s = None
            for q in range(4):
                term = jnp.where(chip_ref[0] == q, refs[n + a][0], refs[a][q]).astype(F32)
                s = term if s is None else s + term
            refs[2 * n + a][...] = s

    rec = [pl.BlockSpec((4, r.shape[1] // SUM_STEPS, r.shape[2]), lambda i, chip_ref: (0, i, 0)) for r in received]
    mine = [pl.BlockSpec((1, r.shape[1] // SUM_STEPS, r.shape[2]), lambda i, chip_ref: (chip_ref[0], i, 0))
            for r in received]
    outs = [pl.BlockSpec((r.shape[1] // SUM_STEPS, r.shape[2]), lambda i, chip_ref: (i, 0)) for r in received]
    grid_spec = pltpu.PrefetchScalarGridSpec(num_scalar_prefetch=1, grid=(SUM_STEPS,), in_specs=rec + mine,
                                             out_specs=outs)
    return pl.pallas_call(
        body, name=name, grid_spec=grid_spec,
        out_shape=[jax.ShapeDtypeStruct(r.shape[1:], F32) for r in received],
        compiler_params=_params(("parallel",)),
    )(jnp.reshape(chip, (1,)).astype(jnp.int32), *received, *own)


def _sum_devices(name, parts):
    _, R, C = parts.shape

    def body(p_ref, s_ref):
        s = p_ref[0]
        for q in range(1, 8):
            s = s + p_ref[q]
        s_ref[...] = s

    return pl.pallas_call(body, name=name, out_shape=jax.ShapeDtypeStruct((R, C), F32))(parts)


def _adamw_rule(w, g, m, v):
    nm = ADAM_B1 * m + (1.0 - ADAM_B1) * g
    nv = ADAM_B2 * v + (1.0 - ADAM_B2) * (g * g)
    m_hat = nm / (1.0 - ADAM_B1 ** ADAM_STEP)
    v_hat = nv / (1.0 - ADAM_B2 ** ADAM_STEP)
    return -ADAM_LR * (m_hat / (jnp.sqrt(v_hat) + ADAM_EPS) + ADAM_WD * w), nm, nv


def _adamw(name, w, g, m, v):
    L, R, C = w.shape
    tr = 128 if R % 128 == 0 else R

    def body(w_ref, g_ref, m_ref, v_ref, d_ref, nm_ref, nv_ref):
        d_ref[...], nm_ref[...], nv_ref[...] = _adamw_rule(w_ref[...], g_ref[...], m_ref[...], v_ref[...])

    spec = pl.BlockSpec((None, tr, C), lambda l, i: (l, i, 0))
    return pl.pallas_call(
        body, name=name, grid=(L, R // tr), in_specs=[spec] * 4, out_specs=[spec] * 3,
        out_shape=[jax.ShapeDtypeStruct((L, R, C), F32)] * 3, compiler_params=_params(("parallel", "parallel")),
    )(w, g, m, v)


def _adamw_layers(name, w, g_own, g_other, m, v, c):
    L, R, C = w.shape
    tr = 128 if R % 128 == 0 else R

    def body(c_ref, w_ref, g0_ref, g1_ref, go_ref, m_ref, v_ref, g_ref, d_ref, nm_ref, nv_ref):
        l = pl.program_id(0)
        own = jnp.where(l == 0, g0_ref[...], g1_ref[...])
        gv = jnp.where(l == c_ref[0], own, go_ref[...])
        g_ref[...] = gv
        d_ref[...], nm_ref[...], nv_ref[...] = _adamw_rule(w_ref[...], gv, m_ref[...], v_ref[...])

    full = pl.BlockSpec((None, tr, C), lambda l, i, c_ref: (l, i, 0))
    part = pl.BlockSpec((tr, C), lambda l, i, c_ref: (i, 0))
    grid_spec = pltpu.PrefetchScalarGridSpec(num_scalar_prefetch=1, grid=(L, R // tr),
                                             in_specs=[full, part, part, part, full, full], out_specs=[full] * 4)
    return pl.pallas_call(
        body, name=name, grid_spec=grid_spec, out_shape=[jax.ShapeDtypeStruct((L, R, C), F32)] * 4,
        compiler_params=_params(("parallel", "parallel")),
    )(jnp.reshape(c, (1,)).astype(jnp.int32), w, g_own[0], g_own[1], g_other, m, v)


BIG = ("ffn1_wg", "ffn1_wu", "ffn1_wd", "w_in", "w_out", "ffn2_wg", "ffn2_wu", "ffn2_wd")
HELD_TRANSPOSED = ("ffn1_wg", "ffn1_wu", "ffn2_wg", "ffn2_wu")
WEIGHTS =("ada_w", "ada_b", "norm_ffn1", "ffn1_wg", "ffn1_wu", "ffn1_wd", "norm_mix", "w_in", "conv_w", "conv_b",
           "dt_bias", "a_log", "d_skip", "ret_gn", "ssm_norm", "w_out", "norm_ffn2", "ffn2_wg", "ffn2_wu", "ffn2_wd",
           "final_ada_w", "final_ada_b", "final_norm")
SMALL = ("ada_b", "norm_ffn1", "norm_mix", "conv_w", "conv_b", "dt_bias", "a_log", "d_skip", "ret_gn", "ssm_norm",
         "norm_ffn2", "final_ada_b", "final_norm")


def _unpack(slab, shapes):
    flat = slab.reshape(-1)
    out, off = [], 0
    for shp in shapes:
        n = math.prod(shp)
        out.append(flat[off:off + n].reshape(shp))
        off += n
    return out


def _pad_lanes(v):
    return jnp.pad(v, (0, LANES - v.shape[0])).reshape(1, LANES)


def _ffn_fwd(tag, h, u, mod3, wg, wu, wd, nxt, side=None):
    (a, b, hm), fetched = _ffn_up(tag + "_up", u, wg, wu, side)
    gatefac = 0.5 * (1.0 + mod3[2:3])
    out, h_new, *u_next = _rowmm(tag + "_down", [(hm, p, wd, p) for p in range(4)], "nn", res=(h, gatefac, nxt))
    saved = dict(h=h, u=u, a=a, b=b, hm=hm, out=out, gatefac=gatefac)
    return h_new, (u_next[0] if nxt else None), saved, fetched


def _ffn_bwd(tag, g, d_out, s_gate, saved, gain, mod3, wg, wu, wd, nxt, side=None):
    sv = saved
    (da, db), handed = _ffn_bwd_mid(tag + "_mid_bwd", d_out, wd, sv["a"], sv["b"], side)
    dwd = _mm_tn_groups(tag + "_dwd", sv["hm"], d_out, BF16)
    dwg = _mm_tn_groups(tag + "_dwg", da, sv["u"], BF16)
    dwu = _mm_tn_groups(tag + "_dwu", db, sv["u"], BF16)
    g_in, s_norm, *below = _rowmm(
        tag + "_du", [(da, p, wg, p) for p in range(4)] + [(db, p, wu, p) for p in range(4)], "nn",
        norm_bwd=(sv["h"], gain, mod3[1:2], g, nxt))
    dmod3 = jnp.concatenate([s_norm[0:1], s_norm[1:2], 0.5 * s_gate[0:1]], axis=0)
    return g_in, below, (dwg, dwu, dwd), dmod3, s_norm[2], handed


def _mixer_fwd(tag, h, u, P, mod3, w_in, w_out, tables, nxt, side=None):
    proj = _mm(tag + "_in_proj", [(u, w_in)], "nn", F32, tm=512, tn=1792, tk=1024, n_outer=True)
    y_ret, ret_st = _ret_fwd(tag + "_ret", proj, P["ret_gn"], tables)
    kv = proj[:, 2560:3584].astype(BF16)
    y_sb, fetched = _sb_fwd(tag + "_sb", proj, kv, side)
    xbc = _conv_fwd(tag + "_conv", proj, P["conv_w"], P["conv_b"])
    y_ssm, ssm_st = _ssd_fwd(tag + "_ssd", xbc, proj, P["dt_bias"], P["a_log"], P["d_skip"], P["ssm_norm"])
    gatefac = 1.0 + mod3[2:3]
    ys = (y_ret, y_sb, y_ssm)
    w_out3 = w_out.reshape(3, 512, D_MODEL)
    mixed, h_new, *u_next = _rowmm(tag + "_out_proj", [(y, None, w_out3, i) for i, y in enumerate(ys)], "nn",
                                   res=(h, gatefac, nxt))
    saved = dict(h=h, u=u, proj=proj, kv=kv, ys=ys, ret_st=ret_st, xbc=xbc, ssm_st=ssm_st, out=mixed, gatefac=gatefac)
    return h_new, (u_next[0] if nxt else None), saved, fetched


def _mixer_bwd(tag, g, d_mixed, s_gate, saved, P, mod3, w_in, w_out, tables, nxt, side=None):
    h, u, proj, kv, ys, ret_st, xbc, ssm_st = (saved[k] for k in ("h", "u", "proj", "kv", "ys", "ret_st", "xbc",
                                                                  "ssm_st"))
    S = h.shape[0]
    dmix = _mm(tag + "_dmix", [(d_mixed, w_out)], "nt", F32, tm=512, tn=512, tk=1024)
    dw_out = jnp.concatenate(
        [_mm(tag + f"_dw_out{i}", [(y, d_mixed)], "tn", BF16, tm=512, tn=1024, tk=2048) for i, y in enumerate(ys)], axis=0)
    d_ret, d_gn = _ret_bwd(tag + "_ret_bwd", proj, P["ret_gn"], tables, ret_st, dmix)
    dq, dk, dv = _sb_bwd(tag + "_sb_bwd", proj, kv, dmix)
    (dxbc, dz, ddt, s_ssd), exchanged = _ssd_bwd(tag + "_ssd_bwd", xbc, proj, P["dt_bias"], P["a_log"], P["d_skip"],
                                                 P["ssm_norm"], ssm_st, dmix, side)
    dp, s_conv = _conv_bwd_act(tag + "_conv_bwd_act", proj, P["conv_w"], P["conv_b"], dxbc)
    dxr = _conv_bwd_in(tag + "_conv_bwd_in", dp, P["conv_w"])
    dproj = jnp.concatenate(
        [d_ret[0], d_ret[1], d_ret[2], d_ret[3], dq, dk.astype(BF16), dv.astype(BF16), dz, dxr, ddt,
         jnp.zeros((S, IN_WP - DT_COL - LANES), BF16)], axis=1)
    dw_in = _mm(tag + "_dw_in", [(u, dproj)], "tn", BF16, tm=1024, tn=768, tk=2048)
    g_in, s_norm, *below = _rowmm(tag + "_du", [(dproj, None, w_in, None)], "nt",
                                  norm_bwd=(h, P["norm_mix"], mod3[1:2], g, nxt))
    dmod3 = jnp.concatenate([s_norm[0:1], s_norm[1:2], s_gate[0:1]], axis=0)
    small = dict(norm_mix=s_norm[2], conv_w=s_conv[0:4], conv_b=s_conv[4], dt_bias=s_ssd[1, 0:8],
                 a_log=s_ssd[1, LANES:LANES + 8], d_skip=s_ssd[1, 2 * LANES:2 * LANES + 8],
                 ret_gn=d_gn[0], ssm_norm=s_ssd[0])
    return g_in, below, dw_in, dw_out, dmod3, small, exchanged


def kernel(x, c, ada_w, ada_b, norm_ffn1, ffn1_wg, ffn1_wu, ffn1_wd, norm_mix, w_in, conv_w, conv_b, dt_bias, a_log, d_skip, ret_gn, ssm_norm, w_out, norm_ffn2, ffn2_wg, ffn2_wu, ffn2_wd, final_ada_w, final_ada_b, final_norm, loss_target, m_ada_w, m_ada_b, m_norm_ffn1, m_ffn1_wg, m_ffn1_wu, m_ffn1_wd, m_norm_mix, m_w_in, m_conv_w, m_conv_b, m_dt_bias, m_a_log, m_d_skip, m_ret_gn, m_ssm_norm, m_w_out, m_norm_ffn2, m_ffn2_wg, m_ffn2_wu, m_ffn2_wd, m_final_ada_w, m_final_ada_b, m_final_norm, v_ada_w, v_ada_b, v_norm_ffn1, v_ffn1_wg, v_ffn1_wu, v_ffn1_wd, v_norm_mix, v_w_in, v_conv_w, v_conv_b, v_dt_bias, v_a_log, v_d_skip, v_ret_gn, v_ssm_norm, v_w_out, v_norm_ffn2, v_ffn2_wg, v_ffn2_wu, v_ffn2_wd, v_final_ada_w, v_final_ada_b, v_final_norm):
    W = dict(ada_w=ada_w, ada_b=ada_b, norm_ffn1=norm_ffn1, ffn1_wg=ffn1_wg, ffn1_wu=ffn1_wu, ffn1_wd=ffn1_wd,
             norm_mix=norm_mix, w_in=w_in, conv_w=conv_w, conv_b=conv_b, dt_bias=dt_bias, a_log=a_log, d_skip=d_skip,
             ret_gn=ret_gn, ssm_norm=ssm_norm, w_out=w_out, norm_ffn2=norm_ffn2, ffn2_wg=ffn2_wg, ffn2_wu=ffn2_wu,
             ffn2_wd=ffn2_wd, final_ada_w=final_ada_w, final_ada_b=final_ada_b, final_norm=final_norm)
    M = dict(ada_w=m_ada_w, ada_b=m_ada_b, norm_ffn1=m_norm_ffn1, ffn1_wg=m_ffn1_wg, ffn1_wu=m_ffn1_wu,
             ffn1_wd=m_ffn1_wd, norm_mix=m_norm_mix, w_in=m_w_in, conv_w=m_conv_w, conv_b=m_conv_b, dt_bias=m_dt_bias,
             a_log=m_a_log, d_skip=m_d_skip, ret_gn=m_ret_gn, ssm_norm=m_ssm_norm, w_out=m_w_out,
             norm_ffn2=m_norm_ffn2, ffn2_wg=m_ffn2_wg, ffn2_wu=m_ffn2_wu, ffn2_wd=m_ffn2_wd,
             final_ada_w=m_final_ada_w, final_ada_b=m_final_ada_b, final_norm=m_final_norm)
    V = dict(ada_w=v_ada_w, ada_b=v_ada_b, norm_ffn1=v_norm_ffn1, ffn1_wg=v_ffn1_wg, ffn1_wu=v_ffn1_wu,
             ffn1_wd=v_ffn1_wd, norm_mix=v_norm_mix, w_in=v_w_in, conv_w=v_conv_w, conv_b=v_conv_b, dt_bias=v_dt_bias,
             a_log=v_a_log, d_skip=v_d_skip, ret_gn=v_ret_gn, ssm_norm=v_ssm_norm, w_out=v_w_out,
             norm_ffn2=v_norm_ffn2, ffn2_wg=v_ffn2_wg, ffn2_wu=v_ffn2_wu, ffn2_wd=v_ffn2_wd,
             final_ada_w=v_final_ada_w, final_ada_b=v_final_ada_b, final_norm=v_final_norm)
    for n in HELD_TRANSPOSED:
        W[n], M[n], V[n] = (jnp.transpose(t[n], (0, 2, 1)) for t in (W, M, V))
    D = D_MODEL
    S = x.shape[1]
    ax, ay, ac = _place()
    me = 4 * ax + 2 * ay + ac
    chip = 2 * ax + ay
    h0 = x[0]

    c_all = _gather_small("gather_c", jnp.pad(c, ((0, 7), (0, 0))))[:, 0, :]
    cond_all = c_all * jax.nn.sigmoid(c_all)
    nmod = 3 * 3 * D // 4
    mod_part = jnp.concatenate(
        [_mm(f"mod_proj{l}", [(cond_all, ada_w[l])], "nn", F32, tm=8, tn=768, tk=D) for l in range(DEPTH)]
        + [_mm("mod_proj_final", [(cond_all, final_ada_w)], "nn", F32, tm=8, tn=512, tk=D),
           conv_w.reshape(DEPTH * SSM_CONV, -1)], axis=1)
    gathered = _gather_small("gather_mod", mod_part)[0::2]
    mine = lax.dynamic_index_in_dim(gathered, me, axis=1, keepdims=False)
    mods = [(jnp.reshape(mine[:, l * nmod:(l + 1) * nmod], (-1,)) + ada_b[l]).reshape(9, D) for l in range(DEPTH)]
    fmod = (jnp.reshape(mine[:, DEPTH * nmod:DEPTH * nmod + 2 * D // 4], (-1,)) + final_ada_b).reshape(2, D)
    conv_full = jnp.transpose(gathered[:, :, DEPTH * nmod + 2 * D // 4:], (1, 0, 2)).reshape(DEPTH, SSM_CONV, -1)

    shards = [W[n].astype(BF16) for n in BIG]

    def layer_weights(l, arrays):
        fw = {n: lax.dynamic_update_index_in_dim(arrays[i], shards[i][l], chip, 0) for i, n in enumerate(BIG)}
        fw["w_in"] = jnp.pad(jnp.transpose(fw["w_in"], (1, 0, 2)).reshape(D, IN_W), ((0, 0), (0, IN_WP - IN_W)))
        fw["w_out"] = fw["w_out"].reshape(MIX_W, D)
        return fw

    full = [layer_weights(0, _exchange_call(_ChipExchange(shards, core=0, layer=0, forward=True),
                                            "gather_weights_l0"))]
    n_first = len(BIG) // 2
    fetch_sides = (_ChipExchange(shards[:n_first], core=1, layer=1), _ChipExchange(shards[n_first:], core=1, layer=1))

    tables = _ret_tables(S)
    small_p = []
    for l in range(DEPTH):
        small_p.append(dict(
            norm_ffn1=norm_ffn1[l:l + 1], norm_mix=norm_mix[l:l + 1], norm_ffn2=norm_ffn2[l:l + 1],
            ret_gn=ret_gn[l:l + 1], ssm_norm=ssm_norm[l:l + 1], conv_w=conv_full[l], conv_b=conv_b[l:l + 1],
            dt_bias=_pad_lanes(dt_bias[l]), a_log=_pad_lanes(a_log[l]), d_skip=_pad_lanes(d_skip[l])))

    def norm_of(l, sub):
        gain = small_p[l][("norm_ffn1", "norm_mix", "norm_ffn2")[sub]]
        return gain, mods[l][3 * sub:3 * sub + 1], mods[l][3 * sub + 1:3 * sub + 2]

    h = h0
    u = _norm_mod("l0_ffn1_norm", h, *norm_of(0, 0))
    saved = []
    for l in range(DEPTH):
        P, fw, mod = small_p[l], full[l], mods[l]
        sides = fetch_sides if l == 0 else (None, None)
        h, u, s1, got_a = _ffn_fwd(f"l{l}_ffn1", h, u, mod[0:3], fw["ffn1_wg"], fw["ffn1_wu"], fw["ffn1_wd"],
                                   norm_of(l, 1), sides[0])
        h, u, sm, got_b = _mixer_fwd(f"l{l}_mix", h, u, P, mod[3:6], fw["w_in"], fw["w_out"], tables, norm_of(l, 2),
                                     sides[1])
        hand_over = _SiblingSend(list(got_a) + list(got_b), sender=1) if l == 0 else None
        h, u, s2, handed = _ffn_fwd(f"l{l}_ffn2", h, u, mod[6:9], fw["ffn2_wg"], fw["ffn2_wu"], fw["ffn2_wd"],
                                    norm_of(l + 1, 0) if l + 1 < DEPTH else None, hand_over)
        saved.append((s1, sm, s2))
        if l == 0:
            full.append(layer_weights(1, [jnp.where(ac == 1, f, hd) for f, hd in zip(hand_over.arrays, handed)]))
    fgain = final_norm.reshape(1, D)
    dy, loss_rows = _final_loss("final_loss", h, loss_target[0], fgain, fmod[0:1], fmod[1:2])

    below = lambda sv: (sv["out"], sv["gatefac"])
    g, s_final, d_out, s_gate = _norm_mod_bwd("final_norm_bwd", dy, h, fgain, fmod[1:2], *below(saved[-1][2]))
    dfmod = s_final[0:2]
    small_g = {n: [None] * DEPTH for n in SMALL}
    big_g = [None] * DEPTH
    dmods = [None] * DEPTH
    chip_sum = [None] * DEPTH
    g_lists = [None] * DEPTH
    for l in reversed(range(DEPTH)):
        P, fw, mod = small_p[l], full[l], mods[l]
        s1, sm, s2 = saved[l]
        hand_over = _SiblingSend(g_lists[1], sender=0) if l == 0 else None
        g, (d_out, s_gate), (dwg2, dwu2, dwd2), dm2, dn2, handed = _ffn_bwd(
            f"l{l}_ffn2", g, d_out, s_gate, s2, P["norm_ffn2"], mod[6:9], fw["ffn2_wg"], fw["ffn2_wu"], fw["ffn2_wd"],
            below(sm), hand_over)
        reduce_side = None
        if l == 0:
            chip_sum[1] = _add_lists("reduce_l1_add", g_lists[1], handed)
            reduce_side = _ChipExchange(chip_sum[1], core=1)
        g, (d_out, s_gate), dw_in, dw_out, dmm, sg, exchanged = _mixer_bwd(
            f"l{l}_mix", g, d_out, s_gate, sm, P, mod[3:6], fw["w_in"], fw["w_out"], tables, below(s1), reduce_side)
        g, nxt_start, (dwg1, dwu1, dwd1), dm1, dn1, _ = _ffn_bwd(
            f"l{l}_ffn1", g, d_out, s_gate, s1, P["norm_ffn1"], mod[0:3], fw["ffn1_wg"], fw["ffn1_wu"], fw["ffn1_wd"],
            below(saved[l - 1][2]) if l > 0 else None)
        if l > 0:
            d_out, s_gate = nxt_start
        dmods[l] = jnp.concatenate([dm1, dmm, dm2], axis=0)
        big_g[l] = dict(ffn1_wg=dwg1, ffn1_wu=dwu1, ffn1_wd=dwd1,
                        w_in=jnp.transpose(dw_in[:, :IN_W].reshape(D, 4, IN_W // 4), (1, 0, 2)),
                        w_out=dw_out.reshape(4, MIX_W // 4, D), ffn2_wg=dwg2, ffn2_wu=dwu2, ffn2_wd=dwd2)
        sg.update(norm_ffn1=dn1, norm_ffn2=dn2)
        for n, val in sg.items():
            small_g[n][l] = val
        g_lists[l] = [big_g[l][n] for n in BIG]
        if l == 0:
            handed = _exchange_call(_SiblingSend(g_lists[0], sender=1), "reduce_l0_sibling")
            chip_sum[0] = _add_lists("reduce_l0_add", g_lists[0], handed)
            from_chips = [_exchange_call(_ChipExchange(chip_sum[0], core=0), "reduce_l0_chips"), exchanged]
    grad_x = g[None]

    n_mod = DEPTH * 9 * D + 2 * D
    vec = [jnp.stack(dmods).reshape(-1), dfmod.reshape(-1)]
    layered = [n for n in SMALL if n not in ("ada_b", "final_ada_b", "final_norm")]
    vec += [jnp.stack(small_g[n]).reshape(-1) for n in layered]
    vec += [s_final[2], jnp.sum(loss_rows[0]).reshape(1)]
    flat = jnp.concatenate(vec)
    vrows = -(-flat.shape[0] // (8 * PACK_C)) * 8
    slab = jnp.pad(flat, (0, vrows * PACK_C - flat.shape[0])).reshape(vrows, PACK_C)
    slabs = _gather_small("gather_small_grads", slab)
    total = _sum_devices("sum_small_grads", slabs).reshape(-1)
    grads = {}
    grads["ada_b"] = total[:DEPTH * 9 * D].reshape(DEPTH, 9 * D)
    grads["final_ada_b"] = total[DEPTH * 9 * D:n_mod]
    off = n_mod
    for n in layered:
        shp = (DEPTH,) + ((SSM_CONV, D) if n == "conv_w" else W[n].shape[1:])
        cnt = math.prod(shp)
        grads[n] = total[off:off + cnt].reshape(shp)
        off += cnt
    grads["final_norm"] = total[off:off + D]
    loss = total[off + D]
    grads["conv_w"] = lax.dynamic_slice_in_dim(grads["conv_w"], chip * (D // 4), D // 4, axis=2)

    dmod_all = slabs[:, :n_mod // PACK_C, :].reshape(8, n_mod)
    grads["ada_w"] = jnp.stack([
        _mm(f"grad_ada_w{l}", [(cond_all, lax.dynamic_slice_in_dim(dmod_all, l * 9 * D + chip * nmod, nmod, axis=1))],
            "tn", F32, tm=D, tn=768, tk=8) for l in range(DEPTH)])
    grads["final_ada_w"] = _mm(
        "grad_final_ada_w",
        [(cond_all, lax.dynamic_slice_in_dim(dmod_all, DEPTH * 9 * D + chip * (2 * D // 4), 2 * D // 4, axis=1))],
        "tn", F32, tm=D, tn=512, tk=8)

    reduced = [_sum_chips(f"reduce_l{l}_sum", from_chips[l], chip_sum[l], chip) for l in range(DEPTH)]
    reduced_other = _sibling_swap("reduce_grads_complete", reduced[1], alt=reduced[0])

    delta, new_m, new_v = {}, {}, {}
    for i, n in enumerate(BIG):
        grads[n], delta[n], new_m[n], new_v[n] = _adamw_layers(
            "adamw_" + n, W[n], (reduced[0][i], reduced[1][i]), reduced_other[i], M[n], V[n], ac)
    delta["ada_w"], new_m["ada_w"], new_v["ada_w"] = _adamw("adamw_ada_w", ada_w, grads["ada_w"], m_ada_w, v_ada_w)
    outs = _adamw("adamw_final_ada_w", final_ada_w[None], grads["final_ada_w"][None], m_final_ada_w[None],
                  v_final_ada_w[None])
    delta["final_ada_w"], new_m["final_ada_w"], new_v["final_ada_w"] = [o[0] for o in outs]
    small_shapes = [W[n].shape for n in SMALL]
    n_small = sum(math.prod(s) for s in small_shapes)
    srows = -(-n_small // (8 * LANES)) * 8
    slab_of = lambda T_: jnp.pad(jnp.concatenate([T_[n].reshape(-1) for n in SMALL]),
                                 (0, srows * LANES - n_small)).reshape(1, srows, LANES)
    outs = _adamw("adamw_small", slab_of(W), slab_of(grads), slab_of(M), slab_of(V))
    for res, o in zip((delta, new_m, new_v), outs):
        for n, val in zip(SMALL, _unpack(o[0], small_shapes)):
            res[n] = val

    for n in HELD_TRANSPOSED:
        for res in (grads, delta, new_m, new_v):
            res[n] = jnp.transpose(res[n], (0, 2, 1))
    return (loss, grad_x, *[grads[n] for n in WEIGHTS], *[delta[n] for n in WEIGHTS],
            *[new_m[n] for n in WEIGHTS], *[new_v[n] for n in WEIGHTS])
```

```python
import functools
import math

import jax
import jax.numpy as jnp
from jax import lax
from jax.experimental import pallas as pl
from jax.experimental.pallas import tpu as pltpu

F32 = jnp.float32
BF16 = jnp.bfloat16
MESH = pl.DeviceIdType.MESH

D_MODEL = 1024
DEPTH = 2
D_FF = 2816
RET_HEADS = 4
SB_HEADS = 4
HEAD_DIM = 128
SSM_HEADS = 8
SSM_P = 64
SSM_N = 128
SSM_GROUPS = 2
SSM_CONV = 4
MIX_W = 1536
IN_W = 5128
IN_WP = 5376
DT_COL = 5120
ROPE_BASE = 10000.0
NORM_EPS = 1e-6
ADAM_LR = 0.001
ADAM_B1 = 0.9
ADAM_B2 = 0.999
ADAM_EPS = 1e-08
ADAM_WD = 0.01
ADAM_STEP = 10

LANES = 128
VMEM_LIMIT = 56 * 1024 * 1024
RET_T = 256
RET_HPS = 4
SB_T = 256
SB_HPS = 2
SSD_T = 256
CONV_T = 512
ROW_T = 512
SB_DEAD = -150.0

NN = (((1,), (0,)), ((), ()))
NT = (((1,), (1,)), ((), ()))
TN = (((0,), (0,)), ((), ()))


def _dot(a, b, dims=NN):
    return lax.dot_general(a.astype(BF16), b.astype(BF16), dims, preferred_element_type=F32)


def _params(sem):
    return pltpu.CompilerParams(dimension_semantics=sem, vmem_limit_bytes=VMEM_LIMIT)


def _sigmoid(x):
    return 1.0 / (1.0 + jnp.exp(-x))


def _split_dot(mat01, x, dims=NN, lhs01=True, pieces=3):
    m = mat01.astype(BF16)
    total, rest = None, x
    for _ in range(pieces):
        p = rest.astype(BF16)
        rest = rest - p.astype(F32)
        d = lax.dot_general(m, p, dims, preferred_element_type=F32) if lhs01 else lax.dot_general(
            p, m, dims, preferred_element_type=F32)
        total = d if total is None else total + d
    return total


def _mm(name, terms, mode, out_dtype, tm=512, tn=512, tk=1024, n_outer=False):
    a0, b0 = terms[0]
    if mode == "nn":
        (M, K), N = a0.shape, b0.shape[1]
    elif mode == "nt":
        (M, K), N = a0.shape, b0.shape[0]
    else:
        (K, M), N = a0.shape, b0.shape[1]
    tm, tn, tk = min(tm, M), min(tn, N), min(tk, K)
    assert M % tm == 0 and N % tn == 0 and K % tk == 0, (name, M, N, K, tm, tn, tk)
    nk = K // tk
    nterm = len(terms)
    dims = {"nn": NN, "nt": NT, "tn": TN}[mode]

    def body(*refs):
        o_ref, acc = refs[2 * nterm], refs[2 * nterm + 1]
        part = None
        for t in range(nterm):
            p = _dot(refs[2 * t][...], refs[2 * t + 1][...], dims)
            part = p if part is None else part + p
        if nk == 1:
            o_ref[...] = part.astype(o_ref.dtype)
        else:
            k = pl.program_id(2)

            @pl.when(k == 0)
            def _():
                acc[...] = part

            @pl.when(k > 0)
            def _():
                acc[...] += part

            @pl.when(k == nk - 1)
            def _():
                o_ref[...] = acc[...].astype(o_ref.dtype)

    ij = (lambda g0, g1: (g1, g0)) if n_outer else (lambda g0, g1: (g0, g1))
    if mode == "nn":
        a_spec = pl.BlockSpec((tm, tk), lambda g0, g1, k: (ij(g0, g1)[0], k))
        b_spec = pl.BlockSpec((tk, tn), lambda g0, g1, k: (k, ij(g0, g1)[1]))
    elif mode == "nt":
        a_spec = pl.BlockSpec((tm, tk), lambda g0, g1, k: (ij(g0, g1)[0], k))
        b_spec = pl.BlockSpec((tn, tk), lambda g0, g1, k: (ij(g0, g1)[1], k))
    else:
        a_spec = pl.BlockSpec((tk, tm), lambda g0, g1, k: (k, ij(g0, g1)[0]))
        b_spec = pl.BlockSpec((tk, tn), lambda g0, g1, k: (k, ij(g0, g1)[1]))
    flat = [r for ab in terms for r in ab]
    return pl.pallas_call(
        body, name=name, grid=(N // tn, M // tm, nk) if n_outer else (M // tm, N // tn, nk),
        in_specs=[a_spec, b_spec] * nterm,
        out_specs=pl.BlockSpec((tm, tn), lambda g0, g1, k: ij(g0, g1)),
        out_shape=jax.ShapeDtypeStruct((M, N), out_dtype),
        scratch_shapes=[pltpu.VMEM((tm, tn) if nk > 1 else (8, LANES), F32)],
        compiler_params=_params(("parallel", "parallel", "arbitrary")),
    )(*flat)


def _norm_bwd_rows(d, x, gain, scale, sums_ref):
    r = lax.rsqrt(jnp.mean(x * x, axis=-1, keepdims=True) + NORM_EPS)
    xh = x * r
    dn = d * (1.0 + scale)
    sums_ref[0:1, :] += jnp.sum(d, axis=0, keepdims=True)
    sums_ref[1:2, :] += jnp.sum(d * xh * gain, axis=0, keepdims=True)
    sums_ref[2:3, :] += jnp.sum(dn * xh, axis=0, keepdims=True)
    dxh = dn * gain
    return r * (dxh - xh * jnp.mean(dxh * xh, axis=-1, keepdims=True))


def _rowmm(name, terms, mode, out_dtype=F32, res=None, norm_bwd=None, side=None):
    S = terms[0][0].shape[-2]
    N = terms[0][2].shape[-1] if mode == "nn" else terms[0][2].shape[-2]
    nterm = len(terms)
    tm = ROW_T
    dims = NN if mode == "nn" else NT
    nxt = None if (res or norm_bwd) is None else (res or norm_bwd)[-1]

    def body(*refs):
        o = None
        for t in range(nterm):
            p = _dot(refs[2 * t][...], refs[2 * t + 1][...], dims)
            o = p if o is None else o + p
        rest = refs[2 * nterm:]
        if norm_bwd is not None:
            h_ref, g_ref, sc_ref, gr_ref = rest[:4]
            dh_ref, sums_ref = rest[4 + (2 if nxt else 0):][:2]

            @pl.when(pl.program_id(0) == 0)
            def _():
                sums_ref[...] = jnp.zeros_like(sums_ref)

            g = gr_ref[...] + _norm_bwd_rows(o, h_ref[...], g_ref[...], sc_ref[...], sums_ref)
            dh_ref[...] = g
            if nxt:
                _gate_rows(g, rest[4], rest[5], rest[8], rest[9])
        elif res is None:
            rest[0][...] = o.astype(out_dtype)
        else:
            h_ref, gf_ref = rest[:2]
            out_ref, hn_ref = rest[2 + (3 if nxt else 0):][:2]
            out_ref[...] = o.astype(out_ref.dtype)
            hn = h_ref[...] + gf_ref[...] * o
            hn_ref[...] = hn
            if nxt:
                rest[7][...] = _norm_rows(hn, rest[2][...], rest[3][...], rest[4][...]).astype(BF16)

    in_specs, flat = [], []
    for a, ai, w, wi in terms:
        if ai is None:
            in_specs.append(pl.BlockSpec((tm, a.shape[1]), lambda i: (i, 0)))
        else:
            in_specs.append(pl.BlockSpec((None, tm, a.shape[2]), lambda i, g=ai: (g, i, 0)))
        if wi is None:
            in_specs.append(pl.BlockSpec(w.shape, lambda i: (0, 0)))
        else:
            in_specs.append(pl.BlockSpec((None,) + w.shape[1:], lambda i, g=wi: (g, 0, 0)))
        flat += [a, w]
    row = pl.BlockSpec((tm, N), lambda i: (i, 0))
    vec = pl.BlockSpec((1, N), lambda i: (0, 0))
    sums = pl.BlockSpec((8, N), lambda i: (0, 0))
    rows_f32, rows_bf16 = jax.ShapeDtypeStruct((S, N), F32), jax.ShapeDtypeStruct((S, N), BF16)
    sums_f32 = jax.ShapeDtypeStruct((8, N), F32)
    if norm_bwd is not None:
        extra = list(norm_bwd[:4]) + (list(nxt) if nxt else [])
        outs, side_outs = _host_call(
            body, side, name, (S // tm,), in_specs + [row, vec, vec, row] + ([row, vec] if nxt else []),
            [row, sums] + ([row, sums] if nxt else []),
            [rows_f32, sums_f32] + ([rows_bf16, sums_f32] if nxt else []), [], (*flat, *extra))
        return list(outs) if side is None else (list(outs), side_outs)
    if res is None:
        return pl.pallas_call(
            body, name=name, grid=(S // tm,), in_specs=in_specs, out_specs=row,
            out_shape=jax.ShapeDtypeStruct((S, N), out_dtype), compiler_params=_params(("parallel",)),
        )(*flat)
    extra = list(res[:2]) + (list(nxt) if nxt else [])
    return pl.pallas_call(
        body, name=name, grid=(S // tm,),
        in_specs=in_specs + [row, vec] + ([vec, vec, vec] if nxt else []),
        out_specs=[row, row] + ([row] if nxt else []),
        out_shape=[rows_bf16, rows_f32] + ([rows_bf16] if nxt else []),
        compiler_params=_params(("parallel",)),
    )(*flat, *extra)


def _mm_tn_groups(name, a, b, out_dtype, tk=2048):
    G = a.shape[0] if a.ndim == 3 else b.shape[0]
    S, M, N = a.shape[-2], a.shape[-1], b.shape[-1]
    tk = min(tk, S)
    nk = S // tk

    def body(a_ref, b_ref, o_ref, acc):
        k = pl.program_id(1)
        part = _dot(a_ref[...], b_ref[...], TN)

        @pl.when(k == 0)
        def _():
            acc[...] = part

        @pl.when(k > 0)
        def _():
            acc[...] += part

        @pl.when(k == nk - 1)
        def _():
            o_ref[...] = acc[...].astype(o_ref.dtype)

    def spec(arr, width):
        if arr.ndim == 3:
            return pl.BlockSpec((None, tk, width), lambda g, k: (g, k, 0))
        return pl.BlockSpec((tk, width), lambda g, k: (k, 0))

    return pl.pallas_call(
        body, name=name, grid=(G, nk), in_specs=[spec(a, M), spec(b, N)],
        out_specs=pl.BlockSpec((None, M, N), lambda g, k: (g, 0, 0)),
        out_shape=jax.ShapeDtypeStruct((G, M, N), out_dtype),
        scratch_shapes=[pltpu.VMEM((M, N), F32)],
        compiler_params=_params(("parallel", "arbitrary")),
    )(a, b)


def _norm_rows(x, gain, shift, scale):
    r = lax.rsqrt(jnp.mean(x * x, axis=-1, keepdims=True) + NORM_EPS)
    return x * r * gain * (1.0 + scale) + shift


def _gate_rows(g, out_ref, gf_ref, d_ref, gsum_ref):
    @pl.when(pl.program_id(0) == 0)
    def _():
        gsum_ref[...] = jnp.zeros_like(gsum_ref)

    d_ref[...] = (g * gf_ref[...]).astype(d_ref.dtype)
    gsum_ref[0:1, :] += jnp.sum(g * out_ref[...].astype(F32), axis=0, keepdims=True)


def _norm_mod(name, h, gain, shift, scale):
    S, D = h.shape
    tm = ROW_T

    def body(h_ref, g_ref, sh_ref, sc_ref, u_ref):
        u_ref[...] = _norm_rows(h_ref[...], g_ref[...], sh_ref[...], sc_ref[...]).astype(u_ref.dtype)

    vec = pl.BlockSpec((1, D), lambda i: (0, 0))
    return pl.pallas_call(
        body, name=name, grid=(S // tm,),
        in_specs=[pl.BlockSpec((tm, D), lambda i: (i, 0)), vec, vec, vec],
        out_specs=pl.BlockSpec((tm, D), lambda i: (i, 0)),
        out_shape=jax.ShapeDtypeStruct((S, D), BF16),
        compiler_params=_params(("parallel",)),
    )(h, gain, shift, scale)


def _norm_mod_bwd(name, du, h, gain, scale, out, gatefac):
    S, D = h.shape
    tm = ROW_T

    def body(du_ref, h_ref, g_ref, sc_ref, o_ref, gf_ref, dh_ref, sums_ref, d_ref, gsum_ref):
        @pl.when(pl.program_id(0) == 0)
        def _():
            sums_ref[...] = jnp.zeros_like(sums_ref)

        g = _norm_bwd_rows(du_ref[...], h_ref[...], g_ref[...], sc_ref[...], sums_ref)
        dh_ref[...] = g
        _gate_rows(g, o_ref, gf_ref, d_ref, gsum_ref)

    row = pl.BlockSpec((tm, D), lambda i: (i, 0))
    vec = pl.BlockSpec((1, D), lambda i: (0, 0))
    sums = pl.BlockSpec((8, D), lambda i: (0, 0))
    return pl.pallas_call(
        body, name=name, grid=(S // tm,),
        in_specs=[row, row, vec, vec, row, vec],
        out_specs=[row, sums, row, sums],
        out_shape=[jax.ShapeDtypeStruct((S, D), F32), jax.ShapeDtypeStruct((8, D), F32),
                   jax.ShapeDtypeStruct((S, D), BF16), jax.ShapeDtypeStruct((8, D), F32)],
        compiler_params=_params(("arbitrary",)),
    )(du, h, gain, scale, out, gatefac)


def _final_loss(name, h, target, gain, shift, scale):
    S, D = h.shape
    tm = ROW_T

    def body(h_ref, t_ref, g_ref, sh_ref, sc_ref, dy_ref, loss_ref):
        @pl.when(pl.program_id(0) == 0)
        def _():
            loss_ref[...] = jnp.zeros_like(loss_ref)

        x = h_ref[...]
        r = lax.rsqrt(jnp.mean(x * x, axis=-1, keepdims=True) + NORM_EPS)
        y = x * r * g_ref[...] * (1.0 + sc_ref[...]) + sh_ref[...]
        e = y - t_ref[...]
        dy_ref[...] = e * (1.0 / D)
        loss_ref[0:1, :] += 0.5 * jnp.sum(e * e, axis=0, keepdims=True) * (1.0 / D)

    row = pl.BlockSpec((tm, D), lambda i: (i, 0))
    vec = pl.BlockSpec((1, D), lambda i: (0, 0))
    return pl.pallas_call(
        body, name=name, grid=(S // tm,),
        in_specs=[row, row, vec, vec, vec],
        out_specs=[row, pl.BlockSpec((8, D), lambda i: (0, 0))],
        out_shape=[jax.ShapeDtypeStruct((S, D), F32), jax.ShapeDtypeStruct((8, D), F32)],
        compiler_params=_params(("arbitrary",)),
    )(h, target, gain, shift, scale)


def _ffn_up(name, u, wg, wu, side=None):
    S, D = u.shape
    G, Fg, _ = wg.shape
    tm = ROW_T

    def body(u_ref, wg_ref, wu_ref, a_ref, b_ref, hm_ref):
        uv = u_ref[...]
        a = _dot(uv, wg_ref[...], NT)
        b = _dot(uv, wu_ref[...], NT)
        sg = _sigmoid(a)
        act = a * sg
        a_ref[...] = (b * sg * (1.0 + a * (1.0 - sg))).astype(a_ref.dtype)
        b_ref[...] = act.astype(b_ref.dtype)
        hm_ref[...] = (act * b).astype(hm_ref.dtype)

    w_spec = pl.BlockSpec((None, Fg, D), lambda g, i: (g, 0, 0))
    o_spec = pl.BlockSpec((None, tm, Fg), lambda g, i: (g, i, 0))
    return _host_call(
        body, side, name, (G, S // tm), [pl.BlockSpec((tm, D), lambda g, i: (i, 0)), w_spec, w_spec], [o_spec] * 3,
        [jax.ShapeDtypeStruct((G, S, Fg), BF16)] * 3, [], (u, wg, wu))


def _ffn_bwd_mid(name, d_out, wd, a, b, side=None):
    S, D = d_out.shape
    G, Fg, _ = wd.shape
    tm = ROW_T

    def body(d_ref, wd_ref, a_ref, b_ref, da_ref, db_ref):
        dhm = _dot(d_ref[...], wd_ref[...], NT)
        da_ref[...] = (dhm * a_ref[...].astype(F32)).astype(da_ref.dtype)
        db_ref[...] = (dhm * b_ref[...].astype(F32)).astype(db_ref.dtype)

    t_spec = pl.BlockSpec((None, tm, Fg), lambda g, i: (g, i, 0))
    return _host_call(
        body, side, name, (G, S // tm),
        [pl.BlockSpec((tm, D), lambda g, i: (i, 0)), pl.BlockSpec((None, Fg, D), lambda g, i: (g, 0, 0)),
         t_spec, t_spec],
        [t_spec] * 2, [jax.ShapeDtypeStruct((G, S, Fg), BF16)] * 2, [], (d_out, wd, a, b))


def _ret_tables(S):
    T = RET_T
    half = HEAD_DIM // 2
    inv_freq = ROPE_BASE ** (-jnp.arange(half, dtype=F32) / half)
    ang = jnp.arange(S, dtype=F32)[:, None] * inv_freq[None, :]
    cos, sin = jnp.cos(ang), jnp.sin(ang)
    cosf = jnp.concatenate([cos, cos], axis=-1)
    sinf = jnp.concatenate([-sin, sin], axis=-1)
    log_gamma = jnp.log1p(-(2.0 ** (-5.0 - jnp.arange(RET_HEADS, dtype=F32))))
    idx = jnp.arange(T, dtype=F32)
    chunk = jnp.arange(T) // 64
    vis = (chunk[None, :] <= chunk[:, None]).astype(F32)
    mask = jnp.exp(log_gamma[:, None, None] * jnp.abs(idx[:, None] - idx[None, :])) * vis[None]
    ones = jnp.ones((1, 1, LANES), F32)
    qdec = jnp.exp(log_gamma[:, None] * (idx + 1.0)[None, :])[:, :, None] * ones
    kdec = jnp.exp(log_gamma[:, None] * (T - 1.0 - idx)[None, :])[:, :, None] * ones
    cdec = jnp.exp(log_gamma * T)[:, None, None] * jnp.ones((1, 8, LANES), F32)
    return cosf, sinf, mask, qdec, kdec, cdec


def _rope(x, cosf, sinf):
    return x * cosf + pltpu.roll(x, HEAD_DIM // 2, 1) * sinf


def _rope_bwd(d, cosf, sinf):
    return d * cosf + pltpu.roll(d * sinf, HEAD_DIM // 2, 1)


def _ret_specs(T, rev_nb=None):
    if rev_nb is None:
        blk = lambda b: b
    else:
        blk = lambda b: rev_nb - 1 - b
    E, W = RET_HPS, RET_HPS * HEAD_DIM
    proj = lambda off: pl.BlockSpec((T, W), lambda hp, b: (blk(b), off * (RET_HEADS // E) + hp))
    rows = pl.BlockSpec((T, HEAD_DIM), lambda hp, b: (blk(b), 0))
    per_head = lambda shape: pl.BlockSpec((E,) + shape, lambda hp, b: (hp, 0, 0))
    return ([proj(0), proj(1), proj(2), proj(3), rows, rows,
             per_head((T, T)), per_head((T, LANES)), per_head((T, LANES)), per_head((8, LANES)),
             pl.BlockSpec((1, W), lambda hp, b: (0, hp))], blk)


def _ret_fwd(name, proj, gn, tables):
    S = proj.shape[0]
    T, E = RET_T, RET_HPS
    nb = S // T
    scale = HEAD_DIM ** -0.5
    specs, _ = _ret_specs(T)

    def body(q_ref, k_ref, v_ref, g_ref, cos_ref, sin_ref, m_ref, qd_ref, kd_ref, cd_ref, gn_ref, y_ref, st_ref, st):
        @pl.when(pl.program_id(1) == 0)
        def _():
            st[...] = jnp.zeros_like(st)

        cosf, sinf = cos_ref[...], sin_ref[...]
        for e in range(E):
            qr = _rope(q_ref[:, _lanes(e)], cosf, sinf)
            kr = _rope(k_ref[:, _lanes(e)], cosf, sinf) * scale
            v = v_ref[:, _lanes(e)]
            sp = st[e]
            st_ref[e, 0] = sp
            w = _dot(qr, kr, NT) * m_ref[e]
            y = _dot(w, v) + _dot(qr * qd_ref[e], sp)
            st[e] = cd_ref[e, 0:1, :] * sp + _dot(kr * kd_ref[e], v, TN)
            r = lax.rsqrt(jnp.mean(y * y, axis=-1, keepdims=True) + NORM_EPS)
            g = g_ref[:, _lanes(e)]
            y_ref[:, _lanes(e)] = y * r * gn_ref[:, _lanes(e)] * (g * _sigmoid(g))

    return pl.pallas_call(
        body, name=name, grid=(RET_HEADS // E, nb), in_specs=specs,
        out_specs=[pl.BlockSpec((T, E * HEAD_DIM), lambda hp, b: (b, hp)),
                   pl.BlockSpec((E, 1, HEAD_DIM, HEAD_DIM), lambda hp, b: (hp, b, 0, 0))],
        out_shape=[jax.ShapeDtypeStruct((S, RET_HEADS * HEAD_DIM), F32),
                   jax.ShapeDtypeStruct((RET_HEADS, nb, HEAD_DIM, HEAD_DIM), F32)],
        scratch_shapes=[pltpu.VMEM((E, HEAD_DIM, HEAD_DIM), F32)],
        compiler_params=_params(("parallel", "arbitrary")),
    )(proj, proj, proj, proj, tables[0], tables[1], tables[2], tables[3], tables[4], tables[5], gn)


def _ret_bwd(name, proj, gn, tables, states, dmix):
    S = proj.shape[0]
    T, E = RET_T, RET_HPS
    W = E * HEAD_DIM
    nb = S // T
    scale = HEAD_DIM ** -0.5
    specs, blk = _ret_specs(T, rev_nb=nb)
    specs = specs + [pl.BlockSpec((E, 1, HEAD_DIM, HEAD_DIM), lambda hp, b: (hp, blk(b), 0, 0)),
                     pl.BlockSpec((T, W), lambda hp, b: (blk(b), hp))]

    def body(q_ref, k_ref, v_ref, g_ref, cos_ref, sin_ref, m_ref, qd_ref, kd_ref, cd_ref, gn_ref, st_ref, do_ref,
             d_ref, dgn_ref, dst):
        @pl.when(pl.program_id(1) == 0)
        def _():
            dst[...] = jnp.zeros_like(dst)
            dgn_ref[...] = jnp.zeros_like(dgn_ref)

        cosf, sinf = cos_ref[...], sin_ref[...]
        for e in range(E):
            qr = _rope(q_ref[:, _lanes(e)], cosf, sinf)
            kr = _rope(k_ref[:, _lanes(e)], cosf, sinf) * scale
            v = v_ref[:, _lanes(e)]
            sp = st_ref[e, 0]
            mask, qd, kd = m_ref[e], qd_ref[e], kd_ref[e]
            w = _dot(qr, kr, NT) * mask
            y = _dot(w, v) + _dot(qr * qd, sp)
            r = lax.rsqrt(jnp.mean(y * y, axis=-1, keepdims=True) + NORM_EPS)
            yh = y * r
            gn_v = gn_ref[:, _lanes(e)]
            g = g_ref[:, _lanes(e)]
            sg = _sigmoid(g)
            do = do_ref[:, _lanes(e)]
            dyn = do * g * sg
            dgn_ref[:, _lanes(e)] += jnp.sum(dyn * yh, axis=0, keepdims=True)
            dyh = dyn * gn_v
            dy = r * (dyh - yh * jnp.mean(dyh * yh, axis=-1, keepdims=True))
            dg = do * yh * gn_v * sg * (1.0 + g * (1.0 - sg))
            ds = dst[e]
            dp = _dot(dy, v, NT) * mask
            dqr = _dot(dp, kr) + _dot(dy, sp, NT) * qd
            dkr = _dot(dp, qr, TN) + _dot(v, ds, NT) * kd
            dv = _dot(w, dy, TN) + _dot(kr * kd, ds)
            dst[e] = cd_ref[e, 0:1, :] * ds + _dot(qr * qd, dy, TN)
            d_ref[0, :, _lanes(e)] = _rope_bwd(dqr, cosf, sinf).astype(d_ref.dtype)
            d_ref[1, :, _lanes(e)] = _rope_bwd(dkr * scale, cosf, sinf).astype(d_ref.dtype)
            d_ref[2, :, _lanes(e)] = dv.astype(d_ref.dtype)
            d_ref[3, :, _lanes(e)] = dg.astype(d_ref.dtype)

    return pl.pallas_call(
        body, name=name, grid=(RET_HEADS // E, nb), in_specs=specs,
        out_specs=[pl.BlockSpec((4, T, W), lambda hp, b: (0, blk(b), hp)),
                   pl.BlockSpec((1, W), lambda hp, b: (0, hp))],
        out_shape=[jax.ShapeDtypeStruct((4, S, RET_HEADS * HEAD_DIM), BF16),
                   jax.ShapeDtypeStruct((1, RET_HEADS * HEAD_DIM), F32)],
        scratch_shapes=[pltpu.VMEM((E, HEAD_DIM, HEAD_DIM), F32)],
        compiler_params=_params(("parallel", "arbitrary")),
    )(proj, proj, proj, proj, tables[0], tables[1], tables[2], tables[3], tables[4], tables[5], gn, states, dmix)


def _sb_logits(qb, kb, i, j, scale):
    T = SB_T
    z = lax.dot_general(qb, kb, NT, preferred_element_type=F32) * scale
    row = lax.broadcasted_iota(jnp.int32, (T, T), 0)
    col = lax.broadcasted_iota(jnp.int32, (T, T), 1)
    vis = jnp.logical_or(j < i, col < row)
    lp = jnp.log1p(jnp.exp(-jnp.abs(z)))
    lb = jnp.minimum(z, 0.0) - lp
    lk = jnp.where(vis, -jnp.maximum(z, 0.0) - lp, 0.0)
    return lb, lk, vis


def _sb_weights(lb, lk, vis, tailc):
    T = SB_T
    row = lax.broadcasted_iota(jnp.int32, (T, T), 0)
    col = lax.broadcasted_iota(jnp.int32, (T, T), 1)
    tail = tailc + _split_dot(row > col, lk, lhs01=False, pieces=2)
    return jnp.where(vis, jnp.exp(lb + tail), 0.0)


def _sb_specs(S):
    T, W = SB_T, SB_HPS * HEAD_DIM
    return [pl.BlockSpec((T, W), lambda hp, i: (i, 2048 // W + hp)),
            pl.BlockSpec((S, W), lambda hp, i: (0, hp)),
            pl.BlockSpec((S, W), lambda hp, i: (0, 512 // W + hp))]


def _lanes(e):
    return slice(e * HEAD_DIM, (e + 1) * HEAD_DIM)


def _sb_fwd(name, proj, kv, side=None):
    S = proj.shape[0]
    T, E = SB_T, SB_HPS
    nq = S // T
    scale = HEAD_DIM ** -0.5

    def body(q_ref, k_ref, v_ref, y_ref):
        i = pl.program_id(1)
        qs = [q_ref[:, _lanes(e)].astype(BF16) for e in range(E)]

        def cond(c):
            return jnp.logical_and(c[0] >= 0, c[1] == 0)

        def step(c):
            j, _, tails, accs = c
            rows = pl.ds(pl.multiple_of(j * T, T), T)
            new_tails, new_accs, worst = [], [], None
            for e in range(E):
                lb, lk, vis = _sb_logits(qs[e], k_ref[rows, _lanes(e)], i, j, scale)
                w = _sb_weights(lb, lk, vis, tails[e])
                new_accs.append(accs[e] + lax.dot_general(w.astype(BF16), v_ref[rows, _lanes(e)], NN,
                                                          preferred_element_type=F32))
                t = tails[e] + jnp.sum(lk, axis=1, keepdims=True)
                new_tails.append(t)
                worst = jnp.max(t) if worst is None else jnp.maximum(worst, jnp.max(t))
            return j - 1, (worst < SB_DEAD).astype(jnp.int32), tuple(new_tails), tuple(new_accs)

        init = (i, jnp.int32(0), (jnp.zeros((T, 1), F32),) * E, (jnp.zeros((T, HEAD_DIM), F32),) * E)
        accs = lax.while_loop(cond, step, init)[3]
        for e in range(E):
            y_ref[:, _lanes(e)] = accs[e]

    (y,), fetched = _host_call(
        body, side, name, (SB_HEADS // E, nq), _sb_specs(S), [pl.BlockSpec((T, E * HEAD_DIM), lambda hp, i: (i, hp))],
        [jax.ShapeDtypeStruct((S, SB_HEADS * HEAD_DIM), F32)], [], (proj, kv, kv))
    return y, fetched


def _sb_bwd(name, proj, kv, dmix):
    S = proj.shape[0]
    T, E = SB_T, SB_HPS
    nq = S // T
    scale = HEAD_DIM ** -0.5

    def body(q_ref, k_ref, v_ref, do_ref, dq_ref, dk_ref, dv_ref):
        i = pl.program_id(1)

        @pl.when(i == 0)
        def _():
            dk_ref[...] = jnp.zeros_like(dk_ref)
            dv_ref[...] = jnp.zeros_like(dv_ref)

        qs = [q_ref[:, _lanes(e)].astype(BF16) for e in range(E)]
        dos = [do_ref[:, _lanes(e)].astype(BF16) for e in range(E)]
        row = lax.broadcasted_iota(jnp.int32, (T, T), 0)
        col = lax.broadcasted_iota(jnp.int32, (T, T), 1)
        zcol = (jnp.zeros((T, 1), F32),) * E

        def cond(c):
            return jnp.logical_and(c[0] >= 0, c[1] == 0)

        def walk_left(c):
            j, _, tails = c
            rows = pl.ds(pl.multiple_of(j * T, T), T)
            new_tails, worst = [], None
            for e in range(E):
                _, lk, _ = _sb_logits(qs[e], k_ref[rows, _lanes(e)], i, j, scale)
                t = tails[e] + jnp.sum(lk, axis=1, keepdims=True)
                new_tails.append(t)
                worst = jnp.max(t) if worst is None else jnp.maximum(worst, jnp.max(t))
            return j - 1, (worst < SB_DEAD).astype(jnp.int32), tuple(new_tails)

        j_end, _, totals = lax.while_loop(cond, walk_left, (i, jnp.int32(0), zcol))

        def walk_back(j, c):
            lefts, used, dqs = c
            rows = pl.ds(pl.multiple_of(j * T, T), T)
            new_lefts, new_used, new_dqs = [], [], []
            for e in range(E):
                kb = k_ref[rows, _lanes(e)]
                lb, lk, vis = _sb_logits(qs[e], kb, i, j, scale)
                u = used[e] + jnp.sum(lk, axis=1, keepdims=True)
                w = _sb_weights(lb, lk, vis, totals[e] - u)
                de = lax.dot_general(dos[e], v_ref[rows, _lanes(e)], NT, preferred_element_type=F32) * w
                dlk = jnp.where(vis, lefts[e] + _split_dot(row < col, de, lhs01=False, pieces=2), 0.0)
                sg = jnp.exp(lb)
                dz = ((de * (1.0 - sg) - dlk * sg) * scale).astype(BF16)
                new_dqs.append(dqs[e] + lax.dot_general(dz, kb, NN, preferred_element_type=F32))
                dk_ref[rows, _lanes(e)] += lax.dot_general(dz, qs[e], TN, preferred_element_type=F32)
                dv_ref[rows, _lanes(e)] += lax.dot_general(w.astype(BF16), dos[e], TN, preferred_element_type=F32)
                new_lefts.append(lefts[e] + jnp.sum(de, axis=1, keepdims=True))
                new_used.append(u)
            return tuple(new_lefts), tuple(new_used), tuple(new_dqs)

        init = (zcol, zcol, (jnp.zeros((T, HEAD_DIM), F32),) * E)
        dqs = lax.fori_loop(j_end + 1, i + 1, walk_back, init)[2]
        for e in range(E):
            dq_ref[:, _lanes(e)] = dqs[e].astype(dq_ref.dtype)

    W = E * HEAD_DIM
    blk = pl.BlockSpec((T, W), lambda hp, i: (i, hp))
    full = pl.BlockSpec((S, W), lambda hp, i: (0, hp))
    shp = jax.ShapeDtypeStruct((S, SB_HEADS * HEAD_DIM), F32)
    return pl.pallas_call(
        body, name=name, grid=(SB_HEADS // E, nq),
        in_specs=_sb_specs(S) + [pl.BlockSpec((T, W), lambda hp, i: (i, 512 // W + hp))],
        out_specs=[blk, full, full],
        out_shape=[jax.ShapeDtypeStruct((S, SB_HEADS * HEAD_DIM), BF16), shp, shp],
        compiler_params=_params(("parallel", "arbitrary")),
    )(proj, kv, kv, dmix)


def _conv_fwd(name, proj, conv_w, conv_b):
    S = proj.shape[0]
    T = CONV_T
    C = 1024
    K = SSM_CONV

    def body(x_ref, w_ref, b_ref, o_ref, buf):
        @pl.when(pl.program_id(0) == 0)
        def _():
            buf[0:8, :] = jnp.zeros((8, C), F32)

        buf[8:T + 8, :] = x_ref[...]
        acc = b_ref[...] + w_ref[K - 1:K, :] * buf[8:T + 8, :]
        for k in range(K - 1):
            acc = acc + w_ref[k:k + 1, :] * buf[5 + k:5 + k + T, :]
        o_ref[...] = acc * _sigmoid(acc)
        buf[0:8, :] = buf[T:T + 8, :]

    return pl.pallas_call(
        body, name=name, grid=(S // T,),
        in_specs=[pl.BlockSpec((T, C), lambda i: (i, 4)), pl.BlockSpec((K, C), lambda i: (0, 0)),
                  pl.BlockSpec((1, C), lambda i: (0, 0))],
        out_specs=pl.BlockSpec((T, C), lambda i: (i, 0)),
        out_shape=jax.ShapeDtypeStruct((S, C), F32),
        scratch_shapes=[pltpu.VMEM((T + 8, C), F32)],
        compiler_params=_params(("arbitrary",)),
    )(proj, conv_w, conv_b)


def _conv_bwd_act(name, proj, conv_w, conv_b, dxbc):
    S = proj.shape[0]
    T = CONV_T
    C = 1024
    K = SSM_CONV

    def body(x_ref, w_ref, b_ref, d_ref, dp_ref, sums_ref, buf):
        @pl.when(pl.program_id(0) == 0)
        def _():
            buf[0:8, :] = jnp.zeros((8, C), F32)
            sums_ref[...] = jnp.zeros_like(sums_ref)

        buf[8:T + 8, :] = x_ref[...]
        acc = b_ref[...] + w_ref[K - 1:K, :] * buf[8:T + 8, :]
        for k in range(K - 1):
            acc = acc + w_ref[k:k + 1, :] * buf[5 + k:5 + k + T, :]
        sg = _sigmoid(acc)
        dp = d_ref[...] * sg * (1.0 + acc * (1.0 - sg))
        dp_ref[...] = dp
        for k in range(K):
            sums_ref[k:k + 1, :] += jnp.sum(dp * buf[5 + k:5 + k + T, :], axis=0, keepdims=True)
        sums_ref[4:5, :] += jnp.sum(dp, axis=0, keepdims=True)
        buf[0:8, :] = buf[T:T + 8, :]

    row = pl.BlockSpec((T, C), lambda i: (i, 0))
    return pl.pallas_call(
        body, name=name, grid=(S // T,),
        in_specs=[pl.BlockSpec((T, C), lambda i: (i, 4)), pl.BlockSpec((K, C), lambda i: (0, 0)),
                  pl.BlockSpec((1, C), lambda i: (0, 0)), row],
        out_specs=[row, pl.BlockSpec((8, C), lambda i: (0, 0))],
        out_shape=[jax.ShapeDtypeStruct((S, C), F32), jax.ShapeDtypeStruct((8, C), F32)],
        scratch_shapes=[pltpu.VMEM((T + 8, C), F32)],
        compiler_params=_params(("arbitrary",)),
    )(proj, conv_w, conv_b, dxbc)


def _conv_bwd_in(name, dp, conv_w):
    S, C = dp.shape
    T = CONV_T
    K = SSM_CONV
    nb = S // T

    def body(d_ref, w_ref, o_ref, buf):
        @pl.when(pl.program_id(0) == 0)
        def _():
            buf[T:T + 8, :] = jnp.zeros((8, C), F32)

        buf[0:T, :] = d_ref[...]
        acc = w_ref[K - 1:K, :] * buf[0:T, :]
        for k in range(K - 1):
            acc = acc + w_ref[k:k + 1, :] * buf[3 - k:3 - k + T, :]
        o_ref[...] = acc.astype(o_ref.dtype)
        buf[T:T + 8, :] = buf[0:8, :]

    row = pl.BlockSpec((T, C), lambda i: (nb - 1 - i, 0))
    return pl.pallas_call(
        body, name=name, grid=(nb,),
        in_specs=[row, pl.BlockSpec((K, C), lambda i: (0, 0))],
        out_specs=row,
        out_shape=jax.ShapeDtypeStruct((S, C), BF16),
        scratch_shapes=[pltpu.VMEM((T + 8, C), F32)],
        compiler_params=_params(("arbitrary",)),
    )(dp, conv_w)


def _softplus(x):
    return jnp.maximum(x, 0.0) + jnp.log1p(jnp.exp(-jnp.abs(x)))


def _ssd_common(xbc_ref, dt_ref, dtb_ref, alog_ref):
    T = SSD_T
    dtr = dt_ref[...] + dtb_ref[...]
    dt = _softplus(dtr)
    a = -jnp.exp(alog_ref[...])
    dta = dt * a
    row = lax.broadcasted_iota(jnp.int32, (T, T), 0)
    col = lax.broadcasted_iota(jnp.int32, (T, T), 1)
    causal = col <= row
    acum = _split_dot(causal, dta)
    acum_t = acum.T
    return dtr, dt, a, acum, acum_t, causal, row


def _ssd_head(xbc_ref, h, dt, acum, acum_t, causal, cb):
    P = SSM_P
    ac = acum[:, h:h + 1]
    decay = jnp.exp(jnp.where(causal, ac - acum_t[h:h + 1, :], -1e30))
    dth = dt[:, h:h + 1]
    xs = xbc_ref[:, h * P:(h + 1) * P]
    xdt = xs * dth
    sc = (cb * decay).astype(BF16)
    aend = acum[SSD_T - 1:SSD_T, h:h + 1]
    return ac, dth, xs, xdt, sc, aend


def _ssd_specs(T, blk):
    vec = lambda n: pl.BlockSpec((1, n), lambda b: (0, 0))
    return [pl.BlockSpec((T, 1024), lambda b: (blk(b), 0)),
            pl.BlockSpec((T, LANES), lambda b: (blk(b), DT_COL // LANES)),
            pl.BlockSpec((T, 512), lambda b: (blk(b), 7)),
            vec(LANES), vec(LANES), vec(LANES), vec(512)]


def _ssd_fwd(name, xbc, proj, dt_bias, a_log, d_skip, gain):
    S = xbc.shape[0]
    T = SSD_T
    nb = S // T
    H, P, N = SSM_HEADS, SSM_P, SSM_N

    def body(xbc_ref, dt_ref, z_ref, dtb_ref, alog_ref, dsk_ref, gain_ref, y_ref, st_ref, st, ybuf):
        @pl.when(pl.program_id(0) == 0)
        def _():
            st[...] = jnp.zeros_like(st)

        _, dt, _, acum, acum_t, causal, _ = _ssd_common(xbc_ref, dt_ref, dtb_ref, alog_ref)
        for g in range(SSM_GROUPS):
            bg = xbc_ref[:, 512 + g * N:512 + (g + 1) * N]
            cg = xbc_ref[:, 768 + g * N:768 + (g + 1) * N]
            cb = _dot(cg, bg, NT)
            for hh in range(H // SSM_GROUPS):
                h = g * (H // SSM_GROUPS) + hh
                ac, _, xs, xdt, sc, aend = _ssd_head(xbc_ref, h, dt, acum, acum_t, causal, cb)
                sp = st[h]
                st_ref[0, h] = sp
                y = _dot(sc, xdt) + jnp.exp(ac) * _dot(cg, sp) + xs * dsk_ref[:, h:h + 1]
                st[h] = jnp.exp(aend) * sp + _dot(bg, xdt * jnp.exp(aend - ac), TN)
                ybuf[:, h * P:(h + 1) * P] = y
        z = z_ref[...]
        yg = ybuf[...] * z * _sigmoid(z)
        r = lax.rsqrt(jnp.mean(yg * yg, axis=-1, keepdims=True) + NORM_EPS)
        y_ref[...] = yg * r * gain_ref[...]

    return pl.pallas_call(
        body, name=name, grid=(nb,), in_specs=_ssd_specs(T, lambda b: b),
        out_specs=[pl.BlockSpec((T, 512), lambda b: (b, 0)), pl.BlockSpec((1, H, N, P), lambda b: (b, 0, 0, 0))],
        out_shape=[jax.ShapeDtypeStruct((S, 512), F32), jax.ShapeDtypeStruct((nb, H, N, P), F32)],
        scratch_shapes=[pltpu.VMEM((H, N, P), F32), pltpu.VMEM((T, 512), F32)],
        compiler_params=_params(("arbitrary",)),
    )(xbc, proj, proj, dt_bias, a_log, d_skip, gain)


def _ssd_bwd(name, xbc, proj, dt_bias, a_log, d_skip, gain, states, dmix, side=None):
    S = xbc.shape[0]
    T = SSD_T
    nb = S // T
    H, P, N = SSM_HEADS, SSM_P, SSM_N
    blk = lambda b: nb - 1 - b

    def body(xbc_ref, dt_ref, z_ref, dtb_ref, alog_ref, dsk_ref, gain_ref, st_ref, do_ref,
             dx_ref, dz_ref, ddt_ref, sums_ref, dst, ybuf):
        @pl.when(pl.program_id(0) == 0)
        def _():
            dst[...] = jnp.zeros_like(dst)
            sums_ref[...] = jnp.zeros_like(sums_ref)

        dtr, dt, a, acum, acum_t, causal, row = _ssd_common(xbc_ref, dt_ref, dtb_ref, alog_ref)
        cbs = []
        for g in range(SSM_GROUPS):
            bg = xbc_ref[:, 512 + g * N:512 + (g + 1) * N]
            cg = xbc_ref[:, 768 + g * N:768 + (g + 1) * N]
            cb = _dot(cg, bg, NT)
            cbs.append(cb)
            for hh in range(H // SSM_GROUPS):
                h = g * (H // SSM_GROUPS) + hh
                ac, _, xs, xdt, sc, _ = _ssd_head(xbc_ref, h, dt, acum, acum_t, causal, cb)
                ybuf[:, h * P:(h + 1) * P] = (_dot(sc, xdt) + jnp.exp(ac) * _dot(cg, st_ref[0, h])
                                              + xs * dsk_ref[:, h:h + 1])
        z = z_ref[...]
        sg = _sigmoid(z)
        sz = z * sg
        yfull = ybuf[...]
        yg = yfull * sz
        r = lax.rsqrt(jnp.mean(yg * yg, axis=-1, keepdims=True) + NORM_EPS)
        yh = yg * r
        do = do_ref[...]
        sums_ref[0:1, :] += jnp.sum(do * yh, axis=0, keepdims=True)
        dyh = do * gain_ref[...]
        dyg = r * (dyh - yh * jnp.mean(dyh * yh, axis=-1, keepdims=True))
        dz_ref[...] = (dyg * yfull * sg * (1.0 + z * (1.0 - sg))).astype(dz_ref.dtype)
        dyv = dyg * sz

        lane = lax.broadcasted_iota(jnp.int32, (T, LANES), 1)
        rowl = lax.broadcasted_iota(jnp.int32, (T, 1), 0)
        dacum = jnp.zeros((T, LANES), F32)
        dacum_t = jnp.zeros((LANES, T), F32)
        sub = lax.broadcasted_iota(jnp.int32, (LANES, T), 0)
        ddt = jnp.zeros((T, LANES), F32)
        dskp = jnp.zeros((T, LANES), F32)
        for g in range(SSM_GROUPS):
            bg = xbc_ref[:, 512 + g * N:512 + (g + 1) * N]
            cg = xbc_ref[:, 768 + g * N:768 + (g + 1) * N]
            cb = cbs[g]
            dcb = jnp.zeros((T, T), F32)
            dbg = jnp.zeros((T, N), F32)
            dcg = jnp.zeros((T, N), F32)
            for hh in range(H // SSM_GROUPS):
                h = g * (H // SSM_GROUPS) + hh
                ac, dth, xs, xdt, sc, aend = _ssd_head(xbc_ref, h, dt, acum, acum_t, causal, cb)
                decay = jnp.exp(jnp.where(causal, ac - acum_t[h:h + 1, :], -1e30))
                dy = dyv[:, h * P:(h + 1) * P]
                sp = st_ref[0, h]
                ea = jnp.exp(ac)
                de = jnp.exp(aend - ac)
                dec = jnp.exp(aend)
                dskp = dskp + jnp.where(lane == h, jnp.sum(dy * xs, axis=1, keepdims=True), 0.0)
                dxdt = _dot(sc, dy, TN)
                dsd = _dot(dy, xdt, NT) * decay
                dcb = dcb + dsd
                e = dsd * cb
                dacum_t = dacum_t + jnp.where(sub == h, jnp.sum(e, axis=0, keepdims=True), 0.0)
                dac = jnp.sum(e, axis=1, keepdims=True)
                dyea = dy * ea
                dac = dac + jnp.sum(dyea * _dot(cg, sp), axis=1, keepdims=True)
                dcg = dcg + _dot(dyea, sp, NT)
                dsp = _dot(cg, dyea, TN)
                dsn = dst[h]
                xde = xdt * de
                dbg = dbg + _dot(xde, dsn, NT)
                wh = _dot(bg, dsn)
                dxdt = dxdt + wh * de
                r_end = jnp.sum(wh * xde, axis=1, keepdims=True)
                dend = jnp.sum(r_end) + jnp.sum(dsn * sp) * dec
                dst[h] = dsp + dec * dsn
                dac = dac - r_end + jnp.where(rowl == T - 1, dend, 0.0)
                dacum = dacum + jnp.where(lane == h, dac, 0.0)
                ddt = ddt + jnp.where(lane == h, jnp.sum(dxdt * xs, axis=1, keepdims=True), 0.0)
                dx_ref[:, h * P:(h + 1) * P] = dxdt * dth + dy * dsk_ref[:, h:h + 1]
            dx_ref[:, 512 + g * N:512 + (g + 1) * N] = dbg + _dot(dcb, cg, TN)
            dx_ref[:, 768 + g * N:768 + (g + 1) * N] = dcg + _dot(dcb, bg)
        ddta = _split_dot(row <= lax.broadcasted_iota(jnp.int32, (T, T), 1), dacum - dacum_t.T)
        ddt = ddt + ddta * a
        ddtr = ddt * _sigmoid(dtr)
        ddt_ref[...] = ddtr.astype(ddt_ref.dtype)
        sums_ref[1:2, 0:LANES] += jnp.sum(ddtr, axis=0, keepdims=True)
        sums_ref[1:2, LANES:2 * LANES] += jnp.sum(ddta * dt, axis=0, keepdims=True) * a
        sums_ref[1:2, 2 * LANES:3 * LANES] += jnp.sum(dskp, axis=0, keepdims=True)

    specs = _ssd_specs(T, blk) + [pl.BlockSpec((1, H, N, P), lambda b: (blk(b), 0, 0, 0)),
                                  pl.BlockSpec((T, 512), lambda b: (blk(b), 2))]
    return _host_call(
        body, side, name, (nb,), specs,
        [pl.BlockSpec((T, 1024), lambda b: (blk(b), 0)), pl.BlockSpec((T, 512), lambda b: (blk(b), 0)),
         pl.BlockSpec((T, LANES), lambda b: (blk(b), 0)), pl.BlockSpec((8, 512), lambda b: (0, 0))],
        [jax.ShapeDtypeStruct((S, 1024), F32), jax.ShapeDtypeStruct((S, 512), BF16),
         jax.ShapeDtypeStruct((S, LANES), BF16), jax.ShapeDtypeStruct((8, 512), F32)],
        [pltpu.VMEM((H, N, P), F32), pltpu.VMEM((T, 512), F32)],
        (xbc, proj, proj, dt_bias, a_log, d_skip, gain, states, dmix))


def _place():
    return lax.axis_index("x"), lax.axis_index("y"), lax.axis_index("c")


def _flip(v, bit):
    return 1 - v if bit else v


def _gather_small(name, v):
    R, C = v.shape

    def body(v_ref, out_ref, send_sems, recv_sems, local_sem):
        x, y, c = _place()
        me = 4 * x + 2 * y + c
        mine = pltpu.make_async_copy(v_ref, out_ref.at[me], local_sem)
        mine.start()
        peers = [(_flip(x, (k >> 2) & 1), _flip(y, (k >> 1) & 1), _flip(c, k & 1)) for k in range(1, 8)]
        sends = []
        for k, peer in enumerate(peers):
            cp = pltpu.make_async_remote_copy(src_ref=v_ref, dst_ref=out_ref.at[me], send_sem=send_sems.at[k],
                                              recv_sem=recv_sems.at[k], device_id=peer, device_id_type=MESH)
            cp.start()
            sends.append(cp)
        for k, (px, py, pc) in enumerate(peers):
            pltpu.make_async_remote_copy(src_ref=v_ref, dst_ref=out_ref.at[4 * px + 2 * py + pc],
                                         send_sem=send_sems.at[k], recv_sem=recv_sems.at[k],
                                         device_id=(px, py, pc), device_id_type=MESH).wait_recv()
        for cp in sends:
            cp.wait_send()
        mine.wait()

    return pl.pallas_call(
        body, name=name, out_shape=jax.ShapeDtypeStruct((8, R, C), v.dtype),
        in_specs=[pl.BlockSpec(memory_space=pltpu.VMEM)], out_specs=pl.BlockSpec(memory_space=pltpu.VMEM),
        scratch_shapes=[pltpu.SemaphoreType.DMA((7,)), pltpu.SemaphoreType.DMA((7,)), pltpu.SemaphoreType.DMA(())],
    )(v)


def _hbm_call(body, name, arrays, out_shapes, n_sems):
    spec = pl.BlockSpec(memory_space=pl.ANY)
    return pl.pallas_call(
        body, name=name, out_shape=out_shapes, in_specs=[spec] * len(arrays), out_specs=[spec] * len(out_shapes),
        scratch_shapes=[pltpu.SemaphoreType.DMA((n_sems,)), pltpu.SemaphoreType.DMA((n_sems,))],
    )(*arrays)


def _exchange_call(ex, name):
    spec = pl.BlockSpec(memory_space=pl.ANY)

    def body(*refs):
        parts = (refs[:ex.n], refs[ex.n:2 * ex.n], refs[2 * ex.n:])
        ex.start(parts)
        ex.wait(parts)

    return pl.pallas_call(body, name=name, out_shape=ex.out_shapes, in_specs=[spec] * ex.n, out_specs=[spec] * ex.n,
                          scratch_shapes=ex.scratch)(*ex.arrays)


class _ChipExchange:
    def __init__(self, arrays, core, layer=None, forward=False):
        self.arrays, self.core, self.layer, self.n, self.forward = list(arrays), core, layer, len(arrays), forward
        self.out_shapes = [jax.ShapeDtypeStruct((4,) + a.shape[1:], a.dtype) for a in arrays]
        self.scratch = [pltpu.SemaphoreType.DMA((3 * self.n,))] * (4 if forward else 2)

    def _copies(self, ins, outs, sems):
        x, y, c = _place()
        chip = 2 * x + y
        peers = [(_flip(x, (k >> 1) & 1), _flip(y, k & 1)) for k in range(1, 4)]

        def copy(a, k, slot):
            px, py = peers[k]
            src = ins[a].at[self.layer] if self.layer is not None else ins[a].at[2 * px + py]
            return pltpu.make_async_remote_copy(
                src_ref=src, dst_ref=outs[a].at[slot], send_sem=sems[0].at[3 * a + k],
                recv_sem=sems[1].at[3 * a + k], device_id=(px, py, c), device_id_type=MESH)

        def passed_on(a, k):
            px, py = peers[k]
            slot = outs[a].at[2 * px + py]
            return pltpu.make_async_remote_copy(
                src_ref=slot, dst_ref=slot, send_sem=sems[2].at[3 * a + k], recv_sem=sems[3].at[3 * a + k],
                device_id=(x, y, 1 - c), device_id_type=MESH)

        pairs = [(a, k) for a in range(self.n) for k in range(3)]
        sends = [copy(a, k, chip) for a, k in pairs]
        recvs = [copy(a, k, 2 * peers[k][0] + peers[k][1]) for a, k in pairs]
        onward = [passed_on(a, k) for a, k in pairs] if self.forward else []
        return c == self.core, sends, recvs, onward

    def start(self, refs, when=True):
        mine, sends, _, _ = self._copies(*refs)

        @pl.when(jnp.logical_and(mine, when))
        def _():
            for cp in sends:
                cp.start()

    def wait(self, refs, when=True):
        mine, sends, recvs, onward = self._copies(*refs)

        @pl.when(jnp.logical_and(mine, when))
        def _():
            for i, cp in enumerate(recvs):
                cp.wait_recv()
                if onward:
                    onward[i].start()
            for cp in sends + onward:
                cp.wait_send()

        if onward:
            @pl.when(jnp.logical_and(jnp.logical_not(mine), when))
            def _():
                for cp in onward:
                    cp.wait_recv()


class _SiblingSend:
    def __init__(self, arrays, sender):
        self.arrays, self.sender, self.n = list(arrays), sender, len(arrays)
        self.out_shapes = [jax.ShapeDtypeStruct(a.shape, a.dtype) for a in arrays]
        self.scratch = [pltpu.SemaphoreType.DMA((self.n,))] * 2

    def _copies(self, ins, outs, sems):
        x, y, c = _place()
        cps = [pltpu.make_async_remote_copy(src_ref=ins[a], dst_ref=outs[a], send_sem=sems[0].at[a],
                                            recv_sem=sems[1].at[a], device_id=(x, y, 1 - c), device_id_type=MESH)
               for a in range(self.n)]
        return c == self.sender, cps

    def start(self, refs, when=True):
        mine, cps = self._copies(*refs)

        @pl.when(jnp.logical_and(mine, when))
        def _():
            for cp in cps:
                cp.start()

    def wait(self, refs, when=True):
        mine, cps = self._copies(*refs)

        @pl.when(jnp.logical_and(mine, when))
        def _():
            for cp in cps:
                cp.wait_send()

        @pl.when(jnp.logical_and(jnp.logical_not(mine), when))
        def _():
            for cp in cps:
                cp.wait_recv()


def _host_call(body, side, name, grid, in_specs, out_specs, out_shape, scratch_shapes, operands):
    params = _params(("arbitrary",) * len(grid))
    if side is None:
        return pl.pallas_call(body, name=name, grid=grid, in_specs=in_specs, out_specs=out_specs, out_shape=out_shape,
                              scratch_shapes=scratch_shapes, compiler_params=params)(*operands), None
    n_in, n_out, n_scr, n = len(in_specs), len(out_specs), len(scratch_shapes), side.n
    hbm = pl.BlockSpec(memory_space=pl.ANY)

    def wrapped(*refs):
        i1 = n_in + n
        i2 = i1 + n_out
        i3 = i2 + n
        i4 = i3 + n_scr
        parts = (refs[n_in:i1], refs[i2:i3], refs[i4:])
        first, last = True, True
        for ax, size in enumerate(grid):
            first = jnp.logical_and(first, pl.program_id(ax) == 0)
            last = jnp.logical_and(last, pl.program_id(ax) == size - 1)
        side.start(parts, first)
        body(*refs[:n_in], *refs[i1:i2], *refs[i3:i4])
        side.wait(parts, last)

    outs = pl.pallas_call(
        wrapped, name=name, grid=grid, in_specs=list(in_specs) + [hbm] * n, out_specs=list(out_specs) + [hbm] * n,
        out_shape=list(out_shape) + side.out_shapes, scratch_shapes=list(scratch_shapes) + side.scratch,
        compiler_params=params)(*operands, *side.arrays)
    return outs[:n_out], outs[n_out:]


def _sibling_swap(name, arrays, alt=None):
    n = len(arrays)

    def body(*refs):
        k = 1 if alt is None else 2
        outs, send_sems, recv_sems = refs[k * n:(k + 1) * n], refs[(k + 1) * n], refs[(k + 1) * n + 1]
        x, y, c = _place()

        def exchange(srcs):
            cps = [pltpu.make_async_remote_copy(src_ref=srcs[a], dst_ref=outs[a], send_sem=send_sems.at[a],
                                                recv_sem=recv_sems.at[a], device_id=(x, y, 1 - c),
                                                device_id_type=MESH) for a in range(n)]
            for cp in cps:
                cp.start()
            for cp in cps:
                cp.wait()

        if alt is None:
            exchange(refs[:n])
        else:
            @pl.when(c == 1)
            def _():
                exchange(refs[:n])

            @pl.when(c == 0)
            def _():
                exchange(refs[n:2 * n])

    shapes = [jax.ShapeDtypeStruct(a.shape, a.dtype) for a in arrays]
    return _hbm_call(body, name, list(arrays) + ([] if alt is None else list(alt)), shapes, n)


PACK_C = 1024
SUM_STEPS = 4


def _add_lists(name, own, other):
    n = len(other)

    def body(*refs):
        for a in range(n):
            refs[2 * n + a][...] = (refs[a][...].astype(F32) + refs[n + a][...].astype(F32)).astype(BF16)

    specs = [pl.BlockSpec((1, o.shape[1] // SUM_STEPS, o.shape[2]), lambda p, i: (p, i, 0)) for o in other]
    return pl.pallas_call(
        body, name=name, grid=(4, SUM_STEPS), in_specs=specs * 2, out_specs=specs,
        out_shape=[jax.ShapeDtypeStruct(o.shape, BF16) for o in other],
        compiler_params=_params(("parallel", "parallel")),
    )(*own, *other)


def _sum_chips(name, received, own, chip):
    n = len(received)

    def body(chip_ref, *refs):
        for a in range(n):
            s = None
            for q in range(4):
                term = jnp.where(chip_ref[0] == q, refs[n + a][0], refs[a][q]).astype(F32)
                s = term if s is None else s + term
            refs[2 * n + a][...] = s

    rec = [pl.BlockSpec((4, r.shape[1] // SUM_STEPS, r.shape[2]), lambda i, chip_ref: (0, i, 0)) for r in received]
    mine = [pl.BlockSpec((1, r.shape[1] // SUM_STEPS, r.shape[2]), lambda i, chip_ref: (chip_ref[0], i, 0))
            for r in received]
    outs = [pl.BlockSpec((r.shape[1] // SUM_STEPS, r.shape[2]), lambda i, chip_ref: (i, 0)) for r in received]
    grid_spec = pltpu.PrefetchScalarGridSpec(num_scalar_prefetch=1, grid=(SUM_STEPS,), in_specs=rec + mine,
                                             out_specs=outs)
    return pl.pallas_call(
        body, name=name, grid_spec=grid_spec,
        out_shape=[jax.ShapeDtypeStruct(r.shape[1:], F32) for r in received],
        compiler_params=_params(("parallel",)),
    )(jnp.reshape(chip, (1,)).astype(jnp.int32), *received, *own)


def _sum_devices(name, parts):
    _, R, C = parts.shape

    def body(p_ref, s_ref):
        s = p_ref[0]
        for q in range(1, 8):
            s = s + p_ref[q]
        s_ref[...] = s

    return pl.pallas_call(body, name=name, out_shape=jax.ShapeDtypeStruct((R, C), F32))(parts)


def _adamw_rule(w, g, m, v):
    nm = ADAM_B1 * m + (1.0 - ADAM_B1) * g
    nv = ADAM_B2 * v + (1.0 - ADAM_B2) * (g * g)
    m_hat = nm / (1.0 - ADAM_B1 ** ADAM_STEP)
    v_hat = nv / (1.0 - ADAM_B2 ** ADAM_STEP)
    return -ADAM_LR * (m_hat / (jnp.sqrt(v_hat) + ADAM_EPS) + ADAM_WD * w), nm, nv


def _adamw(name, w, g, m, v):
    L, R, C = w.shape
    tr = 128 if R % 128 == 0 else R

    def body(w_ref, g_ref, m_ref, v_ref, d_ref, nm_ref, nv_ref):
        d_ref[...], nm_ref[...], nv_ref[...] = _adamw_rule(w_ref[...], g_ref[...], m_ref[...], v_ref[...])

    spec = pl.BlockSpec((None, tr, C), lambda l, i: (l, i, 0))
    return pl.pallas_call(
        body, name=name, grid=(L, R // tr), in_specs=[spec] * 4, out_specs=[spec] * 3,
        out_shape=[jax.ShapeDtypeStruct((L, R, C), F32)] * 3, compiler_params=_params(("parallel", "parallel")),
    )(w, g, m, v)


def _adamw_layers(name, w, g_own, g_other, m, v, c):
    L, R, C = w.shape
    tr = 128 if R % 128 == 0 else R

    def body(c_ref, w_ref, g0_ref, g1_ref, go_ref, m_ref, v_ref, g_ref, d_ref, nm_ref, nv_ref):
        l = pl.program_id(0)
        own = jnp.where(l == 0, g0_ref[...], g1_ref[...])
        gv = jnp.where(l == c_ref[0], own, go_ref[...])
        g_ref[...] = gv
        d_ref[...], nm_ref[...], nv_ref[...] = _adamw_rule(w_ref[...], gv, m_ref[...], v_ref[...])

    full = pl.BlockSpec((None, tr, C), lambda l, i, c_ref: (l, i, 0))
    part = pl.BlockSpec((tr, C), lambda l, i, c_ref: (i, 0))
    grid_spec = pltpu.PrefetchScalarGridSpec(num_scalar_prefetch=1, grid=(L, R // tr),
                                             in_specs=[full, part, part, part, full, full], out_specs=[full] * 4)
    return pl.pallas_call(
        body, name=name, grid_spec=grid_spec, out_shape=[jax.ShapeDtypeStruct((L, R, C), F32)] * 4,
        compiler_params=_params(("parallel", "parallel")),
    )(jnp.reshape(c, (1,)).astype(jnp.int32), w, g_own[0], g_own[1], g_other, m, v)


BIG = ("ffn1_wg", "ffn1_wu", "ffn1_wd", "w_in", "w_out", "ffn2_wg", "ffn2_wu", "ffn2_wd")
HELD_TRANSPOSED = ("ffn1_wg", "ffn1_wu", "ffn2_wg", "ffn2_wu")
WEIGHTS =("ada_w", "ada_b", "norm_ffn1", "ffn1_wg", "ffn1_wu", "ffn1_wd", "norm_mix", "w_in", "conv_w", "conv_b",
           "dt_bias", "a_log", "d_skip", "ret_gn", "ssm_norm", "w_out", "norm_ffn2", "ffn2_wg", "ffn2_wu", "ffn2_wd",
           "final_ada_w", "final_ada_b", "final_norm")
SMALL = ("ada_b", "norm_ffn1", "norm_mix", "conv_w", "conv_b", "dt_bias", "a_log", "d_skip", "ret_gn", "ssm_norm",
         "norm_ffn2", "final_ada_b", "final_norm")


def _unpack(slab, shapes):
    flat = slab.reshape(-1)
    out, off = [], 0
    for shp in shapes:
        n = math.prod(shp)
        out.append(flat[off:off + n].reshape(shp))
        off += n
    return out


def _pad_lanes(v):
    return jnp.pad(v, (0, LANES - v.shape[0])).reshape(1, LANES)


def _ffn_fwd(tag, h, u, mod3, wg, wu, wd, nxt, side=None):
    (a, b, hm), fetched = _ffn_up(tag + "_up", u, wg, wu, side)
    gatefac = 0.5 * (1.0 + mod3[2:3])
    out, h_new, *u_next = _rowmm(tag + "_down", [(hm, p, wd, p) for p in range(4)], "nn", res=(h, gatefac, nxt))
    saved = dict(h=h, u=u, a=a, b=b, hm=hm, out=out, gatefac=gatefac)
    return h_new, (u_next[0] if nxt else None), saved, fetched


def _ffn_bwd(tag, g, d_out, s_gate, saved, gain, mod3, wg, wu, wd, nxt, side=None, then=None):
    sv = saved
    (da, db), handed = _ffn_bwd_mid(tag + "_mid_bwd", d_out, wd, sv["a"], sv["b"], side)
    dwd = _mm_tn_groups(tag + "_dwd", sv["hm"], d_out, BF16)
    dwg = _mm_tn_groups(tag + "_dwg", da, sv["u"], BF16)
    dwu = _mm_tn_groups(tag + "_dwu", db, sv["u"], BF16)
    res = _rowmm(
        tag + "_du", [(da, p, wg, p) for p in range(4)] + [(db, p, wu, p) for p in range(4)], "nn",
        norm_bwd=(sv["h"], gain, mod3[1:2], g, nxt), side=then(handed) if then else None)
    if then:
        res, handed = res
    g_in, s_norm, *below = res
    dmod3 = jnp.concatenate([s_norm[0:1], s_norm[1:2], 0.5 * s_gate[0:1]], axis=0)
    return g_in, below, (dwg, dwu, dwd), dmod3, s_norm[2], handed


def _mixer_fwd(tag, h, u, P, mod3, w_in, w_out, tables, nxt, side=None):
    proj = _mm(tag + "_in_proj", [(u, w_in)], "nn", F32, tm=512, tn=1792, tk=1024, n_outer=True)
    y_ret, ret_st = _ret_fwd(tag + "_ret", proj, P["ret_gn"], tables)
    kv = proj[:, 2560:3584].astype(BF16)
    y_sb, fetched = _sb_fwd(tag + "_sb", proj, kv, side)
    xbc = _conv_fwd(tag + "_conv", proj, P["conv_w"], P["conv_b"])
    y_ssm, ssm_st = _ssd_fwd(tag + "_ssd", xbc, proj, P["dt_bias"], P["a_log"], P["d_skip"], P["ssm_norm"])
    gatefac = 1.0 + mod3[2:3]
    ys = (y_ret, y_sb, y_ssm)
    w_out3 = w_out.reshape(3, 512, D_MODEL)
    mixed, h_new, *u_next = _rowmm(tag + "_out_proj", [(y, None, w_out3, i) for i, y in enumerate(ys)], "nn",
                                   res=(h, gatefac, nxt))
    saved = dict(h=h, u=u, proj=proj, kv=kv, ys=ys, ret_st=ret_st, xbc=xbc, ssm_st=ssm_st, out=mixed, gatefac=gatefac)
    return h_new, (u_next[0] if nxt else None), saved, fetched


def _mixer_bwd(tag, g, d_mixed, s_gate, saved, P, mod3, w_in, w_out, tables, nxt, side=None):
    h, u, proj, kv, ys, ret_st, xbc, ssm_st = (saved[k] for k in ("h", "u", "proj", "kv", "ys", "ret_st", "xbc",
                                                                  "ssm_st"))
    S = h.shape[0]
    dmix = _mm(tag + "_dmix", [(d_mixed, w_out)], "nt", F32, tm=512, tn=512, tk=1024)
    dw_out = jnp.concatenate(
        [_mm(tag + f"_dw_out{i}", [(y, d_mixed)], "tn", BF16, tm=512, tn=1024, tk=2048) for i, y in enumerate(ys)], axis=0)
    d_ret, d_gn = _ret_bwd(tag + "_ret_bwd", proj, P["ret_gn"], tables, ret_st, dmix)
    dq, dk, dv = _sb_bwd(tag + "_sb_bwd", proj, kv, dmix)
    (dxbc, dz, ddt, s_ssd), exchanged = _ssd_bwd(tag + "_ssd_bwd", xbc, proj, P["dt_bias"], P["a_log"], P["d_skip"],
                                                 P["ssm_norm"], ssm_st, dmix, side)
    dp, s_conv = _conv_bwd_act(tag + "_conv_bwd_act", proj, P["conv_w"], P["conv_b"], dxbc)
    dxr = _conv_bwd_in(tag + "_conv_bwd_in", dp, P["conv_w"])
    dproj = jnp.concatenate(
        [d_ret[0], d_ret[1], d_ret[2], d_ret[3], dq, dk.astype(BF16), dv.astype(BF16), dz, dxr, ddt,
         jnp.zeros((S, IN_WP - DT_COL - LANES), BF16)], axis=1)
    dw_in = _mm(tag + "_dw_in", [(u, dproj)], "tn", BF16, tm=1024, tn=768, tk=2048)
    g_in, s_norm, *below = _rowmm(tag + "_du", [(dproj, None, w_in, None)], "nt",
                                  norm_bwd=(h, P["norm_mix"], mod3[1:2], g, nxt))
    dmod3 = jnp.concatenate([s_norm[0:1], s_norm[1:2], s_gate[0:1]], axis=0)
    small = dict(norm_mix=s_norm[2], conv_w=s_conv[0:4], conv_b=s_conv[4], dt_bias=s_ssd[1, 0:8],
                 a_log=s_ssd[1, LANES:LANES + 8], d_skip=s_ssd[1, 2 * LANES:2 * LANES + 8],
                 ret_gn=d_gn[0], ssm_norm=s_ssd[0])
    return g_in, below, dw_in, dw_out, dmod3, small, exchanged


def kernel(x, c, ada_w, ada_b, norm_ffn1, ffn1_wg, ffn1_wu, ffn1_wd, norm_mix, w_in, conv_w, conv_b, dt_bias, a_log, d_skip, ret_gn, ssm_norm, w_out, norm_ffn2, ffn2_wg, ffn2_wu, ffn2_wd, final_ada_w, final_ada_b, final_norm, loss_target, m_ada_w, m_ada_b, m_norm_ffn1, m_ffn1_wg, m_ffn1_wu, m_ffn1_wd, m_norm_mix, m_w_in, m_conv_w, m_conv_b, m_dt_bias, m_a_log, m_d_skip, m_ret_gn, m_ssm_norm, m_w_out, m_norm_ffn2, m_ffn2_wg, m_ffn2_wu, m_ffn2_wd, m_final_ada_w, m_final_ada_b, m_final_norm, v_ada_w, v_ada_b, v_norm_ffn1, v_ffn1_wg, v_ffn1_wu, v_ffn1_wd, v_norm_mix, v_w_in, v_conv_w, v_conv_b, v_dt_bias, v_a_log, v_d_skip, v_ret_gn, v_ssm_norm, v_w_out, v_norm_ffn2, v_ffn2_wg, v_ffn2_wu, v_ffn2_wd, v_final_ada_w, v_final_ada_b, v_final_norm):
    W = dict(ada_w=ada_w, ada_b=ada_b, norm_ffn1=norm_ffn1, ffn1_wg=ffn1_wg, ffn1_wu=ffn1_wu, ffn1_wd=ffn1_wd,
             norm_mix=norm_mix, w_in=w_in, conv_w=conv_w, conv_b=conv_b, dt_bias=dt_bias, a_log=a_log, d_skip=d_skip,
             ret_gn=ret_gn, ssm_norm=ssm_norm, w_out=w_out, norm_ffn2=norm_ffn2, ffn2_wg=ffn2_wg, ffn2_wu=ffn2_wu,
             ffn2_wd=ffn2_wd, final_ada_w=final_ada_w, final_ada_b=final_ada_b, final_norm=final_norm)
    M = dict(ada_w=m_ada_w, ada_b=m_ada_b, norm_ffn1=m_norm_ffn1, ffn1_wg=m_ffn1_wg, ffn1_wu=m_ffn1_wu,
             ffn1_wd=m_ffn1_wd, norm_mix=m_norm_mix, w_in=m_w_in, conv_w=m_conv_w, conv_b=m_conv_b, dt_bias=m_dt_bias,
             a_log=m_a_log, d_skip=m_d_skip, ret_gn=m_ret_gn, ssm_norm=m_ssm_norm, w_out=m_w_out,
             norm_ffn2=m_norm_ffn2, ffn2_wg=m_ffn2_wg, ffn2_wu=m_ffn2_wu, ffn2_wd=m_ffn2_wd,
             final_ada_w=m_final_ada_w, final_ada_b=m_final_ada_b, final_norm=m_final_norm)
    V = dict(ada_w=v_ada_w, ada_b=v_ada_b, norm_ffn1=v_norm_ffn1, ffn1_wg=v_ffn1_wg, ffn1_wu=v_ffn1_wu,
             ffn1_wd=v_ffn1_wd, norm_mix=v_norm_mix, w_in=v_w_in, conv_w=v_conv_w, conv_b=v_conv_b, dt_bias=v_dt_bias,
             a_log=v_a_log, d_skip=v_d_skip, ret_gn=v_ret_gn, ssm_norm=v_ssm_norm, w_out=v_w_out,
             norm_ffn2=v_norm_ffn2, ffn2_wg=v_ffn2_wg, ffn2_wu=v_ffn2_wu, ffn2_wd=v_ffn2_wd,
             final_ada_w=v_final_ada_w, final_ada_b=v_final_ada_b, final_norm=v_final_norm)
    for n in HELD_TRANSPOSED:
        W[n], M[n], V[n] = (jnp.transpose(t[n], (0, 2, 1)) for t in (W, M, V))
    D = D_MODEL
    S = x.shape[1]
    ax, ay, ac = _place()
    me = 4 * ax + 2 * ay + ac
    chip = 2 * ax + ay
    h0 = x[0]

    c_all = _gather_small("gather_c", jnp.pad(c, ((0, 7), (0, 0))))[:, 0, :]
    cond_all = c_all * jax.nn.sigmoid(c_all)
    nmod = 3 * 3 * D // 4
    mod_part = jnp.concatenate(
        [_mm(f"mod_proj{l}", [(cond_all, ada_w[l])], "nn", F32, tm=8, tn=768, tk=D) for l in range(DEPTH)]
        + [_mm("mod_proj_final", [(cond_all, final_ada_w)], "nn", F32, tm=8, tn=512, tk=D),
           conv_w.reshape(DEPTH * SSM_CONV, -1)], axis=1)
    gathered = _gather_small("gather_mod", mod_part)[0::2]
    mine = lax.dynamic_index_in_dim(gathered, me, axis=1, keepdims=False)
    mods = [(jnp.reshape(mine[:, l * nmod:(l + 1) * nmod], (-1,)) + ada_b[l]).reshape(9, D) for l in range(DEPTH)]
    fmod = (jnp.reshape(mine[:, DEPTH * nmod:DEPTH * nmod + 2 * D // 4], (-1,)) + final_ada_b).reshape(2, D)
    conv_full = jnp.transpose(gathered[:, :, DEPTH * nmod + 2 * D // 4:], (1, 0, 2)).reshape(DEPTH, SSM_CONV, -1)

    shards = [W[n].astype(BF16) for n in BIG]

    def layer_weights(l, arrays):
        fw = {n: lax.dynamic_update_index_in_dim(arrays[i], shards[i][l], chip, 0) for i, n in enumerate(BIG)}
        fw["w_in"] = jnp.pad(jnp.transpose(fw["w_in"], (1, 0, 2)).reshape(D, IN_W), ((0, 0), (0, IN_WP - IN_W)))
        fw["w_out"] = fw["w_out"].reshape(MIX_W, D)
        return fw

    full = [layer_weights(0, _exchange_call(_ChipExchange(shards, core=0, layer=0, forward=True),
                                            "gather_weights_l0"))]
    n_first = len(BIG) // 2
    fetch_sides = (_ChipExchange(shards[:n_first], core=1, layer=1), _ChipExchange(shards[n_first:], core=1, layer=1))

    tables = _ret_tables(S)
    small_p = []
    for l in range(DEPTH):
        small_p.append(dict(
            norm_ffn1=norm_ffn1[l:l + 1], norm_mix=norm_mix[l:l + 1], norm_ffn2=norm_ffn2[l:l + 1],
            ret_gn=ret_gn[l:l + 1], ssm_norm=ssm_norm[l:l + 1], conv_w=conv_full[l], conv_b=conv_b[l:l + 1],
            dt_bias=_pad_lanes(dt_bias[l]), a_log=_pad_lanes(a_log[l]), d_skip=_pad_lanes(d_skip[l])))

    def norm_of(l, sub):
        gain = small_p[l][("norm_ffn1", "norm_mix", "norm_ffn2")[sub]]
        return gain, mods[l][3 * sub:3 * sub + 1], mods[l][3 * sub + 1:3 * sub + 2]

    h = h0
    u = _norm_mod("l0_ffn1_norm", h, *norm_of(0, 0))
    saved = []
    for l in range(DEPTH):
        P, fw, mod = small_p[l], full[l], mods[l]
        sides = fetch_sides if l == 0 else (None, None)
        h, u, s1, got_a = _ffn_fwd(f"l{l}_ffn1", h, u, mod[0:3], fw["ffn1_wg"], fw["ffn1_wu"], fw["ffn1_wd"],
                                   norm_of(l, 1), sides[0])
        h, u, sm, got_b = _mixer_fwd(f"l{l}_mix", h, u, P, mod[3:6], fw["w_in"], fw["w_out"], tables, norm_of(l, 2),
                                     sides[1])
        hand_over = _SiblingSend(list(got_a) + list(got_b), sender=1) if l == 0 else None
        h, u, s2, handed = _ffn_fwd(f"l{l}_ffn2", h, u, mod[6:9], fw["ffn2_wg"], fw["ffn2_wu"], fw["ffn2_wd"],
                                    norm_of(l + 1, 0) if l + 1 < DEPTH else None, hand_over)
        saved.append((s1, sm, s2))
        if l == 0:
            full.append(layer_weights(1, [jnp.where(ac == 1, f, hd) for f, hd in zip(hand_over.arrays, handed)]))
    fgain = final_norm.reshape(1, D)
    dy, loss_rows = _final_loss("final_loss", h, loss_target[0], fgain, fmod[0:1], fmod[1:2])

    below = lambda sv: (sv["out"], sv["gatefac"])
    g, s_final, d_out, s_gate = _norm_mod_bwd("final_norm_bwd", dy, h, fgain, fmod[1:2], *below(saved[-1][2]))
    dfmod = s_final[0:2]
    small_g = {n: [None] * DEPTH for n in SMALL}
    big_g = [None] * DEPTH
    dmods = [None] * DEPTH
    chip_sum = [None] * DEPTH
    g_lists = [None] * DEPTH
    for l in reversed(range(DEPTH)):
        P, fw, mod = small_p[l], full[l], mods[l]
        s1, sm, s2 = saved[l]
        hand_over = _SiblingSend(g_lists[1], sender=0) if l == 0 else None
        g, (d_out, s_gate), (dwg2, dwu2, dwd2), dm2, dn2, handed = _ffn_bwd(
            f"l{l}_ffn2", g, d_out, s_gate, s2, P["norm_ffn2"], mod[6:9], fw["ffn2_wg"], fw["ffn2_wu"], fw["ffn2_wd"],
            below(sm), hand_over)
        reduce_side = None
        if l == 0:
            chip_sum[1] = _add_lists("reduce_l1_add", g_lists[1], handed)
            reduce_side = _ChipExchange(chip_sum[1], core=1)
        g, (d_out, s_gate), dw_in, dw_out, dmm, sg, exchanged = _mixer_bwd(
            f"l{l}_mix", g, d_out, s_gate, sm, P, mod[3:6], fw["w_in"], fw["w_out"], tables, below(s1), reduce_side)
        early = [jnp.transpose(dw_in[:, :IN_W].reshape(D, 4, IN_W // 4), (1, 0, 2)), dw_out.reshape(4, MIX_W // 4, D),
                 dwg2, dwu2, dwd2]
        early_sum = []

        def exchange_early(handed_early):
            early_sum.extend(_add_lists("reduce_l0_add_early", early, handed_early))
            return _ChipExchange(early_sum, core=0)

        g, nxt_start, (dwg1, dwu1, dwd1), dm1, dn1, early_from_chips = _ffn_bwd(
            f"l{l}_ffn1", g, d_out, s_gate, s1, P["norm_ffn1"], mod[0:3], fw["ffn1_wg"], fw["ffn1_wu"], fw["ffn1_wd"],
            below(saved[l - 1][2]) if l > 0 else None,
            _SiblingSend(early, sender=1) if l == 0 else None, exchange_early if l == 0 else None)
        if l > 0:
            d_out, s_gate = nxt_start
        dmods[l] = jnp.concatenate([dm1, dmm, dm2], axis=0)
        late = [dwg1, dwu1, dwd1]
        sg.update(norm_ffn1=dn1, norm_ffn2=dn2)
        for n, val in sg.items():
            small_g[n][l] = val
        g_lists[l] = late + early
        if l == 0:
            handed = _exchange_call(_SiblingSend(late, sender=1), "reduce_l0_sibling")
            late_sum = list(_add_lists("reduce_l0_add", late, handed))
            chip_sum[0] = late_sum + early_sum
            from_late = _exchange_call(_ChipExchange(late_sum, core=0), "reduce_l0_chips")
            from_chips = [list(from_late) + list(early_from_chips), exchanged]
    grad_x = g[None]

    n_mod = DEPTH * 9 * D + 2 * D
    vec = [jnp.stack(dmods).reshape(-1), dfmod.reshape(-1)]
    layered = [n for n in SMALL if n not in ("ada_b", "final_ada_b", "final_norm")]
    vec += [jnp.stack(small_g[n]).reshape(-1) for n in layered]
    vec += [s_final[2], jnp.sum(loss_rows[0]).reshape(1)]
    flat = jnp.concatenate(vec)
    vrows = -(-flat.shape[0] // (8 * PACK_C)) * 8
    slab = jnp.pad(flat, (0, vrows * PACK_C - flat.shape[0])).reshape(vrows, PACK_C)
    slabs = _gather_small("gather_small_grads", slab)
    total = _sum_devices("sum_small_grads", slabs).reshape(-1)
    grads = {}
    grads["ada_b"] = total[:DEPTH * 9 * D].reshape(DEPTH, 9 * D)
    grads["final_ada_b"] = total[DEPTH * 9 * D:n_mod]
    off = n_mod
    for n in layered:
        shp = (DEPTH,) + ((SSM_CONV, D) if n == "conv_w" else W[n].shape[1:])
        cnt = math.prod(shp)
        grads[n] = total[off:off + cnt].reshape(shp)
        off += cnt
    grads["final_norm"] = total[off:off + D]
    loss = total[off + D]
    grads["conv_w"] = lax.dynamic_slice_in_dim(grads["conv_w"], chip * (D // 4), D // 4, axis=2)

    dmod_all = slabs[:, :n_mod // PACK_C, :].reshape(8, n_mod)
    grads["ada_w"] = jnp.stack([
        _mm(f"grad_ada_w{l}", [(cond_all, lax.dynamic_slice_in_dim(dmod_all, l * 9 * D + chip * nmod, nmod, axis=1))],
            "tn", F32, tm=D, tn=768, tk=8) for l in range(DEPTH)])
    grads["final_ada_w"] = _mm(
        "grad_final_ada_w",
        [(cond_all, lax.dynamic_slice_in_dim(dmod_all, DEPTH * 9 * D + chip * (2 * D // 4), 2 * D // 4, axis=1))],
        "tn", F32, tm=D, tn=512, tk=8)

    reduced = [_sum_chips(f"reduce_l{l}_sum", from_chips[l], chip_sum[l], chip) for l in range(DEPTH)]
    reduced_other = _sibling_swap("reduce_grads_complete", reduced[1], alt=reduced[0])

    delta, new_m, new_v = {}, {}, {}
    for i, n in enumerate(BIG):
        grads[n], delta[n], new_m[n], new_v[n] = _adamw_layers(
            "adamw_" + n, W[n], (reduced[0][i], reduced[1][i]), reduced_other[i], M[n], V[n], ac)
    delta["ada_w"], new_m["ada_w"], new_v["ada_w"] = _adamw("adamw_ada_w", ada_w, grads["ada_w"], m_ada_w, v_ada_w)
    outs = _adamw("adamw_final_ada_w", final_ada_w[None], grads["final_ada_w"][None], m_final_ada_w[None],
                  v_final_ada_w[None])
    delta["final_ada_w"], new_m["final_ada_w"], new_v["final_ada_w"] = [o[0] for o in outs]
    small_shapes = [W[n].shape for n in SMALL]
    n_small = sum(math.prod(s) for s in small_shapes)
    srows = -(-n_small // (8 * LANES)) * 8
    slab_of = lambda T_: jnp.pad(jnp.concatenate([T_[n].reshape(-1) for n in SMALL]),
                                 (0, srows * LANES - n_small)).reshape(1, srows, LANES)
    outs = _adamw("adamw_small", slab_of(W), slab_of(grads), slab_of(M), slab_of(V))
    for res, o in zip((delta, new_m, new_v), outs):
        for n, val in zip(SMALL, _unpack(o[0], small_shapes)):
            res[n] = val

    for n in HELD_TRANSPOSED:
        for res in (grads, delta, new_m, new_v):
            res[n] = jnp.transpose(res[n], (0, 2, 1))
    return (loss, grad_x, *[grads[n] for n in WEIGHTS], *[delta[n] for n in WEIGHTS],
            *[new_m[n] for n in WEIGHTS], *[new_v[n] for n in WEIGHTS])
```

```python
import functools
import math

import jax
import jax.numpy as jnp
from jax import lax
from jax.experimental import pallas as pl
from jax.experimental.pallas import tpu as pltpu

F32 = jnp.float32
BF16 = jnp.bfloat16
MESH = pl.DeviceIdType.MESH

D_MODEL = 1024
DEPTH = 2
D_FF = 2816
RET_HEADS = 4
SB_HEADS = 4
HEAD_DIM = 128
SSM_HEADS = 8
SSM_P = 64
SSM_N = 128
SSM_GROUPS = 2
SSM_CONV = 4
MIX_W = 1536
IN_W = 5128
IN_WP = 5376
DT_COL = 5120
ROPE_BASE = 10000.0
NORM_EPS = 1e-6
ADAM_LR = 0.001
ADAM_B1 = 0.9
ADAM_B2 = 0.999
ADAM_EPS = 1e-08
ADAM_WD = 0.01
ADAM_STEP = 10

LANES = 128
VMEM_LIMIT = 56 * 1024 * 1024
RET_T = 256
RET_HPS = 4
SB_T = 256
SB_HPS = 2
SSD_T = 256
CONV_T = 512
ROW_T = 512
SB_DEAD = -150.0

NN = (((1,), (0,)), ((), ()))
NT = (((1,), (1,)), ((), ()))
TN = (((0,), (0,)), ((), ()))


def _dot(a, b, dims=NN):
    return lax.dot_general(a.astype(BF16), b.astype(BF16), dims, preferred_element_type=F32)


def _params(sem):
    return pltpu.CompilerParams(dimension_semantics=sem, vmem_limit_bytes=VMEM_LIMIT)


def _sigmoid(x):
    return 1.0 / (1.0 + jnp.exp(-x))


def _split_dot(mat01, x, dims=NN, lhs01=True, pieces=3):
    m = mat01.astype(BF16)
    total, rest = None, x
    for _ in range(pieces):
        p = rest.astype(BF16)
        rest = rest - p.astype(F32)
        d = lax.dot_general(m, p, dims, preferred_element_type=F32) if lhs01 else lax.dot_general(
            p, m, dims, preferred_element_type=F32)
        total = d if total is None else total + d
    return total


def _mm(name, terms, mode, out_dtype, tm=512, tn=512, tk=1024, n_outer=False):
    a0, b0 = terms[0]
    if mode == "nn":
        (M, K), N = a0.shape, b0.shape[1]
    elif mode == "nt":
        (M, K), N = a0.shape, b0.shape[0]
    else:
        (K, M), N = a0.shape, b0.shape[1]
    tm, tn, tk = min(tm, M), min(tn, N), min(tk, K)
    assert M % tm == 0 and N % tn == 0 and K % tk == 0, (name, M, N, K, tm, tn, tk)
    nk = K // tk
    nterm = len(terms)
    dims = {"nn": NN, "nt": NT, "tn": TN}[mode]

    def body(*refs):
        o_ref, acc = refs[2 * nterm], refs[2 * nterm + 1]
        part = None
        for t in range(nterm):
            p = _dot(refs[2 * t][...], refs[2 * t + 1][...], dims)
            part = p if part is None else part + p
        if nk == 1:
            o_ref[...] = part.astype(o_ref.dtype)
        else:
            k = pl.program_id(2)

            @pl.when(k == 0)
            def _():
                acc[...] = part

            @pl.when(k > 0)
            def _():
                acc[...] += part

            @pl.when(k == nk - 1)
            def _():
                o_ref[...] = acc[...].astype(o_ref.dtype)

    ij = (lambda g0, g1: (g1, g0)) if n_outer else (lambda g0, g1: (g0, g1))
    if mode == "nn":
        a_spec = pl.BlockSpec((tm, tk), lambda g0, g1, k: (ij(g0, g1)[0], k))
        b_spec = pl.BlockSpec((tk, tn), lambda g0, g1, k: (k, ij(g0, g1)[1]))
    elif mode == "nt":
        a_spec = pl.BlockSpec((tm, tk), lambda g0, g1, k: (ij(g0, g1)[0], k))
        b_spec = pl.BlockSpec((tn, tk), lambda g0, g1, k: (ij(g0, g1)[1], k))
    else:
        a_spec = pl.BlockSpec((tk, tm), lambda g0, g1, k: (k, ij(g0, g1)[0]))
        b_spec = pl.BlockSpec((tk, tn), lambda g0, g1, k: (k, ij(g0, g1)[1]))
    flat = [r for ab in terms for r in ab]
    return pl.pallas_call(
        body, name=name, grid=(N // tn, M // tm, nk) if n_outer else (M // tm, N // tn, nk),
        in_specs=[a_spec, b_spec] * nterm,
        out_specs=pl.BlockSpec((tm, tn), lambda g0, g1, k: ij(g0, g1)),
        out_shape=jax.ShapeDtypeStruct((M, N), out_dtype),
        scratch_shapes=[pltpu.VMEM((tm, tn) if nk > 1 else (8, LANES), F32)],
        compiler_params=_params(("parallel", "parallel", "arbitrary")),
    )(*flat)


def _norm_bwd_rows(d, x, gain, scale, sums_ref):
    r = lax.rsqrt(jnp.mean(x * x, axis=-1, keepdims=True) + NORM_EPS)
    xh = x * r
    dn = d * (1.0 + scale)
    sums_ref[0:1, :] += jnp.sum(d, axis=0, keepdims=True)
    sums_ref[1:2, :] += jnp.sum(d * xh * gain, axis=0, keepdims=True)
    sums_ref[2:3, :] += jnp.sum(dn * xh, axis=0, keepdims=True)
    dxh = dn * gain
    return r * (dxh - xh * jnp.mean(dxh * xh, axis=-1, keepdims=True))


def _rowmm(name, terms, mode, out_dtype=F32, res=None, norm_bwd=None, side=None):
    S = terms[0][0].shape[-2]
    N = terms[0][2].shape[-1] if mode == "nn" else terms[0][2].shape[-2]
    nterm = len(terms)
    tm = ROW_T
    dims = NN if mode == "nn" else NT
    nxt = None if (res or norm_bwd) is None else (res or norm_bwd)[-1]

    def body(*refs):
        o = None
        for t in range(nterm):
            p = _dot(refs[2 * t][...], refs[2 * t + 1][...], dims)
            o = p if o is None else o + p
        rest = refs[2 * nterm:]
        if norm_bwd is not None:
            h_ref, g_ref, sc_ref, gr_ref = rest[:4]
            dh_ref, sums_ref = rest[4 + (2 if nxt else 0):][:2]

            @pl.when(pl.program_id(0) == 0)
            def _():
                sums_ref[...] = jnp.zeros_like(sums_ref)

            g = gr_ref[...] + _norm_bwd_rows(o, h_ref[...], g_ref[...], sc_ref[...], sums_ref)
            dh_ref[...] = g
            if nxt:
                _gate_rows(g, rest[4], rest[5], rest[8], rest[9])
        elif res is None:
            rest[0][...] = o.astype(out_dtype)
        else:
            h_ref, gf_ref = rest[:2]
            out_ref, hn_ref = rest[2 + (3 if nxt else 0):][:2]
            out_ref[...] = o.astype(out_ref.dtype)
            hn = h_ref[...] + gf_ref[...] * o
            hn_ref[...] = hn
            if nxt:
                rest[7][...] = _norm_rows(hn, rest[2][...], rest[3][...], rest[4][...]).astype(BF16)

    in_specs, flat = [], []
    for a, ai, w, wi in terms:
        if ai is None:
            in_specs.append(pl.BlockSpec((tm, a.shape[1]), lambda i: (i, 0)))
        else:
            in_specs.append(pl.BlockSpec((None, tm, a.shape[2]), lambda i, g=ai: (g, i, 0)))
        if wi is None:
            in_specs.append(pl.BlockSpec(w.shape, lambda i: (0, 0)))
        else:
            in_specs.append(pl.BlockSpec((None,) + w.shape[1:], lambda i, g=wi: (g, 0, 0)))
        flat += [a, w]
    row = pl.BlockSpec((tm, N), lambda i: (i, 0))
    vec = pl.BlockSpec((1, N), lambda i: (0, 0))
    sums = pl.BlockSpec((8, N), lambda i: (0, 0))
    rows_f32, rows_bf16 = jax.ShapeDtypeStruct((S, N), F32), jax.ShapeDtypeStruct((S, N), BF16)
    sums_f32 = jax.ShapeDtypeStruct((8, N), F32)
    if norm_bwd is not None:
        extra = list(norm_bwd[:4]) + (list(nxt) if nxt else [])
        outs, side_outs = _host_call(
            body, side, name, (S // tm,), in_specs + [row, vec, vec, row] + ([row, vec] if nxt else []),
            [row, sums] + ([row, sums] if nxt else []),
            [rows_f32, sums_f32] + ([rows_bf16, sums_f32] if nxt else []), [], (*flat, *extra))
        return list(outs) if side is None else (list(outs), side_outs)
    if res is None:
        return pl.pallas_call(
            body, name=name, grid=(S // tm,), in_specs=in_specs, out_specs=row,
            out_shape=jax.ShapeDtypeStruct((S, N), out_dtype), compiler_params=_params(("parallel",)),
        )(*flat)
    extra = list(res[:2]) + (list(nxt) if nxt else [])
    return pl.pallas_call(
        body, name=name, grid=(S // tm,),
        in_specs=in_specs + [row, vec] + ([vec, vec, vec] if nxt else []),
        out_specs=[row, row] + ([row] if nxt else []),
        out_shape=[rows_bf16, rows_f32] + ([rows_bf16] if nxt else []),
        compiler_params=_params(("parallel",)),
    )(*flat, *extra)


def _mm_tn_groups(name, a, b, out_dtype, tk=2048):
    G = a.shape[0] if a.ndim == 3 else b.shape[0]
    S, M, N = a.shape[-2], a.shape[-1], b.shape[-1]
    tk = min(tk, S)
    nk = S // tk

    def body(a_ref, b_ref, o_ref, acc):
        k = pl.program_id(1)
        part = _dot(a_ref[...], b_ref[...], TN)

        @pl.when(k == 0)
        def _():
            acc[...] = part

        @pl.when(k > 0)
        def _():
            acc[...] += part

        @pl.when(k == nk - 1)
        def _():
            o_ref[...] = acc[...].astype(o_ref.dtype)

    def spec(arr, width):
        if arr.ndim == 3:
            return pl.BlockSpec((None, tk, width), lambda g, k: (g, k, 0))
        return pl.BlockSpec((tk, width), lambda g, k: (k, 0))

    return pl.pallas_call(
        body, name=name, grid=(G, nk), in_specs=[spec(a, M), spec(b, N)],
        out_specs=pl.BlockSpec((None, M, N), lambda g, k: (g, 0, 0)),
        out_shape=jax.ShapeDtypeStruct((G, M, N), out_dtype),
        scratch_shapes=[pltpu.VMEM((M, N), F32)],
        compiler_params=_params(("parallel", "arbitrary")),
    )(a, b)


def _norm_rows(x, gain, shift, scale):
    r = lax.rsqrt(jnp.mean(x * x, axis=-1, keepdims=True) + NORM_EPS)
    return x * r * gain * (1.0 + scale) + shift


def _gate_rows(g, out_ref, gf_ref, d_ref, gsum_ref):
    @pl.when(pl.program_id(0) == 0)
    def _():
        gsum_ref[...] = jnp.zeros_like(gsum_ref)

    d_ref[...] = (g * gf_ref[...]).astype(d_ref.dtype)
    gsum_ref[0:1, :] += jnp.sum(g * out_ref[...].astype(F32), axis=0, keepdims=True)


def _norm_mod(name, h, gain, shift, scale):
    S, D = h.shape
    tm = ROW_T

    def body(h_ref, g_ref, sh_ref, sc_ref, u_ref):
        u_ref[...] = _norm_rows(h_ref[...], g_ref[...], sh_ref[...], sc_ref[...]).astype(u_ref.dtype)

    vec = pl.BlockSpec((1, D), lambda i: (0, 0))
    return pl.pallas_call(
        body, name=name, grid=(S // tm,),
        in_specs=[pl.BlockSpec((tm, D), lambda i: (i, 0)), vec, vec, vec],
        out_specs=pl.BlockSpec((tm, D), lambda i: (i, 0)),
        out_shape=jax.ShapeDtypeStruct((S, D), BF16),
        compiler_params=_params(("parallel",)),
    )(h, gain, shift, scale)


def _norm_mod_bwd(name, du, h, gain, scale, out, gatefac):
    S, D = h.shape
    tm = ROW_T

    def body(du_ref, h_ref, g_ref, sc_ref, o_ref, gf_ref, dh_ref, sums_ref, d_ref, gsum_ref):
        @pl.when(pl.program_id(0) == 0)
        def _():
            sums_ref[...] = jnp.zeros_like(sums_ref)

        g = _norm_bwd_rows(du_ref[...], h_ref[...], g_ref[...], sc_ref[...], sums_ref)
        dh_ref[...] = g
        _gate_rows(g, o_ref, gf_ref, d_ref, gsum_ref)

    row = pl.BlockSpec((tm, D), lambda i: (i, 0))
    vec = pl.BlockSpec((1, D), lambda i: (0, 0))
    sums = pl.BlockSpec((8, D), lambda i: (0, 0))
    return pl.pallas_call(
        body, name=name, grid=(S // tm,),
        in_specs=[row, row, vec, vec, row, vec],
        out_specs=[row, sums, row, sums],
        out_shape=[jax.ShapeDtypeStruct((S, D), F32), jax.ShapeDtypeStruct((8, D), F32),
                   jax.ShapeDtypeStruct((S, D), BF16), jax.ShapeDtypeStruct((8, D), F32)],
        compiler_params=_params(("arbitrary",)),
    )(du, h, gain, scale, out, gatefac)


def _final_loss(name, h, target, gain, shift, scale):
    S, D = h.shape
    tm = ROW_T

    def body(h_ref, t_ref, g_ref, sh_ref, sc_ref, dy_ref, loss_ref):
        @pl.when(pl.program_id(0) == 0)
        def _():
            loss_ref[...] = jnp.zeros_like(loss_ref)

        x = h_ref[...]
        r = lax.rsqrt(jnp.mean(x * x, axis=-1, keepdims=True) + NORM_EPS)
        y = x * r * g_ref[...] * (1.0 + sc_ref[...]) + sh_ref[...]
        e = y - t_ref[...]
        dy_ref[...] = e * (1.0 / D)
        loss_ref[0:1, :] += 0.5 * jnp.sum(e * e, axis=0, keepdims=True) * (1.0 / D)

    row = pl.BlockSpec((tm, D), lambda i: (i, 0))
    vec = pl.BlockSpec((1, D), lambda i: (0, 0))
    return pl.pallas_call(
        body, name=name, grid=(S // tm,),
        in_specs=[row, row, vec, vec, vec],
        out_specs=[row, pl.BlockSpec((8, D), lambda i: (0, 0))],
        out_shape=[jax.ShapeDtypeStruct((S, D), F32), jax.ShapeDtypeStruct((8, D), F32)],
        compiler_params=_params(("arbitrary",)),
    )(h, target, gain, shift, scale)


def _ffn_up(name, u, wg, wu, side=None):
    S, D = u.shape
    G, Fg, _ = wg.shape
    tm = ROW_T

    def body(u_ref, wg_ref, wu_ref, a_ref, b_ref, hm_ref):
        uv = u_ref[...]
        a = _dot(uv, wg_ref[...], NT)
        b = _dot(uv, wu_ref[...], NT)
        sg = _sigmoid(a)
        act = a * sg
        a_ref[...] = (b * sg * (1.0 + a * (1.0 - sg))).astype(a_ref.dtype)
        b_ref[...] = act.astype(b_ref.dtype)
        hm_ref[...] = (act * b).astype(hm_ref.dtype)

    w_spec = pl.BlockSpec((None, Fg, D), lambda g, i: (g, 0, 0))
    o_spec = pl.BlockSpec((None, tm, Fg), lambda g, i: (g, i, 0))
    return _host_call(
        body, side, name, (G, S // tm), [pl.BlockSpec((tm, D), lambda g, i: (i, 0)), w_spec, w_spec], [o_spec] * 3,
        [jax.ShapeDtypeStruct((G, S, Fg), BF16)] * 3, [], (u, wg, wu))


def _ffn_bwd_mid(name, d_out, wd, a, b, side=None):
    S, D = d_out.shape
    G, Fg, _ = wd.shape
    tm = ROW_T

    def body(d_ref, wd_ref, a_ref, b_ref, da_ref, db_ref):
        dhm = _dot(d_ref[...], wd_ref[...], NT)
        da_ref[...] = (dhm * a_ref[...].astype(F32)).astype(da_ref.dtype)
        db_ref[...] = (dhm * b_ref[...].astype(F32)).astype(db_ref.dtype)

    t_spec = pl.BlockSpec((None, tm, Fg), lambda g, i: (g, i, 0))
    return _host_call(
        body, side, name, (G, S // tm),
        [pl.BlockSpec((tm, D), lambda g, i: (i, 0)), pl.BlockSpec((None, Fg, D), lambda g, i: (g, 0, 0)),
         t_spec, t_spec],
        [t_spec] * 2, [jax.ShapeDtypeStruct((G, S, Fg), BF16)] * 2, [], (d_out, wd, a, b))


def _ret_tables(S):
    T = RET_T
    half = HEAD_DIM // 2
    inv_freq = ROPE_BASE ** (-jnp.arange(half, dtype=F32) / half)
    ang = jnp.arange(S, dtype=F32)[:, None] * inv_freq[None, :]
    cos, sin = jnp.cos(ang), jnp.sin(ang)
    cosf = jnp.concatenate([cos, cos], axis=-1)
    sinf = jnp.concatenate([-sin, sin], axis=-1)
    log_gamma = jnp.log1p(-(2.0 ** (-5.0 - jnp.arange(RET_HEADS, dtype=F32))))
    idx = jnp.arange(T, dtype=F32)
    chunk = jnp.arange(T) // 64
    vis = (chunk[None, :] <= chunk[:, None]).astype(F32)
    mask = jnp.exp(log_gamma[:, None, None] * jnp.abs(idx[:, None] - idx[None, :])) * vis[None]
    ones = jnp.ones((1, 1, LANES), F32)
    qdec = jnp.exp(log_gamma[:, None] * (idx + 1.0)[None, :])[:, :, None] * ones
    kdec = jnp.exp(log_gamma[:, None] * (T - 1.0 - idx)[None, :])[:, :, None] * ones
    cdec = jnp.exp(log_gamma * T)[:, None, None] * jnp.ones((1, 8, LANES), F32)
    return cosf, sinf, mask, qdec, kdec, cdec


def _rope(x, cosf, sinf):
    return x * cosf + pltpu.roll(x, HEAD_DIM // 2, 1) * sinf


def _rope_bwd(d, cosf, sinf):
    return d * cosf + pltpu.roll(d * sinf, HEAD_DIM // 2, 1)


def _ret_specs(T, rev_nb=None):
    if rev_nb is None:
        blk = lambda b: b
    else:
        blk = lambda b: rev_nb - 1 - b
    E, W = RET_HPS, RET_HPS * HEAD_DIM
    proj = lambda off: pl.BlockSpec((T, W), lambda hp, b: (blk(b), off * (RET_HEADS // E) + hp))
    rows = pl.BlockSpec((T, HEAD_DIM), lambda hp, b: (blk(b), 0))
    per_head = lambda shape: pl.BlockSpec((E,) + shape, lambda hp, b: (hp, 0, 0))
    return ([proj(0), proj(1), proj(2), proj(3), rows, rows,
             per_head((T, T)), per_head((T, LANES)), per_head((T, LANES)), per_head((8, LANES)),
             pl.BlockSpec((1, W), lambda hp, b: (0, hp))], blk)


def _ret_fwd(name, proj, gn, tables):
    S = proj.shape[0]
    T, E = RET_T, RET_HPS
    nb = S // T
    scale = HEAD_DIM ** -0.5
    specs, _ = _ret_specs(T)

    def body(q_ref, k_ref, v_ref, g_ref, cos_ref, sin_ref, m_ref, qd_ref, kd_ref, cd_ref, gn_ref, y_ref, st_ref, st):
        @pl.when(pl.program_id(1) == 0)
        def _():
            st[...] = jnp.zeros_like(st)

        cosf, sinf = cos_ref[...], sin_ref[...]
        for e in range(E):
            qr = _rope(q_ref[:, _lanes(e)], cosf, sinf)
            kr = _rope(k_ref[:, _lanes(e)], cosf, sinf) * scale
            v = v_ref[:, _lanes(e)]
            sp = st[e]
            st_ref[e, 0] = sp
            w = _dot(qr, kr, NT) * m_ref[e]
            y = _dot(w, v) + _dot(qr * qd_ref[e], sp)
            st[e] = cd_ref[e, 0:1, :] * sp + _dot(kr * kd_ref[e], v, TN)
            r = lax.rsqrt(jnp.mean(y * y, axis=-1, keepdims=True) + NORM_EPS)
            g = g_ref[:, _lanes(e)]
            y_ref[:, _lanes(e)] = y * r * gn_ref[:, _lanes(e)] * (g * _sigmoid(g))

    return pl.pallas_call(
        body, name=name, grid=(RET_HEADS // E, nb), in_specs=specs,
        out_specs=[pl.BlockSpec((T, E * HEAD_DIM), lambda hp, b: (b, hp)),
                   pl.BlockSpec((E, 1, HEAD_DIM, HEAD_DIM), lambda hp, b: (hp, b, 0, 0))],
        out_shape=[jax.ShapeDtypeStruct((S, RET_HEADS * HEAD_DIM), F32),
                   jax.ShapeDtypeStruct((RET_HEADS, nb, HEAD_DIM, HEAD_DIM), F32)],
        scratch_shapes=[pltpu.VMEM((E, HEAD_DIM, HEAD_DIM), F32)],
        compiler_params=_params(("parallel", "arbitrary")),
    )(proj, proj, proj, proj, tables[0], tables[1], tables[2], tables[3], tables[4], tables[5], gn)


def _ret_bwd(name, proj, gn, tables, states, dmix):
    S = proj.shape[0]
    T, E = RET_T, RET_HPS
    W = E * HEAD_DIM
    nb = S // T
    scale = HEAD_DIM ** -0.5
    specs, blk = _ret_specs(T, rev_nb=nb)
    specs = specs + [pl.BlockSpec((E, 1, HEAD_DIM, HEAD_DIM), lambda hp, b: (hp, blk(b), 0, 0)),
                     pl.BlockSpec((T, W), lambda hp, b: (blk(b), hp))]

    def body(q_ref, k_ref, v_ref, g_ref, cos_ref, sin_ref, m_ref, qd_ref, kd_ref, cd_ref, gn_ref, st_ref, do_ref,
             d_ref, dgn_ref, dst):
        @pl.when(pl.program_id(1) == 0)
        def _():
            dst[...] = jnp.zeros_like(dst)
            dgn_ref[...] = jnp.zeros_like(dgn_ref)

        cosf, sinf = cos_ref[...], sin_ref[...]
        for e in range(E):
            qr = _rope(q_ref[:, _lanes(e)], cosf, sinf)
            kr = _rope(k_ref[:, _lanes(e)], cosf, sinf) * scale
            v = v_ref[:, _lanes(e)]
            sp = st_ref[e, 0]
            mask, qd, kd = m_ref[e], qd_ref[e], kd_ref[e]
            w = _dot(qr, kr, NT) * mask
            y = _dot(w, v) + _dot(qr * qd, sp)
            r = lax.rsqrt(jnp.mean(y * y, axis=-1, keepdims=True) + NORM_EPS)
            yh = y * r
            gn_v = gn_ref[:, _lanes(e)]
            g = g_ref[:, _lanes(e)]
            sg = _sigmoid(g)
            do = do_ref[:, _lanes(e)]
            dyn = do * g * sg
            dgn_ref[:, _lanes(e)] += jnp.sum(dyn * yh, axis=0, keepdims=True)
            dyh = dyn * gn_v
            dy = r * (dyh - yh * jnp.mean(dyh * yh, axis=-1, keepdims=True))
            dg = do * yh * gn_v * sg * (1.0 + g * (1.0 - sg))
            ds = dst[e]
            dp = _dot(dy, v, NT) * mask
            dqr = _dot(dp, kr) + _dot(dy, sp, NT) * qd
            dkr = _dot(dp, qr, TN) + _dot(v, ds, NT) * kd
            dv = _dot(w, dy, TN) + _dot(kr * kd, ds)
            dst[e] = cd_ref[e, 0:1, :] * ds + _dot(qr * qd, dy, TN)
            d_ref[0, :, _lanes(e)] = _rope_bwd(dqr, cosf, sinf).astype(d_ref.dtype)
            d_ref[1, :, _lanes(e)] = _rope_bwd(dkr * scale, cosf, sinf).astype(d_ref.dtype)
            d_ref[2, :, _lanes(e)] = dv.astype(d_ref.dtype)
            d_ref[3, :, _lanes(e)] = dg.astype(d_ref.dtype)

    return pl.pallas_call(
        body, name=name, grid=(RET_HEADS // E, nb), in_specs=specs,
        out_specs=[pl.BlockSpec((4, T, W), lambda hp, b: (0, blk(b), hp)),
                   pl.BlockSpec((1, W), lambda hp, b: (0, hp))],
        out_shape=[jax.ShapeDtypeStruct((4, S, RET_HEADS * HEAD_DIM), BF16),
                   jax.ShapeDtypeStruct((1, RET_HEADS * HEAD_DIM), F32)],
        scratch_shapes=[pltpu.VMEM((E, HEAD_DIM, HEAD_DIM), F32)],
        compiler_params=_params(("parallel", "arbitrary")),
    )(proj, proj, proj, proj, tables[0], tables[1], tables[2], tables[3], tables[4], tables[5], gn, states, dmix)


def _sb_logits(qb, kb, i, j, scale):
    T = SB_T
    z = lax.dot_general(qb, kb, NT, preferred_element_type=F32) * scale
    row = lax.broadcasted_iota(jnp.int32, (T, T), 0)
    col = lax.broadcasted_iota(jnp.int32, (T, T), 1)
    vis = jnp.logical_or(j < i, col < row)
    lp = jnp.log1p(jnp.exp(-jnp.abs(z)))
    lb = jnp.minimum(z, 0.0) - lp
    lk = jnp.where(vis, -jnp.maximum(z, 0.0) - lp, 0.0)
    return lb, lk, vis


def _sb_weights(lb, lk, vis, tailc):
    T = SB_T
    row = lax.broadcasted_iota(jnp.int32, (T, T), 0)
    col = lax.broadcasted_iota(jnp.int32, (T, T), 1)
    tail = tailc + _split_dot(row > col, lk, lhs01=False, pieces=2)
    return jnp.where(vis, jnp.exp(lb + tail), 0.0)


def _sb_specs(S):
    T, W = SB_T, SB_HPS * HEAD_DIM
    return [pl.BlockSpec((T, W), lambda hp, i: (i, 2048 // W + hp)),
            pl.BlockSpec((S, W), lambda hp, i: (0, hp)),
            pl.BlockSpec((S, W), lambda hp, i: (0, 512 // W + hp))]


def _lanes(e):
    return slice(e * HEAD_DIM, (e + 1) * HEAD_DIM)


def _sb_fwd(name, proj, kv, side=None):
    S = proj.shape[0]
    T, E = SB_T, SB_HPS
    nq = S // T
    scale = HEAD_DIM ** -0.5

    def body(q_ref, k_ref, v_ref, y_ref):
        i = pl.program_id(1)
        qs = [q_ref[:, _lanes(e)].astype(BF16) for e in range(E)]

        def cond(c):
            return jnp.logical_and(c[0] >= 0, c[1] == 0)

        def step(c):
            j, _, tails, accs = c
            rows = pl.ds(pl.multiple_of(j * T, T), T)
            new_tails, new_accs, worst = [], [], None
            for e in range(E):
                lb, lk, vis = _sb_logits(qs[e], k_ref[rows, _lanes(e)], i, j, scale)
                w = _sb_weights(lb, lk, vis, tails[e])
                new_accs.append(accs[e] + lax.dot_general(w.astype(BF16), v_ref[rows, _lanes(e)], NN,
                                                          preferred_element_type=F32))
                t = tails[e] + jnp.sum(lk, axis=1, keepdims=True)
                new_tails.append(t)
                worst = jnp.max(t) if worst is None else jnp.maximum(worst, jnp.max(t))
            return j - 1, (worst < SB_DEAD).astype(jnp.int32), tuple(new_tails), tuple(new_accs)

        init = (i, jnp.int32(0), (jnp.zeros((T, 1), F32),) * E, (jnp.zeros((T, HEAD_DIM), F32),) * E)
        accs = lax.while_loop(cond, step, init)[3]
        for e in range(E):
            y_ref[:, _lanes(e)] = accs[e]

    (y,), fetched = _host_call(
        body, side, name, (SB_HEADS // E, nq), _sb_specs(S), [pl.BlockSpec((T, E * HEAD_DIM), lambda hp, i: (i, hp))],
        [jax.ShapeDtypeStruct((S, SB_HEADS * HEAD_DIM), F32)], [], (proj, kv, kv))
    return y, fetched


def _sb_bwd(name, proj, kv, dmix):
    S = proj.shape[0]
    T, E = SB_T, SB_HPS
    nq = S // T
    scale = HEAD_DIM ** -0.5

    def body(q_ref, k_ref, v_ref, do_ref, dq_ref, dk_ref, dv_ref):
        i = pl.program_id(1)

        @pl.when(i == 0)
        def _():
            dk_ref[...] = jnp.zeros_like(dk_ref)
            dv_ref[...] = jnp.zeros_like(dv_ref)

        qs = [q_ref[:, _lanes(e)].astype(BF16) for e in range(E)]
        dos = [do_ref[:, _lanes(e)].astype(BF16) for e in range(E)]
        row = lax.broadcasted_iota(jnp.int32, (T, T), 0)
        col = lax.broadcasted_iota(jnp.int32, (T, T), 1)
        zcol = (jnp.zeros((T, 1), F32),) * E

        def cond(c):
            return jnp.logical_and(c[0] >= 0, c[1] == 0)

        def walk_left(c):
            j, _, tails = c
            rows = pl.ds(pl.multiple_of(j * T, T), T)
            new_tails, worst = [], None
            for e in range(E):
                _, lk, _ = _sb_logits(qs[e], k_ref[rows, _lanes(e)], i, j, scale)
                t = tails[e] + jnp.sum(lk, axis=1, keepdims=True)
                new_tails.append(t)
                worst = jnp.max(t) if worst is None else jnp.maximum(worst, jnp.max(t))
            return j - 1, (worst < SB_DEAD).astype(jnp.int32), tuple(new_tails)

        j_end, _, totals = lax.while_loop(cond, walk_left, (i, jnp.int32(0), zcol))

        def walk_back(j, c):
            lefts, used, dqs = c
            rows = pl.ds(pl.multiple_of(j * T, T), T)
            new_lefts, new_used, new_dqs = [], [], []
            for e in range(E):
                kb = k_ref[rows, _lanes(e)]
                lb, lk, vis = _sb_logits(qs[e], kb, i, j, scale)
                u = used[e] + jnp.sum(lk, axis=1, keepdims=True)
                w = _sb_weights(lb, lk, vis, totals[e] - u)
                de = lax.dot_general(dos[e], v_ref[rows, _lanes(e)], NT, preferred_element_type=F32) * w
                dlk = jnp.where(vis, lefts[e] + _split_dot(row < col, de, lhs01=False, pieces=2), 0.0)
                sg = jnp.exp(lb)
                dz = ((de * (1.0 - sg) - dlk * sg) * scale).astype(BF16)
                new_dqs.append(dqs[e] + lax.dot_general(dz, kb, NN, preferred_element_type=F32))
                dk_ref[rows, _lanes(e)] += lax.dot_general(dz, qs[e], TN, preferred_element_type=F32)
                dv_ref[rows, _lanes(e)] += lax.dot_general(w.astype(BF16), dos[e], TN, preferred_element_type=F32)
                new_lefts.append(lefts[e] + jnp.sum(de, axis=1, keepdims=True))
                new_used.append(u)
            return tuple(new_lefts), tuple(new_used), tuple(new_dqs)

        init = (zcol, zcol, (jnp.zeros((T, HEAD_DIM), F32),) * E)
        dqs = lax.fori_loop(j_end + 1, i + 1, walk_back, init)[2]
        for e in range(E):
            dq_ref[:, _lanes(e)] = dqs[e].astype(dq_ref.dtype)

    W = E * HEAD_DIM
    blk = pl.BlockSpec((T, W), lambda hp, i: (i, hp))
    full = pl.BlockSpec((S, W), lambda hp, i: (0, hp))
    shp = jax.ShapeDtypeStruct((S, SB_HEADS * HEAD_DIM), F32)
    return pl.pallas_call(
        body, name=name, grid=(SB_HEADS // E, nq),
        in_specs=_sb_specs(S) + [pl.BlockSpec((T, W), lambda hp, i: (i, 512 // W + hp))],
        out_specs=[blk, full, full],
        out_shape=[jax.ShapeDtypeStruct((S, SB_HEADS * HEAD_DIM), BF16), shp, shp],
        compiler_params=_params(("parallel", "arbitrary")),
    )(proj, kv, kv, dmix)


def _conv_fwd(name, proj, conv_w, conv_b):
    S = proj.shape[0]
    T = CONV_T
    C = 1024
    K = SSM_CONV

    def body(x_ref, w_ref, b_ref, o_ref, buf):
        @pl.when(pl.program_id(0) == 0)
        def _():
            buf[0:8, :] = jnp.zeros((8, C), F32)

        buf[8:T + 8, :] = x_ref[...]
        acc = b_ref[...] + w_ref[K - 1:K, :] * buf[8:T + 8, :]
        for k in range(K - 1):
            acc = acc + w_ref[k:k + 1, :] * buf[5 + k:5 + k + T, :]
        o_ref[...] = acc * _sigmoid(acc)
        buf[0:8, :] = buf[T:T + 8, :]

    return pl.pallas_call(
        body, name=name, grid=(S // T,),
        in_specs=[pl.BlockSpec((T, C), lambda i: (i, 4)), pl.BlockSpec((K, C), lambda i: (0, 0)),
                  pl.BlockSpec((1, C), lambda i: (0, 0))],
        out_specs=pl.BlockSpec((T, C), lambda i: (i, 0)),
        out_shape=jax.ShapeDtypeStruct((S, C), F32),
        scratch_shapes=[pltpu.VMEM((T + 8, C), F32)],
        compiler_params=_params(("arbitrary",)),
    )(proj, conv_w, conv_b)


def _conv_bwd_act(name, proj, conv_w, conv_b, dxbc):
    S = proj.shape[0]
    T = CONV_T
    C = 1024
    K = SSM_CONV

    def body(x_ref, w_ref, b_ref, d_ref, dp_ref, sums_ref, buf):
        @pl.when(pl.program_id(0) == 0)
        def _():
            buf[0:8, :] = jnp.zeros((8, C), F32)
            sums_ref[...] = jnp.zeros_like(sums_ref)

        buf[8:T + 8, :] = x_ref[...]
        acc = b_ref[...] + w_ref[K - 1:K, :] * buf[8:T + 8, :]
        for k in range(K - 1):
            acc = acc + w_ref[k:k + 1, :] * buf[5 + k:5 + k + T, :]
        sg = _sigmoid(acc)
        dp = d_ref[...] * sg * (1.0 + acc * (1.0 - sg))
        dp_ref[...] = dp
        for k in range(K):
            sums_ref[k:k + 1, :] += jnp.sum(dp * buf[5 + k:5 + k + T, :], axis=0, keepdims=True)
        sums_ref[4:5, :] += jnp.sum(dp, axis=0, keepdims=True)
        buf[0:8, :] = buf[T:T + 8, :]

    row = pl.BlockSpec((T, C), lambda i: (i, 0))
    return pl.pallas_call(
        body, name=name, grid=(S // T,),
        in_specs=[pl.BlockSpec((T, C), lambda i: (i, 4)), pl.BlockSpec((K, C), lambda i: (0, 0)),
                  pl.BlockSpec((1, C), lambda i: (0, 0)), row],
        out_specs=[row, pl.BlockSpec((8, C), lambda i: (0, 0))],
        out_shape=[jax.ShapeDtypeStruct((S, C), F32), jax.ShapeDtypeStruct((8, C), F32)],
        scratch_shapes=[pltpu.VMEM((T + 8, C), F32)],
        compiler_params=_params(("arbitrary",)),
    )(proj, conv_w, conv_b, dxbc)


def _conv_bwd_in(name, dp, conv_w):
    S, C = dp.shape
    T = CONV_T
    K = SSM_CONV
    nb = S // T

    def body(d_ref, w_ref, o_ref, buf):
        @pl.when(pl.program_id(0) == 0)
        def _():
            buf[T:T + 8, :] = jnp.zeros((8, C), F32)

        buf[0:T, :] = d_ref[...]
        acc = w_ref[K - 1:K, :] * buf[0:T, :]
        for k in range(K - 1):
            acc = acc + w_ref[k:k + 1, :] * buf[3 - k:3 - k + T, :]
        o_ref[...] = acc.astype(o_ref.dtype)
        buf[T:T + 8, :] = buf[0:8, :]

    row = pl.BlockSpec((T, C), lambda i: (nb - 1 - i, 0))
    return pl.pallas_call(
        body, name=name, grid=(nb,),
        in_specs=[row, pl.BlockSpec((K, C), lambda i: (0, 0))],
        out_specs=row,
        out_shape=jax.ShapeDtypeStruct((S, C), BF16),
        scratch_shapes=[pltpu.VMEM((T + 8, C), F32)],
        compiler_params=_params(("arbitrary",)),
    )(dp, conv_w)


def _softplus(x):
    return jnp.maximum(x, 0.0) + jnp.log1p(jnp.exp(-jnp.abs(x)))


def _ssd_common(xbc_ref, dt_ref, dtb_ref, alog_ref):
    T = SSD_T
    dtr = dt_ref[...] + dtb_ref[...]
    dt = _softplus(dtr)
    a = -jnp.exp(alog_ref[...])
    dta = dt * a
    row = lax.broadcasted_iota(jnp.int32, (T, T), 0)
    col = lax.broadcasted_iota(jnp.int32, (T, T), 1)
    causal = col <= row
    acum = _split_dot(causal, dta)
    acum_t = acum.T
    return dtr, dt, a, acum, acum_t, causal, row


def _ssd_head(xbc_ref, h, dt, acum, acum_t, causal, cb):
    P = SSM_P
    ac = acum[:, h:h + 1]
    decay = jnp.exp(jnp.where(causal, ac - acum_t[h:h + 1, :], -1e30))
    dth = dt[:, h:h + 1]
    xs = xbc_ref[:, h * P:(h + 1) * P]
    xdt = xs * dth
    sc = (cb * decay).astype(BF16)
    aend = acum[SSD_T - 1:SSD_T, h:h + 1]
    return ac, dth, xs, xdt, sc, aend


def _ssd_specs(T, blk):
    vec = lambda n: pl.BlockSpec((1, n), lambda b: (0, 0))
    return [pl.BlockSpec((T, 1024), lambda b: (blk(b), 0)),
            pl.BlockSpec((T, LANES), lambda b: (blk(b), DT_COL // LANES)),
            pl.BlockSpec((T, 512), lambda b: (blk(b), 7)),
            vec(LANES), vec(LANES), vec(LANES), vec(512)]


def _ssd_fwd(name, xbc, proj, dt_bias, a_log, d_skip, gain):
    S = xbc.shape[0]
    T = SSD_T
    nb = S // T
    H, P, N = SSM_HEADS, SSM_P, SSM_N

    def body(xbc_ref, dt_ref, z_ref, dtb_ref, alog_ref, dsk_ref, gain_ref, y_ref, st_ref, st, ybuf):
        @pl.when(pl.program_id(0) == 0)
        def _():
            st[...] = jnp.zeros_like(st)

        _, dt, _, acum, acum_t, causal, _ = _ssd_common(xbc_ref, dt_ref, dtb_ref, alog_ref)
        for g in range(SSM_GROUPS):
            bg = xbc_ref[:, 512 + g * N:512 + (g + 1) * N]
            cg = xbc_ref[:, 768 + g * N:768 + (g + 1) * N]
            cb = _dot(cg, bg, NT)
            for hh in range(H // SSM_GROUPS):
                h = g * (H // SSM_GROUPS) + hh
                ac, _, xs, xdt, sc, aend = _ssd_head(xbc_ref, h, dt, acum, acum_t, causal, cb)
                sp = st[h]
                st_ref[0, h] = sp
                y = _dot(sc, xdt) + jnp.exp(ac) * _dot(cg, sp) + xs * dsk_ref[:, h:h + 1]
                st[h] = jnp.exp(aend) * sp + _dot(bg, xdt * jnp.exp(aend - ac), TN)
                ybuf[:, h * P:(h + 1) * P] = y
        z = z_ref[...]
        yg = ybuf[...] * z * _sigmoid(z)
        r = lax.rsqrt(jnp.mean(yg * yg, axis=-1, keepdims=True) + NORM_EPS)
        y_ref[...] = yg * r * gain_ref[...]

    return pl.pallas_call(
        body, name=name, grid=(nb,), in_specs=_ssd_specs(T, lambda b: b),
        out_specs=[pl.BlockSpec((T, 512), lambda b: (b, 0)), pl.BlockSpec((1, H, N, P), lambda b: (b, 0, 0, 0))],
        out_shape=[jax.ShapeDtypeStruct((S, 512), F32), jax.ShapeDtypeStruct((nb, H, N, P), F32)],
        scratch_shapes=[pltpu.VMEM((H, N, P), F32), pltpu.VMEM((T, 512), F32)],
        compiler_params=_params(("arbitrary",)),
    )(xbc, proj, proj, dt_bias, a_log, d_skip, gain)


def _ssd_bwd(name, xbc, proj, dt_bias, a_log, d_skip, gain, states, dmix, side=None):
    S = xbc.shape[0]
    T = SSD_T
    nb = S // T
    H, P, N = SSM_HEADS, SSM_P, SSM_N
    blk = lambda b: nb - 1 - b

    def body(xbc_ref, dt_ref, z_ref, dtb_ref, alog_ref, dsk_ref, gain_ref, st_ref, do_ref,
             dx_ref, dz_ref, ddt_ref, sums_ref, dst, ybuf):
        @pl.when(pl.program_id(0) == 0)
        def _():
            dst[...] = jnp.zeros_like(dst)
            sums_ref[...] = jnp.zeros_like(sums_ref)

        dtr, dt, a, acum, acum_t, causal, row = _ssd_common(xbc_ref, dt_ref, dtb_ref, alog_ref)
        cbs = []
        for g in range(SSM_GROUPS):
            bg = xbc_ref[:, 512 + g * N:512 + (g + 1) * N]
            cg = xbc_ref[:, 768 + g * N:768 + (g + 1) * N]
            cb = _dot(cg, bg, NT)
            cbs.append(cb)
            for hh in range(H // SSM_GROUPS):
                h = g * (H // SSM_GROUPS) + hh
                ac, _, xs, xdt, sc, _ = _ssd_head(xbc_ref, h, dt, acum, acum_t, causal, cb)
                ybuf[:, h * P:(h + 1) * P] = (_dot(sc, xdt) + jnp.exp(ac) * _dot(cg, st_ref[0, h])
                                              + xs * dsk_ref[:, h:h + 1])
        z = z_ref[...]
        sg = _sigmoid(z)
        sz = z * sg
        yfull = ybuf[...]
        yg = yfull * sz
        r = lax.rsqrt(jnp.mean(yg * yg, axis=-1, keepdims=True) + NORM_EPS)
        yh = yg * r
        do = do_ref[...]
        sums_ref[0:1, :] += jnp.sum(do * yh, axis=0, keepdims=True)
        dyh = do * gain_ref[...]
        dyg = r * (dyh - yh * jnp.mean(dyh * yh, axis=-1, keepdims=True))
        dz_ref[...] = (dyg * yfull * sg * (1.0 + z * (1.0 - sg))).astype(dz_ref.dtype)
        dyv = dyg * sz

        lane = lax.broadcasted_iota(jnp.int32, (T, LANES), 1)
        rowl = lax.broadcasted_iota(jnp.int32, (T, 1), 0)
        dacum = jnp.zeros((T, LANES), F32)
        dacum_t = jnp.zeros((LANES, T), F32)
        sub = lax.broadcasted_iota(jnp.int32, (LANES, T), 0)
        ddt = jnp.zeros((T, LANES), F32)
        dskp = jnp.zeros((T, LANES), F32)
        for g in range(SSM_GROUPS):
            bg = xbc_ref[:, 512 + g * N:512 + (g + 1) * N]
            cg = xbc_ref[:, 768 + g * N:768 + (g + 1) * N]
            cb = cbs[g]
            dcb = jnp.zeros((T, T), F32)
            dbg = jnp.zeros((T, N), F32)
            dcg = jnp.zeros((T, N), F32)
            for hh in range(H // SSM_GROUPS):
                h = g * (H // SSM_GROUPS) + hh
                ac, dth, xs, xdt, sc, aend = _ssd_head(xbc_ref, h, dt, acum, acum_t, causal, cb)
                decay = jnp.exp(jnp.where(causal, ac - acum_t[h:h + 1, :], -1e30))
                dy = dyv[:, h * P:(h + 1) * P]
                sp = st_ref[0, h]
                ea = jnp.exp(ac)
                de = jnp.exp(aend - ac)
                dec = jnp.exp(aend)
                dskp = dskp + jnp.where(lane == h, jnp.sum(dy * xs, axis=1, keepdims=True), 0.0)
                dxdt = _dot(sc, dy, TN)
                dsd = _dot(dy, xdt, NT) * decay
                dcb = dcb + dsd
                e = dsd * cb
                dacum_t = dacum_t + jnp.where(sub == h, jnp.sum(e, axis=0, keepdims=True), 0.0)
                dac = jnp.sum(e, axis=1, keepdims=True)
                dyea = dy * ea
                dac = dac + jnp.sum(dyea * _dot(cg, sp), axis=1, keepdims=True)
                dcg = dcg + _dot(dyea, sp, NT)
                dsp = _dot(cg, dyea, TN)
                dsn = dst[h]
                xde = xdt * de
                dbg = dbg + _dot(xde, dsn, NT)
                wh = _dot(bg, dsn)
                dxdt = dxdt + wh * de
                r_end = jnp.sum(wh * xde, axis=1, keepdims=True)
                dend = jnp.sum(r_end) + jnp.sum(dsn * sp) * dec
                dst[h] = dsp + dec * dsn
                dac = dac - r_end + jnp.where(rowl == T - 1, dend, 0.0)
                dacum = dacum + jnp.where(lane == h, dac, 0.0)
                ddt = ddt + jnp.where(lane == h, jnp.sum(dxdt * xs, axis=1, keepdims=True), 0.0)
                dx_ref[:, h * P:(h + 1) * P] = dxdt * dth + dy * dsk_ref[:, h:h + 1]
            dx_ref[:, 512 + g * N:512 + (g + 1) * N] = dbg + _dot(dcb, cg, TN)
            dx_ref[:, 768 + g * N:768 + (g + 1) * N] = dcg + _dot(dcb, bg)
        ddta = _split_dot(row <= lax.broadcasted_iota(jnp.int32, (T, T), 1), dacum - dacum_t.T)
        ddt = ddt + ddta * a
        ddtr = ddt * _sigmoid(dtr)
        ddt_ref[...] = ddtr.astype(ddt_ref.dtype)
        sums_ref[1:2, 0:LANES] += jnp.sum(ddtr, axis=0, keepdims=True)
        sums_ref[1:2, LANES:2 * LANES] += jnp.sum(ddta * dt, axis=0, keepdims=True) * a
        sums_ref[1:2, 2 * LANES:3 * LANES] += jnp.sum(dskp, axis=0, keepdims=True)

    specs = _ssd_specs(T, blk) + [pl.BlockSpec((1, H, N, P), lambda b: (blk(b), 0, 0, 0)),
                                  pl.BlockSpec((T, 512), lambda b: (blk(b), 2))]
    return _host_call(
        body, side, name, (nb,), specs,
        [pl.BlockSpec((T, 1024), lambda b: (blk(b), 0)), pl.BlockSpec((T, 512), lambda b: (blk(b), 0)),
         pl.BlockSpec((T, LANES), lambda b: (blk(b), 0)), pl.BlockSpec((8, 512), lambda b: (0, 0))],
        [jax.ShapeDtypeStruct((S, 1024), F32), jax.ShapeDtypeStruct((S, 512), BF16),
         jax.ShapeDtypeStruct((S, LANES), BF16), jax.ShapeDtypeStruct((8, 512), F32)],
        [pltpu.VMEM((H, N, P), F32), pltpu.VMEM((T, 512), F32)],
        (xbc, proj, proj, dt_bias, a_log, d_skip, gain, states, dmix))


def _place():
    return lax.axis_index("x"), lax.axis_index("y"), lax.axis_index("c")


def _flip(v, bit):
    return 1 - v if bit else v


def _gather_small(name, v):
    R, C = v.shape

    def body(v_ref, out_ref, send_sems, recv_sems, local_sem):
        x, y, c = _place()
        me = 4 * x + 2 * y + c
        mine = pltpu.make_async_copy(v_ref, out_ref.at[me], local_sem)
        mine.start()
        peers = [(_flip(x, (k >> 2) & 1), _flip(y, (k >> 1) & 1), _flip(c, k & 1)) for k in range(1, 8)]
        sends = []
        for k, peer in enumerate(peers):
            cp = pltpu.make_async_remote_copy(src_ref=v_ref, dst_ref=out_ref.at[me], send_sem=send_sems.at[k],
                                              recv_sem=recv_sems.at[k], device_id=peer, device_id_type=MESH)
            cp.start()
            sends.append(cp)
        for k, (px, py, pc) in enumerate(peers):
            pltpu.make_async_remote_copy(src_ref=v_ref, dst_ref=out_ref.at[4 * px + 2 * py + pc],
                                         send_sem=send_sems.at[k], recv_sem=recv_sems.at[k],
                                         device_id=(px, py, pc), device_id_type=MESH).wait_recv()
        for cp in sends:
            cp.wait_send()
        mine.wait()

    return pl.pallas_call(
        body, name=name, out_shape=jax.ShapeDtypeStruct((8, R, C), v.dtype),
        in_specs=[pl.BlockSpec(memory_space=pltpu.VMEM)], out_specs=pl.BlockSpec(memory_space=pltpu.VMEM),
        scratch_shapes=[pltpu.SemaphoreType.DMA((7,)), pltpu.SemaphoreType.DMA((7,)), pltpu.SemaphoreType.DMA(())],
    )(v)


def _hbm_call(body, name, arrays, out_shapes, n_sems):
    spec = pl.BlockSpec(memory_space=pl.ANY)
    return pl.pallas_call(
        body, name=name, out_shape=out_shapes, in_specs=[spec] * len(arrays), out_specs=[spec] * len(out_shapes),
        scratch_shapes=[pltpu.SemaphoreType.DMA((n_sems,)), pltpu.SemaphoreType.DMA((n_sems,))],
    )(*arrays)


def _exchange_call(ex, name):
    spec = pl.BlockSpec(memory_space=pl.ANY)

    def body(*refs):
        parts = (refs[:ex.n], refs[ex.n:2 * ex.n], refs[2 * ex.n:])
        ex.start(parts)
        ex.wait(parts)

    return pl.pallas_call(body, name=name, out_shape=ex.out_shapes, in_specs=[spec] * ex.n, out_specs=[spec] * ex.n,
                          scratch_shapes=ex.scratch)(*ex.arrays)


class _ChipExchange:
    def __init__(self, arrays, core, layer=None, forward=False):
        self.arrays, self.core, self.layer, self.n, self.forward = list(arrays), core, layer, len(arrays), forward
        self.out_shapes = [jax.ShapeDtypeStruct((4,) + a.shape[1:], a.dtype) for a in arrays]
        self.scratch = [pltpu.SemaphoreType.DMA((3 * self.n,))] * (4 if forward else 2)

    def _copies(self, ins, outs, sems):
        x, y, c = _place()
        chip = 2 * x + y
        peers = [(_flip(x, (k >> 1) & 1), _flip(y, k & 1)) for k in range(1, 4)]

        def copy(a, k, slot):
            px, py = peers[k]
            src = ins[a].at[self.layer] if self.layer is not None else ins[a].at[2 * px + py]
            return pltpu.make_async_remote_copy(
                src_ref=src, dst_ref=outs[a].at[slot], send_sem=sems[0].at[3 * a + k],
                recv_sem=sems[1].at[3 * a + k], device_id=(px, py, c), device_id_type=MESH)

        def passed_on(a, k):
            px, py = peers[k]
            slot = outs[a].at[2 * px + py]
            return pltpu.make_async_remote_copy(
                src_ref=slot, dst_ref=slot, send_sem=sems[2].at[3 * a + k], recv_sem=sems[3].at[3 * a + k],
                device_id=(x, y, 1 - c), device_id_type=MESH)

        pairs = [(a, k) for a in range(self.n) for k in range(3)]
        sends = [copy(a, k, chip) for a, k in pairs]
        recvs = [copy(a, k, 2 * peers[k][0] + peers[k][1]) for a, k in pairs]
        onward = [passed_on(a, k) for a, k in pairs] if self.forward else []
        return c == self.core, sends, recvs, onward

    def start(self, refs, when=True):
        mine, sends, _, _ = self._copies(*refs)

        @pl.when(jnp.logical_and(mine, when))
        def _():
            for cp in sends:
                cp.start()

    def wait(self, refs, when=True):
        mine, sends, recvs, onward = self._copies(*refs)

        @pl.when(jnp.logical_and(mine, when))
        def _():
            for i, cp in enumerate(recvs):
                cp.wait_recv()
                if onward:
                    onward[i].start()
            for cp in sends + onward:
                cp.wait_send()

        if onward:
            @pl.when(jnp.logical_and(jnp.logical_not(mine), when))
            def _():
                for cp in onward:
                    cp.wait_recv()


class _SiblingSend:
    def __init__(self, arrays, sender):
        self.arrays, self.sender, self.n = list(arrays), sender, len(arrays)
        self.out_shapes = [jax.ShapeDtypeStruct(a.shape, a.dtype) for a in arrays]
        self.scratch = [pltpu.SemaphoreType.DMA((self.n,))] * 2

    def _copies(self, ins, outs, sems):
        x, y, c = _place()
        cps = [pltpu.make_async_remote_copy(src_ref=ins[a], dst_ref=outs[a], send_sem=sems[0].at[a],
                                            recv_sem=sems[1].at[a], device_id=(x, y, 1 - c), device_id_type=MESH)
               for a in range(self.n)]
        return c == self.sender, cps

    def start(self, refs, when=True):
        mine, cps = self._copies(*refs)

        @pl.when(jnp.logical_and(mine, when))
        def _():
            for cp in cps:
                cp.start()

    def wait(self, refs, when=True):
        mine, cps = self._copies(*refs)

        @pl.when(jnp.logical_and(mine, when))
        def _():
            for cp in cps:
                cp.wait_send()

        @pl.when(jnp.logical_and(jnp.logical_not(mine), when))
        def _():
            for cp in cps:
                cp.wait_recv()


def _host_call(body, side, name, grid, in_specs, out_specs, out_shape, scratch_shapes, operands):
    params = _params(("arbitrary",) * len(grid))
    if side is None:
        return pl.pallas_call(body, name=name, grid=grid, in_specs=in_specs, out_specs=out_specs, out_shape=out_shape,
                              scratch_shapes=scratch_shapes, compiler_params=params)(*operands), None
    n_in, n_out, n_scr, n = len(in_specs), len(out_specs), len(scratch_shapes), side.n
    hbm = pl.BlockSpec(memory_space=pl.ANY)

    def wrapped(*refs):
        i1 = n_in + n
        i2 = i1 + n_out
        i3 = i2 + n
        i4 = i3 + n_scr
        parts = (refs[n_in:i1], refs[i2:i3], refs[i4:])
        first, last = True, True
        for ax, size in enumerate(grid):
            first = jnp.logical_and(first, pl.program_id(ax) == 0)
            last = jnp.logical_and(last, pl.program_id(ax) == size - 1)
        side.start(parts, first)
        body(*refs[:n_in], *refs[i1:i2], *refs[i3:i4])
        side.wait(parts, last)

    outs = pl.pallas_call(
        wrapped, name=name, grid=grid, in_specs=list(in_specs) + [hbm] * n, out_specs=list(out_specs) + [hbm] * n,
        out_shape=list(out_shape) + side.out_shapes, scratch_shapes=list(scratch_shapes) + side.scratch,
        compiler_params=params)(*operands, *side.arrays)
    return outs[:n_out], outs[n_out:]


def _sibling_swap(name, arrays, alt=None):
    n = len(arrays)

    def body(*refs):
        k = 1 if alt is None else 2
        outs, send_sems, recv_sems = refs[k * n:(k + 1) * n], refs[(k + 1) * n], refs[(k + 1) * n + 1]
        x, y, c = _place()

        def exchange(srcs):
            cps = [pltpu.make_async_remote_copy(src_ref=srcs[a], dst_ref=outs[a], send_sem=send_sems.at[a],
                                                recv_sem=recv_sems.at[a], device_id=(x, y, 1 - c),
                                                device_id_type=MESH) for a in range(n)]
            for cp in cps:
                cp.start()
            for cp in cps:
                cp.wait()

        if alt is None:
            exchange(refs[:n])
        else:
            @pl.when(c == 1)
            def _():
                exchange(refs[:n])

            @pl.when(c == 0)
            def _():
                exchange(refs[n:2 * n])

    shapes = [jax.ShapeDtypeStruct(a.shape, a.dtype) for a in arrays]
    return _hbm_call(body, name, list(arrays) + ([] if alt is None else list(alt)), shapes, n)


PACK_C = 1024
SUM_STEPS = 4


def _add_lists(name, own, other):
    n = len(other)

    def body(*refs):
        for a in range(n):
            refs[2 * n + a][...] = (refs[a][...].astype(F32) + refs[n + a][...].astype(F32)).astype(BF16)

    specs = [pl.BlockSpec((1, o.shape[1] // SUM_STEPS, o.shape[2]), lambda p, i: (p, i, 0)) for o in other]
    return pl.pallas_call(
        body, name=name, grid=(4, SUM_STEPS), in_specs=specs * 2, out_specs=specs,
        out_shape=[jax.ShapeDtypeStruct(o.shape, BF16) for o in other],
        compiler_params=_params(("parallel", "parallel")),
    )(*own, *other)


def _sum_chips(name, received, own, chip):
    n = len(received)

    def body(chip_ref, *refs):
        for a in range(n):
            s = None
            for q in range(4):
                term = jnp.where(chip_ref[0] == q, refs[n + a][0], refs[a][q]).astype(F32)
                s = term if s is None else s + term
            refs[2 * n + a][...] = s

    rec = [pl.BlockSpec((4, r.shape[1] // SUM_STEPS, r.shape[2]), lambda i, chip_ref: (0, i, 0)) for r in received]
    mine = [pl.BlockSpec((1, r.shape[1] // SUM_STEPS, r.shape[2]), lambda i, chip_ref: (chip_ref[0], i, 0))
            for r in received]
    outs = [pl.BlockSpec((r.shape[1] // SUM_STEPS, r.shape[2]), lambda i, chip_ref: (i, 0)) for r in received]
    grid_spec = pltpu.PrefetchScalarGridSpec(num_scalar_prefetch=1, grid=(SUM_STEPS,), in_specs=rec + mine,
                                             out_specs=outs)
    return pl.pallas_call(
        body, name=name, grid_spec=grid_spec,
        out_shape=[jax.ShapeDtypeStruct(r.shape[1:], F32) for r in received],
        compiler_params=_params(("parallel",)),
    )(jnp.reshape(chip, (1,)).astype(jnp.int32), *received, *own)


def _sum_devices(name, parts):
    _, R, C = parts.shape

    def body(p_ref, s_ref):
        s = p_ref[0]
        for q in range(1, 8):
            s = s + p_ref[q]
        s_ref[...] = s

    return pl.pallas_call(body, name=name, out_shape=jax.ShapeDtypeStruct((R, C), F32))(parts)


def _adamw_rule(w, g, m, v):
    nm = ADAM_B1 * m + (1.0 - ADAM_B1) * g
    nv = ADAM_B2 * v + (1.0 - ADAM_B2) * (g * g)
    m_hat = nm / (1.0 - ADAM_B1 ** ADAM_STEP)
    v_hat = nv / (1.0 - ADAM_B2 ** ADAM_STEP)
    return -ADAM_LR * (m_hat / (jnp.sqrt(v_hat) + ADAM_EPS) + ADAM_WD * w), nm, nv


def _adamw(name, w, g, m, v):
    L, R, C = w.shape
    tr = 128 if R % 128 == 0 else R

    def body(w_ref, g_ref, m_ref, v_ref, d_ref, nm_ref, nv_ref):
        d_ref[...], nm_ref[...], nv_ref[...] = _adamw_rule(w_ref[...], g_ref[...], m_ref[...], v_ref[...])

    spec = pl.BlockSpec((None, tr, C), lambda l, i: (l, i, 0))
    return pl.pallas_call(
        body, name=name, grid=(L, R // tr), in_specs=[spec] * 4, out_specs=[spec] * 3,
        out_shape=[jax.ShapeDtypeStruct((L, R, C), F32)] * 3, compiler_params=_params(("parallel", "parallel")),
    )(w, g, m, v)


def _adamw_layers(name, w, g_own, g_other, m, v, c):
    L, R, C = w.shape
    tr = 128 if R % 128 == 0 else R

    def body(c_ref, w_ref, g0_ref, g1_ref, go_ref, m_ref, v_ref, g_ref, d_ref, nm_ref, nv_ref):
        l = pl.program_id(0)
        own = jnp.where(l == 0, g0_ref[...], g1_ref[...])
        gv = jnp.where(l == c_ref[0], own, go_ref[...])
        g_ref[...] = gv
        d_ref[...], nm_ref[...], nv_ref[...] = _adamw_rule(w_ref[...], gv, m_ref[...], v_ref[...])

    full = pl.BlockSpec((None, tr, C), lambda l, i, c_ref: (l, i, 0))
    part = pl.BlockSpec((tr, C), lambda l, i, c_ref: (i, 0))
    grid_spec = pltpu.PrefetchScalarGridSpec(num_scalar_prefetch=1, grid=(L, R // tr),
                                             in_specs=[full, part, part, part, full, full], out_specs=[full] * 4)
    return pl.pallas_call(
        body, name=name, grid_spec=grid_spec, out_shape=[jax.ShapeDtypeStruct((L, R, C), F32)] * 4,
        compiler_params=_params(("parallel", "parallel")),
    )(jnp.reshape(c, (1,)).astype(jnp.int32), w, g_own[0], g_own[1], g_other, m, v)


BIG = ("ffn1_wg", "ffn1_wu", "ffn1_wd", "w_in", "w_out", "ffn2_wg", "ffn2_wu", "ffn2_wd")
HELD_TRANSPOSED = ("ffn1_wg", "ffn1_wu", "ffn2_wg", "ffn2_wu")
WEIGHTS =("ada_w", "ada_b", "norm_ffn1", "ffn1_wg", "ffn1_wu", "ffn1_wd", "norm_mix", "w_in", "conv_w", "conv_b",
           "dt_bias", "a_log", "d_skip", "ret_gn", "ssm_norm", "w_out", "norm_ffn2", "ffn2_wg", "ffn2_wu", "ffn2_wd",
           "final_ada_w", "final_ada_b", "final_norm")
SMALL = ("ada_b", "norm_ffn1", "norm_mix", "conv_w", "conv_b", "dt_bias", "a_log", "d_skip", "ret_gn", "ssm_norm",
         "norm_ffn2", "final_ada_b", "final_norm")


def _unpack(slab, shapes):
    flat = slab.reshape(-1)
    out, off = [], 0
    for shp in shapes:
        n = math.prod(shp)
        out.append(flat[off:off + n].reshape(shp))
        off += n
    return out


def _pad_lanes(v):
    return jnp.pad(v, (0, LANES - v.shape[0])).reshape(1, LANES)


def _ffn_fwd(tag, h, u, mod3, wg, wu, wd, nxt, side=None):
    (a, b, hm), fetched = _ffn_up(tag + "_up", u, wg, wu, side)
    gatefac = 0.5 * (1.0 + mod3[2:3])
    out, h_new, *u_next = _rowmm(tag + "_down", [(hm, p, wd, p) for p in range(4)], "nn", res=(h, gatefac, nxt))
    saved = dict(h=h, u=u, a=a, b=b, hm=hm, out=out, gatefac=gatefac)
    return h_new, (u_next[0] if nxt else None), saved, fetched


def _ffn_bwd(tag, g, d_out, s_gate, saved, gain, mod3, wg, wu, wd, nxt, side=None, then=None):
    sv = saved
    (da, db), handed = _ffn_bwd_mid(tag + "_mid_bwd", d_out, wd, sv["a"], sv["b"], side)
    dwd = _mm_tn_groups(tag + "_dwd", sv["hm"], d_out, BF16)
    dwg = _mm_tn_groups(tag + "_dwg", da, sv["u"], BF16)
    dwu = _mm_tn_groups(tag + "_dwu", db, sv["u"], BF16)
    res = _rowmm(
        tag + "_du", [(da, p, wg, p) for p in range(4)] + [(db, p, wu, p) for p in range(4)], "nn",
        norm_bwd=(sv["h"], gain, mod3[1:2], g, nxt), side=then(handed) if then else None)
    if then:
        res, handed = res
    g_in, s_norm, *below = res
    dmod3 = jnp.concatenate([s_norm[0:1], s_norm[1:2], 0.5 * s_gate[0:1]], axis=0)
    return g_in, below, (dwg, dwu, dwd), dmod3, s_norm[2], handed


def _mixer_fwd(tag, h, u, P, mod3, w_in, w_out, tables, nxt, side=None):
    proj = _mm(tag + "_in_proj", [(u, w_in)], "nn", F32, tm=512, tn=1792, tk=1024, n_outer=True)
    y_ret, ret_st = _ret_fwd(tag + "_ret", proj, P["ret_gn"], tables)
    kv = proj[:, 2560:3584].astype(BF16)
    y_sb, fetched = _sb_fwd(tag + "_sb", proj, kv, side)
    xbc = _conv_fwd(tag + "_conv", proj, P["conv_w"], P["conv_b"])
    y_ssm, ssm_st = _ssd_fwd(tag + "_ssd", xbc, proj, P["dt_bias"], P["a_log"], P["d_skip"], P["ssm_norm"])
    gatefac = 1.0 + mod3[2:3]
    ys = (y_ret, y_sb, y_ssm)
    w_out3 = w_out.reshape(3, 512, D_MODEL)
    mixed, h_new, *u_next = _rowmm(tag + "_out_proj", [(y, None, w_out3, i) for i, y in enumerate(ys)], "nn",
                                   res=(h, gatefac, nxt))
    saved = dict(h=h, u=u, proj=proj, kv=kv, ys=ys, ret_st=ret_st, xbc=xbc, ssm_st=ssm_st, out=mixed, gatefac=gatefac)
    return h_new, (u_next[0] if nxt else None), saved, fetched


def _mixer_bwd(tag, g, d_mixed, s_gate, saved, P, mod3, w_in, w_out, tables, nxt, side=None):
    h, u, proj, kv, ys, ret_st, xbc, ssm_st = (saved[k] for k in ("h", "u", "proj", "kv", "ys", "ret_st", "xbc",
                                                                  "ssm_st"))
    S = h.shape[0]
    dmix = _mm(tag + "_dmix", [(d_mixed, w_out)], "nt", F32, tm=512, tn=512, tk=1024)
    dw_out = jnp.concatenate(
        [_mm(tag + f"_dw_out{i}", [(y, d_mixed)], "tn", BF16, tm=512, tn=1024, tk=2048) for i, y in enumerate(ys)], axis=0)
    d_ret, d_gn = _ret_bwd(tag + "_ret_bwd", proj, P["ret_gn"], tables, ret_st, dmix)
    dq, dk, dv = _sb_bwd(tag + "_sb_bwd", proj, kv, dmix)
    (dxbc, dz, ddt, s_ssd), exchanged = _ssd_bwd(tag + "_ssd_bwd", xbc, proj, P["dt_bias"], P["a_log"], P["d_skip"],
                                                 P["ssm_norm"], ssm_st, dmix, side)
    dp, s_conv = _conv_bwd_act(tag + "_conv_bwd_act", proj, P["conv_w"], P["conv_b"], dxbc)
    dxr = _conv_bwd_in(tag + "_conv_bwd_in", dp, P["conv_w"])
    dproj = jnp.concatenate(
        [d_ret[0], d_ret[1], d_ret[2], d_ret[3], dq, dk.astype(BF16), dv.astype(BF16), dz, dxr, ddt,
         jnp.zeros((S, IN_WP - DT_COL - LANES), BF16)], axis=1)
    dw_in = _mm(tag + "_dw_in", [(u, dproj)], "tn", BF16, tm=1024, tn=768, tk=2048)
    g_in, s_norm, *below = _rowmm(tag + "_du", [(dproj, None, w_in, None)], "nt",
                                  norm_bwd=(h, P["norm_mix"], mod3[1:2], g, nxt))
    dmod3 = jnp.concatenate([s_norm[0:1], s_norm[1:2], s_gate[0:1]], axis=0)
    small = dict(norm_mix=s_norm[2], conv_w=s_conv[0:4], conv_b=s_conv[4], dt_bias=s_ssd[1, 0:8],
                 a_log=s_ssd[1, LANES:LANES + 8], d_skip=s_ssd[1, 2 * LANES:2 * LANES + 8],
                 ret_gn=d_gn[0], ssm_norm=s_ssd[0])
    return g_in, below, dw_in, dw_out, dmod3, small, exchanged


def kernel(x, c, ada_w, ada_b, norm_ffn1, ffn1_wg, ffn1_wu, ffn1_wd, norm_mix, w_in, conv_w, conv_b, dt_bias, a_log, d_skip, ret_gn, ssm_norm, w_out, norm_ffn2, ffn2_wg, ffn2_wu, ffn2_wd, final_ada_w, final_ada_b, final_norm, loss_target, m_ada_w, m_ada_b, m_norm_ffn1, m_ffn1_wg, m_ffn1_wu, m_ffn1_wd, m_norm_mix, m_w_in, m_conv_w, m_conv_b, m_dt_bias, m_a_log, m_d_skip, m_ret_gn, m_ssm_norm, m_w_out, m_norm_ffn2, m_ffn2_wg, m_ffn2_wu, m_ffn2_wd, m_final_ada_w, m_final_ada_b, m_final_norm, v_ada_w, v_ada_b, v_norm_ffn1, v_ffn1_wg, v_ffn1_wu, v_ffn1_wd, v_norm_mix, v_w_in, v_conv_w, v_conv_b, v_dt_bias, v_a_log, v_d_skip, v_ret_gn, v_ssm_norm, v_w_out, v_norm_ffn2, v_ffn2_wg, v_ffn2_wu, v_ffn2_wd, v_final_ada_w, v_final_ada_b, v_final_norm):
    W = dict(ada_w=ada_w, ada_b=ada_b, norm_ffn1=norm_ffn1, ffn1_wg=ffn1_wg, ffn1_wu=ffn1_wu, ffn1_wd=ffn1_wd,
             norm_mix=norm_mix, w_in=w_in, conv_w=conv_w, conv_b=conv_b, dt_bias=dt_bias, a_log=a_log, d_skip=d_skip,
             ret_gn=ret_gn, ssm_norm=ssm_norm, w_out=w_out, norm_ffn2=norm_ffn2, ffn2_wg=ffn2_wg, ffn2_wu=ffn2_wu,
             ffn2_wd=ffn2_wd, final_ada_w=final_ada_w, final_ada_b=final_ada_b, final_norm=final_norm)
    M = dict(ada_w=m_ada_w, ada_b=m_ada_b, norm_ffn1=m_norm_ffn1, ffn1_wg=m_ffn1_wg, ffn1_wu=m_ffn1_wu,
             ffn1_wd=m_ffn1_wd, norm_mix=m_norm_mix, w_in=m_w_in, conv_w=m_conv_w, conv_b=m_conv_b, dt_bias=m_dt_bias,
             a_log=m_a_log, d_skip=m_d_skip, ret_gn=m_ret_gn, ssm_norm=m_ssm_norm, w_out=m_w_out,
             norm_ffn2=m_norm_ffn2, ffn2_wg=m_ffn2_wg, ffn2_wu=m_ffn2_wu, ffn2_wd=m_ffn2_wd,
             final_ada_w=m_final_ada_w, final_ada_b=m_final_ada_b, final_norm=m_final_norm)
    V = dict(ada_w=v_ada_w, ada_b=v_ada_b, norm_ffn1=v_norm_ffn1, ffn1_wg=v_ffn1_wg, ffn1_wu=v_ffn1_wu,
             ffn1_wd=v_ffn1_wd, norm_mix=v_norm_mix, w_in=v_w_in, conv_w=v_conv_w, conv_b=v_conv_b, dt_bias=v_dt_bias,
             a_log=v_a_log, d_skip=v_d_skip, ret_gn=v_ret_gn, ssm_norm=v_ssm_norm, w_out=v_w_out,
             norm_ffn2=v_norm_ffn2, ffn2_wg=v_ffn2_wg, ffn2_wu=v_ffn2_wu, ffn2_wd=v_ffn2_wd,
             final_ada_w=v_final_ada_w, final_ada_b=v_final_ada_b, final_norm=v_final_norm)
    for n in HELD_TRANSPOSED:
        W[n], M[n], V[n] = (jnp.transpose(t[n], (0, 2, 1)) for t in (W, M, V))
    D = D_MODEL
    S = x.shape[1]
    ax, ay, ac = _place()
    me = 4 * ax + 2 * ay + ac
    chip = 2 * ax + ay
    h0 = x[0]

    c_all = _gather_small("gather_c", jnp.pad(c, ((0, 7), (0, 0))))[:, 0, :]
    cond_all = c_all * jax.nn.sigmoid(c_all)
    nmod = 3 * 3 * D // 4
    mod_part = jnp.concatenate(
        [_mm(f"mod_proj{l}", [(cond_all, ada_w[l])], "nn", F32, tm=8, tn=768, tk=D) for l in range(DEPTH)]
        + [_mm("mod_proj_final", [(cond_all, final_ada_w)], "nn", F32, tm=8, tn=512, tk=D),
           conv_w.reshape(DEPTH * SSM_CONV, -1)], axis=1)
    gathered = _gather_small("gather_mod", mod_part)[0::2]
    mine = lax.dynamic_index_in_dim(gathered, me, axis=1, keepdims=False)
    mods = [(jnp.reshape(mine[:, l * nmod:(l + 1) * nmod], (-1,)) + ada_b[l]).reshape(9, D) for l in range(DEPTH)]
    fmod = (jnp.reshape(mine[:, DEPTH * nmod:DEPTH * nmod + 2 * D // 4], (-1,)) + final_ada_b).reshape(2, D)
    conv_full = jnp.transpose(gathered[:, :, DEPTH * nmod + 2 * D // 4:], (1, 0, 2)).reshape(DEPTH, SSM_CONV, -1)

    shards = [W[n].astype(BF16) for n in BIG]

    def layer_weights(l, arrays):
        fw = {n: lax.dynamic_update_index_in_dim(arrays[i], shards[i][l], chip, 0) for i, n in enumerate(BIG)}
        fw["w_in"] = jnp.pad(jnp.transpose(fw["w_in"], (1, 0, 2)).reshape(D, IN_W), ((0, 0), (0, IN_WP - IN_W)))
        fw["w_out"] = fw["w_out"].reshape(MIX_W, D)
        return fw

    full = [layer_weights(0, _exchange_call(_ChipExchange(shards, core=0, layer=0, forward=True),
                                            "gather_weights_l0"))]
    n_first = len(BIG) // 2
    fetch_sides = (_ChipExchange(shards[:n_first], core=1, layer=1), _ChipExchange(shards[n_first:], core=1, layer=1))

    tables = _ret_tables(S)
    small_p = []
    for l in range(DEPTH):
        small_p.append(dict(
            norm_ffn1=norm_ffn1[l:l + 1], norm_mix=norm_mix[l:l + 1], norm_ffn2=norm_ffn2[l:l + 1],
            ret_gn=ret_gn[l:l + 1], ssm_norm=ssm_norm[l:l + 1], conv_w=conv_full[l], conv_b=conv_b[l:l + 1],
            dt_bias=_pad_lanes(dt_bias[l]), a_log=_pad_lanes(a_log[l]), d_skip=_pad_lanes(d_skip[l])))

    def norm_of(l, sub):
        gain = small_p[l][("norm_ffn1", "norm_mix", "norm_ffn2")[sub]]
        return gain, mods[l][3 * sub:3 * sub + 1], mods[l][3 * sub + 1:3 * sub + 2]

    h = h0
    u = _norm_mod("l0_ffn1_norm", h, *norm_of(0, 0))
    saved = []
    for l in range(DEPTH):
        P, fw, mod = small_p[l], full[l], mods[l]
        sides = fetch_sides if l == 0 else (None, None)
        h, u, s1, got_a = _ffn_fwd(f"l{l}_ffn1", h, u, mod[0:3], fw["ffn1_wg"], fw["ffn1_wu"], fw["ffn1_wd"],
                                   norm_of(l, 1), sides[0])
        h, u, sm, got_b = _mixer_fwd(f"l{l}_mix", h, u, P, mod[3:6], fw["w_in"], fw["w_out"], tables, norm_of(l, 2),
                                     sides[1])
        hand_over = _SiblingSend(list(got_a) + list(got_b), sender=1) if l == 0 else None
        h, u, s2, handed = _ffn_fwd(f"l{l}_ffn2", h, u, mod[6:9], fw["ffn2_wg"], fw["ffn2_wu"], fw["ffn2_wd"],
                                    norm_of(l + 1, 0) if l + 1 < DEPTH else None, hand_over)
        saved.append((s1, sm, s2))
        if l == 0:
            full.append(layer_weights(1, [jnp.where(ac == 1, f, hd) for f, hd in zip(hand_over.arrays, handed)]))
    fgain = final_norm.reshape(1, D)
    dy, loss_rows = _final_loss("final_loss", h, loss_target[0], fgain, fmod[0:1], fmod[1:2])

    below = lambda sv: (sv["out"], sv["gatefac"])
    g, s_final, d_out, s_gate = _norm_mod_bwd("final_norm_bwd", dy, h, fgain, fmod[1:2], *below(saved[-1][2]))
    dfmod = s_final[0:2]
    small_g = {n: [None] * DEPTH for n in SMALL}
    big_g = [None] * DEPTH
    dmods = [None] * DEPTH
    chip_sum = [None] * DEPTH
    g_lists = [None] * DEPTH
    for l in reversed(range(DEPTH)):
        P, fw, mod = small_p[l], full[l], mods[l]
        s1, sm, s2 = saved[l]
        hand_over = _SiblingSend(g_lists[1], sender=0) if l == 0 else None
        g, (d_out, s_gate), (dwg2, dwu2, dwd2), dm2, dn2, handed = _ffn_bwd(
            f"l{l}_ffn2", g, d_out, s_gate, s2, P["norm_ffn2"], mod[6:9], fw["ffn2_wg"], fw["ffn2_wu"], fw["ffn2_wd"],
            below(sm), None, (lambda _: hand_over) if l == 0 else None)
        reduce_side = None
        if l == 0:
            chip_sum[1] = _add_lists("reduce_l1_add", g_lists[1], handed)
            reduce_side = _ChipExchange(chip_sum[1], core=1)
        g, (d_out, s_gate), dw_in, dw_out, dmm, sg, exchanged = _mixer_bwd(
            f"l{l}_mix", g, d_out, s_gate, sm, P, mod[3:6], fw["w_in"], fw["w_out"], tables, below(s1), reduce_side)
        early = [jnp.transpose(dw_in[:, :IN_W].reshape(D, 4, IN_W // 4), (1, 0, 2)), dw_out.reshape(4, MIX_W // 4, D),
                 dwg2, dwu2, dwd2]
        early_sum = []

        def exchange_early(handed_early):
            early_sum.extend(_add_lists("reduce_l0_add_early", early, handed_early))
            return _ChipExchange(early_sum, core=0)

        g, nxt_start, (dwg1, dwu1, dwd1), dm1, dn1, early_from_chips = _ffn_bwd(
            f"l{l}_ffn1", g, d_out, s_gate, s1, P["norm_ffn1"], mod[0:3], fw["ffn1_wg"], fw["ffn1_wu"], fw["ffn1_wd"],
            below(saved[l - 1][2]) if l > 0 else None,
            _SiblingSend(early, sender=1) if l == 0 else None, exchange_early if l == 0 else None)
        if l > 0:
            d_out, s_gate = nxt_start
        dmods[l] = jnp.concatenate([dm1, dmm, dm2], axis=0)
        late = [dwg1, dwu1, dwd1]
        sg.update(norm_ffn1=dn1, norm_ffn2=dn2)
        for n, val in sg.items():
            small_g[n][l] = val
        g_lists[l] = late + early
        if l == 0:
            handed = _exchange_call(_SiblingSend(late, sender=1), "reduce_l0_sibling")
            late_sum = list(_add_lists("reduce_l0_add", late, handed))
            chip_sum[0] = late_sum + early_sum
            from_late = _exchange_call(_ChipExchange(late_sum, core=0), "reduce_l0_chips")
            from_chips = [list(from_late) + list(early_from_chips), exchanged]
    grad_x = g[None]

    n_mod = DEPTH * 9 * D + 2 * D
    vec = [jnp.stack(dmods).reshape(-1), dfmod.reshape(-1)]
    layered = [n for n in SMALL if n not in ("ada_b", "final_ada_b", "final_norm")]
    vec += [jnp.stack(small_g[n]).reshape(-1) for n in layered]
    vec += [s_final[2], jnp.sum(loss_rows[0]).reshape(1)]
    flat = jnp.concatenate(vec)
    vrows = -(-flat.shape[0] // (8 * PACK_C)) * 8
    slab = jnp.pad(flat, (0, vrows * PACK_C - flat.shape[0])).reshape(vrows, PACK_C)
    slabs = _gather_small("gather_small_grads", slab)
    total = _sum_devices("sum_small_grads", slabs).reshape(-1)
    grads = {}
    grads["ada_b"] = total[:DEPTH * 9 * D].reshape(DEPTH, 9 * D)
    grads["final_ada_b"] = total[DEPTH * 9 * D:n_mod]
    off = n_mod
    for n in layered:
        shp = (DEPTH,) + ((SSM_CONV, D) if n == "conv_w" else W[n].shape[1:])
        cnt = math.prod(shp)
        grads[n] = total[off:off + cnt].reshape(shp)
        off += cnt
    grads["final_norm"] = total[off:off + D]
    loss = total[off + D]
    grads["conv_w"] = lax.dynamic_slice_in_dim(grads["conv_w"], chip * (D // 4), D // 4, axis=2)

    dmod_all = slabs[:, :n_mod // PACK_C, :].reshape(8, n_mod)
    grads["ada_w"] = jnp.stack([
        _mm(f"grad_ada_w{l}", [(cond_all, lax.dynamic_slice_in_dim(dmod_all, l * 9 * D + chip * nmod, nmod, axis=1))],
            "tn", F32, tm=D, tn=768, tk=8) for l in range(DEPTH)])
    grads["final_ada_w"] = _mm(
        "grad_final_ada_w",
        [(cond_all, lax.dynamic_slice_in_dim(dmod_all, DEPTH * 9 * D + chip * (2 * D // 4), 2 * D // 4, axis=1))],
        "tn", F32, tm=D, tn=512, tk=8)

    reduced = [_sum_chips(f"reduce_l{l}_sum", from_chips[l], chip_sum[l], chip) for l in range(DEPTH)]
    reduced_other = _sibling_swap("reduce_grads_complete", reduced[1], alt=reduced[0])

    delta, new_m, new_v = {}, {}, {}
    for i, n in enumerate(BIG):
        grads[n], delta[n], new_m[n], new_v[n] = _adamw_layers(
            "adamw_" + n, W[n], (reduced[0][i], reduced[1][i]), reduced_other[i], M[n], V[n], ac)
    delta["ada_w"], new_m["ada_w"], new_v["ada_w"] = _adamw("adamw_ada_w", ada_w, grads["ada_w"], m_ada_w, v_ada_w)
    outs = _adamw("adamw_final_ada_w", final_ada_w[None], grads["final_ada_w"][None], m_final_ada_w[None],
                  v_final_ada_w[None])
    delta["final_ada_w"], new_m["final_ada_w"], new_v["final_ada_w"] = [o[0] for o in outs]
    small_shapes = [W[n].shape for n in SMALL]
    n_small = sum(math.prod(s) for s in small_shapes)
    srows = -(-n_small // (8 * LANES)) * 8
    slab_of = lambda T_: jnp.pad(jnp.concatenate([T_[n].reshape(-1) for n in SMALL]),
                                 (0, srows * LANES - n_small)).reshape(1, srows, LANES)
    outs = _adamw("adamw_small", slab_of(W), slab_of(grads), slab_of(M), slab_of(V))
    for res, o in zip((delta, new_m, new_v), outs):
        for n, val in zip(SMALL, _unpack(o[0], small_shapes)):
            res[n] = val

    for n in HELD_TRANSPOSED:
        for res in (grads, delta, new_m, new_v):
            res[n] = jnp.transpose(res[n], (0, 2, 1))
    return (loss, grad_x, *[grads[n] for n in WEIGHTS], *[delta[n] for n in WEIGHTS],
            *[new_m[n] for n in WEIGHTS], *[new_v[n] for n in WEIGHTS])
```
